```python
import math
import jax, jax.numpy as jnp
from jax import lax
import numpy as np

D_MODEL = 1024
BATCH = 8
SEQ = 2048
DEPTH = 4
DEC_BATCH = 128
DEC_SEQ = 4
PAST_LEN = 16384
PAGE_SIZE = 128

D_PLE = 256
MIX_A = D_MODEL // 2
MIX_B = D_MODEL - MIX_A
H_A = 4
DV_A = MIX_A // H_A
DK_A = DV_A // 2
H_B = 4
DV_B = MIX_B // H_B
DK_B = DV_B
N_EXPERTS = 32
TOP_K = 4
D_FF = D_MODEL
SWIGLU_LIMIT = 7.0
SWIGLU_ALPHA = 1.702
CHUNK = 64
EXPERT_BLOCK = 128
LN_EPS = 1e-5
NORM_EPS = 1e-6
LB_FLOOR = 1e-20
DEEPNORM_ALPHA = (2 * DEPTH) ** 0.25
DEEPNORM_BETA = (8 * DEPTH) ** -0.25
SPLIT_SIZES = (H_A * DK_A, H_A * DK_A, MIX_A, MIX_A, H_A, H_A, H_B * DK_B, H_B * DK_B, MIX_B, MIX_B)
N_IN = sum(SPLIT_SIZES)
SPLIT_POINTS = tuple(int(s) for s in np.cumsum(SPLIT_SIZES)[:-1])

kernel_name = 'mlstm_hgrn2_moe_deepnorm_step'


def _layernorm(x, g, b):
    xf = x.astype(jnp.float32)
    mu = jnp.mean(xf, axis=-1, keepdims=True)
    var = jnp.mean(jnp.square(xf - mu), axis=-1, keepdims=True)
    y = (xf - mu) * lax.rsqrt(var + LN_EPS) * g.astype(jnp.float32) + b.astype(jnp.float32)
    return y.astype(x.dtype)


def _head_rmsnorm(h, g):
    H, Dv = h.shape[-2:]
    h = h * lax.rsqrt(jnp.mean(jnp.square(h), axis=-1, keepdims=True) + NORM_EPS)
    return h * g.astype(jnp.float32).reshape(H, Dv)


def _chunk_len(T):
    return CHUNK if T % CHUNK == 0 else T


def _to_chunks(a, L):
    B, T = a.shape[:2]
    return jnp.swapaxes(a.reshape((B, T // L, L) + a.shape[2:]), 0, 1)


def _from_chunks(a):
    nc, B, L = a.shape[:3]
    return jnp.swapaxes(a, 0, 1).reshape((B, nc * L) + a.shape[3:])


def _masked_exp(log_v, mask):
    return jnp.where(mask, jnp.exp(jnp.where(mask, log_v, 0.0)), 0.0)


def _mlstm_scan(q, k, v, i_log, f_log, C0, n0, m0):
    T = q.shape[1]
    L = _chunk_len(T)
    causal = jnp.tril(jnp.ones((L, L), dtype=bool))[None, :, :, None]

    def step(carry, xs):
        C, n, m = carry
        qc, kc, vc, ic, fc = xs
        b = jnp.cumsum(fc, axis=1)
        m_t = jnp.maximum(b + m[:, None], b + lax.cummax(ic - b, axis=1))
        inter = jnp.exp(b + m[:, None] - m_t)
        log_d = b[:, :, None] - b[:, None, :] + ic[:, None, :] - m_t[:, :, None]
        d = _masked_exp(log_d, causal)
        s = jnp.einsum('bthd,bshd->btsh', qc, kc) * d
        num = jnp.einsum('btsh,bshv->bthv', s, vc) + inter[..., None] * jnp.einsum('bthd,bhdv->bthv', qc, C)
        den = jnp.sum(s, axis=2) + inter * jnp.einsum('bthd,bhd->bth', qc, n)
        h = num / jnp.maximum(jnp.abs(den), jnp.exp(-m_t))[..., None]
        m_new = m_t[:, -1]
        w = jnp.exp(b[:, -1:] - b + ic - m_new[:, None])
        decay = jnp.exp(b[:, -1] + m - m_new)
        C = decay[..., None, None] * C + jnp.einsum('bsh,bshd,bshv->bhdv', w, kc, vc)
        n = decay[..., None] * n + jnp.einsum('bsh,bshd->bhd', w, kc)
        return (C, n, m_new), h

    xs = tuple(_to_chunks(a, L) for a in (q, k, v, i_log, f_log))
    (C, n, m), h = lax.scan(step, (C0, n0, m0), xs)
    return _from_chunks(h), C, n, m


def _hgrn2_scan(q, k, v, f_log, S0):
    T = q.shape[1]
    L = _chunk_len(T)
    causal = jnp.tril(jnp.ones((L, L), dtype=bool))[None, :, :, None, None]

    def step(S, xs):
        qc, kc, vc, fc = xs
        g = jnp.cumsum(fc, axis=1)
        dec = _masked_exp(g[:, :, None] - g[:, None, :], causal)
        a = jnp.sum(qc[:, :, None] * kc[:, None] * dec, axis=-1)
        o = jnp.einsum('btsh,bshv->bthv', a, vc) + jnp.einsum('bthd,bhdv->bthv', qc * jnp.exp(g), S)
        g_last = g[:, -1]
        S = jnp.exp(g_last)[..., None] * S + jnp.einsum('bshd,bshv->bhdv', kc * jnp.exp(g_last[:, None] - g), vc)
        return S, o

    xs = tuple(_to_chunks(a, L) for a in (q, k, v, f_log))
    S, o = lax.scan(step, S0, xs)
    return _from_chunks(o), S


def _mixer(x, C0, n0, m0, S0, w_in_l, ig_bias_l, fg_bias_l, norm_a_l, lb_l, norm_b_l, w_out_l):
    B, T, _ = x.shape
    f32 = jnp.float32
    proj = jnp.einsum('btd,dn->btn', x, w_in_l).astype(f32)
    qa, ka, va, oa, ia, fa, qb, fb, ib, gb = jnp.split(proj, SPLIT_POINTS, axis=-1)
    qa = qa.reshape(B, T, H_A, DK_A)
    ka = ka.reshape(B, T, H_A, DK_A) * (DK_A ** -0.5)
    va = va.reshape(B, T, H_A, DV_A)
    i_log = ia + ig_bias_l.astype(f32)
    f_log = jax.nn.log_sigmoid(fa + fg_bias_l.astype(f32))
    h_a, C, n, m = _mlstm_scan(qa, ka, va, i_log, f_log, C0.astype(f32), n0.astype(f32), m0.astype(f32))
    h_a = jax.nn.sigmoid(oa) * _head_rmsnorm(h_a, norm_a_l).reshape(B, T, MIX_A)
    lb = lb_l.reshape(H_B, DK_B)
    zf = fb.reshape(B, T, H_B, DK_B)
    f_log_b = jnp.logaddexp(jnp.log(jnp.maximum(lb, LB_FLOOR)), jnp.log1p(-lb) + jax.nn.log_sigmoid(zf))
    k_b = (1.0 - lb) * jax.nn.sigmoid(-zf)
    q_b = jax.nn.silu(qb).reshape(B, T, H_B, DK_B)
    v_b = ib.reshape(B, T, H_B, DV_B)
    o_b, S = _hgrn2_scan(q_b, k_b, v_b, f_log_b, S0.astype(f32))
    h_b = jax.nn.silu(gb) * _head_rmsnorm(o_b, norm_b_l).reshape(B, T, MIX_B)
    h = jnp.concatenate([h_a, h_b], axis=-1).astype(x.dtype)
    return h @ w_out_l, C, n, m, S


def _moe(x, w_router_l, b_router_l, wg, bg, wu, bu, wd, bd):
    B, T, D = x.shape
    n_tok = B * T
    xt = x.reshape(n_tok, D)
    logits = (xt @ w_router_l + b_router_l).astype(jnp.float32)
    top_v, top_i = lax.top_k(logits, TOP_K)
    gates = jax.nn.softmax(top_v, axis=-1)
    n_assign = n_tok * TOP_K
    e_flat = top_i.reshape(-1)
    order = jnp.argsort(e_flat)
    e_sorted = e_flat[order]
    tok_sorted = (order // TOP_K).astype(jnp.int32)
    gate_sorted = gates.reshape(-1)[order]
    counts = jnp.bincount(e_flat, length=N_EXPERTS)
    padded = (counts + EXPERT_BLOCK - 1) // EXPERT_BLOCK * EXPERT_BLOCK
    start = jnp.cumsum(counts) - counts
    pend = jnp.cumsum(padded)
    pstart = pend - padded
    dest = pstart[e_sorted] + jnp.arange(n_assign) - start[e_sorted]
    n_rows = -(-n_assign // EXPERT_BLOCK) * EXPERT_BLOCK + N_EXPERTS * EXPERT_BLOCK
    n_blocks = n_rows // EXPERT_BLOCK
    row_tok = jnp.zeros((n_rows,), jnp.int32).at[dest].set(tok_sorted)
    row_gate = jnp.zeros((n_rows,), jnp.float32).at[dest].set(gate_sorted)
    block_e = jnp.clip(jnp.searchsorted(pend, jnp.arange(n_blocks) * EXPERT_BLOCK, side='right'), 0, N_EXPERTS - 1)
    xin = xt[row_tok].reshape(n_blocks, EXPERT_BLOCK, D)

    def expert_block(args):
        xb, e = args
        gate = xb @ wg[e] + bg[e]
        up = xb @ wu[e] + bu[e]
        gate = jnp.minimum(gate, SWIGLU_LIMIT)
        up = jnp.clip(up, -SWIGLU_LIMIT, SWIGLU_LIMIT)
        glu = gate * jax.nn.sigmoid(gate * SWIGLU_ALPHA)
        return ((up + 1.0) * glu) @ wd[e] + bd[e]

    y_rows = lax.map(expert_block, (xin, block_e)).reshape(n_rows, D)
    y = jax.ops.segment_sum(y_rows.astype(jnp.float32) * row_gate[:, None], row_tok, num_segments=n_tok)
    return y.reshape(B, T, D).astype(x.dtype)


def setup_inputs(seed: int = 0) -> dict:
    key = jax.random.key(seed)
    ks = jax.random.split(key, 32)
    f32 = jnp.float32

    def nrm(k, shape, scale=1.0):
        return scale * jax.random.normal(k, shape, f32)

    return {
        'x_prompt': nrm(ks[0], (BATCH, SEQ, D_MODEL)),
        'x_sample': nrm(ks[1], (DEC_BATCH, DEC_SEQ, D_MODEL)),
        'state_mlstm_C': nrm(ks[2], (DEPTH, DEC_BATCH, H_A, DK_A, DV_A), 0.5),
        'state_mlstm_n': nrm(ks[3], (DEPTH, DEC_BATCH, H_A, DK_A), 0.5),
        'state_mlstm_m': nrm(ks[4], (DEPTH, DEC_BATCH, H_A)),
        'state_hgrn_S': nrm(ks[5], (DEPTH, DEC_BATCH, H_B, DK_B, DV_B), 0.5),
        'p_prompt': nrm(ks[6], (DEPTH, BATCH, SEQ, D_PLE)),
        'p_sample': nrm(ks[7], (DEPTH, DEC_BATCH, DEC_SEQ, D_PLE)),
        'ln_in_g': 1.0 + nrm(ks[8], (D_MODEL,), 0.02),
        'ln_in_b': nrm(ks[9], (D_MODEL,), 0.02),
        'w_in': nrm(ks[10], (DEPTH, D_MODEL, N_IN), D_MODEL ** -0.5),
        'mlstm_ig_bias': nrm(ks[11], (DEPTH, H_A), 0.1),
        'mlstm_fg_bias': jnp.linspace(3.0, 6.0, H_A, dtype=f32)[None, :] + nrm(ks[12], (DEPTH, H_A), 0.1),
        'mlstm_norm_g': 1.0 + nrm(ks[13], (DEPTH, MIX_A), 0.02),
        'hgrn_lb_logits': nrm(ks[14], (DEPTH, H_B * DK_B), 0.1),
        'hgrn_norm_g': 1.0 + nrm(ks[15], (DEPTH, MIX_B), 0.02),
        'w_out': nrm(ks[16], (DEPTH, D_MODEL, D_MODEL), DEEPNORM_BETA * D_MODEL ** -0.5),
        'ln1_g': 1.0 + nrm(ks[17], (DEPTH, D_MODEL), 0.02),
        'ln1_b': nrm(ks[18], (DEPTH, D_MODEL), 0.02),
        'w_router': nrm(ks[19], (DEPTH, D_MODEL, N_EXPERTS), D_MODEL ** -0.5),
        'b_router': nrm(ks[20], (DEPTH, N_EXPERTS), 0.01),
        'w_gate': nrm(ks[21], (DEPTH, N_EXPERTS, D_MODEL, D_FF), D_MODEL ** -0.5),
        'b_gate': nrm(ks[22], (DEPTH, N_EXPERTS, D_FF), 0.01),
        'w_up': nrm(ks[23], (DEPTH, N_EXPERTS, D_MODEL, D_FF), D_MODEL ** -0.5),
        'b_up': nrm(ks[24], (DEPTH, N_EXPERTS, D_FF), 0.01),
        'w_down': nrm(ks[25], (DEPTH, N_EXPERTS, D_FF, D_MODEL), DEEPNORM_BETA * D_FF ** -0.5),
        'b_down': nrm(ks[26], (DEPTH, N_EXPERTS, D_MODEL), 0.01),
        'w_ple_gate': nrm(ks[27], (DEPTH, D_MODEL, D_MODEL), D_MODEL ** -0.5),
        'w_ple_proj': nrm(ks[28], (DEPTH, D_PLE, D_MODEL), D_PLE ** -0.5),
        'ln2_g': 1.0 + nrm(ks[29], (DEPTH, D_MODEL), 0.02),
        'ln2_b': nrm(ks[30], (DEPTH, D_MODEL), 0.02),
    }


def reference(x_prompt, x_sample, state_mlstm_C, state_mlstm_n, state_mlstm_m, state_hgrn_S,
              p_prompt, p_sample, ln_in_g, ln_in_b, w_in, mlstm_ig_bias, mlstm_fg_bias,
              mlstm_norm_g, hgrn_lb_logits, hgrn_norm_g, w_out, ln1_g, ln1_b,
              w_router, b_router, w_gate, b_gate, w_up, b_up, w_down, b_down,
              w_ple_gate, w_ple_proj, ln2_g, ln2_b):
    f32 = jnp.float32
    lb_soft = jax.nn.softmax(hgrn_lb_logits.astype(f32), axis=0)
    lower_bounds = jnp.cumsum(lb_soft, axis=0) - lb_soft[0]

    def trunk(x, p, C_in, n_in, m_in, S_in):
        x = _layernorm(x, ln_in_g, ln_in_b)
        Cs, ns, ms, Ss = [], [], [], []
        for l in range(DEPTH):
            mix, C, n, m, S = _mixer(x, C_in[l], n_in[l], m_in[l], S_in[l], w_in[l], mlstm_ig_bias[l],
                                     mlstm_fg_bias[l], mlstm_norm_g[l], lower_bounds[l],
                                     hgrn_norm_g[l], w_out[l])
            x = _layernorm(DEEPNORM_ALPHA * x + mix, ln1_g[l], ln1_b[l])
            ple = jax.nn.sigmoid(x @ w_ple_gate[l]) * (p[l] @ w_ple_proj[l])
            ffn = _moe(x, w_router[l], b_router[l], w_gate[l], b_gate[l], w_up[l], b_up[l],
                       w_down[l], b_down[l])
            x = _layernorm(DEEPNORM_ALPHA * x + ffn + ple, ln2_g[l], ln2_b[l])
            Cs.append(C)
            ns.append(n)
            ms.append(m)
            Ss.append(S)
        return x, jnp.stack(Cs), jnp.stack(ns), jnp.stack(ms), jnp.stack(Ss)

    bp = x_prompt.shape[0]
    C_zero = jnp.zeros((DEPTH, bp, H_A, DK_A, DV_A), f32)
    n_zero = jnp.zeros((DEPTH, bp, H_A, DK_A), f32)
    m_zero = jnp.zeros((DEPTH, bp, H_A), f32)
    S_zero = jnp.zeros((DEPTH, bp, H_B, DK_B, DV_B), f32)
    y_prompt, C_p, n_p, m_p, S_p = trunk(x_prompt, p_prompt, C_zero, n_zero, m_zero, S_zero)
    y_sample, C_s, n_s, m_s, S_s = trunk(x_sample, p_sample, state_mlstm_C, state_mlstm_n,
                                         state_mlstm_m, state_hgrn_S)
    return (y_prompt, y_sample, C_p, n_p, m_p, S_p, C_s, n_s, m_s, S_s)
```

```python
import functools

import jax
import jax.numpy as jnp
import numpy as np
from jax import lax
from jax.experimental import pallas as pl
from jax.experimental.pallas import tpu as pltpu

F32 = jnp.float32
BF16 = jnp.bfloat16
HIGHEST = lax.Precision.HIGHEST

D_MODEL = 1024
DEPTH = 4
D_PLE = 256
N_HEADS = 4
DK_A = 64
D_HEAD = 128
MIX = 512
N_EXPERTS = 32
TOP_K = 4
SWIGLU_LIMIT = 7.0
SWIGLU_ALPHA = 1.702
CHUNK = 64
LN_EPS = 1e-5
NORM_EPS = 1e-6
LB_FLOOR = 1e-20
DEEPNORM_ALPHA = (2 * DEPTH) ** 0.25
SPLIT_SIZES = (256, 256, 512, 512, 4, 4, 512, 512, 512, 512)
SPLIT_POINTS = tuple(int(s) for s in np.cumsum(SPLIT_SIZES)[:-1])

LANES = 128
SUBLANES = 8
VMEM_LIMIT = 56 * 1024 * 1024

QA0, KA0, VA0, OA0 = 0, 512, 1024, 1536
QB0, FB0, IB0, GB0 = 2048, 2560, 3072, 3584
IG0, FG0 = 4096, 4224
N_PROJ = 4352

ROW_TILE = 256
EXPERT_ROWS = 256
HGRN_SUB = 16

NT_DIMS = (((1,), (1,)), ((), ()))
TN_DIMS = (((0,), (0,)), ((), ()))


def _params(*sem):
    return pltpu.CompilerParams(dimension_semantics=sem, vmem_limit_bytes=VMEM_LIMIT)


def _log_sigmoid(x):
    return jnp.minimum(x, 0.0) - jnp.log1p(jnp.exp(-jnp.abs(x)))


def _layernorm_rows(x, g, b):
    mu = jnp.mean(x, axis=-1, keepdims=True)
    xc = x - mu
    var = jnp.mean(xc * xc, axis=-1, keepdims=True)
    return xc * lax.rsqrt(var + LN_EPS) * g + b


def _ln_kernel(x_ref, g_ref, b_ref, o_ref):
    o_ref[...] = _layernorm_rows(x_ref[...], g_ref[...], b_ref[...])


def _ln(x, g, b):
    n, d = x.shape
    row = pl.BlockSpec((ROW_TILE, d), lambda i: (i, 0))
    vec = pl.BlockSpec((1, d), lambda i: (0, 0))
    return pl.pallas_call(
        _ln_kernel, grid=(n // ROW_TILE,), in_specs=[row, vec, vec], out_specs=row,
        out_shape=jax.ShapeDtypeStruct((n, d), F32), compiler_params=_params("parallel"),
        name="ln_in")(x, g.reshape(1, d), b.reshape(1, d))


def _add_ln_kernel(a_ref, b_ref, g_ref, beta_ref, o_ref):
    o_ref[...] = _layernorm_rows(a_ref[...] + b_ref[...], g_ref[...], beta_ref[...])


def _add_ln(a, b, g, beta):
    n, d = a.shape
    row = pl.BlockSpec((ROW_TILE, d), lambda i: (i, 0))
    vec = pl.BlockSpec((1, d), lambda i: (0, 0))
    return pl.pallas_call(
        _add_ln_kernel, grid=(n // ROW_TILE,), in_specs=[row, row, vec, vec], out_specs=row,
        out_shape=jax.ShapeDtypeStruct((n, d), F32), compiler_params=_params("parallel"),
        name="add_ln2")(a, b, g.reshape(1, d), beta.reshape(1, d))


def _inproj_kernel(x_ref, w_ref, o_ref):
    o_ref[...] = jnp.dot(x_ref[...].astype(BF16), w_ref[...], preferred_element_type=F32)


def _inproj(x, w):
    n, d = x.shape
    return pl.pallas_call(
        _inproj_kernel, grid=(n // ROW_TILE,),
        in_specs=[pl.BlockSpec((ROW_TILE, d), lambda i: (i, 0)),
                  pl.BlockSpec((d, N_PROJ), lambda i: (0, 0))],
        out_specs=pl.BlockSpec((ROW_TILE, N_PROJ), lambda i: (i, 0)),
        out_shape=jax.ShapeDtypeStruct((n, N_PROJ), F32), compiler_params=_params("parallel"),
        name="in_proj")(x, w)


def _scan_kernel(*refs, L, last, SB, has_state):
    it = iter(refs)
    proj_ref, gbias_ref, lb_ref, nga_ref, ngb_ref = (next(it) for _ in range(5))
    if has_state:
        C0_ref, n0_ref, m0_ref, S0_ref = (next(it) for _ in range(4))
    h_ref, C_out, n_out, m_out, S_out = (next(it) for _ in range(5))
    C_s, n_s, m_s, St_s, g_s, k_s, v_s = (next(it) for _ in range(7))

    c = pl.program_id(1)
    nc = pl.num_programs(1)

    @pl.when(c == 0)
    def _init():
        if has_state:
            for h in range(N_HEADS):
                C_s[h, 0:DK_A, :] = C0_ref[0, h]
                C_s[h, DK_A:D_HEAD, :] = jnp.zeros((D_HEAD - DK_A, D_HEAD), F32)
                n_s[h] = n0_ref[0, h]
                St_s[h] = S0_ref[0, h].T
            m_s[...] = m0_ref[0]
        else:
            C_s[...] = jnp.zeros(C_s.shape, F32)
            n_s[...] = jnp.zeros(n_s.shape, F32)
            m_s[...] = jnp.zeros(m_s.shape, F32)
            St_s[...] = jnp.zeros(St_s.shape, F32)

    row = lax.broadcasted_iota(jnp.int32, (L, L), 0)
    col = lax.broadcasted_iota(jnp.int32, (L, L), 1)
    causal = row >= col
    tril = causal.astype(F32)
    ones_l = jnp.ones((L, LANES), F32)
    lane = lax.broadcasted_iota(jnp.int32, (L, LANES), 1)
    trow = lax.broadcasted_iota(jnp.int32, (L, 1), 0)
    valid = trow <= last
    padded = last < L - 1

    i_t = proj_ref[:, IG0:IG0 + LANES] + gbias_ref[0:1, :]
    f_t = _log_sigmoid(proj_ref[:, FG0:FG0 + LANES] + gbias_ref[1:2, :])
    b_t = jnp.dot(tril, f_t, precision=HIGHEST, preferred_element_type=F32)
    r_t = i_t - b_t
    m_all = m_s[...]

    for h in range(N_HEADS):
        hs = slice(LANES * h, LANES * (h + 1))
        qf = proj_ref[:, QA0 + LANES * h:QA0 + LANES * (h + 1)]
        kf = proj_ref[:, KA0 + LANES * h:KA0 + LANES * (h + 1)]
        q = qf.astype(BF16)
        k = kf.astype(BF16)
        v = proj_ref[:, VA0 + LANES * h:VA0 + LANES * (h + 1)].astype(BF16)
        b_col = b_t[:, h:h + 1]
        i_col = i_t[:, h:h + 1]
        r_m = lax.dot_general(ones_l, jnp.where(lane == h, r_t, 0.0), NT_DIMS,
                              precision=HIGHEST, preferred_element_type=F32)
        m_prev = m_all[:, h:h + 1]
        cm = jnp.max(jnp.where(causal, r_m, -jnp.inf), axis=1, keepdims=True)
        m_t = b_col + jnp.maximum(m_prev, cm)
        inter = jnp.exp(b_col + m_prev - m_t)
        d = jnp.exp(jnp.where(causal, (b_col - m_t) + r_m, -1e30))
        s = lax.dot_general(q, k, NT_DIMS, preferred_element_type=F32) * d
        C_prev = C_s[h]
        num = (jnp.dot(s.astype(BF16), v, preferred_element_type=F32)
               + inter * jnp.dot(q, C_prev.astype(BF16), preferred_element_type=F32))
        den = (jnp.sum(s, axis=1, keepdims=True)
               + inter * jnp.sum(qf * n_s[h], axis=1, keepdims=True))
        hh = num / jnp.maximum(jnp.abs(den), jnp.exp(-m_t))

        m_new = m_t[last:last + 1, :]
        b_last = b_col[last:last + 1, :]
        w_arg = b_last - b_col + i_col - m_new
        if padded:
            w_arg = jnp.where(valid, w_arg, -1e30)
        kw = kf * jnp.exp(w_arg)
        decay = jnp.exp(b_last + m_prev - m_new)
        C_s[h] = decay * C_prev + lax.dot_general(kw.astype(BF16), v, TN_DIMS,
                                                  preferred_element_type=F32)
        n_s[h] = decay * n_s[h] + jnp.sum(kw, axis=0, keepdims=True)
        m_s[:, h:h + 1] = m_new

        hn = hh * lax.rsqrt(jnp.mean(hh * hh, axis=-1, keepdims=True) + NORM_EPS) * nga_ref[:, hs]
        h_ref[:, hs] = jax.nn.sigmoid(proj_ref[:, OA0 + LANES * h:OA0 + LANES * (h + 1)]) * hn

    zf = proj_ref[:, FB0:FB0 + MIX]
    lb = lb_ref[...]
    la = jnp.log(jnp.maximum(lb, LB_FLOOR))
    bb = jnp.log1p(-lb) + _log_sigmoid(zf)
    f_log = jnp.maximum(la, bb) + jnp.log1p(jnp.exp(-jnp.abs(la - bb)))
    g_s[...] = jnp.dot(tril, f_log, precision=HIGHEST, preferred_element_type=F32)
    k_s[...] = (1.0 - lb) * jax.nn.sigmoid(-zf)
    v_s[...] = proj_ref[:, IB0:IB0 + MIX]
    rsb = lax.broadcasted_iota(jnp.int32, (SB, 1), 0)

    for h in range(N_HEADS):
        hs = slice(LANES * h, LANES * (h + 1))
        gh = g_s[:, hs]
        kh = k_s[:, hs]
        vh = v_s[:, hs]
        qr = proj_ref[:, QB0 + LANES * h:QB0 + LANES * (h + 1)]
        qh = qr * jax.nn.sigmoid(qr)
        St = St_s[h]
        o = lax.dot_general((qh * jnp.exp(gh)).astype(BF16), St.astype(BF16), NT_DIMS,
                            preferred_element_type=F32)
        blocks = []
        for blk in range(L // SB):
            r0 = blk * SB
            g_blk = gh[r0:r0 + SB]
            q_blk = qh[r0:r0 + SB]
            if blk > 0:
                g_ref0 = gh[r0:r0 + 1]
                qt = (q_blk * jnp.exp(g_blk - g_ref0)).astype(BF16)
                kt = (kh[0:r0] * jnp.exp(g_ref0 - gh[0:r0])).astype(BF16)
                a = lax.dot_general(qt, kt, NT_DIMS, preferred_element_type=F32)
                acc = jnp.dot(a.astype(BF16), vh[0:r0].astype(BF16), preferred_element_type=F32)
            else:
                acc = jnp.zeros((SB, LANES), F32)

            for j in range(SB):
                g_j = g_s[r0 + j:r0 + j + 1, hs]
                k_j = k_s[r0 + j:r0 + j + 1, hs]
                v_j = v_s[r0 + j:r0 + j + 1, hs]
                e = jnp.exp(jnp.where(rsb >= j, g_blk - g_j, -1e30))
                a_col = jnp.sum(q_blk * k_j * e, axis=1, keepdims=True)
                acc = acc + a_col * v_j
            blocks.append(acc)
        o = o + (blocks[0] if len(blocks) == 1 else jnp.concatenate(blocks, axis=0))

        g_last = gh[last:last + 1]
        dec_arg = g_last - gh
        if padded:
            dec_arg = jnp.where(valid, dec_arg, -1e30)
        kdec = kh * jnp.exp(dec_arg)
        St_s[h] = jnp.exp(g_last) * St + lax.dot_general(vh.astype(BF16), kdec.astype(BF16), TN_DIMS,
                                                        preferred_element_type=F32)
        on = o * lax.rsqrt(jnp.mean(o * o, axis=-1, keepdims=True) + NORM_EPS) * ngb_ref[:, hs]
        gr = proj_ref[:, GB0 + LANES * h:GB0 + LANES * (h + 1)]
        h_ref[:, MIX + LANES * h:MIX + LANES * (h + 1)] = gr * jax.nn.sigmoid(gr) * on

    @pl.when(c == nc - 1)
    def _finish():
        for h in range(N_HEADS):
            C_out[0, h] = C_s[h, 0:DK_A, :]
            n_out[0, h] = n_s[h]
            S_out[0, h] = St_s[h].T
        m_out[0] = m_s[...]


def _scan(proj, gbias, lb, nga, ngb, state, *, batch, n_chunks, L, last):
    has_state = state is not None
    SB = min(HGRN_SUB, L)
    const2 = lambda b, c: (0, 0)
    in_specs = [
        pl.BlockSpec((L, N_PROJ), lambda b, c: (b * n_chunks + c, 0)),
        pl.BlockSpec((2, LANES), const2),
        pl.BlockSpec((1, MIX), const2),
        pl.BlockSpec((1, MIX), const2),
        pl.BlockSpec((1, MIX), const2),
    ]
    args = [proj, gbias, lb, nga, ngb]
    st4 = lambda b, c: (b, 0, 0, 0)
    st3 = lambda b, c: (b, 0, 0)
    if has_state:
        in_specs += [
            pl.BlockSpec((1, N_HEADS, DK_A, D_HEAD), st4),
            pl.BlockSpec((1, N_HEADS, 1, LANES), st4),
            pl.BlockSpec((1, 1, LANES), st3),
            pl.BlockSpec((1, N_HEADS, D_HEAD, D_HEAD), st4),
        ]
        args += list(state)
    out_specs = [
        pl.BlockSpec((L, D_MODEL), lambda b, c: (b * n_chunks + c, 0)),
        pl.BlockSpec((1, N_HEADS, DK_A, D_HEAD), st4),
        pl.BlockSpec((1, N_HEADS, 1, LANES), st4),
        pl.BlockSpec((1, 1, LANES), st3),
        pl.BlockSpec((1, N_HEADS, D_HEAD, D_HEAD), st4),
    ]
    out_shape = [
        jax.ShapeDtypeStruct((batch * n_chunks * L, D_MODEL), F32),
        jax.ShapeDtypeStruct((batch, N_HEADS, DK_A, D_HEAD), F32),
        jax.ShapeDtypeStruct((batch, N_HEADS, 1, LANES), F32),
        jax.ShapeDtypeStruct((batch, 1, LANES), F32),
        jax.ShapeDtypeStruct((batch, N_HEADS, D_HEAD, D_HEAD), F32),
    ]
    scratch = [
        pltpu.VMEM((N_HEADS, D_HEAD, D_HEAD), F32),
        pltpu.VMEM((N_HEADS, 1, LANES), F32),
        pltpu.VMEM((1, LANES), F32),
        pltpu.VMEM((N_HEADS, D_HEAD, D_HEAD), F32),
        pltpu.VMEM((L, MIX), F32),
        pltpu.VMEM((L, MIX), F32),
        pltpu.VMEM((L, MIX), F32),
    ]
    kern = functools.partial(_scan_kernel, L=L, last=last, SB=SB, has_state=has_state)
    return pl.pallas_call(
        kern, grid=(batch, n_chunks), in_specs=in_specs, out_specs=out_specs, out_shape=out_shape,
        scratch_shapes=scratch, compiler_params=_params("parallel", "arbitrary"),
        name="scan_state" if has_state else "scan_prompt")(*args)


def _post_mixer_kernel(x_ref, h_ref, p_ref, wout_ref, g1_ref, b1_ref, wpg_ref, wpp_ref, wr_ref, br_ref,
                       x1b_ref, resid_ref, logit_ref):
    mix = jnp.dot(h_ref[...].astype(BF16), wout_ref[...], preferred_element_type=F32)
    x1 = _layernorm_rows(DEEPNORM_ALPHA * x_ref[...] + mix, g1_ref[...], b1_ref[...])
    x1b = x1.astype(BF16)
    gate = jax.nn.sigmoid(jnp.dot(x1b, wpg_ref[...], preferred_element_type=F32))
    pp = jnp.dot(p_ref[...].astype(BF16), wpp_ref[...], preferred_element_type=F32)
    resid_ref[...] = DEEPNORM_ALPHA * x1 + gate * pp
    x1b_ref[...] = x1b
    logit_ref[...] = jnp.dot(x1, wr_ref[...], precision=HIGHEST,
                             preferred_element_type=F32) + br_ref[...]


def _post_mixer(x, h, p, wout, g1, b1, wpg, wpp, wr, br):
    n, d = x.shape
    row = lambda w: pl.BlockSpec((ROW_TILE, w), lambda i: (i, 0))
    full = lambda a: pl.BlockSpec(a.shape, lambda i: (0, 0))
    g1 = g1.reshape(1, d)
    b1 = b1.reshape(1, d)
    return pl.pallas_call(
        _post_mixer_kernel, grid=(n // ROW_TILE,),
        in_specs=[row(d), row(d), row(D_PLE), full(wout), full(g1), full(b1), full(wpg), full(wpp),
                  full(wr), full(br)],
        out_specs=[row(d), row(d), row(LANES)],
        out_shape=[jax.ShapeDtypeStruct((n, d), BF16), jax.ShapeDtypeStruct((n, d), F32),
                   jax.ShapeDtypeStruct((n, LANES), F32)],
        compiler_params=_params("parallel"), name="post_mixer")(x, h, p, wout, g1, b1, wpg, wpp, wr, br)


def _expert_kernel(be_ref, nb_ref, x_ref, rg_ref, wg_ref, bg_ref, wu_ref, bu_ref, wd_ref, bd_ref, o_ref):
    i = pl.program_id(0)

    @pl.when(i < nb_ref[0])
    def _compute():
        xb = x_ref[...]
        gate = jnp.dot(xb, wg_ref[0], preferred_element_type=F32) + bg_ref[0]
        up = jnp.dot(xb, wu_ref[0], preferred_element_type=F32) + bu_ref[0]
        gate = jnp.minimum(gate, SWIGLU_LIMIT)
        up = jnp.clip(up, -SWIGLU_LIMIT, SWIGLU_LIMIT)
        glu = gate * jax.nn.sigmoid(gate * SWIGLU_ALPHA)
        act = ((up + 1.0) * glu).astype(BF16)
        y = jnp.dot(act, wd_ref[0], preferred_element_type=F32) + bd_ref[0]
        o_ref[...] = y * rg_ref[...]

    @pl.when(i >= nb_ref[0])
    def _skip():
        o_ref[...] = jnp.zeros(o_ref.shape, F32)


def _experts(block_e, n_used, xin, row_gate, wg, bg, wu, bu, wd, bd):
    n_rows, d = xin.shape
    n_blocks = n_rows // EXPERT_ROWS
    rows = lambda w: pl.BlockSpec((EXPERT_ROWS, w), lambda i, be, nb: (i, 0))
    wspec = pl.BlockSpec((1, d, d), lambda i, be, nb: (be[i], 0, 0))
    bspec = pl.BlockSpec((1, 1, d), lambda i, be, nb: (be[i], 0, 0))
    grid_spec = pltpu.PrefetchScalarGridSpec(
        num_scalar_prefetch=2, grid=(n_blocks,),
        in_specs=[rows(d), rows(1), wspec, bspec, wspec, bspec, wspec, bspec],
        out_specs=rows(d))
    return pl.pallas_call(
        _expert_kernel, grid_spec=grid_spec, out_shape=jax.ShapeDtypeStruct((n_rows, d), F32),
        compiler_params=_params("arbitrary"), name="experts")(
            block_e, n_used, xin, row_gate, wg, bg, wu, bu, wd, bd)


def _moe(x1b, logits, wg, bg, wu, bu, wd, bd):
    n_tok = x1b.shape[0]
    n_assign = n_tok * TOP_K
    top_v, top_i = lax.top_k(logits[:, :N_EXPERTS], TOP_K)
    gates = jax.nn.softmax(top_v, axis=-1)
    e_flat = top_i.reshape(-1).astype(jnp.int32)
    onehot = (e_flat[:, None] == jnp.arange(N_EXPERTS, dtype=jnp.int32)[None, :]).astype(jnp.int32)
    rank = jnp.take_along_axis(jnp.cumsum(onehot, axis=0), e_flat[:, None], axis=1)[:, 0] - 1
    counts = jnp.sum(onehot, axis=0)
    padded = (counts + EXPERT_ROWS - 1) // EXPERT_ROWS * EXPERT_ROWS
    pend = jnp.cumsum(padded)
    pstart = pend - padded
    dest = pstart[e_flat] + rank
    n_rows = -(-n_assign // EXPERT_ROWS) * EXPERT_ROWS + N_EXPERTS * EXPERT_ROWS
    n_blocks = n_rows // EXPERT_ROWS
    tok = jnp.arange(n_assign, dtype=jnp.int32) // TOP_K
    row_tok = jnp.zeros((n_rows,), jnp.int32).at[dest].set(tok)
    row_gate = jnp.zeros((n_rows,), F32).at[dest].set(gates.reshape(-1))
    block_e = jnp.clip(jnp.searchsorted(pend, jnp.arange(n_blocks, dtype=jnp.int32) * EXPERT_ROWS,
                                        side='right'), 0, N_EXPERTS - 1).astype(jnp.int32)
    n_used = (pend[-1] // EXPERT_ROWS).astype(jnp.int32).reshape(1)
    xin = x1b[row_tok]
    y_rows = _experts(block_e, n_used, xin, row_gate.reshape(n_rows, 1), wg, bg, wu, bu, wd, bd)
    return jnp.sum(y_rows[dest.reshape(n_tok, TOP_K)], axis=1)


def _pack_w_in(w_in):
    qa, ka, va, oa, ia, fa, qb, fb, ib, gb = jnp.split(w_in, SPLIT_POINTS, axis=-1)

    def pad_heads(w):
        w = w.reshape(w.shape[:-1] + (N_HEADS, DK_A))
        w = jnp.pad(w, ((0, 0),) * (w.ndim - 1) + ((0, D_HEAD - DK_A),))
        return w.reshape(w.shape[:-2] + (N_HEADS * D_HEAD,))

    def pad_gate(w):
        return jnp.pad(w, ((0, 0),) * (w.ndim - 1) + ((0, LANES - N_HEADS),))

    cols = [pad_heads(qa), pad_heads(ka * (DK_A ** -0.5)), va, oa, qb, fb, ib, gb, pad_gate(ia), pad_gate(fa)]
    return jnp.concatenate(cols, axis=-1).astype(BF16)


def kernel(x_prompt, x_sample, state_mlstm_C, state_mlstm_n, state_mlstm_m, state_hgrn_S, p_prompt, p_sample,
           ln_in_g, ln_in_b, w_in, mlstm_ig_bias, mlstm_fg_bias, mlstm_norm_g, hgrn_lb_logits, hgrn_norm_g,
           w_out, ln1_g, ln1_b, w_router, b_router, w_gate, b_gate, w_up, b_up, w_down, b_down,
           w_ple_gate, w_ple_proj, ln2_g, ln2_b):
    bp, tp, d = x_prompt.shape
    bs, ts, _ = x_sample.shape
    n_p = bp * tp
    n_s = bs * ts
    ts_pad = SUBLANES
    assert tp % CHUNK == 0 and ts <= ts_pad and ts % CHUNK != 0

    lb_soft = jax.nn.softmax(hgrn_lb_logits.astype(F32), axis=0)
    lower_bounds = jnp.cumsum(lb_soft, axis=0) - lb_soft[0]

    w_in_p = _pack_w_in(w_in)
    gbias = jnp.stack([jnp.pad(mlstm_ig_bias, ((0, 0), (0, LANES - N_HEADS))),
                       jnp.pad(mlstm_fg_bias, ((0, 0), (0, LANES - N_HEADS)))], axis=1)
    w_out_b = w_out.astype(BF16)
    w_pg_b = w_ple_gate.astype(BF16)
    w_pp_b = w_ple_proj.astype(BF16)
    w_r = jnp.pad(w_router, ((0, 0), (0, 0), (0, LANES - N_EXPERTS)))
    b_r = jnp.pad(b_router, ((0, 0), (0, LANES - N_EXPERTS))).reshape(DEPTH, 1, LANES)
    n0_pad = jnp.pad(state_mlstm_n, ((0, 0), (0, 0), (0, 0), (0, D_HEAD - DK_A))).reshape(
        DEPTH, bs, N_HEADS, 1, D_HEAD)
    m0_pad = jnp.pad(state_mlstm_m, ((0, 0), (0, 0), (0, LANES - N_HEADS))).reshape(DEPTH, bs, 1, LANES)

    x = jnp.concatenate([x_prompt.reshape(n_p, d), x_sample.reshape(n_s, d)], axis=0)
    p_all = jnp.concatenate([p_prompt.reshape(DEPTH, n_p, D_PLE), p_sample.reshape(DEPTH, n_s, D_PLE)], axis=1)
    x = _ln(x, ln_in_g, ln_in_b)

    outs = {k: [] for k in ("Cp", "np", "mp", "Sp", "Cs", "ns", "ms", "Ss")}
    for l in range(DEPTH):
        proj = _inproj(x, w_in_p[l])
        lb = lower_bounds[l].reshape(1, MIX)
        nga = mlstm_norm_g[l].reshape(1, MIX)
        ngb = hgrn_norm_g[l].reshape(1, MIX)
        h_p, C_p, n_pr, m_pr, S_p = _scan(proj[:n_p], gbias[l], lb, nga, ngb, None,
                                          batch=bp, n_chunks=tp // CHUNK, L=CHUNK, last=CHUNK - 1)
        proj_s = jnp.pad(proj[n_p:].reshape(bs, ts, N_PROJ), ((0, 0), (0, ts_pad - ts), (0, 0)))
        h_s, C_s, n_sr, m_sr, S_s = _scan(proj_s.reshape(bs * ts_pad, N_PROJ), gbias[l], lb, nga, ngb,
                                          (state_mlstm_C[l], n0_pad[l], m0_pad[l], state_hgrn_S[l]),
                                          batch=bs, n_chunks=1, L=ts_pad, last=ts - 1)
        h_s = h_s.reshape(bs, ts_pad, d)[:, :ts].reshape(n_s, d)
        h_all = jnp.concatenate([h_p, h_s], axis=0)
        x1b, resid, logits = _post_mixer(x, h_all, p_all[l], w_out_b[l], ln1_g[l], ln1_b[l],
                                         w_pg_b[l], w_pp_b[l], w_r[l], b_r[l])
        ffn = _moe(x1b, logits, w_gate[l].astype(BF16), b_gate[l].reshape(N_EXPERTS, 1, d),
                   w_up[l].astype(BF16), b_up[l].reshape(N_EXPERTS, 1, d),
                   w_down[l].astype(BF16), b_down[l].reshape(N_EXPERTS, 1, d))
        x = _add_ln(resid, ffn, ln2_g[l], ln2_b[l])
        outs["Cp"].append(C_p)
        outs["np"].append(n_pr[:, :, 0, :DK_A])
        outs["mp"].append(m_pr[:, 0, :N_HEADS])
        outs["Sp"].append(S_p)
        outs["Cs"].append(C_s)
        outs["ns"].append(n_sr[:, :, 0, :DK_A])
        outs["ms"].append(m_sr[:, 0, :N_HEADS])
        outs["Ss"].append(S_s)

    y_prompt = x[:n_p].reshape(bp, tp, d)
    y_sample = x[n_p:].reshape(bs, ts, d)
    st = {k: jnp.stack(v) for k, v in outs.items()}
    return (y_prompt, y_sample, st["Cp"], st["np"], st["mp"], st["Sp"],
            st["Cs"], st["ns"], st["ms"], st["Ss"])
```

```python
import functools

import jax
import jax.numpy as jnp
import numpy as np
from jax import lax
from jax.experimental import pallas as pl
from jax.experimental.pallas import tpu as pltpu

F32 = jnp.float32
BF16 = jnp.bfloat16
HIGHEST = lax.Precision.HIGHEST

D_MODEL = 1024
DEPTH = 4
D_PLE = 256
N_HEADS = 4
DK_A = 64
D_HEAD = 128
MIX = 512
N_EXPERTS = 32
TOP_K = 4
SWIGLU_LIMIT = 7.0
SWIGLU_ALPHA = 1.702
CHUNK = 64
LN_EPS = 1e-5
NORM_EPS = 1e-6
LB_FLOOR = 1e-20
DEEPNORM_ALPHA = (2 * DEPTH) ** 0.25
SPLIT_SIZES = (256, 256, 512, 512, 4, 4, 512, 512, 512, 512)
SPLIT_POINTS = tuple(int(s) for s in np.cumsum(SPLIT_SIZES)[:-1])

LANES = 128
SUBLANES = 8
VMEM_LIMIT = 56 * 1024 * 1024

QA0, KA0, VA0, OA0 = 0, 512, 1024, 1536
QB0, FB0, IB0, GB0 = 2048, 2560, 3072, 3584
IG0, FG0 = 4096, 4224
N_PROJ = 4352

ROW_TILE = 256
EXPERT_ROWS = 256
HGRN_SUB = 16

NT_DIMS = (((1,), (1,)), ((), ()))
TN_DIMS = (((0,), (0,)), ((), ()))


def _params(*sem):
    return pltpu.CompilerParams(dimension_semantics=sem, vmem_limit_bytes=VMEM_LIMIT)


def _log_sigmoid(x):
    return jnp.minimum(x, 0.0) - jnp.log1p(jnp.exp(-jnp.abs(x)))


def _layernorm_rows(x, g, b):
    mu = jnp.mean(x, axis=-1, keepdims=True)
    xc = x - mu
    var = jnp.mean(xc * xc, axis=-1, keepdims=True)
    return xc * lax.rsqrt(var + LN_EPS) * g + b


def _ln_kernel(x_ref, g_ref, b_ref, o_ref):
    o_ref[...] = _layernorm_rows(x_ref[...], g_ref[...], b_ref[...])


def _ln(x, g, b):
    n, d = x.shape
    row = pl.BlockSpec((ROW_TILE, d), lambda i: (i, 0))
    vec = pl.BlockSpec((1, d), lambda i: (0, 0))
    return pl.pallas_call(
        _ln_kernel, grid=(n // ROW_TILE,), in_specs=[row, vec, vec], out_specs=row,
        out_shape=jax.ShapeDtypeStruct((n, d), F32), compiler_params=_params("parallel"),
        name="ln_in")(x, g.reshape(1, d), b.reshape(1, d))


def _combine_ln_kernel(resid_ref, y4_ref, gate_ref, g_ref, beta_ref, o_ref):
    d = resid_ref.shape[1]
    acc = resid_ref[...]
    for k in range(TOP_K):
        acc = acc + gate_ref[:, k:k + 1] * y4_ref[:, k * d:(k + 1) * d]
    o_ref[...] = _layernorm_rows(acc, g_ref[...], beta_ref[...])


def _combine_ln(resid, y4, gates, g, beta):
    n, d = resid.shape
    row = lambda w: pl.BlockSpec((ROW_TILE, w), lambda i: (i, 0))
    vec = pl.BlockSpec((1, d), lambda i: (0, 0))
    return pl.pallas_call(
        _combine_ln_kernel, grid=(n // ROW_TILE,), in_specs=[row(d), row(TOP_K * d), row(LANES), vec, vec],
        out_specs=row(d), out_shape=jax.ShapeDtypeStruct((n, d), F32), compiler_params=_params("parallel"),
        name="combine_ln2")(resid, y4, gates, g.reshape(1, d), beta.reshape(1, d))


def _inproj_kernel(x_ref, w_ref, o_ref):
    o_ref[...] = jnp.dot(x_ref[...].astype(BF16), w_ref[...], preferred_element_type=F32)


def _inproj(x, w):
    n, d = x.shape
    return pl.pallas_call(
        _inproj_kernel, grid=(n // ROW_TILE,),
        in_specs=[pl.BlockSpec((ROW_TILE, d), lambda i: (i, 0)),
                  pl.BlockSpec((d, N_PROJ), lambda i: (0, 0))],
        out_specs=pl.BlockSpec((ROW_TILE, N_PROJ), lambda i: (i, 0)),
        out_shape=jax.ShapeDtypeStruct((n, N_PROJ), F32), compiler_params=_params("parallel"),
        name="in_proj")(x, w)


def _scan_kernel(*refs, L, last, SB, has_state):
    it = iter(refs)
    proj_ref, gbias_ref, lb_ref, nga_ref, ngb_ref = (next(it) for _ in range(5))
    if has_state:
        C0_ref, n0_ref, m0_ref, S0_ref = (next(it) for _ in range(4))
    h_ref, C_out, n_out, m_out, S_out = (next(it) for _ in range(5))
    C_s, n_s, m_s, St_s, g_s, k_s, v_s = (next(it) for _ in range(7))

    c = pl.program_id(1)
    nc = pl.num_programs(1)

    @pl.when(c == 0)
    def _init():
        if has_state:
            for h in range(N_HEADS):
                C_s[h, 0:DK_A, :] = C0_ref[0, h]
                C_s[h, DK_A:D_HEAD, :] = jnp.zeros((D_HEAD - DK_A, D_HEAD), F32)
                n_s[h] = n0_ref[0, h]
                St_s[h] = S0_ref[0, h].T
            m_s[...] = m0_ref[0]
        else:
            C_s[...] = jnp.zeros(C_s.shape, F32)
            n_s[...] = jnp.zeros(n_s.shape, F32)
            m_s[...] = jnp.zeros(m_s.shape, F32)
            St_s[...] = jnp.zeros(St_s.shape, F32)

    row = lax.broadcasted_iota(jnp.int32, (L, L), 0)
    col = lax.broadcasted_iota(jnp.int32, (L, L), 1)
    causal = row >= col
    tril = causal.astype(F32)
    ones_l = jnp.ones((L, LANES), F32)
    lane = lax.broadcasted_iota(jnp.int32, (L, LANES), 1)
    trow = lax.broadcasted_iota(jnp.int32, (L, 1), 0)
    valid = trow <= last
    padded = last < L - 1

    i_t = proj_ref[:, IG0:IG0 + LANES] + gbias_ref[0:1, :]
    f_t = _log_sigmoid(proj_ref[:, FG0:FG0 + LANES] + gbias_ref[1:2, :])
    b_t = jnp.dot(tril, f_t, precision=HIGHEST, preferred_element_type=F32)
    r_t = i_t - b_t
    m_all = m_s[...]

    for h in range(N_HEADS):
        hs = slice(LANES * h, LANES * (h + 1))
        qf = proj_ref[:, QA0 + LANES * h:QA0 + LANES * (h + 1)]
        kf = proj_ref[:, KA0 + LANES * h:KA0 + LANES * (h + 1)]
        q = qf.astype(BF16)
        k = kf.astype(BF16)
        v = proj_ref[:, VA0 + LANES * h:VA0 + LANES * (h + 1)].astype(BF16)
        b_col = b_t[:, h:h + 1]
        i_col = i_t[:, h:h + 1]
        r_m = lax.dot_general(ones_l, jnp.where(lane == h, r_t, 0.0), NT_DIMS,
                              precision=HIGHEST, preferred_element_type=F32)
        m_prev = m_all[:, h:h + 1]
        cm = jnp.max(jnp.where(causal, r_m, -jnp.inf), axis=1, keepdims=True)
        m_t = b_col + jnp.maximum(m_prev, cm)
        inter = jnp.exp(b_col + m_prev - m_t)
        d = jnp.exp(jnp.where(causal, (b_col - m_t) + r_m, -1e30))
        s = lax.dot_general(q, k, NT_DIMS, preferred_element_type=F32) * d
        C_prev = C_s[h]
        num = (jnp.dot(s.astype(BF16), v, preferred_element_type=F32)
               + inter * jnp.dot(q, C_prev.astype(BF16), preferred_element_type=F32))
        den = (jnp.sum(s, axis=1, keepdims=True)
               + inter * jnp.sum(qf * n_s[h], axis=1, keepdims=True))
        hh = num / jnp.maximum(jnp.abs(den), jnp.exp(-m_t))

        m_new = m_t[last:last + 1, :]
        b_last = b_col[last:last + 1, :]
        w_arg = b_last - b_col + i_col - m_new
        if padded:
            w_arg = jnp.where(valid, w_arg, -1e30)
        kw = kf * jnp.exp(w_arg)
        decay = jnp.exp(b_last + m_prev - m_new)
        C_s[h] = decay * C_prev + lax.dot_general(kw.astype(BF16), v, TN_DIMS,
                                                  preferred_element_type=F32)
        n_s[h] = decay * n_s[h] + jnp.sum(kw, axis=0, keepdims=True)
        m_s[:, h:h + 1] = m_new

        hn = hh * lax.rsqrt(jnp.mean(hh * hh, axis=-1, keepdims=True) + NORM_EPS) * nga_ref[:, hs]
        h_ref[:, hs] = jax.nn.sigmoid(proj_ref[:, OA0 + LANES * h:OA0 + LANES * (h + 1)]) * hn

    zf = proj_ref[:, FB0:FB0 + MIX]
    lb = lb_ref[...]
    la = jnp.log(jnp.maximum(lb, LB_FLOOR))
    bb = jnp.log1p(-lb) + _log_sigmoid(zf)
    f_log = jnp.maximum(la, bb) + jnp.log1p(jnp.exp(-jnp.abs(la - bb)))
    g_s[...] = jnp.dot(tril, f_log, precision=HIGHEST, preferred_element_type=F32)
    k_s[...] = (1.0 - lb) * jax.nn.sigmoid(-zf)
    v_s[...] = proj_ref[:, IB0:IB0 + MIX]
    rsb = lax.broadcasted_iota(jnp.int32, (SB, 1), 0)

    for h in range(N_HEADS):
        hs = slice(LANES * h, LANES * (h + 1))
        gh = g_s[:, hs]
        kh = k_s[:, hs]
        vh = v_s[:, hs]
        qr = proj_ref[:, QB0 + LANES * h:QB0 + LANES * (h + 1)]
        qh = qr * jax.nn.sigmoid(qr)
        St = St_s[h]
        o = lax.dot_general((qh * jnp.exp(gh)).astype(BF16), St.astype(BF16), NT_DIMS,
                            preferred_element_type=F32)
        blocks = []
        for blk in range(L // SB):
            r0 = blk * SB
            g_blk = gh[r0:r0 + SB]
            q_blk = qh[r0:r0 + SB]
            if blk > 0:
                g_ref0 = gh[r0:r0 + 1]
                qt = (q_blk * jnp.exp(g_blk - g_ref0)).astype(BF16)
                kt = (kh[0:r0] * jnp.exp(g_ref0 - gh[0:r0])).astype(BF16)
                a = lax.dot_general(qt, kt, NT_DIMS, preferred_element_type=F32)
                acc = jnp.dot(a.astype(BF16), vh[0:r0].astype(BF16), preferred_element_type=F32)
            else:
                acc = jnp.zeros((SB, LANES), F32)

            for j in range(SB):
                g_j = g_s[r0 + j:r0 + j + 1, hs]
                k_j = k_s[r0 + j:r0 + j + 1, hs]
                v_j = v_s[r0 + j:r0 + j + 1, hs]
                e = jnp.exp(jnp.where(rsb >= j, g_blk - g_j, -1e30))
                a_col = jnp.sum(q_blk * k_j * e, axis=1, keepdims=True)
                acc = acc + a_col * v_j
            blocks.append(acc)
        o = o + (blocks[0] if len(blocks) == 1 else jnp.concatenate(blocks, axis=0))

        g_last = gh[last:last + 1]
        dec_arg = g_last - gh
        if padded:
            dec_arg = jnp.where(valid, dec_arg, -1e30)
        kdec = kh * jnp.exp(dec_arg)
        St_s[h] = jnp.exp(g_last) * St + lax.dot_general(vh.astype(BF16), kdec.astype(BF16), TN_DIMS,
                                                        preferred_element_type=F32)
        on = o * lax.rsqrt(jnp.mean(o * o, axis=-1, keepdims=True) + NORM_EPS) * ngb_ref[:, hs]
        gr = proj_ref[:, GB0 + LANES * h:GB0 + LANES * (h + 1)]
        h_ref[:, MIX + LANES * h:MIX + LANES * (h + 1)] = gr * jax.nn.sigmoid(gr) * on

    @pl.when(c == nc - 1)
    def _finish():
        for h in range(N_HEADS):
            C_out[0, h] = C_s[h, 0:DK_A, :]
            n_out[0, h] = n_s[h]
            S_out[0, h] = St_s[h].T
        m_out[0] = m_s[...]


def _scan(proj, gbias, lb, nga, ngb, state, *, batch, n_chunks, L, last):
    has_state = state is not None
    SB = min(HGRN_SUB, L)
    const2 = lambda b, c: (0, 0)
    in_specs = [
        pl.BlockSpec((L, N_PROJ), lambda b, c: (b * n_chunks + c, 0)),
        pl.BlockSpec((2, LANES), const2),
        pl.BlockSpec((1, MIX), const2),
        pl.BlockSpec((1, MIX), const2),
        pl.BlockSpec((1, MIX), const2),
    ]
    args = [proj, gbias, lb, nga, ngb]
    st4 = lambda b, c: (b, 0, 0, 0)
    st3 = lambda b, c: (b, 0, 0)
    if has_state:
        in_specs += [
            pl.BlockSpec((1, N_HEADS, DK_A, D_HEAD), st4),
            pl.BlockSpec((1, N_HEADS, 1, LANES), st4),
            pl.BlockSpec((1, 1, LANES), st3),
            pl.BlockSpec((1, N_HEADS, D_HEAD, D_HEAD), st4),
        ]
        args += list(state)
    out_specs = [
        pl.BlockSpec((L, D_MODEL), lambda b, c: (b * n_chunks + c, 0)),
        pl.BlockSpec((1, N_HEADS, DK_A, D_HEAD), st4),
        pl.BlockSpec((1, N_HEADS, 1, LANES), st4),
        pl.BlockSpec((1, 1, LANES), st3),
        pl.BlockSpec((1, N_HEADS, D_HEAD, D_HEAD), st4),
    ]
    out_shape = [
        jax.ShapeDtypeStruct((batch * n_chunks * L, D_MODEL), F32),
        jax.ShapeDtypeStruct((batch, N_HEADS, DK_A, D_HEAD), F32),
        jax.ShapeDtypeStruct((batch, N_HEADS, 1, LANES), F32),
        jax.ShapeDtypeStruct((batch, 1, LANES), F32),
        jax.ShapeDtypeStruct((batch, N_HEADS, D_HEAD, D_HEAD), F32),
    ]
    scratch = [
        pltpu.VMEM((N_HEADS, D_HEAD, D_HEAD), F32),
        pltpu.VMEM((N_HEADS, 1, LANES), F32),
        pltpu.VMEM((1, LANES), F32),
        pltpu.VMEM((N_HEADS, D_HEAD, D_HEAD), F32),
        pltpu.VMEM((L, MIX), F32),
        pltpu.VMEM((L, MIX), F32),
        pltpu.VMEM((L, MIX), F32),
    ]
    kern = functools.partial(_scan_kernel, L=L, last=last, SB=SB, has_state=has_state)
    return pl.pallas_call(
        kern, grid=(batch, n_chunks), in_specs=in_specs, out_specs=out_specs, out_shape=out_shape,
        scratch_shapes=scratch, compiler_params=_params("parallel", "arbitrary"),
        name="scan_state" if has_state else "scan_prompt")(*args)


def _post_mixer_kernel(x_ref, h_ref, p_ref, wout_ref, g1_ref, b1_ref, wpg_ref, wpp_ref, wr_ref, br_ref,
                       x1_ref, resid_ref, logit_ref):
    mix = jnp.dot(h_ref[...].astype(BF16), wout_ref[...], preferred_element_type=F32)
    x1 = _layernorm_rows(DEEPNORM_ALPHA * x_ref[...] + mix, g1_ref[...], b1_ref[...])
    x1b = x1.astype(BF16)
    gate = jax.nn.sigmoid(jnp.dot(x1b, wpg_ref[...], preferred_element_type=F32))
    pp = jnp.dot(p_ref[...].astype(BF16), wpp_ref[...], preferred_element_type=F32)
    resid_ref[...] = DEEPNORM_ALPHA * x1 + gate * pp
    x1_ref[...] = x1
    logit_ref[...] = jnp.dot(x1, wr_ref[...], precision=HIGHEST,
                             preferred_element_type=F32) + br_ref[...]


def _post_mixer(x, h, p, wout, g1, b1, wpg, wpp, wr, br):
    n, d = x.shape
    row = lambda w: pl.BlockSpec((ROW_TILE, w), lambda i: (i, 0))
    full = lambda a: pl.BlockSpec(a.shape, lambda i: (0, 0))
    g1 = g1.reshape(1, d)
    b1 = b1.reshape(1, d)
    return pl.pallas_call(
        _post_mixer_kernel, grid=(n // ROW_TILE,),
        in_specs=[row(d), row(d), row(D_PLE), full(wout), full(g1), full(b1), full(wpg), full(wpp),
                  full(wr), full(br)],
        out_specs=[row(d), row(d), row(LANES)],
        out_shape=[jax.ShapeDtypeStruct((n, d), F32), jax.ShapeDtypeStruct((n, d), F32),
                   jax.ShapeDtypeStruct((n, LANES), F32)],
        compiler_params=_params("parallel"), name="post_mixer")(x, h, p, wout, g1, b1, wpg, wpp, wr, br)


def _route_kernel(l_ref, idx_ref, rank_ref, gate_ref, cnt_ref, carry_s):
    i = pl.program_id(0)

    @pl.when(i == 0)
    def _init():
        carry_s[...] = jnp.zeros(carry_s.shape, F32)

    tm = l_ref.shape[0]
    lane = lax.broadcasted_iota(jnp.int32, (tm, LANES), 1)
    lane_f = lane.astype(F32)
    l = jnp.where(lane < N_EXPERTS, l_ref[...], -jnp.inf)
    vals, idxs, onehots = [], [], []
    for _ in range(TOP_K):
        mx = jnp.max(l, axis=1, keepdims=True)
        ix = jnp.min(jnp.where(l == mx, lane_f, float(LANES)), axis=1, keepdims=True)
        sel = lane_f == ix
        vals.append(mx)
        idxs.append(ix)
        onehots.append(sel.astype(F32))
        l = jnp.where(sel, -jnp.inf, l)
    w = [jnp.exp(val - vals[0]) for val in vals]
    tot = w[0] + w[1] + w[2] + w[3]
    oh_all = onehots[0] + onehots[1] + onehots[2] + onehots[3]
    row = lax.broadcasted_iota(jnp.int32, (tm, tm), 0)
    col = lax.broadcasted_iota(jnp.int32, (tm, tm), 1)
    earlier = (row > col).astype(BF16)
    prefix = jnp.dot(earlier, oh_all.astype(BF16), preferred_element_type=F32) + carry_s[...]
    idx_o = jnp.zeros((tm, LANES), F32)
    rank_o = jnp.zeros((tm, LANES), F32)
    gate_o = jnp.zeros((tm, LANES), F32)
    for k in range(TOP_K):
        rank_k = jnp.sum(onehots[k] * prefix, axis=1, keepdims=True)
        idx_o = jnp.where(lane == k, idxs[k], idx_o)
        rank_o = jnp.where(lane == k, rank_k, rank_o)
        gate_o = jnp.where(lane == k, w[k] / tot, gate_o)
    idx_ref[...] = idx_o.astype(jnp.int32)
    rank_ref[...] = rank_o.astype(jnp.int32)
    gate_ref[...] = gate_o
    carry_s[...] = carry_s[...] + jnp.sum(oh_all, axis=0, keepdims=True)
    cnt_ref[...] = carry_s[...].astype(jnp.int32)


def _route(logits):
    n = logits.shape[0]
    row = pl.BlockSpec((ROW_TILE, LANES), lambda i: (i, 0))
    return pl.pallas_call(
        _route_kernel, grid=(n // ROW_TILE,), in_specs=[row],
        out_specs=[row, row, row, pl.BlockSpec((1, LANES), lambda i: (0, 0))],
        out_shape=[jax.ShapeDtypeStruct((n, LANES), jnp.int32), jax.ShapeDtypeStruct((n, LANES), jnp.int32),
                   jax.ShapeDtypeStruct((n, LANES), F32), jax.ShapeDtypeStruct((1, LANES), jnp.int32)],
        scratch_shapes=[pltpu.VMEM((1, LANES), F32)],
        compiler_params=_params("arbitrary"), name="route")(logits)


def _expert_kernel(blk_ref, e_ref, lo_ref, hi_ref, nv_ref, x_ref, wg_ref, bg_ref, wu_ref, bu_ref, wd_ref,
                   bd_ref, o_ref, wg_s, wu_s, wd_s):
    v = pl.program_id(0)
    prev = jnp.maximum(v - 1, 0)
    valid = v < nv_ref[0]
    new_expert = jnp.logical_or(v == 0, e_ref[v] != e_ref[prev])
    new_block = jnp.logical_or(v == 0, blk_ref[v] != blk_ref[prev])

    @pl.when(jnp.logical_and(valid, new_expert))
    def _cast_weights():
        wg_s[...] = wg_ref[...].astype(BF16)
        wu_s[...] = wu_ref[...].astype(BF16)
        wd_s[...] = wd_ref[...].astype(BF16)

    @pl.when(valid)
    def _compute():
        xb = x_ref[...].astype(BF16)
        gate = jnp.dot(xb, wg_s[...], preferred_element_type=F32) + bg_ref[0]
        up = jnp.dot(xb, wu_s[...], preferred_element_type=F32) + bu_ref[0]
        gate = jnp.minimum(gate, SWIGLU_LIMIT)
        up = jnp.clip(up, -SWIGLU_LIMIT, SWIGLU_LIMIT)
        glu = gate * jax.nn.sigmoid(gate * SWIGLU_ALPHA)
        act = ((up + 1.0) * glu).astype(BF16)
        y = jnp.dot(act, wd_s[...], preferred_element_type=F32) + bd_ref[0]
        rows = blk_ref[v] * EXPERT_ROWS + lax.broadcasted_iota(jnp.int32, (EXPERT_ROWS, 1), 0)
        mine = jnp.logical_and(rows >= lo_ref[v], rows < hi_ref[v])

        @pl.when(new_block)
        def _first():
            o_ref[...] = jnp.where(mine, y, 0.0)

        @pl.when(jnp.logical_not(new_block))
        def _again():
            o_ref[...] = jnp.where(mine, y, o_ref[...])


def _experts(layer, blk_v, e_v, lo_v, hi_v, n_vis, xin, wg, bg, wu, bu, wd, bd):
    n_rows = xin.shape[0]
    d = wg.shape[-1]
    n_visits = blk_v.shape[0]
    xspec = pl.BlockSpec((EXPERT_ROWS, d), lambda v, blk, e, lo, hi, nv: (blk[v], 0))
    ospec = pl.BlockSpec((EXPERT_ROWS, d), lambda v, blk, e, lo, hi, nv: (blk[v], 0))
    wspec = pl.BlockSpec((None, None, d, d), lambda v, blk, e, lo, hi, nv: (layer, e[v], 0, 0))
    bspec = pl.BlockSpec((None, 1, 1, d), lambda v, blk, e, lo, hi, nv: (layer, e[v], 0, 0))
    grid_spec = pltpu.PrefetchScalarGridSpec(
        num_scalar_prefetch=5, grid=(n_visits,),
        in_specs=[xspec, wspec, bspec, wspec, bspec, wspec, bspec], out_specs=ospec,
        scratch_shapes=[pltpu.VMEM((d, d), BF16)] * 3)
    return pl.pallas_call(
        _expert_kernel, grid_spec=grid_spec, out_shape=jax.ShapeDtypeStruct((n_rows, d), F32),
        compiler_params=_params("arbitrary"), name="experts")(
            blk_v, e_v, lo_v, hi_v, n_vis, xin, wg, bg, wu, bu, wd, bd)


def _moe(layer, x1, logits, wg, bg, wu, bu, wd, bd):
    n_tok = x1.shape[0]
    n_assign = n_tok * TOP_K
    n_blocks = n_assign // EXPERT_ROWS
    idx_o, rank_o, gate_o, cnt = _route(logits)
    counts = cnt[0, :N_EXPERTS]
    ends = jnp.cumsum(counts)
    offs = ends - counts
    experts = jnp.arange(N_EXPERTS, dtype=jnp.int32)
    top_i = idx_o[:, :TOP_K]
    off_tok = jnp.sum(jnp.where(top_i[:, :, None] == experts, offs, 0), axis=-1)
    pos = (off_tok + rank_o[:, :TOP_K]).reshape(n_assign)
    _, order = lax.sort((pos, jnp.arange(n_assign, dtype=jnp.int32)), num_keys=1)
    xin = x1[order // TOP_K]

    first_blk = offs // EXPERT_ROWS
    last_blk = (ends - 1) // EXPERT_ROWS
    nvis = jnp.where(counts > 0, last_blk - first_blk + 1, 0)
    vend = jnp.cumsum(nvis)
    vstart = vend - nvis
    n_visits = n_blocks + N_EXPERTS - 1
    v = jnp.arange(n_visits, dtype=jnp.int32)
    e_v = jnp.sum((vend[None, :] <= v[:, None]).astype(jnp.int32), axis=1)
    e_last = jnp.max(jnp.where(counts > 0, experts, 0))
    e_v = jnp.minimum(e_v, e_last)
    pick = lambda tab: jnp.sum(jnp.where(e_v[:, None] == experts, tab, 0), axis=-1)
    blk_v = jnp.minimum(pick(first_blk) + v - pick(vstart), n_blocks - 1)
    y_rows = _experts(layer, blk_v.astype(jnp.int32), e_v.astype(jnp.int32), pick(offs).astype(jnp.int32),
                      pick(ends).astype(jnp.int32), vend[-1:].astype(jnp.int32), xin, wg, bg, wu, bu, wd, bd)
    return y_rows[pos].reshape(n_tok, TOP_K * y_rows.shape[1]), gate_o


def _pack_w_in(w_in):
    qa, ka, va, oa, ia, fa, qb, fb, ib, gb = jnp.split(w_in, SPLIT_POINTS, axis=-1)

    def pad_heads(w):
        w = w.reshape(w.shape[:-1] + (N_HEADS, DK_A))
        w = jnp.pad(w, ((0, 0),) * (w.ndim - 1) + ((0, D_HEAD - DK_A),))
        return w.reshape(w.shape[:-2] + (N_HEADS * D_HEAD,))

    def pad_gate(w):
        return jnp.pad(w, ((0, 0),) * (w.ndim - 1) + ((0, LANES - N_HEADS),))

    cols = [pad_heads(qa), pad_heads(ka * (DK_A ** -0.5)), va, oa, qb, fb, ib, gb, pad_gate(ia), pad_gate(fa)]
    return jnp.concatenate(cols, axis=-1).astype(BF16)


def kernel(x_prompt, x_sample, state_mlstm_C, state_mlstm_n, state_mlstm_m, state_hgrn_S, p_prompt, p_sample,
           ln_in_g, ln_in_b, w_in, mlstm_ig_bias, mlstm_fg_bias, mlstm_norm_g, hgrn_lb_logits, hgrn_norm_g,
           w_out, ln1_g, ln1_b, w_router, b_router, w_gate, b_gate, w_up, b_up, w_down, b_down,
           w_ple_gate, w_ple_proj, ln2_g, ln2_b):
    bp, tp, d = x_prompt.shape
    bs, ts, _ = x_sample.shape
    n_p = bp * tp
    n_s = bs * ts
    ts_pad = SUBLANES
    assert tp % CHUNK == 0 and ts <= ts_pad and ts % CHUNK != 0

    lb_soft = jax.nn.softmax(hgrn_lb_logits.astype(F32), axis=0)
    lower_bounds = jnp.cumsum(lb_soft, axis=0) - lb_soft[0]

    w_in_p = _pack_w_in(w_in)
    gbias = jnp.stack([jnp.pad(mlstm_ig_bias, ((0, 0), (0, LANES - N_HEADS))),
                       jnp.pad(mlstm_fg_bias, ((0, 0), (0, LANES - N_HEADS)))], axis=1)
    w_out_b = w_out.astype(BF16)
    w_pg_b = w_ple_gate.astype(BF16)
    w_pp_b = w_ple_proj.astype(BF16)
    w_r = jnp.pad(w_router, ((0, 0), (0, 0), (0, LANES - N_EXPERTS)))
    b_r = jnp.pad(b_router, ((0, 0), (0, LANES - N_EXPERTS))).reshape(DEPTH, 1, LANES)
    n0_pad = jnp.pad(state_mlstm_n, ((0, 0), (0, 0), (0, 0), (0, D_HEAD - DK_A))).reshape(
        DEPTH, bs, N_HEADS, 1, D_HEAD)
    m0_pad = jnp.pad(state_mlstm_m, ((0, 0), (0, 0), (0, LANES - N_HEADS))).reshape(DEPTH, bs, 1, LANES)

    x = jnp.concatenate([x_prompt.reshape(n_p, d), x_sample.reshape(n_s, d)], axis=0)
    p_all = jnp.concatenate([p_prompt.reshape(DEPTH, n_p, D_PLE), p_sample.reshape(DEPTH, n_s, D_PLE)], axis=1)
    x = _ln(x, ln_in_g, ln_in_b)

    outs = {k: [] for k in ("Cp", "np", "mp", "Sp", "Cs", "ns", "ms", "Ss")}
    for l in range(DEPTH):
        proj = _inproj(x, w_in_p[l])
        lb = lower_bounds[l].reshape(1, MIX)
        nga = mlstm_norm_g[l].reshape(1, MIX)
        ngb = hgrn_norm_g[l].reshape(1, MIX)
        h_p, C_p, n_pr, m_pr, S_p = _scan(proj, gbias[l], lb, nga, ngb, None,
                                          batch=bp, n_chunks=tp // CHUNK, L=CHUNK, last=CHUNK - 1)
        proj_s = jnp.pad(proj[n_p:].reshape(bs, ts, N_PROJ), ((0, 0), (0, ts_pad - ts), (0, 0)))
        h_s, C_s, n_sr, m_sr, S_s = _scan(proj_s.reshape(bs * ts_pad, N_PROJ), gbias[l], lb, nga, ngb,
                                          (state_mlstm_C[l], n0_pad[l], m0_pad[l], state_hgrn_S[l]),
                                          batch=bs, n_chunks=1, L=ts_pad, last=ts - 1)
        h_s = h_s.reshape(bs, ts_pad, d)[:, :ts].reshape(n_s, d)
        h_all = jnp.concatenate([h_p, h_s], axis=0)
        x1, resid, logits = _post_mixer(x, h_all, p_all[l], w_out_b[l], ln1_g[l], ln1_b[l],
                                         w_pg_b[l], w_pp_b[l], w_r[l], b_r[l])
        y4, gates = _moe(l, x1, logits, w_gate, b_gate.reshape(DEPTH, N_EXPERTS, 1, d),
                         w_up, b_up.reshape(DEPTH, N_EXPERTS, 1, d),
                         w_down, b_down.reshape(DEPTH, N_EXPERTS, 1, d))
        x = _combine_ln(resid, y4, gates, ln2_g[l], ln2_b[l])
        outs["Cp"].append(C_p)
        outs["np"].append(n_pr[:, :, 0, :DK_A])
        outs["mp"].append(m_pr[:, 0, :N_HEADS])
        outs["Sp"].append(S_p)
        outs["Cs"].append(C_s)
        outs["ns"].append(n_sr[:, :, 0, :DK_A])
        outs["ms"].append(m_sr[:, 0, :N_HEADS])
        outs["Ss"].append(S_s)

    y_prompt = x[:n_p].reshape(bp, tp, d)
    y_sample = x[n_p:].reshape(bs, ts, d)
    st = {k: jnp.stack(v) for k, v in outs.items()}
    return (y_prompt, y_sample, st["Cp"], st["np"], st["mp"], st["Sp"],
            st["Cs"], st["ns"], st["ms"], st["Ss"])
```

```python
import functools

import jax
import jax.numpy as jnp
import numpy as np
from jax import lax
from jax.experimental import pallas as pl
from jax.experimental.pallas import tpu as pltpu

F32 = jnp.float32
BF16 = jnp.bfloat16
HIGHEST = lax.Precision.HIGHEST

D_MODEL = 1024
DEPTH = 4
D_PLE = 256
N_HEADS = 4
DK_A = 64
D_HEAD = 128
MIX = 512
N_EXPERTS = 32
TOP_K = 4
SWIGLU_LIMIT = 7.0
SWIGLU_ALPHA = 1.702
CHUNK = 64
LN_EPS = 1e-5
NORM_EPS = 1e-6
LB_FLOOR = 1e-20
DEEPNORM_ALPHA = (2 * DEPTH) ** 0.25
SPLIT_SIZES = (256, 256, 512, 512, 4, 4, 512, 512, 512, 512)
SPLIT_POINTS = tuple(int(s) for s in np.cumsum(SPLIT_SIZES)[:-1])

LANES = 128
SUBLANES = 8
VMEM_LIMIT = 56 * 1024 * 1024

QA0, KA0, VA0, OA0 = 0, 512, 1024, 1536
QB0, FB0, IB0, GB0 = 2048, 2560, 3072, 3584
IG0, FG0 = 4096, 4224
N_PROJ = 4352

ROW_TILE = 256
EXPERT_ROWS = 256

NT_DIMS = (((1,), (1,)), ((), ()))
TN_DIMS = (((0,), (0,)), ((), ()))


def _params(*sem):
    return pltpu.CompilerParams(dimension_semantics=sem, vmem_limit_bytes=VMEM_LIMIT)


def _log_sigmoid(x):
    return jnp.minimum(x, 0.0) - jnp.log(1.0 + jnp.exp(-jnp.abs(x)))


def _split3(x):
    x1 = x.astype(BF16)
    r1 = x - x1.astype(F32)
    x2 = r1.astype(BF16)
    x3 = (r1 - x2.astype(F32)).astype(BF16)
    return x1, x2, x3


def _dot01(sel, x):
    return sum(jnp.dot(sel, t, preferred_element_type=F32) for t in _split3(x))


def _layernorm_rows(x, g, b):
    mu = jnp.mean(x, axis=-1, keepdims=True)
    xc = x - mu
    var = jnp.mean(xc * xc, axis=-1, keepdims=True)
    return xc * lax.rsqrt(var + LN_EPS) * g + b


def _ln_kernel(x_ref, g_ref, b_ref, o_ref):
    o_ref[...] = _layernorm_rows(x_ref[...], g_ref[...], b_ref[...])


def _ln(x, g, b):
    n, d = x.shape
    row = pl.BlockSpec((ROW_TILE, d), lambda i: (i, 0))
    vec = pl.BlockSpec((1, d), lambda i: (0, 0))
    return pl.pallas_call(
        _ln_kernel, grid=(n // ROW_TILE,), in_specs=[row, vec, vec], out_specs=row,
        out_shape=jax.ShapeDtypeStruct((n, d), F32), compiler_params=_params("parallel"),
        name="ln_in")(x, g.reshape(1, d), b.reshape(1, d))


def _combine_ln_kernel(resid_ref, y0_ref, y1_ref, y2_ref, y3_ref, gate_ref, g_ref, beta_ref, o_ref):
    acc = resid_ref[...]
    for k, y_ref in enumerate((y0_ref, y1_ref, y2_ref, y3_ref)):
        acc = acc + gate_ref[:, k:k + 1] * y_ref[...]
    o_ref[...] = _layernorm_rows(acc, g_ref[...], beta_ref[...])


def _combine_ln(resid, y_k, gates, g, beta):
    n, d = resid.shape
    nt = n // ROW_TILE
    row = lambda w: pl.BlockSpec((ROW_TILE, w), lambda i: (i, 0))
    yk = lambda k: pl.BlockSpec((ROW_TILE, d), lambda i: (k * nt + i, 0))
    vec = pl.BlockSpec((1, d), lambda i: (0, 0))
    return pl.pallas_call(
        _combine_ln_kernel, grid=(nt,),
        in_specs=[row(d)] + [yk(k) for k in range(TOP_K)] + [row(LANES), vec, vec],
        out_specs=row(d), out_shape=jax.ShapeDtypeStruct((n, d), F32), compiler_params=_params("parallel"),
        name="combine_ln2")(resid, y_k, y_k, y_k, y_k, gates, g.reshape(1, d), beta.reshape(1, d))


def _inproj_kernel(x_ref, w_ref, o_ref):
    o_ref[...] = jnp.dot(x_ref[...].astype(BF16), w_ref[...], preferred_element_type=F32)


def _inproj(x, w):
    n, d = x.shape
    return pl.pallas_call(
        _inproj_kernel, grid=(n // ROW_TILE,),
        in_specs=[pl.BlockSpec((ROW_TILE, d), lambda i: (i, 0)),
                  pl.BlockSpec((d, N_PROJ), lambda i: (0, 0))],
        out_specs=pl.BlockSpec((ROW_TILE, N_PROJ), lambda i: (i, 0)),
        out_shape=jax.ShapeDtypeStruct((n, N_PROJ), F32), compiler_params=_params("parallel"),
        name="in_proj")(x, w)


def _scan_kernel(*refs, L, last, has_state, has_stacked):
    it = iter(refs)
    proj_ref, gbias_ref, lb_ref, nga_ref, ngb_ref = (next(it) for _ in range(5))
    if has_state:
        C0_ref, n0_ref, m0_ref, S0_ref = (next(it) for _ in range(4))
    if has_stacked:
        for _ in range(4):
            next(it)
    h_ref, C_out, n_out, m_out, S_out = (next(it) for _ in range(5))
    C_s, n_s, m_s, St_s = (next(it) for _ in range(4))

    c = pl.program_id(1)
    nc = pl.num_programs(1)
    H = range(N_HEADS)

    @pl.when(c == 0)
    def _init():
        if has_state:
            for h in H:
                C_s[h, 0:DK_A, :] = C0_ref[0, h]
                C_s[h, DK_A:D_HEAD, :] = jnp.zeros((D_HEAD - DK_A, D_HEAD), F32)
                n_s[h] = n0_ref[0, h]
                St_s[h] = S0_ref[0, h].T
            m_s[...] = m0_ref[0]
        else:
            C_s[...] = jnp.zeros(C_s.shape, F32)
            n_s[...] = jnp.zeros(n_s.shape, F32)
            m_s[...] = jnp.zeros(m_s.shape, F32)
            St_s[...] = jnp.zeros(St_s.shape, F32)

    row = lax.broadcasted_iota(jnp.int32, (L, L), 0)
    col = lax.broadcasted_iota(jnp.int32, (L, L), 1)
    causal = row >= col
    tril = causal.astype(BF16)
    ones_l = jnp.ones((L, LANES), BF16)
    lane = lax.broadcasted_iota(jnp.int32, (L, LANES), 1)
    trow = lax.broadcasted_iota(jnp.int32, (L, 1), 0)
    valid = trow <= last
    padded = last < L - 1
    tile = lambda base, h: slice(base + LANES * h, base + LANES * (h + 1))

    i_t = proj_ref[:, IG0:IG0 + LANES] + gbias_ref[0:1, :]
    f_t = _log_sigmoid(proj_ref[:, FG0:FG0 + LANES] + gbias_ref[1:2, :])
    zf = proj_ref[:, FB0:FB0 + MIX]
    lb = lb_ref[...]
    la = jnp.log(jnp.maximum(lb, LB_FLOOR))
    bb = jnp.log1p(-lb) + _log_sigmoid(zf)
    f_log = jnp.maximum(la, bb) + jnp.log(1.0 + jnp.exp(-jnp.abs(la - bb)))
    b_t = _dot01(tril, f_t)
    g = _dot01(tril, f_log)
    m_all = m_s[...]
    C_prev = [C_s[h] for h in H]
    n_prev = [n_s[h] for h in H]
    St = [St_s[h] for h in H]
    qf = [proj_ref[:, tile(QA0, h)] for h in H]
    kf = [proj_ref[:, tile(KA0, h)] for h in H]
    q = [x.astype(BF16) for x in qf]
    k = [x.astype(BF16) for x in kf]
    v = [proj_ref[:, tile(VA0, h)].astype(BF16) for h in H]
    kb = (1.0 - lb) * jax.nn.sigmoid(-zf)
    qr = proj_ref[:, QB0:QB0 + MIX]
    qb = qr * jax.nn.sigmoid(qr)
    vb = [proj_ref[:, tile(IB0, h)].astype(BF16) for h in H]
    qb16 = qb.astype(BF16)
    kb16 = kb.astype(BF16)
    r_t = i_t - b_t
    pad_rows = [jnp.zeros((LANES - L, LANES), BF16)] if L < LANES else []
    zero16 = jnp.zeros((L, LANES), BF16)
    r_all = sum(
        lax.dot_general(ones_l, jnp.concatenate(
            [piece for h in H for piece in [jnp.where(lane == h, term, zero16)] + pad_rows], axis=0),
            NT_DIMS, preferred_element_type=F32)
        for term in _split3(r_t))
    r_m = [r_all[:, LANES * h:LANES * h + L] for h in H]
    qk = [lax.dot_general(q[h], k[h], NT_DIMS, preferred_element_type=F32) for h in H]
    qC = [jnp.dot(q[h], C_prev[h].astype(BF16), preferred_element_type=F32) for h in H]
    levels = []
    bs = 1
    while bs < L:
        levels.append(bs)
        bs *= 2
    small = [bs for bs in levels if bs < SUBLANES]
    g_anchor = {}
    if small:
        sel = jnp.concatenate(
            [(col == (row // (2 * bs)) * (2 * bs) + bs).astype(BF16) for bs in small], axis=0)
        picked = _dot01(sel, g)
        for i, bs in enumerate(small):
            g_anchor[bs] = picked[i * L:(i + 1) * L]
    for bs in levels:
        if bs >= SUBLANES:
            g_anchor[bs] = jnp.concatenate(
                [jnp.broadcast_to(g[p0 + bs:p0 + bs + 1], (2 * bs, MIX)) for p0 in range(0, L, 2 * bs)], axis=0)
    qg = (qb * jnp.exp(g)).astype(BF16)
    o_inter = [lax.dot_general(qg[:, tile(0, h)], St[h].astype(BF16), NT_DIMS, preferred_element_type=F32)
               for h in H]
    a = [jnp.where(row == col, lax.dot_general(qb16[:, tile(0, h)], kb16[:, tile(0, h)], NT_DIMS,
                                                preferred_element_type=F32), 0.0) for h in H]
    qn = [jnp.sum(qf[h] * n_prev[h], axis=1, keepdims=True) for h in H]
    b_col = [b_t[:, h:h + 1] for h in H]
    i_col = [i_t[:, h:h + 1] for h in H]
    m_prev = [m_all[:, h:h + 1] for h in H]
    cm = [jnp.max(jnp.where(causal, r_m[h], -jnp.inf), axis=1, keepdims=True) for h in H]
    m_t = [b_col[h] + jnp.maximum(m_prev[h], cm[h]) for h in H]
    inter = [jnp.exp(b_col[h] + m_prev[h] - m_t[h]) for h in H]
    d = [jnp.exp(jnp.where(causal, (b_col[h] - m_t[h]) + r_m[h], -1e30)) for h in H]
    s = [qk[h] * d[h] for h in H]
    log2e = 1.4426950408889634
    e = [jnp.exp2((g - g_anchor[bs]) * jnp.where((trow // bs) % 2 == 1, log2e, -log2e)) for bs in levels]
    qe = [(qb * x).astype(BF16) for x in e]
    ke = [(kb * x).astype(BF16) for x in e]
    sv = [jnp.dot(s[h].astype(BF16), v[h], preferred_element_type=F32) for h in H]
    sc = [[lax.dot_general(qe[i][:, tile(0, h)], ke[i][:, tile(0, h)], NT_DIMS, preferred_element_type=F32)
           for h in H] for i in range(len(levels))]
    den = [jnp.sum(s[h], axis=1, keepdims=True) + inter[h] * qn[h] for h in H]
    m_new = [m_t[h][last:last + 1, :] for h in H]
    b_last = [b_col[h][last:last + 1, :] for h in H]
    w_arg = [b_last[h] - b_col[h] + i_col[h] - m_new[h] for h in H]
    if padded:
        w_arg = [jnp.where(valid, x, -1e30) for x in w_arg]
    kw = [kf[h] * jnp.exp(w_arg[h]) for h in H]
    decay = [jnp.exp(b_last[h] + m_prev[h] - m_new[h]) for h in H]
    kv = [lax.dot_general(kw[h].astype(BF16), v[h], TN_DIMS, preferred_element_type=F32) for h in H]
    for i, bs in enumerate(levels):
        pair = jnp.logical_and(row // (2 * bs) == col // (2 * bs),
                               jnp.logical_and((row // bs) % 2 == 1, (col // bs) % 2 == 0))
        for h in H:
            a[h] = jnp.where(pair, sc[i][h], a[h])
    o = [o_inter[h] + jnp.dot(a[h].astype(BF16), vb[h], preferred_element_type=F32) for h in H]
    g_last = g[last:last + 1]
    dec_arg = g_last - g
    if padded:
        dec_arg = jnp.where(valid, dec_arg, -1e30)
    kdec = (kb * jnp.exp(dec_arg)).astype(BF16)
    eg_last = jnp.exp(g_last)
    vk = [lax.dot_general(vb[h], kdec[:, tile(0, h)], TN_DIMS, preferred_element_type=F32) for h in H]
    hh = [(sv[h] + inter[h] * qC[h]) / jnp.maximum(jnp.abs(den[h]), jnp.exp(-m_t[h])) for h in H]
    for h in H:
        hn = hh[h] * lax.rsqrt(jnp.mean(hh[h] * hh[h], axis=-1, keepdims=True) + NORM_EPS) * nga_ref[:, tile(0, h)]
        h_ref[:, tile(0, h)] = jax.nn.sigmoid(proj_ref[:, tile(OA0, h)]) * hn
    for h in H:
        on = o[h] * lax.rsqrt(jnp.mean(o[h] * o[h], axis=-1, keepdims=True) + NORM_EPS) * ngb_ref[:, tile(0, h)]
        gr = proj_ref[:, tile(GB0, h)]
        h_ref[:, tile(MIX, h)] = gr * jax.nn.sigmoid(gr) * on
    for h in H:
        C_s[h] = decay[h] * C_prev[h] + kv[h]
        n_s[h] = decay[h] * n_prev[h] + jnp.sum(kw[h], axis=0, keepdims=True)
        St_s[h] = eg_last[:, tile(0, h)] * St[h] + vk[h]
    m_upd = m_all
    lane1 = lax.broadcasted_iota(jnp.int32, (1, LANES), 1)
    for h in H:
        m_upd = jnp.where(lane1 == h, m_new[h], m_upd)
    m_s[...] = m_upd

    @pl.when(c == nc - 1)
    def _finish():
        for h in H:
            C_out[0, h] = C_s[h, 0:DK_A, :]
            n_out[0, h] = n_s[h]
            S_out[0, h] = St_s[h].T
        m_out[0] = m_s[...]


def _scan(proj, gbias, lb, nga, ngb, state, stacked, *, layer, batch, n_chunks, L, last):
    has_state = state is not None
    const2 = lambda b, c: (0, 0)
    in_specs = [
        pl.BlockSpec((L, N_PROJ), lambda b, c: (b * n_chunks + c, 0)),
        pl.BlockSpec((2, LANES), const2),
        pl.BlockSpec((1, MIX), const2),
        pl.BlockSpec((1, MIX), const2),
        pl.BlockSpec((1, MIX), const2),
    ]
    args = [proj, gbias, lb, nga, ngb]
    st4 = lambda b, c: (b, 0, 0, 0)
    st3 = lambda b, c: (b, 0, 0)
    if has_state:
        in_specs += [
            pl.BlockSpec((1, N_HEADS, DK_A, D_HEAD), st4),
            pl.BlockSpec((1, N_HEADS, 1, LANES), st4),
            pl.BlockSpec((1, 1, LANES), st3),
            pl.BlockSpec((1, N_HEADS, D_HEAD, D_HEAD), st4),
        ]
        args += list(state)
    aliases = {}
    if stacked is not None:
        aliases = {len(args) + i: 1 + i for i in range(4)}
        in_specs += [pl.BlockSpec(memory_space=pl.ANY)] * 4
        args += list(stacked)
    lst4 = lambda b, c: (layer, b, 0, 0, 0)
    lst3 = lambda b, c: (layer, b, 0, 0)
    out_specs = [
        pl.BlockSpec((L, D_MODEL), lambda b, c: (b * n_chunks + c, 0)),
        pl.BlockSpec((None, 1, N_HEADS, DK_A, D_HEAD), lst4),
        pl.BlockSpec((None, 1, N_HEADS, 1, LANES), lst4),
        pl.BlockSpec((None, 1, 1, LANES), lst3),
        pl.BlockSpec((None, 1, N_HEADS, D_HEAD, D_HEAD), lst4),
    ]
    out_shape = [
        jax.ShapeDtypeStruct((batch * n_chunks * L, D_MODEL), F32),
        jax.ShapeDtypeStruct((DEPTH, batch, N_HEADS, DK_A, D_HEAD), F32),
        jax.ShapeDtypeStruct((DEPTH, batch, N_HEADS, 1, LANES), F32),
        jax.ShapeDtypeStruct((DEPTH, batch, 1, LANES), F32),
        jax.ShapeDtypeStruct((DEPTH, batch, N_HEADS, D_HEAD, D_HEAD), F32),
    ]
    scratch = [
        pltpu.VMEM((N_HEADS, D_HEAD, D_HEAD), F32),
        pltpu.VMEM((N_HEADS, 1, LANES), F32),
        pltpu.VMEM((1, LANES), F32),
        pltpu.VMEM((N_HEADS, D_HEAD, D_HEAD), F32),
    ]
    kern = functools.partial(_scan_kernel, L=L, last=last, has_state=has_state, has_stacked=stacked is not None)
    res = pl.pallas_call(
        kern, grid=(batch, n_chunks), in_specs=in_specs, out_specs=out_specs, out_shape=out_shape,
        scratch_shapes=scratch, input_output_aliases=aliases,
        compiler_params=_params("parallel", "arbitrary"),
        name="scan_state" if has_state else "scan_prompt")(*args)
    return res[0], tuple(res[1:])


def _post_mixer_kernel(x_ref, h_ref, p_ref, wout_ref, g1_ref, b1_ref, wpg_ref, wpp_ref, wr_ref, br_ref,
                       x1_ref, resid_ref, logit_ref):
    mix = jnp.dot(h_ref[...].astype(BF16), wout_ref[...], preferred_element_type=F32)
    x1 = _layernorm_rows(DEEPNORM_ALPHA * x_ref[...] + mix, g1_ref[...], b1_ref[...])
    x1b = x1.astype(BF16)
    gate = jax.nn.sigmoid(jnp.dot(x1b, wpg_ref[...], preferred_element_type=F32))
    pp = jnp.dot(p_ref[...].astype(BF16), wpp_ref[...], preferred_element_type=F32)
    resid_ref[...] = DEEPNORM_ALPHA * x1 + gate * pp
    x1_ref[...] = x1
    logit_ref[...] = jnp.dot(x1, wr_ref[...], precision=HIGHEST,
                             preferred_element_type=F32) + br_ref[...]


def _post_mixer(x, h, p, wout, g1, b1, wpg, wpp, wr, br):
    n, d = x.shape
    row = lambda w: pl.BlockSpec((ROW_TILE, w), lambda i: (i, 0))
    full = lambda a: pl.BlockSpec(a.shape, lambda i: (0, 0))
    g1 = g1.reshape(1, d)
    b1 = b1.reshape(1, d)
    return pl.pallas_call(
        _post_mixer_kernel, grid=(n // ROW_TILE,),
        in_specs=[row(d), row(d), row(D_PLE), full(wout), full(g1), full(b1), full(wpg), full(wpp),
                  full(wr), full(br)],
        out_specs=[row(d), row(d), row(LANES)],
        out_shape=[jax.ShapeDtypeStruct((n, d), F32), jax.ShapeDtypeStruct((n, d), F32),
                   jax.ShapeDtypeStruct((n, LANES), F32)],
        compiler_params=_params("parallel"), name="post_mixer")(x, h, p, wout, g1, b1, wpg, wpp, wr, br)


def _route_kernel(l_ref, idx_ref, rank_ref, gate_ref, cnt_ref, carry_s):
    i = pl.program_id(0)

    @pl.when(i == 0)
    def _init():
        carry_s[...] = jnp.zeros(carry_s.shape, F32)

    tm = l_ref.shape[0]
    lane = lax.broadcasted_iota(jnp.int32, (tm, LANES), 1)
    lane_f = lane.astype(F32)
    l = jnp.where(lane < N_EXPERTS, l_ref[...], -jnp.inf)
    vals, idxs, onehots = [], [], []
    for _ in range(TOP_K):
        mx = jnp.max(l, axis=1, keepdims=True)
        ix = jnp.min(jnp.where(l == mx, lane_f, float(LANES)), axis=1, keepdims=True)
        sel = lane_f == ix
        vals.append(mx)
        idxs.append(ix)
        onehots.append(sel.astype(F32))
        l = jnp.where(sel, -jnp.inf, l)
    w = [jnp.exp(val - vals[0]) for val in vals]
    tot = w[0] + w[1] + w[2] + w[3]
    oh_all = onehots[0] + onehots[1] + onehots[2] + onehots[3]
    row = lax.broadcasted_iota(jnp.int32, (tm, tm), 0)
    col = lax.broadcasted_iota(jnp.int32, (tm, tm), 1)
    earlier = (row > col).astype(BF16)
    prefix = jnp.dot(earlier, oh_all.astype(BF16), preferred_element_type=F32) + carry_s[...]
    idx_o = jnp.zeros((tm, LANES), F32)
    rank_o = jnp.zeros((tm, LANES), F32)
    gate_o = jnp.zeros((tm, LANES), F32)
    for k in range(TOP_K):
        rank_k = jnp.sum(onehots[k] * prefix, axis=1, keepdims=True)
        idx_o = jnp.where(lane == k, idxs[k], idx_o)
        rank_o = jnp.where(lane == k, rank_k, rank_o)
        gate_o = jnp.where(lane == k, w[k] / tot, gate_o)
    idx_ref[...] = idx_o.astype(jnp.int32)
    rank_ref[...] = rank_o.astype(jnp.int32)
    gate_ref[...] = gate_o
    carry_s[...] = carry_s[...] + jnp.sum(oh_all, axis=0, keepdims=True)
    cnt_ref[...] = carry_s[...].astype(jnp.int32)


def _route(logits):
    n = logits.shape[0]
    row = pl.BlockSpec((ROW_TILE, LANES), lambda i: (i, 0))
    return pl.pallas_call(
        _route_kernel, grid=(n // ROW_TILE,), in_specs=[row],
        out_specs=[row, row, row, pl.BlockSpec((1, LANES), lambda i: (0, 0))],
        out_shape=[jax.ShapeDtypeStruct((n, LANES), jnp.int32), jax.ShapeDtypeStruct((n, LANES), jnp.int32),
                   jax.ShapeDtypeStruct((n, LANES), F32), jax.ShapeDtypeStruct((1, LANES), jnp.int32)],
        scratch_shapes=[pltpu.VMEM((1, LANES), F32)],
        compiler_params=_params("arbitrary"), name="route")(logits)


def _expert_kernel(blk_ref, e_ref, lo_ref, hi_ref, nv_ref, x_ref, wg_ref, bg_ref, wu_ref, bu_ref, wd_ref,
                   bd_ref, o_ref, wg_s, wu_s, wd_s):
    v = pl.program_id(0)
    prev = jnp.maximum(v - 1, 0)
    valid = v < nv_ref[0]
    new_expert = jnp.logical_or(v == 0, e_ref[v] != e_ref[prev])
    new_block = jnp.logical_or(v == 0, blk_ref[v] != blk_ref[prev])

    @pl.when(jnp.logical_and(valid, new_expert))
    def _cast_weights():
        wg_s[...] = wg_ref[...].astype(BF16)
        wu_s[...] = wu_ref[...].astype(BF16)
        wd_s[...] = wd_ref[...].astype(BF16)

    @pl.when(valid)
    def _compute():
        xb = x_ref[...].astype(BF16)
        gate = jnp.dot(xb, wg_s[...], preferred_element_type=F32) + bg_ref[0]
        up = jnp.dot(xb, wu_s[...], preferred_element_type=F32) + bu_ref[0]
        gate = jnp.minimum(gate, SWIGLU_LIMIT)
        up = jnp.clip(up, -SWIGLU_LIMIT, SWIGLU_LIMIT)
        glu = gate * jax.nn.sigmoid(gate * SWIGLU_ALPHA)
        act = ((up + 1.0) * glu).astype(BF16)
        y = jnp.dot(act, wd_s[...], preferred_element_type=F32) + bd_ref[0]
        rows = blk_ref[v] * EXPERT_ROWS + lax.broadcasted_iota(jnp.int32, (EXPERT_ROWS, 1), 0)
        mine = jnp.logical_and(rows >= lo_ref[v], rows < hi_ref[v])

        @pl.when(new_block)
        def _first():
            o_ref[...] = jnp.where(mine, y, 0.0)

        @pl.when(jnp.logical_not(new_block))
        def _again():
            o_ref[...] = jnp.where(mine, y, o_ref[...])


def _experts(layer, blk_v, e_v, lo_v, hi_v, n_vis, xin, wg, bg, wu, bu, wd, bd):
    n_rows = xin.shape[0]
    d = wg.shape[-1]
    n_visits = blk_v.shape[0]
    xspec = pl.BlockSpec((EXPERT_ROWS, d), lambda v, blk, e, lo, hi, nv: (blk[v], 0))
    ospec = pl.BlockSpec((EXPERT_ROWS, d), lambda v, blk, e, lo, hi, nv: (blk[v], 0))
    wspec = pl.BlockSpec((None, None, d, d), lambda v, blk, e, lo, hi, nv: (layer, e[v], 0, 0))
    bspec = pl.BlockSpec((None, 1, 1, d), lambda v, blk, e, lo, hi, nv: (layer, e[v], 0, 0))
    grid_spec = pltpu.PrefetchScalarGridSpec(
        num_scalar_prefetch=5, grid=(n_visits,),
        in_specs=[xspec, wspec, bspec, wspec, bspec, wspec, bspec], out_specs=ospec,
        scratch_shapes=[pltpu.VMEM((d, d), BF16)] * 3)
    return pl.pallas_call(
        _expert_kernel, grid_spec=grid_spec, out_shape=jax.ShapeDtypeStruct((n_rows, d), F32),
        compiler_params=_params("arbitrary"), name="experts")(
            blk_v, e_v, lo_v, hi_v, n_vis, xin, wg, bg, wu, bu, wd, bd)


def _moe(layer, x1, logits, wg, bg, wu, bu, wd, bd):
    n_tok = x1.shape[0]
    n_assign = n_tok * TOP_K
    n_blocks = n_assign // EXPERT_ROWS
    idx_o, rank_o, gate_o, cnt = _route(logits)
    counts = cnt[0, :N_EXPERTS]
    ends = jnp.cumsum(counts)
    offs = ends - counts
    experts = jnp.arange(N_EXPERTS, dtype=jnp.int32)
    top_i = idx_o[:, :TOP_K]
    off_tok = jnp.sum(jnp.where(top_i[:, :, None] == experts, offs, 0), axis=-1)
    pos = (off_tok + rank_o[:, :TOP_K]).reshape(n_assign)
    _, order = lax.sort((pos, jnp.arange(n_assign, dtype=jnp.int32)), num_keys=1)
    xin = x1[order // TOP_K]

    first_blk = offs // EXPERT_ROWS
    last_blk = (ends - 1) // EXPERT_ROWS
    nvis = jnp.where(counts > 0, last_blk - first_blk + 1, 0)
    vend = jnp.cumsum(nvis)
    vstart = vend - nvis
    n_visits = n_blocks + N_EXPERTS - 1
    v = jnp.arange(n_visits, dtype=jnp.int32)
    e_v = jnp.sum((vend[None, :] <= v[:, None]).astype(jnp.int32), axis=1)
    e_last = jnp.max(jnp.where(counts > 0, experts, 0))
    e_v = jnp.minimum(e_v, e_last)
    pick = lambda tab: jnp.sum(jnp.where(e_v[:, None] == experts, tab, 0), axis=-1)
    blk_v = jnp.minimum(pick(first_blk) + v - pick(vstart), n_blocks - 1)
    y_rows = _experts(layer, blk_v.astype(jnp.int32), e_v.astype(jnp.int32), pick(offs).astype(jnp.int32),
                      pick(ends).astype(jnp.int32), vend[-1:].astype(jnp.int32), xin, wg, bg, wu, bu, wd, bd)
    return y_rows[pos.reshape(n_tok, TOP_K).T.reshape(n_assign)], gate_o


def _pack_w_in(w_in):
    qa, ka, va, oa, ia, fa, qb, fb, ib, gb = jnp.split(w_in, SPLIT_POINTS, axis=-1)

    def pad_heads(w):
        w = w.reshape(w.shape[:-1] + (N_HEADS, DK_A))
        w = jnp.pad(w, ((0, 0),) * (w.ndim - 1) + ((0, D_HEAD - DK_A),))
        return w.reshape(w.shape[:-2] + (N_HEADS * D_HEAD,))

    def pad_gate(w):
        return jnp.pad(w, ((0, 0),) * (w.ndim - 1) + ((0, LANES - N_HEADS),))

    cols = [pad_heads(qa), pad_heads(ka * (DK_A ** -0.5)), va, oa, qb, fb, ib, gb, pad_gate(ia), pad_gate(fa)]
    return jnp.concatenate(cols, axis=-1).astype(BF16)


def kernel(x_prompt, x_sample, state_mlstm_C, state_mlstm_n, state_mlstm_m, state_hgrn_S, p_prompt, p_sample,
           ln_in_g, ln_in_b, w_in, mlstm_ig_bias, mlstm_fg_bias, mlstm_norm_g, hgrn_lb_logits, hgrn_norm_g,
           w_out, ln1_g, ln1_b, w_router, b_router, w_gate, b_gate, w_up, b_up, w_down, b_down,
           w_ple_gate, w_ple_proj, ln2_g, ln2_b):
    bp, tp, d = x_prompt.shape
    bs, ts, _ = x_sample.shape
    n_p = bp * tp
    n_s = bs * ts
    ts_pad = SUBLANES
    assert tp % CHUNK == 0 and ts <= ts_pad and ts % CHUNK != 0

    lb_soft = jax.nn.softmax(hgrn_lb_logits.astype(F32), axis=0)
    lower_bounds = jnp.cumsum(lb_soft, axis=0) - lb_soft[0]

    w_in_p = _pack_w_in(w_in)
    gbias = jnp.stack([jnp.pad(mlstm_ig_bias, ((0, 0), (0, LANES - N_HEADS))),
                       jnp.pad(mlstm_fg_bias, ((0, 0), (0, LANES - N_HEADS)))], axis=1)
    w_out_b = w_out.astype(BF16)
    w_pg_b = w_ple_gate.astype(BF16)
    w_pp_b = w_ple_proj.astype(BF16)
    w_r = jnp.pad(w_router, ((0, 0), (0, 0), (0, LANES - N_EXPERTS)))
    b_r = jnp.pad(b_router, ((0, 0), (0, LANES - N_EXPERTS))).reshape(DEPTH, 1, LANES)
    n0_pad = jnp.pad(state_mlstm_n, ((0, 0), (0, 0), (0, 0), (0, D_HEAD - DK_A))).reshape(
        DEPTH, bs, N_HEADS, 1, D_HEAD)
    m0_pad = jnp.pad(state_mlstm_m, ((0, 0), (0, 0), (0, LANES - N_HEADS))).reshape(DEPTH, bs, 1, LANES)

    x = jnp.concatenate([x_prompt.reshape(n_p, d), x_sample.reshape(n_s, d)], axis=0)
    p_all = jnp.concatenate([p_prompt.reshape(DEPTH, n_p, D_PLE), p_sample.reshape(DEPTH, n_s, D_PLE)], axis=1)
    x = _ln(x, ln_in_g, ln_in_b)

    st_p = None
    st_s = None
    for l in range(DEPTH):
        proj = _inproj(x, w_in_p[l])
        lb = lower_bounds[l].reshape(1, MIX)
        nga = mlstm_norm_g[l].reshape(1, MIX)
        ngb = hgrn_norm_g[l].reshape(1, MIX)
        h_p, st_p = _scan(proj, gbias[l], lb, nga, ngb, None, st_p, layer=l,
                          batch=bp, n_chunks=tp // CHUNK, L=CHUNK, last=CHUNK - 1)
        proj_s = jnp.pad(proj[n_p:].reshape(bs, ts, N_PROJ), ((0, 0), (0, ts_pad - ts), (0, 0)))
        h_s, st_s = _scan(proj_s.reshape(bs * ts_pad, N_PROJ), gbias[l], lb, nga, ngb,
                          (state_mlstm_C[l], n0_pad[l], m0_pad[l], state_hgrn_S[l]), st_s, layer=l,
                          batch=bs, n_chunks=1, L=ts_pad, last=ts - 1)
        h_s = h_s.reshape(bs, ts_pad, d)[:, :ts].reshape(n_s, d)
        h_all = jnp.concatenate([h_p, h_s], axis=0)
        x1, resid, logits = _post_mixer(x, h_all, p_all[l], w_out_b[l], ln1_g[l], ln1_b[l],
                                        w_pg_b[l], w_pp_b[l], w_r[l], b_r[l])
        y_k, gates = _moe(l, x1, logits, w_gate, b_gate.reshape(DEPTH, N_EXPERTS, 1, d),
                          w_up, b_up.reshape(DEPTH, N_EXPERTS, 1, d),
                          w_down, b_down.reshape(DEPTH, N_EXPERTS, 1, d))
        x = _combine_ln(resid, y_k, gates, ln2_g[l], ln2_b[l])

    y_prompt = x[:n_p].reshape(bp, tp, d)
    y_sample = x[n_p:].reshape(bs, ts, d)
    unpad = lambda st: (st[0], st[1][:, :, :, 0, :DK_A], st[2][:, :, 0, :N_HEADS], st[3])
    return (y_prompt, y_sample) + unpad(st_p) + unpad(st_s)
```

```python
import functools

import jax
import jax.numpy as jnp
import numpy as np
from jax import lax
from jax.experimental import pallas as pl
from jax.experimental.pallas import tpu as pltpu

F32 = jnp.float32
BF16 = jnp.bfloat16

D_MODEL = 1024
DEPTH = 4
D_PLE = 256
N_HEADS = 4
DK_A = 64
D_HEAD = 128
MIX = 512
N_EXPERTS = 32
TOP_K = 4
SWIGLU_LIMIT = 7.0
SWIGLU_ALPHA = 1.702
CHUNK = 64
LN_EPS = 1e-5
NORM_EPS = 1e-6
LB_FLOOR = 1e-20
DEEPNORM_ALPHA = (2 * DEPTH) ** 0.25
SPLIT_SIZES = (256, 256, 512, 512, 4, 4, 512, 512, 512, 512)
SPLIT_POINTS = tuple(int(s) for s in np.cumsum(SPLIT_SIZES)[:-1])

LANES = 128
SUBLANES = 8
VMEM_LIMIT = 56 * 1024 * 1024

QA0, KA0, VA0, OA0 = 0, 512, 1024, 1536
QB0, FB0, IB0, GB0 = 2048, 2560, 3072, 3584
IG0, FG0 = 4096, 4224
N_PROJ = 4352

ROW_TILE = 256
EXPERT_ROWS = 256
TOK_BITS = 15

NT_DIMS = (((1,), (1,)), ((), ()))
TN_DIMS = (((0,), (0,)), ((), ()))


def _params(*sem):
    return pltpu.CompilerParams(dimension_semantics=sem, vmem_limit_bytes=VMEM_LIMIT)


def _log_sigmoid(x):
    return jnp.minimum(x, 0.0) - jnp.log(1.0 + jnp.exp(-jnp.abs(x)))


def _split3(x):
    x1 = x.astype(BF16)
    r1 = x - x1.astype(F32)
    x2 = r1.astype(BF16)
    x3 = (r1 - x2.astype(F32)).astype(BF16)
    return x1, x2, x3


def _dot01(sel, x):
    return sum(jnp.dot(sel, t, preferred_element_type=F32) for t in _split3(x))


def _layernorm_rows(x, g, b):
    mu = jnp.mean(x, axis=-1, keepdims=True)
    xc = x - mu
    var = jnp.mean(xc * xc, axis=-1, keepdims=True)
    return xc * lax.rsqrt(var + LN_EPS) * g + b


def _ln_kernel(x_ref, g_ref, b_ref, o_ref):
    o_ref[...] = _layernorm_rows(x_ref[...], g_ref[...], b_ref[...])


def _ln(x, g, b):
    n, d = x.shape
    row = pl.BlockSpec((ROW_TILE, d), lambda i: (i, 0))
    vec = pl.BlockSpec((1, d), lambda i: (0, 0))
    return pl.pallas_call(
        _ln_kernel, grid=(n // ROW_TILE,), in_specs=[row, vec, vec], out_specs=row,
        out_shape=jax.ShapeDtypeStruct((n, d), F32), compiler_params=_params("parallel"),
        name="ln_in")(x, g.reshape(1, d), b.reshape(1, d))


def _combine_ln_kernel(resid_ref, y0_ref, y1_ref, y2_ref, y3_ref, gate_ref, g_ref, beta_ref, o_ref):
    acc = resid_ref[...]
    for k, y_ref in enumerate((y0_ref, y1_ref, y2_ref, y3_ref)):
        acc = acc + gate_ref[:, k:k + 1] * y_ref[...]
    o_ref[...] = _layernorm_rows(acc, g_ref[...], beta_ref[...])


def _combine_ln(resid, y_k, gates, g, beta):
    n, d = resid.shape
    nt = n // ROW_TILE
    row = lambda w: pl.BlockSpec((ROW_TILE, w), lambda i: (i, 0))
    yk = lambda k: pl.BlockSpec((ROW_TILE, d), lambda i: (k * nt + i, 0))
    vec = pl.BlockSpec((1, d), lambda i: (0, 0))
    return pl.pallas_call(
        _combine_ln_kernel, grid=(nt,),
        in_specs=[row(d)] + [yk(k) for k in range(TOP_K)] + [row(LANES), vec, vec],
        out_specs=row(d), out_shape=jax.ShapeDtypeStruct((n, d), F32), compiler_params=_params("parallel"),
        name="combine_ln2")(resid, y_k, y_k, y_k, y_k, gates, g.reshape(1, d), beta.reshape(1, d))


def _inproj_kernel(x_ref, w_ref, o_ref):
    o_ref[...] = jnp.dot(x_ref[...].astype(BF16), w_ref[...], preferred_element_type=F32)


def _inproj(x, w):
    n, d = x.shape
    return pl.pallas_call(
        _inproj_kernel, grid=(n // ROW_TILE,),
        in_specs=[pl.BlockSpec((ROW_TILE, d), lambda i: (i, 0)),
                  pl.BlockSpec((d, N_PROJ), lambda i: (0, 0))],
        out_specs=pl.BlockSpec((ROW_TILE, N_PROJ), lambda i: (i, 0)),
        out_shape=jax.ShapeDtypeStruct((n, N_PROJ), F32), compiler_params=_params("parallel"),
        name="in_proj")(x, w)


def _scan_kernel(*refs, L, last, has_state, has_stacked):
    it = iter(refs)
    proj_ref, gbias_ref, lb_ref, nga_ref, ngb_ref = (next(it) for _ in range(5))
    if has_state:
        C0_ref, n0_ref, m0_ref, S0_ref = (next(it) for _ in range(4))
    if has_stacked:
        for _ in range(4):
            next(it)
    h_ref, C_out, n_out, m_out, S_out = (next(it) for _ in range(5))
    C_s, n_s, m_s, St_s = (next(it) for _ in range(4))

    c = pl.program_id(1)
    nc = pl.num_programs(1)
    H = range(N_HEADS)

    @pl.when(c == 0)
    def _init():
        if has_state:
            for h in H:
                C_s[h, 0:DK_A, :] = C0_ref[0, h]
                C_s[h, DK_A:D_HEAD, :] = jnp.zeros((D_HEAD - DK_A, D_HEAD), F32)
                n_s[h] = n0_ref[0, h]
                St_s[h] = S0_ref[0, h].T
            m_s[...] = m0_ref[0]
        else:
            C_s[...] = jnp.zeros(C_s.shape, F32)
            n_s[...] = jnp.zeros(n_s.shape, F32)
            m_s[...] = jnp.zeros(m_s.shape, F32)
            St_s[...] = jnp.zeros(St_s.shape, F32)

    row = lax.broadcasted_iota(jnp.int32, (L, L), 0)
    col = lax.broadcasted_iota(jnp.int32, (L, L), 1)
    causal = row >= col
    tril = causal.astype(BF16)
    ones_l = jnp.ones((L, LANES), BF16)
    lane = lax.broadcasted_iota(jnp.int32, (L, LANES), 1)
    trow = lax.broadcasted_iota(jnp.int32, (L, 1), 0)
    valid = trow <= last
    padded = last < L - 1
    tile = lambda base, h: slice(base + LANES * h, base + LANES * (h + 1))

    i_t = proj_ref[:, IG0:IG0 + LANES] + gbias_ref[0:1, :]
    f_t = _log_sigmoid(proj_ref[:, FG0:FG0 + LANES] + gbias_ref[1:2, :])
    zf = proj_ref[:, FB0:FB0 + MIX]
    lb = lb_ref[...]
    la = jnp.log(jnp.maximum(lb, LB_FLOOR))
    bb = jnp.log1p(-lb) + _log_sigmoid(zf)
    f_log = jnp.maximum(la, bb) + jnp.log(1.0 + jnp.exp(-jnp.abs(la - bb)))
    b_t = _dot01(tril, f_t)
    g = _dot01(tril, f_log)
    m_all = m_s[...]
    C_prev = [C_s[h] for h in H]
    n_prev = [n_s[h] for h in H]
    St = [St_s[h] for h in H]
    qf = [proj_ref[:, tile(QA0, h)] for h in H]
    kf = [proj_ref[:, tile(KA0, h)] for h in H]
    q = [x.astype(BF16) for x in qf]
    k = [x.astype(BF16) for x in kf]
    v = [proj_ref[:, tile(VA0, h)].astype(BF16) for h in H]
    kb = (1.0 - lb) * jax.nn.sigmoid(-zf)
    qr = proj_ref[:, QB0:QB0 + MIX]
    qb = qr * jax.nn.sigmoid(qr)
    vb = [proj_ref[:, tile(IB0, h)].astype(BF16) for h in H]
    qb16 = qb.astype(BF16)
    kb16 = kb.astype(BF16)
    r_t = i_t - b_t
    pad_rows = [jnp.zeros((LANES - L, LANES), BF16)] if L < LANES else []
    zero16 = jnp.zeros((L, LANES), BF16)
    r_all = sum(
        lax.dot_general(ones_l, jnp.concatenate(
            [piece for h in H for piece in [jnp.where(lane == h, term, zero16)] + pad_rows], axis=0),
            NT_DIMS, preferred_element_type=F32)
        for term in _split3(r_t))
    r_m = [r_all[:, LANES * h:LANES * h + L] for h in H]
    qk = [lax.dot_general(q[h], k[h], NT_DIMS, preferred_element_type=F32) for h in H]
    qC = [jnp.dot(q[h], C_prev[h].astype(BF16), preferred_element_type=F32) for h in H]
    levels = []
    bs = 1
    while bs < L:
        levels.append(bs)
        bs *= 2
    small = [bs for bs in levels if bs < SUBLANES]
    g_anchor = {}
    if small:
        sel = jnp.concatenate(
            [(col == (row // (2 * bs)) * (2 * bs) + bs).astype(BF16) for bs in small], axis=0)
        picked = _dot01(sel, g)
        for i, bs in enumerate(small):
            g_anchor[bs] = picked[i * L:(i + 1) * L]
    for bs in levels:
        if bs >= SUBLANES:
            g_anchor[bs] = jnp.concatenate(
                [jnp.broadcast_to(g[p0 + bs:p0 + bs + 1], (2 * bs, MIX)) for p0 in range(0, L, 2 * bs)], axis=0)
    qg = (qb * jnp.exp(g)).astype(BF16)
    o_inter = [lax.dot_general(qg[:, tile(0, h)], St[h].astype(BF16), NT_DIMS, preferred_element_type=F32)
               for h in H]
    a = [jnp.where(row == col, lax.dot_general(qb16[:, tile(0, h)], kb16[:, tile(0, h)], NT_DIMS,
                                                preferred_element_type=F32), 0.0) for h in H]
    qn = [jnp.sum(qf[h] * n_prev[h], axis=1, keepdims=True) for h in H]
    b_col = [b_t[:, h:h + 1] for h in H]
    i_col = [i_t[:, h:h + 1] for h in H]
    m_prev = [m_all[:, h:h + 1] for h in H]
    cm = [jnp.max(jnp.where(causal, r_m[h], -jnp.inf), axis=1, keepdims=True) for h in H]
    m_t = [b_col[h] + jnp.maximum(m_prev[h], cm[h]) for h in H]
    inter = [jnp.exp(b_col[h] + m_prev[h] - m_t[h]) for h in H]
    d = [jnp.exp(jnp.where(causal, (b_col[h] - m_t[h]) + r_m[h], -1e30)) for h in H]
    s = [qk[h] * d[h] for h in H]
    log2e = 1.4426950408889634
    e = [jnp.exp2((g - g_anchor[bs]) * jnp.where((trow // bs) % 2 == 1, log2e, -log2e)) for bs in levels]
    qe = [(qb * x).astype(BF16) for x in e]
    ke = [(kb * x).astype(BF16) for x in e]
    sv = [jnp.dot(s[h].astype(BF16), v[h], preferred_element_type=F32) for h in H]
    sc = [[lax.dot_general(qe[i][:, tile(0, h)], ke[i][:, tile(0, h)], NT_DIMS, preferred_element_type=F32)
           for h in H] for i in range(len(levels))]
    den = [jnp.sum(s[h], axis=1, keepdims=True) + inter[h] * qn[h] for h in H]
    m_new = [m_t[h][last:last + 1, :] for h in H]
    b_last = [b_col[h][last:last + 1, :] for h in H]
    w_arg = [b_last[h] - b_col[h] + i_col[h] - m_new[h] for h in H]
    if padded:
        w_arg = [jnp.where(valid, x, -1e30) for x in w_arg]
    kw = [kf[h] * jnp.exp(w_arg[h]) for h in H]
    decay = [jnp.exp(b_last[h] + m_prev[h] - m_new[h]) for h in H]
    kv = [lax.dot_general(kw[h].astype(BF16), v[h], TN_DIMS, preferred_element_type=F32) for h in H]
    for i, bs in enumerate(levels):
        pair = jnp.logical_and(row // (2 * bs) == col // (2 * bs),
                               jnp.logical_and((row // bs) % 2 == 1, (col // bs) % 2 == 0))
        for h in H:
            a[h] = jnp.where(pair, sc[i][h], a[h])
    o = [o_inter[h] + jnp.dot(a[h].astype(BF16), vb[h], preferred_element_type=F32) for h in H]
    g_last = g[last:last + 1]
    dec_arg = g_last - g
    if padded:
        dec_arg = jnp.where(valid, dec_arg, -1e30)
    kdec = (kb * jnp.exp(dec_arg)).astype(BF16)
    eg_last = jnp.exp(g_last)
    vk = [lax.dot_general(vb[h], kdec[:, tile(0, h)], TN_DIMS, preferred_element_type=F32) for h in H]
    hh = [(sv[h] + inter[h] * qC[h]) / jnp.maximum(jnp.abs(den[h]), jnp.exp(-m_t[h])) for h in H]
    for h in H:
        hn = hh[h] * lax.rsqrt(jnp.mean(hh[h] * hh[h], axis=-1, keepdims=True) + NORM_EPS) * nga_ref[:, tile(0, h)]
        h_ref[:, tile(0, h)] = jax.nn.sigmoid(proj_ref[:, tile(OA0, h)]) * hn
    for h in H:
        on = o[h] * lax.rsqrt(jnp.mean(o[h] * o[h], axis=-1, keepdims=True) + NORM_EPS) * ngb_ref[:, tile(0, h)]
        gr = proj_ref[:, tile(GB0, h)]
        h_ref[:, tile(MIX, h)] = gr * jax.nn.sigmoid(gr) * on
    for h in H:
        C_s[h] = decay[h] * C_prev[h] + kv[h]
        n_s[h] = decay[h] * n_prev[h] + jnp.sum(kw[h], axis=0, keepdims=True)
        St_s[h] = eg_last[:, tile(0, h)] * St[h] + vk[h]
    m_upd = m_all
    lane1 = lax.broadcasted_iota(jnp.int32, (1, LANES), 1)
    for h in H:
        m_upd = jnp.where(lane1 == h, m_new[h], m_upd)
    m_s[...] = m_upd

    @pl.when(c == nc - 1)
    def _finish():
        for h in H:
            C_out[0, h] = C_s[h, 0:DK_A, :]
            n_out[0, h] = n_s[h]
            S_out[0, h] = St_s[h].T
        m_out[0] = m_s[...]


def _scan(proj, gbias, lb, nga, ngb, state, stacked, *, layer, batch, n_chunks, L, last):
    has_state = state is not None
    const2 = lambda b, c: (0, 0)
    in_specs = [
        pl.BlockSpec((L, N_PROJ), lambda b, c: (b * n_chunks + c, 0)),
        pl.BlockSpec((2, LANES), const2),
        pl.BlockSpec((1, MIX), const2),
        pl.BlockSpec((1, MIX), const2),
        pl.BlockSpec((1, MIX), const2),
    ]
    args = [proj, gbias, lb, nga, ngb]
    st4 = lambda b, c: (b, 0, 0, 0)
    st3 = lambda b, c: (b, 0, 0)
    if has_state:
        in_specs += [
            pl.BlockSpec((1, N_HEADS, DK_A, D_HEAD), st4),
            pl.BlockSpec((1, N_HEADS, 1, LANES), st4),
            pl.BlockSpec((1, 1, LANES), st3),
            pl.BlockSpec((1, N_HEADS, D_HEAD, D_HEAD), st4),
        ]
        args += list(state)
    aliases = {}
    if stacked is not None:
        aliases = {len(args) + i: 1 + i for i in range(4)}
        in_specs += [pl.BlockSpec(memory_space=pl.ANY)] * 4
        args += list(stacked)
    lst4 = lambda b, c: (layer, b, 0, 0, 0)
    lst3 = lambda b, c: (layer, b, 0, 0)
    out_specs = [
        pl.BlockSpec((L, D_MODEL), lambda b, c: (b * n_chunks + c, 0)),
        pl.BlockSpec((None, 1, N_HEADS, DK_A, D_HEAD), lst4),
        pl.BlockSpec((None, 1, N_HEADS, 1, LANES), lst4),
        pl.BlockSpec((None, 1, 1, LANES), lst3),
        pl.BlockSpec((None, 1, N_HEADS, D_HEAD, D_HEAD), lst4),
    ]
    out_shape = [
        jax.ShapeDtypeStruct((batch * n_chunks * L, D_MODEL), F32),
        jax.ShapeDtypeStruct((DEPTH, batch, N_HEADS, DK_A, D_HEAD), F32),
        jax.ShapeDtypeStruct((DEPTH, batch, N_HEADS, 1, LANES), F32),
        jax.ShapeDtypeStruct((DEPTH, batch, 1, LANES), F32),
        jax.ShapeDtypeStruct((DEPTH, batch, N_HEADS, D_HEAD, D_HEAD), F32),
    ]
    scratch = [
        pltpu.VMEM((N_HEADS, D_HEAD, D_HEAD), F32),
        pltpu.VMEM((N_HEADS, 1, LANES), F32),
        pltpu.VMEM((1, LANES), F32),
        pltpu.VMEM((N_HEADS, D_HEAD, D_HEAD), F32),
    ]
    kern = functools.partial(_scan_kernel, L=L, last=last, has_state=has_state, has_stacked=stacked is not None)
    res = pl.pallas_call(
        kern, grid=(batch, n_chunks), in_specs=in_specs, out_specs=out_specs, out_shape=out_shape,
        scratch_shapes=scratch, input_output_aliases=aliases,
        compiler_params=_params("parallel", "arbitrary"),
        name="scan_state" if has_state else "scan_prompt")(*args)
    return res[0], tuple(res[1:])


def _post_mixer_kernel(x_ref, h_ref, p_ref, wout_ref, g1_ref, b1_ref, wpg_ref, wpp_ref, wr2_ref, wrh_ref, br_ref,
                       x1_ref, resid_ref, idx_ref, rank_ref, gate_ref, cnt_ref, carry_s):
    i = pl.program_id(0)

    @pl.when(i == 0)
    def _init():
        carry_s[...] = jnp.zeros(carry_s.shape, F32)

    mix = jnp.dot(h_ref[...].astype(BF16), wout_ref[...], preferred_element_type=F32)
    x1 = _layernorm_rows(DEEPNORM_ALPHA * x_ref[...] + mix, g1_ref[...], b1_ref[...])
    x1b = x1.astype(BF16)
    x1_ref[...] = x1
    x1_lo = (x1 - x1b.astype(F32)).astype(BF16)
    l2 = jnp.dot(x1b, wr2_ref[...], preferred_element_type=F32)
    logits = (l2[:, :LANES] + l2[:, LANES:]
              + jnp.dot(x1_lo, wrh_ref[...], preferred_element_type=F32) + br_ref[...])

    tm = logits.shape[0]
    lane = lax.broadcasted_iota(jnp.int32, (tm, LANES), 1)
    lane_f = lane.astype(F32)
    l = jnp.where(lane < N_EXPERTS, logits, -jnp.inf)
    vals, idxs, onehots = [], [], []
    for _ in range(TOP_K):
        mx = jnp.max(l, axis=1, keepdims=True)
        ix = jnp.min(jnp.where(l == mx, lane_f, float(LANES)), axis=1, keepdims=True)
        sel = lane_f == ix
        vals.append(mx)
        idxs.append(ix)
        onehots.append(sel.astype(F32))
        l = jnp.where(sel, -jnp.inf, l)
    gate = jax.nn.sigmoid(jnp.dot(x1b, wpg_ref[...], preferred_element_type=F32))
    pp = jnp.dot(p_ref[...].astype(BF16), wpp_ref[...], preferred_element_type=F32)
    resid_ref[...] = DEEPNORM_ALPHA * x1 + gate * pp

    w = [jnp.exp(val - vals[0]) for val in vals]
    tot = w[0] + w[1] + w[2] + w[3]
    oh_all = onehots[0] + onehots[1] + onehots[2] + onehots[3]
    row = lax.broadcasted_iota(jnp.int32, (tm, tm), 0)
    col = lax.broadcasted_iota(jnp.int32, (tm, tm), 1)
    earlier = (row > col).astype(BF16)
    prefix = jnp.dot(earlier, oh_all.astype(BF16), preferred_element_type=F32) + carry_s[...]
    idx_o = jnp.zeros((tm, LANES), F32)
    rank_o = jnp.zeros((tm, LANES), F32)
    gate_o = jnp.zeros((tm, LANES), F32)
    for k in range(TOP_K):
        rank_k = jnp.sum(onehots[k] * prefix, axis=1, keepdims=True)
        idx_o = jnp.where(lane == k, idxs[k], idx_o)
        rank_o = jnp.where(lane == k, rank_k, rank_o)
        gate_o = jnp.where(lane == k, w[k] / tot, gate_o)
    idx_ref[...] = idx_o.astype(jnp.int32)
    rank_ref[...] = rank_o.astype(jnp.int32)
    gate_ref[...] = gate_o
    carry_s[...] = carry_s[...] + jnp.sum(oh_all, axis=0, keepdims=True)
    cnt_ref[...] = carry_s[...].astype(jnp.int32)


def _post_mixer(x, h, p, wout, g1, b1, wpg, wpp, wr2, wrh, br):
    n, d = x.shape
    row = lambda w: pl.BlockSpec((ROW_TILE, w), lambda i: (i, 0))
    full = lambda a: pl.BlockSpec(a.shape, lambda i: (0, 0))
    g1 = g1.reshape(1, d)
    b1 = b1.reshape(1, d)
    return pl.pallas_call(
        _post_mixer_kernel, grid=(n // ROW_TILE,),
        in_specs=[row(d), row(d), row(D_PLE), full(wout), full(g1), full(b1), full(wpg), full(wpp),
                  full(wr2), full(wrh), full(br)],
        out_specs=[row(d), row(d), row(LANES), row(LANES), row(LANES), pl.BlockSpec((1, LANES), lambda i: (0, 0))],
        out_shape=[jax.ShapeDtypeStruct((n, d), F32), jax.ShapeDtypeStruct((n, d), F32),
                   jax.ShapeDtypeStruct((n, LANES), jnp.int32), jax.ShapeDtypeStruct((n, LANES), jnp.int32),
                   jax.ShapeDtypeStruct((n, LANES), F32), jax.ShapeDtypeStruct((1, LANES), jnp.int32)],
        scratch_shapes=[pltpu.VMEM((1, LANES), F32)],
        compiler_params=_params("arbitrary"), name="post_mixer")(x, h, p, wout, g1, b1, wpg, wpp, wr2, wrh, br)


def _expert_kernel(blk_ref, e_ref, lo_ref, hi_ref, nv_ref, x_ref, wg_ref, bg_ref, wu_ref, bu_ref, wd_ref,
                   bd_ref, o_ref, wg_s, wu_s, wd_s):
    v = pl.program_id(0)
    prev = jnp.maximum(v - 1, 0)
    valid = v < nv_ref[0]
    new_expert = jnp.logical_or(v == 0, e_ref[v] != e_ref[prev])
    new_block = jnp.logical_or(v == 0, blk_ref[v] != blk_ref[prev])

    @pl.when(jnp.logical_and(valid, new_expert))
    def _cast_weights():
        wg_s[...] = wg_ref[...].astype(BF16)
        wu_s[...] = wu_ref[...].astype(BF16)
        wd_s[...] = wd_ref[...].astype(BF16)

    @pl.when(valid)
    def _compute():
        xb = x_ref[...].astype(BF16)
        gate = jnp.dot(xb, wg_s[...], preferred_element_type=F32) + bg_ref[0]
        up = jnp.dot(xb, wu_s[...], preferred_element_type=F32) + bu_ref[0]
        gate = jnp.minimum(gate, SWIGLU_LIMIT)
        up = jnp.clip(up, -SWIGLU_LIMIT, SWIGLU_LIMIT)
        glu = gate * jax.nn.sigmoid(gate * SWIGLU_ALPHA)
        act = ((up + 1.0) * glu).astype(BF16)
        y = jnp.dot(act, wd_s[...], preferred_element_type=F32) + bd_ref[0]
        rows = blk_ref[v] * EXPERT_ROWS + lax.broadcasted_iota(jnp.int32, (EXPERT_ROWS, 1), 0)
        mine = jnp.logical_and(rows >= lo_ref[v], rows < hi_ref[v])

        @pl.when(new_block)
        def _first():
            o_ref[...] = jnp.where(mine, y, 0.0)

        @pl.when(jnp.logical_not(new_block))
        def _again():
            o_ref[...] = jnp.where(mine, y, o_ref[...])


def _experts(layer, blk_v, e_v, lo_v, hi_v, n_vis, xin, wg, bg, wu, bu, wd, bd):
    n_rows = xin.shape[0]
    d = wg.shape[-1]
    n_visits = blk_v.shape[0]
    xspec = pl.BlockSpec((EXPERT_ROWS, d), lambda v, blk, e, lo, hi, nv: (blk[v], 0))
    ospec = pl.BlockSpec((EXPERT_ROWS, d), lambda v, blk, e, lo, hi, nv: (blk[v], 0))
    wspec = pl.BlockSpec((None, None, d, d), lambda v, blk, e, lo, hi, nv: (layer, e[v], 0, 0))
    bspec = pl.BlockSpec((None, 1, 1, d), lambda v, blk, e, lo, hi, nv: (layer, e[v], 0, 0))
    grid_spec = pltpu.PrefetchScalarGridSpec(
        num_scalar_prefetch=5, grid=(n_visits,),
        in_specs=[xspec, wspec, bspec, wspec, bspec, wspec, bspec], out_specs=ospec,
        scratch_shapes=[pltpu.VMEM((d, d), BF16)] * 3)
    return pl.pallas_call(
        _expert_kernel, grid_spec=grid_spec, out_shape=jax.ShapeDtypeStruct((n_rows, d), F32),
        compiler_params=_params("arbitrary"), name="experts")(
            blk_v, e_v, lo_v, hi_v, n_vis, xin, wg, bg, wu, bu, wd, bd)


def _moe(layer, x1, routing, wg, bg, wu, bu, wd, bd):
    n_tok = x1.shape[0]
    n_assign = n_tok * TOP_K
    n_blocks = n_assign // EXPERT_ROWS
    idx_o, rank_o, gate_o, cnt = routing
    counts = cnt[0, :N_EXPERTS]
    ends = jnp.cumsum(counts)
    offs = ends - counts
    experts = jnp.arange(N_EXPERTS, dtype=jnp.int32)
    top_i = idx_o[:, :TOP_K]
    off_tok = jnp.sum(jnp.where(top_i[:, :, None] == experts, offs, 0), axis=-1)
    pos = (off_tok + rank_o[:, :TOP_K]).reshape(n_assign)
    assert n_tok <= (1 << TOK_BITS) and n_assign <= (1 << (32 - TOK_BITS))
    tok = jnp.arange(n_assign, dtype=jnp.uint32) // TOP_K
    key = lax.sort(pos.astype(jnp.uint32) * (1 << TOK_BITS) + tok)
    xin = x1[(key & ((1 << TOK_BITS) - 1)).astype(jnp.int32)]

    first_blk = offs // EXPERT_ROWS
    last_blk = (ends - 1) // EXPERT_ROWS
    nvis = jnp.where(counts > 0, last_blk - first_blk + 1, 0)
    vend = jnp.cumsum(nvis)
    vstart = vend - nvis
    n_visits = n_blocks + N_EXPERTS - 1
    v = jnp.arange(n_visits, dtype=jnp.int32)
    e_v = jnp.sum((vend[None, :] <= v[:, None]).astype(jnp.int32), axis=1)
    e_last = jnp.max(jnp.where(counts > 0, experts, 0))
    e_v = jnp.minimum(e_v, e_last)
    pick = lambda tab: jnp.sum(jnp.where(e_v[:, None] == experts, tab, 0), axis=-1)
    blk_v = jnp.minimum(pick(first_blk) + v - pick(vstart), n_blocks - 1)
    y_rows = _experts(layer, blk_v.astype(jnp.int32), e_v.astype(jnp.int32), pick(offs).astype(jnp.int32),
                      pick(ends).astype(jnp.int32), vend[-1:].astype(jnp.int32), xin, wg, bg, wu, bu, wd, bd)
    return y_rows[pos.reshape(n_tok, TOP_K).T.reshape(n_assign)], gate_o


def _pack_w_in(w_in):
    qa, ka, va, oa, ia, fa, qb, fb, ib, gb = jnp.split(w_in, SPLIT_POINTS, axis=-1)

    def pad_heads(w):
        w = w.reshape(w.shape[:-1] + (N_HEADS, DK_A))
        w = jnp.pad(w, ((0, 0),) * (w.ndim - 1) + ((0, D_HEAD - DK_A),))
        return w.reshape(w.shape[:-2] + (N_HEADS * D_HEAD,))

    def pad_gate(w):
        return jnp.pad(w, ((0, 0),) * (w.ndim - 1) + ((0, LANES - N_HEADS),))

    cols = [pad_heads(qa), pad_heads(ka * (DK_A ** -0.5)), va, oa, qb, fb, ib, gb, pad_gate(ia), pad_gate(fa)]
    return jnp.concatenate(cols, axis=-1).astype(BF16)


def kernel(x_prompt, x_sample, state_mlstm_C, state_mlstm_n, state_mlstm_m, state_hgrn_S, p_prompt, p_sample,
           ln_in_g, ln_in_b, w_in, mlstm_ig_bias, mlstm_fg_bias, mlstm_norm_g, hgrn_lb_logits, hgrn_norm_g,
           w_out, ln1_g, ln1_b, w_router, b_router, w_gate, b_gate, w_up, b_up, w_down, b_down,
           w_ple_gate, w_ple_proj, ln2_g, ln2_b):
    bp, tp, d = x_prompt.shape
    bs, ts, _ = x_sample.shape
    n_p = bp * tp
    n_s = bs * ts
    ts_pad = SUBLANES
    assert tp % CHUNK == 0 and ts <= ts_pad and ts % CHUNK != 0

    lb_soft = jax.nn.softmax(hgrn_lb_logits.astype(F32), axis=0)
    lower_bounds = jnp.cumsum(lb_soft, axis=0) - lb_soft[0]

    w_in_p = _pack_w_in(w_in)
    gbias = jnp.stack([jnp.pad(mlstm_ig_bias, ((0, 0), (0, LANES - N_HEADS))),
                       jnp.pad(mlstm_fg_bias, ((0, 0), (0, LANES - N_HEADS)))], axis=1)
    w_out_b = w_out.astype(BF16)
    w_pg_b = w_ple_gate.astype(BF16)
    w_pp_b = w_ple_proj.astype(BF16)
    w_r = jnp.pad(w_router, ((0, 0), (0, 0), (0, LANES - N_EXPERTS)))
    w_r_hi = w_r.astype(BF16)
    w_r2 = jnp.concatenate([w_r_hi, (w_r - w_r_hi.astype(F32)).astype(BF16)], axis=-1)
    b_r = jnp.pad(b_router, ((0, 0), (0, LANES - N_EXPERTS))).reshape(DEPTH, 1, LANES)
    n0_pad = jnp.pad(state_mlstm_n, ((0, 0), (0, 0), (0, 0), (0, D_HEAD - DK_A))).reshape(
        DEPTH, bs, N_HEADS, 1, D_HEAD)
    m0_pad = jnp.pad(state_mlstm_m, ((0, 0), (0, 0), (0, LANES - N_HEADS))).reshape(DEPTH, bs, 1, LANES)

    x = jnp.concatenate([x_prompt.reshape(n_p, d), x_sample.reshape(n_s, d)], axis=0)
    p_all = jnp.concatenate([p_prompt.reshape(DEPTH, n_p, D_PLE), p_sample.reshape(DEPTH, n_s, D_PLE)], axis=1)
    x = _ln(x, ln_in_g, ln_in_b)

    st_p = None
    st_s = None
    for l in range(DEPTH):
        proj = _inproj(x, w_in_p[l])
        lb = lower_bounds[l].reshape(1, MIX)
        nga = mlstm_norm_g[l].reshape(1, MIX)
        ngb = hgrn_norm_g[l].reshape(1, MIX)
        h_p, st_p = _scan(proj, gbias[l], lb, nga, ngb, None, st_p, layer=l,
                          batch=bp, n_chunks=tp // CHUNK, L=CHUNK, last=CHUNK - 1)
        proj_s = jnp.pad(proj[n_p:].reshape(bs, ts, N_PROJ), ((0, 0), (0, ts_pad - ts), (0, 0)))
        h_s, st_s = _scan(proj_s.reshape(bs * ts_pad, N_PROJ), gbias[l], lb, nga, ngb,
                          (state_mlstm_C[l], n0_pad[l], m0_pad[l], state_hgrn_S[l]), st_s, layer=l,
                          batch=bs, n_chunks=1, L=ts_pad, last=ts - 1)
        h_s = h_s.reshape(bs, ts_pad, d)[:, :ts].reshape(n_s, d)
        h_all = jnp.concatenate([h_p, h_s], axis=0)
        x1, resid, *routing = _post_mixer(x, h_all, p_all[l], w_out_b[l], ln1_g[l], ln1_b[l],
                                          w_pg_b[l], w_pp_b[l], w_r2[l], w_r_hi[l], b_r[l])
        y_k, gates = _moe(l, x1, routing, w_gate, b_gate.reshape(DEPTH, N_EXPERTS, 1, d),
                          w_up, b_up.reshape(DEPTH, N_EXPERTS, 1, d),
                          w_down, b_down.reshape(DEPTH, N_EXPERTS, 1, d))
        x = _combine_ln(resid, y_k, gates, ln2_g[l], ln2_b[l])

    y_prompt = x[:n_p].reshape(bp, tp, d)
    y_sample = x[n_p:].reshape(bs, ts, d)
    unpad = lambda st: (st[0], st[1][:, :, :, 0, :DK_A], st[2][:, :, 0, :N_HEADS], st[3])
    return (y_prompt, y_sample) + unpad(st_p) + unpad(st_s)
```

```python
import functools

import jax
import jax.numpy as jnp
import numpy as np
from jax import lax
from jax.experimental import pallas as pl
from jax.experimental.pallas import tpu as pltpu

F32 = jnp.float32
BF16 = jnp.bfloat16

D_MODEL = 1024
DEPTH = 4
D_PLE = 256
N_HEADS = 4
DK_A = 64
D_HEAD = 128
MIX = 512
N_EXPERTS = 32
TOP_K = 4
SWIGLU_LIMIT = 7.0
SWIGLU_ALPHA = 1.702
CHUNK = 64
LN_EPS = 1e-5
NORM_EPS = 1e-6
LB_FLOOR = 1e-20
DEEPNORM_ALPHA = (2 * DEPTH) ** 0.25
SPLIT_SIZES = (256, 256, 512, 512, 4, 4, 512, 512, 512, 512)
SPLIT_POINTS = tuple(int(s) for s in np.cumsum(SPLIT_SIZES)[:-1])

LANES = 128
SUBLANES = 8
VMEM_LIMIT = 56 * 1024 * 1024

QA0, KA0, VA0, OA0 = 0, 512, 1024, 1536
QB0, FB0, IB0, GB0 = 2048, 2560, 3072, 3584
IG0, FG0 = 4096, 4224
N_PROJ = 4352

ROW_TILE = 256
EXPERT_ROWS = 256
TOK_BITS = 15

NT_DIMS = (((1,), (1,)), ((), ()))
TN_DIMS = (((0,), (0,)), ((), ()))


def _params(*sem):
    return pltpu.CompilerParams(dimension_semantics=sem, vmem_limit_bytes=VMEM_LIMIT)


def _log_sigmoid(x):
    return jnp.minimum(x, 0.0) - jnp.log(1.0 + jnp.exp(-jnp.abs(x)))


def _split3(x):
    x1 = x.astype(BF16)
    r1 = x - x1.astype(F32)
    x2 = r1.astype(BF16)
    x3 = (r1 - x2.astype(F32)).astype(BF16)
    return x1, x2, x3


def _dot01(sel, x):
    return sum(jnp.dot(sel, t, preferred_element_type=F32) for t in _split3(x))


def _layernorm_rows(x, g, b):
    mu = jnp.mean(x, axis=-1, keepdims=True)
    xc = x - mu
    var = jnp.mean(xc * xc, axis=-1, keepdims=True)
    return xc * lax.rsqrt(var + LN_EPS) * g + b


def _ln_kernel(x_ref, g_ref, b_ref, o_ref):
    o_ref[...] = _layernorm_rows(x_ref[...], g_ref[...], b_ref[...])


def _ln(x, g, b):
    n, d = x.shape
    row = pl.BlockSpec((ROW_TILE, d), lambda i: (i, 0))
    vec = pl.BlockSpec((1, d), lambda i: (0, 0))
    return pl.pallas_call(
        _ln_kernel, grid=(n // ROW_TILE,), in_specs=[row, vec, vec], out_specs=row,
        out_shape=jax.ShapeDtypeStruct((n, d), F32), compiler_params=_params("parallel"),
        name="ln_in")(x, g.reshape(1, d), b.reshape(1, d))


def _combine_ln_kernel(resid_ref, y0_ref, y1_ref, y2_ref, y3_ref, gate_ref, g_ref, beta_ref, o_ref):
    acc = resid_ref[...]
    for k, y_ref in enumerate((y0_ref, y1_ref, y2_ref, y3_ref)):
        acc = acc + gate_ref[:, k:k + 1] * y_ref[...]
    o_ref[...] = _layernorm_rows(acc, g_ref[...], beta_ref[...])


def _combine_ln(resid, y_k, gates, g, beta):
    n, d = resid.shape
    nt = n // ROW_TILE
    row = lambda w: pl.BlockSpec((ROW_TILE, w), lambda i: (i, 0))
    yk = lambda k: pl.BlockSpec((ROW_TILE, d), lambda i: (k * nt + i, 0))
    vec = pl.BlockSpec((1, d), lambda i: (0, 0))
    return pl.pallas_call(
        _combine_ln_kernel, grid=(nt,),
        in_specs=[row(d)] + [yk(k) for k in range(TOP_K)] + [row(LANES), vec, vec],
        out_specs=row(d), out_shape=jax.ShapeDtypeStruct((n, d), F32), compiler_params=_params("parallel"),
        name="combine_ln2")(resid, y_k, y_k, y_k, y_k, gates, g.reshape(1, d), beta.reshape(1, d))


def _inproj_kernel(x_ref, w_ref, o_ref):
    o_ref[...] = jnp.dot(x_ref[...].astype(BF16), w_ref[...], preferred_element_type=F32)


def _inproj(x, w, row0, n_rows):
    d = x.shape[1]
    t0 = row0 // ROW_TILE
    return pl.pallas_call(
        _inproj_kernel, grid=(n_rows // ROW_TILE,),
        in_specs=[pl.BlockSpec((ROW_TILE, d), lambda i: (t0 + i, 0)),
                  pl.BlockSpec((d, N_PROJ), lambda i: (0, 0))],
        out_specs=pl.BlockSpec((ROW_TILE, N_PROJ), lambda i: (i, 0)),
        out_shape=jax.ShapeDtypeStruct((n_rows, N_PROJ), F32), compiler_params=_params("parallel"),
        name="in_proj")(x, w)


def _scan_kernel(*refs, NB, L, last, has_state):
    it = iter(refs)
    proj_ref, gbias_ref, lb_ref, nga_ref, ngb_ref = (next(it) for _ in range(5))
    if has_state:
        C0_ref, n0_ref, m0_ref, S0_ref = (next(it) for _ in range(4))
    for _ in range(4):
        next(it)
    h_ref, C_out, n_out, m_out, S_out = (next(it) for _ in range(5))
    C_s, n_s, m_s, St_s = (next(it) for _ in range(4))

    c = pl.program_id(1)
    nc = pl.num_programs(1)
    H = range(N_HEADS)
    N = range(NB)
    R = NB * L
    assert R == LANES

    @pl.when(c == 0)
    def _init():
        if has_state:
            for n in N:
                for h in H:
                    C_s[n, h, 0:DK_A, :] = C0_ref[n, h]
                    C_s[n, h, DK_A:D_HEAD, :] = jnp.zeros((D_HEAD - DK_A, D_HEAD), F32)
                    St_s[n, h] = S0_ref[n, h].T
            n_s[...] = n0_ref[...]
            m_s[...] = m0_ref[...]
        else:
            C_s[...] = jnp.zeros(C_s.shape, F32)
            n_s[...] = jnp.zeros(n_s.shape, F32)
            m_s[...] = jnp.zeros(m_s.shape, F32)
            St_s[...] = jnp.zeros(St_s.shape, F32)

    row = lax.broadcasted_iota(jnp.int32, (R, R), 0)
    col = lax.broadcasted_iota(jnp.int32, (R, R), 1)
    causal = jnp.logical_and(row // L == col // L, row >= col)
    tril = causal.astype(BF16)
    ones_r = jnp.ones((R, LANES), BF16)
    lane = lax.broadcasted_iota(jnp.int32, (R, LANES), 1)
    step = lax.broadcasted_iota(jnp.int32, (R, 1), 0) % L
    valid = step <= last
    padded = last < L - 1
    tile = lambda base, h: slice(base + LANES * h, base + LANES * (h + 1))
    seq = lambda n: slice(n * L, (n + 1) * L)
    cols = lambda base, width: proj_ref[:, :, base:base + width].reshape(R, width)
    per_seq = lambda vals: jnp.concatenate([jnp.broadcast_to(x, (L, x.shape[-1])) for x in vals], axis=0)
    last_rows = lambda x: [x[n * L + last:n * L + last + 1] for n in N]

    i_t = cols(IG0, LANES) + gbias_ref[0:1, :]
    f_t = _log_sigmoid(cols(FG0, LANES) + gbias_ref[1:2, :])
    zf = cols(FB0, MIX)
    lb = lb_ref[...]
    la = jnp.log(jnp.maximum(lb, LB_FLOOR))
    bb = jnp.log1p(-lb) + _log_sigmoid(zf)
    f_log = jnp.maximum(la, bb) + jnp.log(1.0 + jnp.exp(-jnp.abs(la - bb)))
    b_t = _dot01(tril, f_t)
    g = _dot01(tril, f_log)
    m_all = [m_s[n] for n in N]
    C_prev = [[C_s[n, h] for h in H] for n in N]
    n_prev = [[n_s[n, h] for h in H] for n in N]
    St = [[St_s[n, h] for h in H] for n in N]
    qf = [cols(QA0 + LANES * h, LANES) for h in H]
    kf = [cols(KA0 + LANES * h, LANES) for h in H]
    q = [x.astype(BF16) for x in qf]
    k = [x.astype(BF16) for x in kf]
    v = [cols(VA0 + LANES * h, LANES).astype(BF16) for h in H]
    kb = (1.0 - lb) * jax.nn.sigmoid(-zf)
    qr = cols(QB0, MIX)
    qb = qr * jax.nn.sigmoid(qr)
    vb = [cols(IB0 + LANES * h, LANES).astype(BF16) for h in H]
    qb16 = qb.astype(BF16)
    kb16 = kb.astype(BF16)
    r_t = i_t - b_t
    zero16 = jnp.zeros((R, LANES), BF16)
    r_all = sum(
        lax.dot_general(ones_r, jnp.concatenate([jnp.where(lane == h, term, zero16) for h in H], axis=0),
                        NT_DIMS, preferred_element_type=F32)
        for term in _split3(r_t))
    r_m = [r_all[:, R * h:R * (h + 1)] for h in H]
    qk = [lax.dot_general(q[h], k[h], NT_DIMS, preferred_element_type=F32) for h in H]
    qC = [jnp.concatenate([jnp.dot(q[h][seq(n)], C_prev[n][h].astype(BF16), preferred_element_type=F32)
                           for n in N], axis=0) for h in H]
    levels = []
    bs = 1
    while bs < L:
        levels.append(bs)
        bs *= 2
    small = [bs for bs in levels if bs < SUBLANES]
    g_anchor = {}
    if small:
        sel = jnp.concatenate(
            [(col == (row // (2 * bs)) * (2 * bs) + bs).astype(BF16) for bs in small], axis=0)
        picked = _dot01(sel, g)
        for i, bs in enumerate(small):
            g_anchor[bs] = picked[i * R:(i + 1) * R]
    for bs in levels:
        if bs >= SUBLANES:
            g_anchor[bs] = jnp.concatenate(
                [jnp.broadcast_to(g[p0 + bs:p0 + bs + 1], (2 * bs, MIX)) for p0 in range(0, R, 2 * bs)], axis=0)
    qg = (qb * jnp.exp(g)).astype(BF16)
    o_inter = [jnp.concatenate([lax.dot_general(qg[seq(n), tile(0, h)], St[n][h].astype(BF16), NT_DIMS,
                                                preferred_element_type=F32) for n in N], axis=0) for h in H]
    a = [jnp.where(row == col, lax.dot_general(qb16[:, tile(0, h)], kb16[:, tile(0, h)], NT_DIMS,
                                                preferred_element_type=F32), 0.0) for h in H]
    qn = [jnp.sum(qf[h] * per_seq([n_prev[n][h] for n in N]), axis=1, keepdims=True) for h in H]
    b_col = [b_t[:, h:h + 1] for h in H]
    i_col = [i_t[:, h:h + 1] for h in H]
    m_prev = [per_seq([m_all[n][:, h:h + 1] for n in N]) for h in H]
    cm = [jnp.max(jnp.where(causal, r_m[h], -jnp.inf), axis=1, keepdims=True) for h in H]
    m_t = [b_col[h] + jnp.maximum(m_prev[h], cm[h]) for h in H]
    inter = [jnp.exp(b_col[h] + m_prev[h] - m_t[h]) for h in H]
    d = [jnp.exp(jnp.where(causal, (b_col[h] - m_t[h]) + r_m[h], -1e30)) for h in H]
    s = [qk[h] * d[h] for h in H]
    log2e = 1.4426950408889634
    e = [jnp.exp2((g - g_anchor[bs]) * jnp.where((step // bs) % 2 == 1, log2e, -log2e)) for bs in levels]
    qe = [(qb * x).astype(BF16) for x in e]
    ke = [(kb * x).astype(BF16) for x in e]
    sv = [jnp.dot(s[h].astype(BF16), v[h], preferred_element_type=F32) for h in H]
    sc = [[lax.dot_general(qe[i][:, tile(0, h)], ke[i][:, tile(0, h)], NT_DIMS, preferred_element_type=F32)
           for h in H] for i in range(len(levels))]
    den = [jnp.sum(s[h], axis=1, keepdims=True) + inter[h] * qn[h] for h in H]
    m_new = [last_rows(m_t[h]) for h in H]
    b_last = [last_rows(b_col[h]) for h in H]
    w_arg = [per_seq(b_last[h]) - b_col[h] + i_col[h] - per_seq(m_new[h]) for h in H]
    if padded:
        w_arg = [jnp.where(valid, x, -1e30) for x in w_arg]
    kw = [kf[h] * jnp.exp(w_arg[h]) for h in H]
    kw16 = [x.astype(BF16) for x in kw]
    decay = [[jnp.exp(b_last[h][n] + m_all[n][:, h:h + 1] - m_new[h][n]) for h in H] for n in N]
    kv = [[lax.dot_general(kw16[h][seq(n)], v[h][seq(n)], TN_DIMS, preferred_element_type=F32) for h in H]
          for n in N]
    for i, bs in enumerate(levels):
        pair = jnp.logical_and(row // (2 * bs) == col // (2 * bs),
                               jnp.logical_and((row // bs) % 2 == 1, (col // bs) % 2 == 0))
        for h in H:
            a[h] = jnp.where(pair, sc[i][h], a[h])
    o = [o_inter[h] + jnp.dot(a[h].astype(BF16), vb[h], preferred_element_type=F32) for h in H]
    g_last = last_rows(g)
    dec_arg = per_seq(g_last) - g
    if padded:
        dec_arg = jnp.where(valid, dec_arg, -1e30)
    kdec = (kb * jnp.exp(dec_arg)).astype(BF16)
    eg_last = [jnp.exp(x) for x in g_last]
    vk = [[lax.dot_general(vb[h][seq(n)], kdec[seq(n), tile(0, h)], TN_DIMS, preferred_element_type=F32)
           for h in H] for n in N]
    hh = [(sv[h] + inter[h] * qC[h]) / jnp.maximum(jnp.abs(den[h]), jnp.exp(-m_t[h])) for h in H]
    for h in H:
        hn = hh[h] * lax.rsqrt(jnp.mean(hh[h] * hh[h], axis=-1, keepdims=True) + NORM_EPS) * nga_ref[:, tile(0, h)]
        out_a = jax.nn.sigmoid(cols(OA0 + LANES * h, LANES)) * hn
        h_ref[:, :, tile(0, h)] = out_a.reshape(NB, L, LANES)
    for h in H:
        on = o[h] * lax.rsqrt(jnp.mean(o[h] * o[h], axis=-1, keepdims=True) + NORM_EPS) * ngb_ref[:, tile(0, h)]
        gr = cols(GB0 + LANES * h, LANES)
        h_ref[:, :, tile(MIX, h)] = (gr * jax.nn.sigmoid(gr) * on).reshape(NB, L, LANES)
    lane1 = lax.broadcasted_iota(jnp.int32, (1, LANES), 1)
    for n in N:
        for h in H:
            C_s[n, h] = decay[n][h] * C_prev[n][h] + kv[n][h]
            n_s[n, h] = decay[n][h] * n_prev[n][h] + jnp.sum(kw[h][seq(n)], axis=0, keepdims=True)
            St_s[n, h] = eg_last[n][:, tile(0, h)] * St[n][h] + vk[n][h]
        m_upd = m_all[n]
        for h in H:
            m_upd = jnp.where(lane1 == h, m_new[h][n], m_upd)
        m_s[n] = m_upd

    @pl.when(c == nc - 1)
    def _finish():
        for n in N:
            for h in H:
                C_out[n, h] = C_s[n, h, 0:DK_A, :]
                S_out[n, h] = St_s[n, h].T
        n_out[...] = n_s[...]
        m_out[...] = m_s[...]


def _scan(proj, gbias, lb, nga, ngb, state, stacked, *, layer, NB, n_chunks, L, last):
    batch = proj.shape[0]
    has_state = state is not None
    const2 = lambda b, c: (0, 0)
    in_specs = [
        pl.BlockSpec((NB, L, N_PROJ), lambda b, c: (b, c, 0)),
        pl.BlockSpec((2, LANES), const2),
        pl.BlockSpec((1, MIX), const2),
        pl.BlockSpec((1, MIX), const2),
        pl.BlockSpec((1, MIX), const2),
    ]
    args = [proj, gbias, lb, nga, ngb]
    st4 = lambda b, c: (b, 0, 0, 0)
    st3 = lambda b, c: (b, 0, 0)
    if has_state:
        in_specs += [
            pl.BlockSpec((NB, N_HEADS, DK_A, D_HEAD), st4),
            pl.BlockSpec((NB, N_HEADS, 1, LANES), st4),
            pl.BlockSpec((NB, 1, LANES), st3),
            pl.BlockSpec((NB, N_HEADS, D_HEAD, D_HEAD), st4),
        ]
        args += list(state)
    aliases = {len(args) + i: 1 + i for i in range(4)}
    in_specs += [pl.BlockSpec(memory_space=pl.ANY)] * 4
    args += list(stacked)
    lst4 = lambda b, c: (layer, b, 0, 0, 0)
    lst3 = lambda b, c: (layer, b, 0, 0)
    out_specs = [
        pl.BlockSpec((NB, L, D_MODEL), lambda b, c: (b, c, 0)),
        pl.BlockSpec((None, NB, N_HEADS, DK_A, D_HEAD), lst4),
        pl.BlockSpec((None, NB, N_HEADS, 1, LANES), lst4),
        pl.BlockSpec((None, NB, 1, LANES), lst3),
        pl.BlockSpec((None, NB, N_HEADS, D_HEAD, D_HEAD), lst4),
    ]
    out_shape = [jax.ShapeDtypeStruct((batch, n_chunks * L, D_MODEL), F32)] + [
        jax.ShapeDtypeStruct(a.shape, a.dtype) for a in stacked]
    scratch = [
        pltpu.VMEM((NB, N_HEADS, D_HEAD, D_HEAD), F32),
        pltpu.VMEM((NB, N_HEADS, 1, LANES), F32),
        pltpu.VMEM((NB, 1, LANES), F32),
        pltpu.VMEM((NB, N_HEADS, D_HEAD, D_HEAD), F32),
    ]
    kern = functools.partial(_scan_kernel, NB=NB, L=L, last=last, has_state=has_state)
    res = pl.pallas_call(
        kern, grid=(batch // NB, n_chunks), in_specs=in_specs, out_specs=out_specs, out_shape=out_shape,
        scratch_shapes=scratch, input_output_aliases=aliases,
        compiler_params=_params("parallel", "arbitrary"),
        name="scan_state" if has_state else "scan_prompt")(*args)
    return res[0], tuple(res[1:])


def _stacked_states(batch):
    return (jnp.zeros((DEPTH, batch, N_HEADS, DK_A, D_HEAD), F32),
            jnp.zeros((DEPTH, batch, N_HEADS, 1, LANES), F32),
            jnp.zeros((DEPTH, batch, 1, LANES), F32),
            jnp.zeros((DEPTH, batch, N_HEADS, D_HEAD, D_HEAD), F32))


def _post_mixer_kernel(x_ref, h_ref, p_ref, wout_ref, g1_ref, b1_ref, wpg_ref, wpp_ref, wr2_ref, wrh_ref, br_ref,
                       x1_ref, resid_ref, idx_ref, rank_ref, gate_ref, cnt_ref, carry_s):
    i = pl.program_id(0)

    @pl.when(i == 0)
    def _init():
        carry_s[...] = jnp.zeros(carry_s.shape, F32)

    mix = jnp.dot(h_ref[...].astype(BF16), wout_ref[...], preferred_element_type=F32)
    x1 = _layernorm_rows(DEEPNORM_ALPHA * x_ref[...] + mix, g1_ref[...], b1_ref[...])
    x1b = x1.astype(BF16)
    x1_ref[...] = x1
    x1_lo = (x1 - x1b.astype(F32)).astype(BF16)
    l2 = jnp.dot(x1b, wr2_ref[...], preferred_element_type=F32)
    logits = (l2[:, :LANES] + l2[:, LANES:]
              + jnp.dot(x1_lo, wrh_ref[...], preferred_element_type=F32) + br_ref[...])

    tm = logits.shape[0]
    lane = lax.broadcasted_iota(jnp.int32, (tm, LANES), 1)
    lane_f = lane.astype(F32)
    l = jnp.where(lane < N_EXPERTS, logits, -jnp.inf)
    vals, idxs, onehots = [], [], []
    for _ in range(TOP_K):
        mx = jnp.max(l, axis=1, keepdims=True)
        ix = jnp.min(jnp.where(l == mx, lane_f, float(LANES)), axis=1, keepdims=True)
        sel = lane_f == ix
        vals.append(mx)
        idxs.append(ix)
        onehots.append(sel.astype(F32))
        l = jnp.where(sel, -jnp.inf, l)
    gate = jax.nn.sigmoid(jnp.dot(x1b, wpg_ref[...], preferred_element_type=F32))
    pp = jnp.dot(p_ref[...].astype(BF16), wpp_ref[...], preferred_element_type=F32)
    resid_ref[...] = DEEPNORM_ALPHA * x1 + gate * pp

    w = [jnp.exp(val - vals[0]) for val in vals]
    tot = w[0] + w[1] + w[2] + w[3]
    oh_all = onehots[0] + onehots[1] + onehots[2] + onehots[3]
    row = lax.broadcasted_iota(jnp.int32, (tm, tm), 0)
    col = lax.broadcasted_iota(jnp.int32, (tm, tm), 1)
    earlier = (row > col).astype(BF16)
    prefix = jnp.dot(earlier, oh_all.astype(BF16), preferred_element_type=F32) + carry_s[...]
    idx_o = jnp.zeros((tm, LANES), F32)
    rank_o = jnp.zeros((tm, LANES), F32)
    gate_o = jnp.zeros((tm, LANES), F32)
    for k in range(TOP_K):
        rank_k = jnp.sum(onehots[k] * prefix, axis=1, keepdims=True)
        idx_o = jnp.where(lane == k, idxs[k], idx_o)
        rank_o = jnp.where(lane == k, rank_k, rank_o)
        gate_o = jnp.where(lane == k, w[k] / tot, gate_o)
    idx_ref[...] = idx_o.astype(jnp.int32)
    rank_ref[...] = rank_o.astype(jnp.int32)
    gate_ref[...] = gate_o
    carry_s[...] = carry_s[...] + jnp.sum(oh_all, axis=0, keepdims=True)
    cnt_ref[...] = carry_s[...].astype(jnp.int32)


def _post_mixer(x, h, p, wout, g1, b1, wpg, wpp, wr2, wrh, br):
    n, d = x.shape
    row = lambda w: pl.BlockSpec((ROW_TILE, w), lambda i: (i, 0))
    full = lambda a: pl.BlockSpec(a.shape, lambda i: (0, 0))
    g1 = g1.reshape(1, d)
    b1 = b1.reshape(1, d)
    return pl.pallas_call(
        _post_mixer_kernel, grid=(n // ROW_TILE,),
        in_specs=[row(d), row(d), row(D_PLE), full(wout), full(g1), full(b1), full(wpg), full(wpp),
                  full(wr2), full(wrh), full(br)],
        out_specs=[row(d), row(d), row(LANES), row(LANES), row(LANES), pl.BlockSpec((1, LANES), lambda i: (0, 0))],
        out_shape=[jax.ShapeDtypeStruct((n, d), F32), jax.ShapeDtypeStruct((n, d), F32),
                   jax.ShapeDtypeStruct((n, LANES), jnp.int32), jax.ShapeDtypeStruct((n, LANES), jnp.int32),
                   jax.ShapeDtypeStruct((n, LANES), F32), jax.ShapeDtypeStruct((1, LANES), jnp.int32)],
        scratch_shapes=[pltpu.VMEM((1, LANES), F32)],
        compiler_params=_params("arbitrary"), name="post_mixer")(x, h, p, wout, g1, b1, wpg, wpp, wr2, wrh, br)


def _expert_kernel(blk_ref, e_ref, lo_ref, hi_ref, nv_ref, x_ref, wg_ref, bg_ref, wu_ref, bu_ref, wd_ref,
                   bd_ref, o_ref, wg_s, wu_s, wd_s):
    v = pl.program_id(0)
    prev = jnp.maximum(v - 1, 0)
    valid = v < nv_ref[0]
    new_expert = jnp.logical_or(v == 0, e_ref[v] != e_ref[prev])
    new_block = jnp.logical_or(v == 0, blk_ref[v] != blk_ref[prev])

    @pl.when(jnp.logical_and(valid, new_expert))
    def _cast_weights():
        wg_s[...] = wg_ref[...].astype(BF16)
        wu_s[...] = wu_ref[...].astype(BF16)
        wd_s[...] = wd_ref[...].astype(BF16)

    @pl.when(valid)
    def _compute():
        xb = x_ref[...].astype(BF16)
        gate = jnp.dot(xb, wg_s[...], preferred_element_type=F32) + bg_ref[0]
        up = jnp.dot(xb, wu_s[...], preferred_element_type=F32) + bu_ref[0]
        gate = jnp.minimum(gate, SWIGLU_LIMIT)
        up = jnp.clip(up, -SWIGLU_LIMIT, SWIGLU_LIMIT)
        glu = gate * jax.nn.sigmoid(gate * SWIGLU_ALPHA)
        act = ((up + 1.0) * glu).astype(BF16)
        y = jnp.dot(act, wd_s[...], preferred_element_type=F32) + bd_ref[0]
        rows = blk_ref[v] * EXPERT_ROWS + lax.broadcasted_iota(jnp.int32, (EXPERT_ROWS, 1), 0)
        mine = jnp.logical_and(rows >= lo_ref[v], rows < hi_ref[v])

        @pl.when(new_block)
        def _first():
            o_ref[...] = jnp.where(mine, y, 0.0)

        @pl.when(jnp.logical_not(new_block))
        def _again():
            o_ref[...] = jnp.where(mine, y, o_ref[...])


def _experts(layer, blk_v, e_v, lo_v, hi_v, n_vis, xin, wg, bg, wu, bu, wd, bd):
    n_rows = xin.shape[0]
    d = wg.shape[-1]
    n_visits = blk_v.shape[0]
    xspec = pl.BlockSpec((EXPERT_ROWS, d), lambda v, blk, e, lo, hi, nv: (blk[v], 0))
    ospec = pl.BlockSpec((EXPERT_ROWS, d), lambda v, blk, e, lo, hi, nv: (blk[v], 0))
    wspec = pl.BlockSpec((None, None, d, d), lambda v, blk, e, lo, hi, nv: (layer, e[v], 0, 0))
    bspec = pl.BlockSpec((None, 1, 1, d), lambda v, blk, e, lo, hi, nv: (layer, e[v], 0, 0))
    grid_spec = pltpu.PrefetchScalarGridSpec(
        num_scalar_prefetch=5, grid=(n_visits,),
        in_specs=[xspec, wspec, bspec, wspec, bspec, wspec, bspec], out_specs=ospec,
        scratch_shapes=[pltpu.VMEM((d, d), BF16)] * 3)
    return pl.pallas_call(
        _expert_kernel, grid_spec=grid_spec, out_shape=jax.ShapeDtypeStruct((n_rows, d), F32),
        compiler_params=_params("arbitrary"), name="experts")(
            blk_v, e_v, lo_v, hi_v, n_vis, xin, wg, bg, wu, bu, wd, bd)


def _moe(layer, x1, routing, wg, bg, wu, bu, wd, bd):
    n_tok = x1.shape[0]
    n_assign = n_tok * TOP_K
    n_blocks = n_assign // EXPERT_ROWS
    idx_o, rank_o, gate_o, cnt = routing
    counts = cnt[0, :N_EXPERTS]
    ends = jnp.cumsum(counts)
    offs = ends - counts
    experts = jnp.arange(N_EXPERTS, dtype=jnp.int32)
    top_i = idx_o[:, :TOP_K]
    off_tok = jnp.sum(jnp.where(top_i[:, :, None] == experts, offs, 0), axis=-1)
    pos = (off_tok + rank_o[:, :TOP_K]).reshape(n_assign)
    assert n_tok <= (1 << TOK_BITS) and n_assign <= (1 << (32 - TOK_BITS))
    tok = jnp.arange(n_assign, dtype=jnp.uint32) // TOP_K
    key = lax.sort(pos.astype(jnp.uint32) * (1 << TOK_BITS) + tok)
    xin = x1[(key & ((1 << TOK_BITS) - 1)).astype(jnp.int32)]

    first_blk = offs // EXPERT_ROWS
    last_blk = (ends - 1) // EXPERT_ROWS
    nvis = jnp.where(counts > 0, last_blk - first_blk + 1, 0)
    vend = jnp.cumsum(nvis)
    vstart = vend - nvis
    n_visits = n_blocks + N_EXPERTS - 1
    v = jnp.arange(n_visits, dtype=jnp.int32)
    e_v = jnp.sum((vend[None, :] <= v[:, None]).astype(jnp.int32), axis=1)
    e_last = jnp.max(jnp.where(counts > 0, experts, 0))
    e_v = jnp.minimum(e_v, e_last)
    pick = lambda tab: jnp.sum(jnp.where(e_v[:, None] == experts, tab, 0), axis=-1)
    blk_v = jnp.minimum(pick(first_blk) + v - pick(vstart), n_blocks - 1)
    y_rows = _experts(layer, blk_v.astype(jnp.int32), e_v.astype(jnp.int32), pick(offs).astype(jnp.int32),
                      pick(ends).astype(jnp.int32), vend[-1:].astype(jnp.int32), xin, wg, bg, wu, bu, wd, bd)
    return y_rows[pos.reshape(n_tok, TOP_K).T.reshape(n_assign)], gate_o


def _pack_w_in(w_in):
    qa, ka, va, oa, ia, fa, qb, fb, ib, gb = jnp.split(w_in, SPLIT_POINTS, axis=-1)

    def pad_heads(w):
        w = w.reshape(w.shape[:-1] + (N_HEADS, DK_A))
        w = jnp.pad(w, ((0, 0),) * (w.ndim - 1) + ((0, D_HEAD - DK_A),))
        return w.reshape(w.shape[:-2] + (N_HEADS * D_HEAD,))

    def pad_gate(w):
        return jnp.pad(w, ((0, 0),) * (w.ndim - 1) + ((0, LANES - N_HEADS),))

    cols = [pad_heads(qa), pad_heads(ka * (DK_A ** -0.5)), va, oa, qb, fb, ib, gb, pad_gate(ia), pad_gate(fa)]
    return jnp.concatenate(cols, axis=-1).astype(BF16)


def kernel(x_prompt, x_sample, state_mlstm_C, state_mlstm_n, state_mlstm_m, state_hgrn_S, p_prompt, p_sample,
           ln_in_g, ln_in_b, w_in, mlstm_ig_bias, mlstm_fg_bias, mlstm_norm_g, hgrn_lb_logits, hgrn_norm_g,
           w_out, ln1_g, ln1_b, w_router, b_router, w_gate, b_gate, w_up, b_up, w_down, b_down,
           w_ple_gate, w_ple_proj, ln2_g, ln2_b):
    bp, tp, d = x_prompt.shape
    bs, ts, _ = x_sample.shape
    n_p = bp * tp
    n_s = bs * ts
    ts_pad = SUBLANES
    assert tp % CHUNK == 0 and ts <= ts_pad and ts % CHUNK != 0

    lb_soft = jax.nn.softmax(hgrn_lb_logits.astype(F32), axis=0)
    lower_bounds = jnp.cumsum(lb_soft, axis=0) - lb_soft[0]

    w_in_p = _pack_w_in(w_in)
    gbias = jnp.stack([jnp.pad(mlstm_ig_bias, ((0, 0), (0, LANES - N_HEADS))),
                       jnp.pad(mlstm_fg_bias, ((0, 0), (0, LANES - N_HEADS)))], axis=1)
    w_out_b = w_out.astype(BF16)
    w_pg_b = w_ple_gate.astype(BF16)
    w_pp_b = w_ple_proj.astype(BF16)
    w_r = jnp.pad(w_router, ((0, 0), (0, 0), (0, LANES - N_EXPERTS)))
    w_r_hi = w_r.astype(BF16)
    w_r2 = jnp.concatenate([w_r_hi, (w_r - w_r_hi.astype(F32)).astype(BF16)], axis=-1)
    b_r = jnp.pad(b_router, ((0, 0), (0, LANES - N_EXPERTS))).reshape(DEPTH, 1, LANES)
    n0_pad = jnp.pad(state_mlstm_n, ((0, 0), (0, 0), (0, 0), (0, D_HEAD - DK_A))).reshape(
        DEPTH, bs, N_HEADS, 1, D_HEAD)
    m0_pad = jnp.pad(state_mlstm_m, ((0, 0), (0, 0), (0, LANES - N_HEADS))).reshape(DEPTH, bs, 1, LANES)

    x = jnp.concatenate([x_prompt.reshape(n_p, d), x_sample.reshape(n_s, d)], axis=0)
    p_all = jnp.concatenate([p_prompt.reshape(DEPTH, n_p, D_PLE), p_sample.reshape(DEPTH, n_s, D_PLE)], axis=1)
    x = _ln(x, ln_in_g, ln_in_b)

    assert n_p % ROW_TILE == 0 and n_s % ROW_TILE == 0
    nb_p = LANES // CHUNK
    nb_s = LANES // ts_pad
    st_p = _stacked_states(bp)
    st_s = _stacked_states(bs)
    for l in range(DEPTH):
        lb = lower_bounds[l].reshape(1, MIX)
        nga = mlstm_norm_g[l].reshape(1, MIX)
        ngb = hgrn_norm_g[l].reshape(1, MIX)
        proj_p = _inproj(x, w_in_p[l], 0, n_p).reshape(bp, tp, N_PROJ)
        h_p, st_p = _scan(proj_p, gbias[l], lb, nga, ngb, None, st_p, layer=l,
                          NB=nb_p, n_chunks=tp // CHUNK, L=CHUNK, last=CHUNK - 1)
        proj_s = jnp.pad(_inproj(x, w_in_p[l], n_p, n_s).reshape(bs, ts, N_PROJ), ((0, 0), (0, ts_pad - ts), (0, 0)))
        h_s, st_s = _scan(proj_s, gbias[l], lb, nga, ngb,
                          (state_mlstm_C[l], n0_pad[l], m0_pad[l], state_hgrn_S[l]), st_s, layer=l,
                          NB=nb_s, n_chunks=1, L=ts_pad, last=ts - 1)
        h_all = jnp.concatenate([h_p.reshape(n_p, d), h_s[:, :ts].reshape(n_s, d)], axis=0)
        x1, resid, *routing = _post_mixer(x, h_all, p_all[l], w_out_b[l], ln1_g[l], ln1_b[l],
                                          w_pg_b[l], w_pp_b[l], w_r2[l], w_r_hi[l], b_r[l])
        y_k, gates = _moe(l, x1, routing, w_gate, b_gate.reshape(DEPTH, N_EXPERTS, 1, d),
                          w_up, b_up.reshape(DEPTH, N_EXPERTS, 1, d),
                          w_down, b_down.reshape(DEPTH, N_EXPERTS, 1, d))
        x = _combine_ln(resid, y_k, gates, ln2_g[l], ln2_b[l])

    y_prompt = x[:n_p].reshape(bp, tp, d)
    y_sample = x[n_p:].reshape(bs, ts, d)
    unpad = lambda st: (st[0], st[1][:, :, :, 0, :DK_A], st[2][:, :, 0, :N_HEADS], st[3])
    return (y_prompt, y_sample) + unpad(st_p) + unpad(st_s)
```

```python
import functools

import jax
import jax.numpy as jnp
import numpy as np
from jax import lax
from jax.experimental import pallas as pl
from jax.experimental.pallas import tpu as pltpu

F32 = jnp.float32
BF16 = jnp.bfloat16

D_MODEL = 1024
DEPTH = 4
D_PLE = 256
N_HEADS = 4
DK_A = 64
D_HEAD = 128
MIX = 512
N_EXPERTS = 32
TOP_K = 4
SWIGLU_LIMIT = 7.0
SWIGLU_ALPHA = 1.702
CHUNK = 64
LN_EPS = 1e-5
NORM_EPS = 1e-6
LB_FLOOR = 1e-20
DEEPNORM_ALPHA = (2 * DEPTH) ** 0.25
SPLIT_SIZES = (256, 256, 512, 512, 4, 4, 512, 512, 512, 512)
SPLIT_POINTS = tuple(int(s) for s in np.cumsum(SPLIT_SIZES)[:-1])

LANES = 128
SUBLANES = 8
VMEM_LIMIT = 56 * 1024 * 1024

QA0, KA0, VA0, OA0 = 0, 512, 1024, 1536
QB0, FB0, IB0, GB0 = 2048, 2560, 3072, 3584
IG0, FG0 = 4096, 4224
N_PROJ = 4352

ROW_TILE = 256
EXPERT_ROWS = 256
TOK_BITS = 15

NT_DIMS = (((1,), (1,)), ((), ()))
TN_DIMS = (((0,), (0,)), ((), ()))


def _params(*sem):
    return pltpu.CompilerParams(dimension_semantics=sem, vmem_limit_bytes=VMEM_LIMIT)


def _log_sigmoid(x):
    return jnp.minimum(x, 0.0) - jnp.log(1.0 + jnp.exp(-jnp.abs(x)))


def _split3(x):
    x1 = x.astype(BF16)
    r1 = x - x1.astype(F32)
    x2 = r1.astype(BF16)
    x3 = (r1 - x2.astype(F32)).astype(BF16)
    return x1, x2, x3


def _dot01(sel, x):
    return sum(jnp.dot(sel, t, preferred_element_type=F32) for t in _split3(x))


def _layernorm_rows(x, g, b):
    mu = jnp.mean(x, axis=-1, keepdims=True)
    xc = x - mu
    var = jnp.mean(xc * xc, axis=-1, keepdims=True)
    return xc * lax.rsqrt(var + LN_EPS) * g + b


def _two_sources(n_a, n_b, width, lead=()):
    ta = n_a // ROW_TILE
    assert n_a % ROW_TILE == 0 and n_b % ROW_TILE == 0
    block = (None,) * len(lead) + (ROW_TILE, width)
    spec_a = pl.BlockSpec(block, lambda i: lead + (jnp.minimum(i, ta - 1), 0))
    spec_b = pl.BlockSpec(block, lambda i: lead + (jnp.maximum(i - ta, 0), 0))
    return spec_a, spec_b, ta


def _ln_kernel(xa_ref, xb_ref, g_ref, b_ref, o_ref, *, tiles_a):
    x = jnp.where(pl.program_id(0) < tiles_a, xa_ref[...], xb_ref[...])
    o_ref[...] = _layernorm_rows(x, g_ref[...], b_ref[...])


def _ln(xa, xb, g, b):
    d = xa.shape[1]
    n = xa.shape[0] + xb.shape[0]
    spec_a, spec_b, ta = _two_sources(xa.shape[0], xb.shape[0], d)
    vec = pl.BlockSpec((1, d), lambda i: (0, 0))
    return pl.pallas_call(
        functools.partial(_ln_kernel, tiles_a=ta), grid=(n // ROW_TILE,), in_specs=[spec_a, spec_b, vec, vec],
        out_specs=pl.BlockSpec((ROW_TILE, d), lambda i: (i, 0)),
        out_shape=jax.ShapeDtypeStruct((n, d), F32), compiler_params=_params("parallel"),
        name="ln_in")(xa, xb, g.reshape(1, d), b.reshape(1, d))


def _combine_ln_kernel(resid_ref, y0_ref, y1_ref, y2_ref, y3_ref, gate_ref, g_ref, beta_ref, o_ref):
    acc = resid_ref[...]
    for k, y_ref in enumerate((y0_ref, y1_ref, y2_ref, y3_ref)):
        acc = acc + gate_ref[:, k:k + 1] * y_ref[...]
    o_ref[...] = _layernorm_rows(acc, g_ref[...], beta_ref[...])


def _combine_ln(resid, y_k, gates, g, beta):
    n, d = resid.shape
    nt = n // ROW_TILE
    row = lambda w: pl.BlockSpec((ROW_TILE, w), lambda i: (i, 0))
    yk = lambda k: pl.BlockSpec((ROW_TILE, d), lambda i: (k * nt + i, 0))
    vec = pl.BlockSpec((1, d), lambda i: (0, 0))
    return pl.pallas_call(
        _combine_ln_kernel, grid=(nt,),
        in_specs=[row(d)] + [yk(k) for k in range(TOP_K)] + [row(LANES), vec, vec],
        out_specs=row(d), out_shape=jax.ShapeDtypeStruct((n, d), F32), compiler_params=_params("parallel"),
        name="combine_ln2")(resid, y_k, y_k, y_k, y_k, gates, g.reshape(1, d), beta.reshape(1, d))


def _inproj_kernel(x_ref, w_ref, o_ref):
    o_ref[...] = jnp.dot(x_ref[...].astype(BF16), w_ref[...], preferred_element_type=F32)


def _inproj(x, w, row0, n_rows):
    d = x.shape[1]
    t0 = row0 // ROW_TILE
    return pl.pallas_call(
        _inproj_kernel, grid=(n_rows // ROW_TILE,),
        in_specs=[pl.BlockSpec((ROW_TILE, d), lambda i: (t0 + i, 0)),
                  pl.BlockSpec((d, N_PROJ), lambda i: (0, 0))],
        out_specs=pl.BlockSpec((ROW_TILE, N_PROJ), lambda i: (i, 0)),
        out_shape=jax.ShapeDtypeStruct((n_rows, N_PROJ), F32), compiler_params=_params("parallel"),
        name="in_proj")(x, w)


def _scan_kernel(*refs, NB, L, last, has_state):
    it = iter(refs)
    proj_ref, gbias_ref, lb_ref, nga_ref, ngb_ref = (next(it) for _ in range(5))
    if has_state:
        C0_ref, n0_ref, m0_ref, S0_ref = (next(it) for _ in range(4))
    for _ in range(4):
        next(it)
    h_ref, C_out, n_out, m_out, S_out = (next(it) for _ in range(5))
    C_s, n_s, m_s, St_s = (next(it) for _ in range(4))

    c = pl.program_id(1)
    nc = pl.num_programs(1)
    H = range(N_HEADS)
    N = range(NB)
    R = NB * L
    assert R == LANES

    @pl.when(c == 0)
    def _init():
        if has_state:
            for n in N:
                for h in H:
                    C_s[n, h, 0:DK_A, :] = C0_ref[n, h]
                    C_s[n, h, DK_A:D_HEAD, :] = jnp.zeros((D_HEAD - DK_A, D_HEAD), F32)
                    St_s[n, h] = S0_ref[n, h].T
            n_s[...] = n0_ref[...]
            m_s[...] = m0_ref[...]
        else:
            C_s[...] = jnp.zeros(C_s.shape, F32)
            n_s[...] = jnp.zeros(n_s.shape, F32)
            m_s[...] = jnp.zeros(m_s.shape, F32)
            St_s[...] = jnp.zeros(St_s.shape, F32)

    row = lax.broadcasted_iota(jnp.int32, (R, R), 0)
    col = lax.broadcasted_iota(jnp.int32, (R, R), 1)
    causal = jnp.logical_and(row // L == col // L, row >= col)
    tril = causal.astype(BF16)
    ones_r = jnp.ones((R, LANES), BF16)
    lane = lax.broadcasted_iota(jnp.int32, (R, LANES), 1)
    step = lax.broadcasted_iota(jnp.int32, (R, 1), 0) % L
    valid = step <= last
    padded = last < L - 1
    tile = lambda base, h: slice(base + LANES * h, base + LANES * (h + 1))
    seq = lambda n: slice(n * L, (n + 1) * L)
    cols = lambda base, width: proj_ref[:, :, base:base + width].reshape(R, width)
    per_seq = lambda vals: jnp.concatenate([jnp.broadcast_to(x, (L, x.shape[-1])) for x in vals], axis=0)
    last_rows = lambda x: [x[n * L + last:n * L + last + 1] for n in N]

    i_t = cols(IG0, LANES) + gbias_ref[0:1, :]
    f_t = _log_sigmoid(cols(FG0, LANES) + gbias_ref[1:2, :])
    zf = cols(FB0, MIX)
    lb = lb_ref[...]
    la = jnp.log(jnp.maximum(lb, LB_FLOOR))
    bb = jnp.log1p(-lb) + _log_sigmoid(zf)
    f_log = jnp.maximum(la, bb) + jnp.log(1.0 + jnp.exp(-jnp.abs(la - bb)))
    b_t = _dot01(tril, f_t)
    g = _dot01(tril, f_log)
    m_all = [m_s[n] for n in N]
    C_prev = [[C_s[n, h] for h in H] for n in N]
    n_prev = [[n_s[n, h] for h in H] for n in N]
    St = [[St_s[n, h] for h in H] for n in N]
    qf = [cols(QA0 + LANES * h, LANES) for h in H]
    kf = [cols(KA0 + LANES * h, LANES) for h in H]
    q = [x.astype(BF16) for x in qf]
    k = [x.astype(BF16) for x in kf]
    v = [cols(VA0 + LANES * h, LANES).astype(BF16) for h in H]
    kb = (1.0 - lb) * jax.nn.sigmoid(-zf)
    qr = cols(QB0, MIX)
    qb = qr * jax.nn.sigmoid(qr)
    vb = [cols(IB0 + LANES * h, LANES).astype(BF16) for h in H]
    qb16 = qb.astype(BF16)
    kb16 = kb.astype(BF16)
    r_t = i_t - b_t
    zero16 = jnp.zeros((R, LANES), BF16)
    r_all = sum(
        lax.dot_general(ones_r, jnp.concatenate([jnp.where(lane == h, term, zero16) for h in H], axis=0),
                        NT_DIMS, preferred_element_type=F32)
        for term in _split3(r_t))
    r_m = [r_all[:, R * h:R * (h + 1)] for h in H]
    qk = [lax.dot_general(q[h], k[h], NT_DIMS, preferred_element_type=F32) for h in H]
    qC = [jnp.concatenate([jnp.dot(q[h][seq(n)], C_prev[n][h].astype(BF16), preferred_element_type=F32)
                           for n in N], axis=0) for h in H]
    levels = []
    bs = 1
    while bs < L:
        levels.append(bs)
        bs *= 2
    small = [bs for bs in levels if bs < SUBLANES]
    g_anchor = {}
    if small:
        sel = jnp.concatenate(
            [(col == (row // (2 * bs)) * (2 * bs) + bs).astype(BF16) for bs in small], axis=0)
        picked = _dot01(sel, g)
        for i, bs in enumerate(small):
            g_anchor[bs] = picked[i * R:(i + 1) * R]
    for bs in levels:
        if bs >= SUBLANES:
            g_anchor[bs] = jnp.concatenate(
                [jnp.broadcast_to(g[p0 + bs:p0 + bs + 1], (2 * bs, MIX)) for p0 in range(0, R, 2 * bs)], axis=0)
    qg = (qb * jnp.exp(g)).astype(BF16)
    o_inter = [jnp.concatenate([lax.dot_general(qg[seq(n), tile(0, h)], St[n][h].astype(BF16), NT_DIMS,
                                                preferred_element_type=F32) for n in N], axis=0) for h in H]
    a = [jnp.where(row == col, lax.dot_general(qb16[:, tile(0, h)], kb16[:, tile(0, h)], NT_DIMS,
                                                preferred_element_type=F32), 0.0) for h in H]
    qn = [jnp.sum(qf[h] * per_seq([n_prev[n][h] for n in N]), axis=1, keepdims=True) for h in H]
    b_col = [b_t[:, h:h + 1] for h in H]
    i_col = [i_t[:, h:h + 1] for h in H]
    m_prev = [per_seq([m_all[n][:, h:h + 1] for n in N]) for h in H]
    cm = [jnp.max(jnp.where(causal, r_m[h], -jnp.inf), axis=1, keepdims=True) for h in H]
    m_t = [b_col[h] + jnp.maximum(m_prev[h], cm[h]) for h in H]
    inter = [jnp.exp(b_col[h] + m_prev[h] - m_t[h]) for h in H]
    d = [jnp.exp(jnp.where(causal, (b_col[h] - m_t[h]) + r_m[h], -1e30)) for h in H]
    s = [qk[h] * d[h] for h in H]
    log2e = 1.4426950408889634
    e = [jnp.exp2((g - g_anchor[bs]) * jnp.where((step // bs) % 2 == 1, log2e, -log2e)) for bs in levels]
    qe = [(qb * x).astype(BF16) for x in e]
    ke = [(kb * x).astype(BF16) for x in e]
    sv = [jnp.dot(s[h].astype(BF16), v[h], preferred_element_type=F32) for h in H]
    sc = [[lax.dot_general(qe[i][:, tile(0, h)], ke[i][:, tile(0, h)], NT_DIMS, preferred_element_type=F32)
           for h in H] for i in range(len(levels))]
    den = [jnp.sum(s[h], axis=1, keepdims=True) + inter[h] * qn[h] for h in H]
    m_new = [last_rows(m_t[h]) for h in H]
    b_last = [last_rows(b_col[h]) for h in H]
    w_arg = [per_seq(b_last[h]) - b_col[h] + i_col[h] - per_seq(m_new[h]) for h in H]
    if padded:
        w_arg = [jnp.where(valid, x, -1e30) for x in w_arg]
    kw = [kf[h] * jnp.exp(w_arg[h]) for h in H]
    kw16 = [x.astype(BF16) for x in kw]
    decay = [[jnp.exp(b_last[h][n] + m_all[n][:, h:h + 1] - m_new[h][n]) for h in H] for n in N]
    kv = [[lax.dot_general(kw16[h][seq(n)], v[h][seq(n)], TN_DIMS, preferred_element_type=F32) for h in H]
          for n in N]
    for i, bs in enumerate(levels):
        pair = jnp.logical_and(row // (2 * bs) == col // (2 * bs),
                               jnp.logical_and((row // bs) % 2 == 1, (col // bs) % 2 == 0))
        for h in H:
            a[h] = jnp.where(pair, sc[i][h], a[h])
    o = [o_inter[h] + jnp.dot(a[h].astype(BF16), vb[h], preferred_element_type=F32) for h in H]
    g_last = last_rows(g)
    dec_arg = per_seq(g_last) - g
    if padded:
        dec_arg = jnp.where(valid, dec_arg, -1e30)
    kdec = (kb * jnp.exp(dec_arg)).astype(BF16)
    eg_last = [jnp.exp(x) for x in g_last]
    vk = [[lax.dot_general(vb[h][seq(n)], kdec[seq(n), tile(0, h)], TN_DIMS, preferred_element_type=F32)
           for h in H] for n in N]
    hh = [(sv[h] + inter[h] * qC[h]) / jnp.maximum(jnp.abs(den[h]), jnp.exp(-m_t[h])) for h in H]
    for h in H:
        hn = hh[h] * lax.rsqrt(jnp.mean(hh[h] * hh[h], axis=-1, keepdims=True) + NORM_EPS) * nga_ref[:, tile(0, h)]
        out_a = jax.nn.sigmoid(cols(OA0 + LANES * h, LANES)) * hn
        h_ref[:, :, tile(0, h)] = out_a.reshape(NB, L, LANES)
    for h in H:
        on = o[h] * lax.rsqrt(jnp.mean(o[h] * o[h], axis=-1, keepdims=True) + NORM_EPS) * ngb_ref[:, tile(0, h)]
        gr = cols(GB0 + LANES * h, LANES)
        h_ref[:, :, tile(MIX, h)] = (gr * jax.nn.sigmoid(gr) * on).reshape(NB, L, LANES)
    lane1 = lax.broadcasted_iota(jnp.int32, (1, LANES), 1)
    for n in N:
        for h in H:
            C_s[n, h] = decay[n][h] * C_prev[n][h] + kv[n][h]
            n_s[n, h] = decay[n][h] * n_prev[n][h] + jnp.sum(kw[h][seq(n)], axis=0, keepdims=True)
            St_s[n, h] = eg_last[n][:, tile(0, h)] * St[n][h] + vk[n][h]
        m_upd = m_all[n]
        for h in H:
            m_upd = jnp.where(lane1 == h, m_new[h][n], m_upd)
        m_s[n] = m_upd

    @pl.when(c == nc - 1)
    def _finish():
        for n in N:
            for h in H:
                C_out[n, h] = C_s[n, h, 0:DK_A, :]
                S_out[n, h] = St_s[n, h].T
        n_out[...] = n_s[...]
        m_out[...] = m_s[...]


def _scan(proj, gbias, lb, nga, ngb, state, stacked, *, layer, NB, n_chunks, L, last):
    batch = proj.shape[0]
    has_state = state is not None
    const2 = lambda b, c: (0, 0)
    in_specs = [
        pl.BlockSpec((NB, L, N_PROJ), lambda b, c: (b, c, 0)),
        pl.BlockSpec((2, LANES), const2),
        pl.BlockSpec((1, MIX), const2),
        pl.BlockSpec((1, MIX), const2),
        pl.BlockSpec((1, MIX), const2),
    ]
    args = [proj, gbias, lb, nga, ngb]
    st4 = lambda b, c: (b, 0, 0, 0)
    st3 = lambda b, c: (b, 0, 0)
    if has_state:
        in_specs += [
            pl.BlockSpec((NB, N_HEADS, DK_A, D_HEAD), st4),
            pl.BlockSpec((NB, N_HEADS, 1, LANES), st4),
            pl.BlockSpec((NB, 1, LANES), st3),
            pl.BlockSpec((NB, N_HEADS, D_HEAD, D_HEAD), st4),
        ]
        args += list(state)
    aliases = {len(args) + i: 1 + i for i in range(4)}
    in_specs += [pl.BlockSpec(memory_space=pl.ANY)] * 4
    args += list(stacked)
    lst4 = lambda b, c: (layer, b, 0, 0, 0)
    lst3 = lambda b, c: (layer, b, 0, 0)
    out_specs = [
        pl.BlockSpec((NB, L, D_MODEL), lambda b, c: (b, c, 0)),
        pl.BlockSpec((None, NB, N_HEADS, DK_A, D_HEAD), lst4),
        pl.BlockSpec((None, NB, N_HEADS, 1, LANES), lst4),
        pl.BlockSpec((None, NB, 1, LANES), lst3),
        pl.BlockSpec((None, NB, N_HEADS, D_HEAD, D_HEAD), lst4),
    ]
    out_shape = [jax.ShapeDtypeStruct((batch, n_chunks * L, D_MODEL), F32)] + [
        jax.ShapeDtypeStruct(a.shape, a.dtype) for a in stacked]
    scratch = [
        pltpu.VMEM((NB, N_HEADS, D_HEAD, D_HEAD), F32),
        pltpu.VMEM((NB, N_HEADS, 1, LANES), F32),
        pltpu.VMEM((NB, 1, LANES), F32),
        pltpu.VMEM((NB, N_HEADS, D_HEAD, D_HEAD), F32),
    ]
    kern = functools.partial(_scan_kernel, NB=NB, L=L, last=last, has_state=has_state)
    res = pl.pallas_call(
        kern, grid=(batch // NB, n_chunks), in_specs=in_specs, out_specs=out_specs, out_shape=out_shape,
        scratch_shapes=scratch, input_output_aliases=aliases,
        compiler_params=_params("parallel", "arbitrary"),
        name="scan_state" if has_state else "scan_prompt")(*args)
    return res[0], tuple(res[1:])


def _stacked_states(batch):
    return (jnp.zeros((DEPTH, batch, N_HEADS, DK_A, D_HEAD), F32),
            jnp.zeros((DEPTH, batch, N_HEADS, 1, LANES), F32),
            jnp.zeros((DEPTH, batch, 1, LANES), F32),
            jnp.zeros((DEPTH, batch, N_HEADS, D_HEAD, D_HEAD), F32))


def _post_mixer_kernel(x_ref, ha_ref, hb_ref, pa_ref, pb_ref, wout_ref, g1_ref, b1_ref, wpg_ref, wpp_ref, wr2_ref,
                       wrh_ref, br_ref, x1_ref, resid_ref, idx_ref, rank_ref, gate_ref, cnt_ref, carry_s, *, tiles_a):
    i = pl.program_id(0)

    @pl.when(i == 0)
    def _init():
        carry_s[...] = jnp.zeros(carry_s.shape, F32)

    from_a = i < tiles_a
    h = jnp.where(from_a, ha_ref[...], hb_ref[...])
    p = jnp.where(from_a, pa_ref[...], pb_ref[...])
    mix = jnp.dot(h.astype(BF16), wout_ref[...], preferred_element_type=F32)
    x1 = _layernorm_rows(DEEPNORM_ALPHA * x_ref[...] + mix, g1_ref[...], b1_ref[...])
    x1b = x1.astype(BF16)
    x1_ref[...] = x1
    x1_lo = (x1 - x1b.astype(F32)).astype(BF16)
    l2 = jnp.dot(x1b, wr2_ref[...], preferred_element_type=F32)
    logits = (l2[:, :LANES] + l2[:, LANES:]
              + jnp.dot(x1_lo, wrh_ref[...], preferred_element_type=F32) + br_ref[...])

    tm = logits.shape[0]
    lane = lax.broadcasted_iota(jnp.int32, (tm, LANES), 1)
    lane_f = lane.astype(F32)
    l = jnp.where(lane < N_EXPERTS, logits, -jnp.inf)
    vals, idxs, onehots = [], [], []
    for _ in range(TOP_K):
        mx = jnp.max(l, axis=1, keepdims=True)
        ix = jnp.min(jnp.where(l == mx, lane_f, float(LANES)), axis=1, keepdims=True)
        sel = lane_f == ix
        vals.append(mx)
        idxs.append(ix)
        onehots.append(sel.astype(F32))
        l = jnp.where(sel, -jnp.inf, l)
    gate = jax.nn.sigmoid(jnp.dot(x1b, wpg_ref[...], preferred_element_type=F32))
    pp = jnp.dot(p.astype(BF16), wpp_ref[...], preferred_element_type=F32)
    resid_ref[...] = DEEPNORM_ALPHA * x1 + gate * pp

    w = [jnp.exp(val - vals[0]) for val in vals]
    tot = w[0] + w[1] + w[2] + w[3]
    oh_all = onehots[0] + onehots[1] + onehots[2] + onehots[3]
    row = lax.broadcasted_iota(jnp.int32, (tm, tm), 0)
    col = lax.broadcasted_iota(jnp.int32, (tm, tm), 1)
    earlier = (row > col).astype(BF16)
    prefix = jnp.dot(earlier, oh_all.astype(BF16), preferred_element_type=F32) + carry_s[...]
    idx_o = jnp.zeros((tm, LANES), F32)
    rank_o = jnp.zeros((tm, LANES), F32)
    gate_o = jnp.zeros((tm, LANES), F32)
    for k in range(TOP_K):
        rank_k = jnp.sum(onehots[k] * prefix, axis=1, keepdims=True)
        idx_o = jnp.where(lane == k, idxs[k], idx_o)
        rank_o = jnp.where(lane == k, rank_k, rank_o)
        gate_o = jnp.where(lane == k, w[k] / tot, gate_o)
    idx_ref[...] = idx_o.astype(jnp.int32)
    rank_ref[...] = rank_o.astype(jnp.int32)
    gate_ref[...] = gate_o
    carry_s[...] = carry_s[...] + jnp.sum(oh_all, axis=0, keepdims=True)
    cnt_ref[...] = carry_s[...].astype(jnp.int32)


def _post_mixer(layer, x, ha, hb, pa, pb, wout, g1, b1, wpg, wpp, wr2, wrh, br):
    n, d = x.shape
    row = lambda w: pl.BlockSpec((ROW_TILE, w), lambda i: (i, 0))
    full = lambda a: pl.BlockSpec(a.shape, lambda i: (0, 0))
    ha_spec, hb_spec, ta = _two_sources(ha.shape[0], hb.shape[0], d)
    pa_spec, pb_spec, _ = _two_sources(pa.shape[1], pb.shape[1], D_PLE, lead=(layer,))
    g1 = g1.reshape(1, d)
    b1 = b1.reshape(1, d)
    return pl.pallas_call(
        functools.partial(_post_mixer_kernel, tiles_a=ta), grid=(n // ROW_TILE,),
        in_specs=[row(d), ha_spec, hb_spec, pa_spec, pb_spec, full(wout), full(g1), full(b1), full(wpg), full(wpp),
                  full(wr2), full(wrh), full(br)],
        out_specs=[row(d), row(d), row(LANES), row(LANES), row(LANES), pl.BlockSpec((1, LANES), lambda i: (0, 0))],
        out_shape=[jax.ShapeDtypeStruct((n, d), F32), jax.ShapeDtypeStruct((n, d), F32),
                   jax.ShapeDtypeStruct((n, LANES), jnp.int32), jax.ShapeDtypeStruct((n, LANES), jnp.int32),
                   jax.ShapeDtypeStruct((n, LANES), F32), jax.ShapeDtypeStruct((1, LANES), jnp.int32)],
        scratch_shapes=[pltpu.VMEM((1, LANES), F32)],
        compiler_params=_params("arbitrary"), name="post_mixer")(
            x, ha, hb, pa, pb, wout, g1, b1, wpg, wpp, wr2, wrh, br)


def _expert_kernel(blk_ref, e_ref, lo_ref, hi_ref, nv_ref, nxt_ref, slot_ref, x_ref, wg_hbm, bg_ref, wu_hbm, bu_ref,
                   wd_hbm, bd_ref, o_ref, w32_s, wg_s, wu_s, wd_s, sem, *, layer):
    v = pl.program_id(0)
    prev = jnp.maximum(v - 1, 0)
    valid = v < nv_ref[0]
    new_expert = jnp.logical_or(v == 0, e_ref[v] != e_ref[prev])
    new_block = jnp.logical_or(v == 0, blk_ref[v] != blk_ref[prev])
    slot = slot_ref[v]

    def weight_copies(expert, to_slot):
        return [pltpu.make_async_copy(w_hbm.at[layer, expert], w32_s.at[to_slot, j], sem.at[to_slot, j])
                for j, w_hbm in enumerate((wg_hbm, wu_hbm, wd_hbm))]

    @pl.when(v == 0)
    def _prime():
        for cp in weight_copies(e_ref[0], slot):
            cp.start()

    @pl.when(jnp.logical_and(valid, new_expert))
    def _next_weights():
        for cp in weight_copies(e_ref[v], slot):
            cp.wait()

        @pl.when(nxt_ref[v] != e_ref[v])
        def _prefetch():
            for cp in weight_copies(nxt_ref[v], 1 - slot):
                cp.start()

        wg_s[...] = w32_s[slot, 0].astype(BF16)
        wu_s[...] = w32_s[slot, 1].astype(BF16)
        wd_s[...] = w32_s[slot, 2].astype(BF16)

    @pl.when(valid)
    def _compute():
        xb = x_ref[...].astype(BF16)
        gate = jnp.dot(xb, wg_s[...], preferred_element_type=F32) + bg_ref[0]
        up = jnp.dot(xb, wu_s[...], preferred_element_type=F32) + bu_ref[0]
        gate = jnp.minimum(gate, SWIGLU_LIMIT)
        up = jnp.clip(up, -SWIGLU_LIMIT, SWIGLU_LIMIT)
        glu = gate * jax.nn.sigmoid(gate * SWIGLU_ALPHA)
        act = ((up + 1.0) * glu).astype(BF16)
        y = jnp.dot(act, wd_s[...], preferred_element_type=F32) + bd_ref[0]
        rows = blk_ref[v] * EXPERT_ROWS + lax.broadcasted_iota(jnp.int32, (EXPERT_ROWS, 1), 0)
        mine = jnp.logical_and(rows >= lo_ref[v], rows < hi_ref[v])

        @pl.when(new_block)
        def _first():
            o_ref[...] = jnp.where(mine, y, 0.0)

        @pl.when(jnp.logical_not(new_block))
        def _again():
            o_ref[...] = jnp.where(mine, y, o_ref[...])


def _experts(layer, blk_v, e_v, lo_v, hi_v, n_vis, nxt_v, slot_v, xin, wg, bg, wu, bu, wd, bd):
    n_rows = xin.shape[0]
    d = wg.shape[-1]
    n_visits = blk_v.shape[0]
    xspec = pl.BlockSpec((EXPERT_ROWS, d), lambda v, blk, e, *_: (blk[v], 0))
    ospec = pl.BlockSpec((EXPERT_ROWS, d), lambda v, blk, e, *_: (blk[v], 0))
    wspec = pl.BlockSpec(memory_space=pl.ANY)
    bspec = pl.BlockSpec((None, 1, 1, d), lambda v, blk, e, *_: (layer, e[v], 0, 0))
    grid_spec = pltpu.PrefetchScalarGridSpec(
        num_scalar_prefetch=7, grid=(n_visits,),
        in_specs=[xspec, wspec, bspec, wspec, bspec, wspec, bspec], out_specs=ospec,
        scratch_shapes=[pltpu.VMEM((2, 3, d, d), F32)] + [pltpu.VMEM((d, d), BF16)] * 3
        + [pltpu.SemaphoreType.DMA((2, 3))])
    return pl.pallas_call(
        functools.partial(_expert_kernel, layer=layer), grid_spec=grid_spec,
        out_shape=jax.ShapeDtypeStruct((n_rows, d), F32),
        compiler_params=_params("arbitrary"), name="experts")(
            blk_v, e_v, lo_v, hi_v, n_vis, nxt_v, slot_v, xin, wg, bg, wu, bu, wd, bd)


def _moe(layer, x1, routing, wg, bg, wu, bu, wd, bd):
    n_tok = x1.shape[0]
    n_assign = n_tok * TOP_K
    n_blocks = n_assign // EXPERT_ROWS
    idx_o, rank_o, gate_o, cnt = routing
    counts = cnt[0, :N_EXPERTS]
    ends = jnp.cumsum(counts)
    offs = ends - counts
    experts = jnp.arange(N_EXPERTS, dtype=jnp.int32)
    top_i = idx_o[:, :TOP_K]
    off_tok = jnp.sum(jnp.where(top_i[:, :, None] == experts, offs, 0), axis=-1)
    pos = (off_tok + rank_o[:, :TOP_K]).reshape(n_assign)
    assert n_tok <= (1 << TOK_BITS) and n_assign <= (1 << (32 - TOK_BITS))
    tok = jnp.arange(n_assign, dtype=jnp.uint32) // TOP_K
    key = lax.sort(pos.astype(jnp.uint32) * (1 << TOK_BITS) + tok)
    xin = x1[(key & ((1 << TOK_BITS) - 1)).astype(jnp.int32)]

    first_blk = offs // EXPERT_ROWS
    last_blk = (ends - 1) // EXPERT_ROWS
    nvis = jnp.where(counts > 0, last_blk - first_blk + 1, 0)
    vend = jnp.cumsum(nvis)
    vstart = vend - nvis
    n_visits = n_blocks + N_EXPERTS - 1
    v = jnp.arange(n_visits, dtype=jnp.int32)
    e_v = jnp.sum((vend[None, :] <= v[:, None]).astype(jnp.int32), axis=1)
    e_last = jnp.max(jnp.where(counts > 0, experts, 0))
    e_v = jnp.minimum(e_v, e_last)
    pick = lambda tab: jnp.sum(jnp.where(e_v[:, None] == experts, tab, 0), axis=-1)
    blk_v = jnp.minimum(pick(first_blk) + v - pick(vstart), n_blocks - 1)
    active = counts > 0
    later = jnp.where(jnp.logical_and(active[None, :], experts[None, :] > experts[:, None]), experts[None, :], N_EXPERTS)
    nxt_e = jnp.min(later, axis=1)
    nxt_e = jnp.where(nxt_e == N_EXPERTS, experts, nxt_e)
    order_e = jnp.cumsum(active.astype(jnp.int32)) - 1
    y_rows = _experts(layer, blk_v.astype(jnp.int32), e_v.astype(jnp.int32), pick(offs).astype(jnp.int32),
                      pick(ends).astype(jnp.int32), vend[-1:].astype(jnp.int32),
                      pick(nxt_e).astype(jnp.int32), (pick(order_e) % 2).astype(jnp.int32),
                      xin, wg, bg, wu, bu, wd, bd)
    return y_rows[pos.reshape(n_tok, TOP_K).T.reshape(n_assign)], gate_o


def _pack_w_in(w_in):
    qa, ka, va, oa, ia, fa, qb, fb, ib, gb = jnp.split(w_in, SPLIT_POINTS, axis=-1)

    def pad_heads(w):
        w = w.reshape(w.shape[:-1] + (N_HEADS, DK_A))
        w = jnp.pad(w, ((0, 0),) * (w.ndim - 1) + ((0, D_HEAD - DK_A),))
        return w.reshape(w.shape[:-2] + (N_HEADS * D_HEAD,))

    def pad_gate(w):
        return jnp.pad(w, ((0, 0),) * (w.ndim - 1) + ((0, LANES - N_HEADS),))

    cols = [pad_heads(qa), pad_heads(ka * (DK_A ** -0.5)), va, oa, qb, fb, ib, gb, pad_gate(ia), pad_gate(fa)]
    return jnp.concatenate(cols, axis=-1).astype(BF16)


def kernel(x_prompt, x_sample, state_mlstm_C, state_mlstm_n, state_mlstm_m, state_hgrn_S, p_prompt, p_sample,
           ln_in_g, ln_in_b, w_in, mlstm_ig_bias, mlstm_fg_bias, mlstm_norm_g, hgrn_lb_logits, hgrn_norm_g,
           w_out, ln1_g, ln1_b, w_router, b_router, w_gate, b_gate, w_up, b_up, w_down, b_down,
           w_ple_gate, w_ple_proj, ln2_g, ln2_b):
    bp, tp, d = x_prompt.shape
    bs, ts, _ = x_sample.shape
    n_p = bp * tp
    n_s = bs * ts
    ts_pad = SUBLANES
    assert tp % CHUNK == 0 and ts <= ts_pad and ts % CHUNK != 0

    lb_soft = jax.nn.softmax(hgrn_lb_logits.astype(F32), axis=0)
    lower_bounds = jnp.cumsum(lb_soft, axis=0) - lb_soft[0]

    w_in_p = _pack_w_in(w_in)
    gbias = jnp.stack([jnp.pad(mlstm_ig_bias, ((0, 0), (0, LANES - N_HEADS))),
                       jnp.pad(mlstm_fg_bias, ((0, 0), (0, LANES - N_HEADS)))], axis=1)
    w_out_b = w_out.astype(BF16)
    w_pg_b = w_ple_gate.astype(BF16)
    w_pp_b = w_ple_proj.astype(BF16)
    w_r = jnp.pad(w_router, ((0, 0), (0, 0), (0, LANES - N_EXPERTS)))
    w_r_hi = w_r.astype(BF16)
    w_r2 = jnp.concatenate([w_r_hi, (w_r - w_r_hi.astype(F32)).astype(BF16)], axis=-1)
    b_r = jnp.pad(b_router, ((0, 0), (0, LANES - N_EXPERTS))).reshape(DEPTH, 1, LANES)
    n0_pad = jnp.pad(state_mlstm_n, ((0, 0), (0, 0), (0, 0), (0, D_HEAD - DK_A))).reshape(
        DEPTH, bs, N_HEADS, 1, D_HEAD)
    m0_pad = jnp.pad(state_mlstm_m, ((0, 0), (0, 0), (0, LANES - N_HEADS))).reshape(DEPTH, bs, 1, LANES)

    p_p = p_prompt.reshape(DEPTH, n_p, D_PLE)
    p_s = p_sample.reshape(DEPTH, n_s, D_PLE)
    x = _ln(x_prompt.reshape(n_p, d), x_sample.reshape(n_s, d), ln_in_g, ln_in_b)

    assert n_p % ROW_TILE == 0 and n_s % ROW_TILE == 0
    nb_p = LANES // CHUNK
    nb_s = LANES // ts_pad
    st_p = _stacked_states(bp)
    st_s = _stacked_states(bs)
    for l in range(DEPTH):
        lb = lower_bounds[l].reshape(1, MIX)
        nga = mlstm_norm_g[l].reshape(1, MIX)
        ngb = hgrn_norm_g[l].reshape(1, MIX)
        proj_p = _inproj(x, w_in_p[l], 0, n_p).reshape(bp, tp, N_PROJ)
        h_p, st_p = _scan(proj_p, gbias[l], lb, nga, ngb, None, st_p, layer=l,
                          NB=nb_p, n_chunks=tp // CHUNK, L=CHUNK, last=CHUNK - 1)
        proj_s = jnp.pad(_inproj(x, w_in_p[l], n_p, n_s).reshape(bs, ts, N_PROJ), ((0, 0), (0, ts_pad - ts), (0, 0)))
        h_s, st_s = _scan(proj_s, gbias[l], lb, nga, ngb,
                          (state_mlstm_C[l], n0_pad[l], m0_pad[l], state_hgrn_S[l]), st_s, layer=l,
                          NB=nb_s, n_chunks=1, L=ts_pad, last=ts - 1)
        x1, resid, *routing = _post_mixer(l, x, h_p.reshape(n_p, d), h_s[:, :ts].reshape(n_s, d), p_p, p_s,
                                          w_out_b[l], ln1_g[l], ln1_b[l], w_pg_b[l], w_pp_b[l],
                                          w_r2[l], w_r_hi[l], b_r[l])
        y_k, gates = _moe(l, x1, routing, w_gate, b_gate.reshape(DEPTH, N_EXPERTS, 1, d),
                          w_up, b_up.reshape(DEPTH, N_EXPERTS, 1, d),
                          w_down, b_down.reshape(DEPTH, N_EXPERTS, 1, d))
        x = _combine_ln(resid, y_k, gates, ln2_g[l], ln2_b[l])

    y_prompt = x[:n_p].reshape(bp, tp, d)
    y_sample = x[n_p:].reshape(bs, ts, d)
    unpad = lambda st: (st[0], st[1][:, :, :, 0, :DK_A], st[2][:, :, 0, :N_HEADS], st[3])
    return (y_prompt, y_sample) + unpad(st_p) + unpad(st_s)
```

```python
import functools

import jax
import jax.numpy as jnp
import numpy as np
from jax import lax
from jax.experimental import pallas as pl
from jax.experimental.pallas import tpu as pltpu
from jax.experimental.pallas import tpu_sc as plsc

F32 = jnp.float32
BF16 = jnp.bfloat16

D_MODEL = 1024
DEPTH = 4
D_PLE = 256
N_HEADS = 4
DK_A = 64
D_HEAD = 128
MIX = 512
N_EXPERTS = 32
TOP_K = 4
SWIGLU_LIMIT = 7.0
SWIGLU_ALPHA = 1.702
CHUNK = 64
LN_EPS = 1e-5
NORM_EPS = 1e-6
LB_FLOOR = 1e-20
DEEPNORM_ALPHA = (2 * DEPTH) ** 0.25
SPLIT_SIZES = (256, 256, 512, 512, 4, 4, 512, 512, 512, 512)
SPLIT_POINTS = tuple(int(s) for s in np.cumsum(SPLIT_SIZES)[:-1])

LANES = 128
SUBLANES = 8
VMEM_LIMIT = 56 * 1024 * 1024

QA0, KA0, VA0, OA0 = 0, 512, 1024, 1536
QB0, FB0, IB0, GB0 = 2048, 2560, 3072, 3584
IG0, FG0 = 4096, 4224
N_PROJ = 4352

ROW_TILE = 256
EXPERT_ROWS = 256
DISPATCH_ROWS = 48

NT_DIMS = (((1,), (1,)), ((), ()))
TN_DIMS = (((0,), (0,)), ((), ()))


def _params(*sem):
    return pltpu.CompilerParams(dimension_semantics=sem, vmem_limit_bytes=VMEM_LIMIT)


def _log_sigmoid(x):
    return jnp.minimum(x, 0.0) - jnp.log(1.0 + jnp.exp(-jnp.abs(x)))


def _split3(x):
    x1 = x.astype(BF16)
    r1 = x - x1.astype(F32)
    x2 = r1.astype(BF16)
    x3 = (r1 - x2.astype(F32)).astype(BF16)
    return x1, x2, x3


def _dot01(sel, x):
    return sum(jnp.dot(sel, t, preferred_element_type=F32) for t in _split3(x))


def _layernorm_rows(x, g, b):
    mu = jnp.mean(x, axis=-1, keepdims=True)
    xc = x - mu
    var = jnp.mean(xc * xc, axis=-1, keepdims=True)
    return xc * lax.rsqrt(var + LN_EPS) * g + b


def _two_sources(n_a, n_b, width, lead=()):
    ta = n_a // ROW_TILE
    assert n_a % ROW_TILE == 0 and n_b % ROW_TILE == 0
    block = (None,) * len(lead) + (ROW_TILE, width)
    spec_a = pl.BlockSpec(block, lambda i: lead + (jnp.minimum(i, ta - 1), 0))
    spec_b = pl.BlockSpec(block, lambda i: lead + (jnp.maximum(i - ta, 0), 0))
    return spec_a, spec_b, ta


def _ln_kernel(xa_ref, xb_ref, g_ref, b_ref, o_ref, *, tiles_a):
    x = jnp.where(pl.program_id(0) < tiles_a, xa_ref[...], xb_ref[...])
    o_ref[...] = _layernorm_rows(x, g_ref[...], b_ref[...])


def _ln(xa, xb, g, b):
    d = xa.shape[1]
    n = xa.shape[0] + xb.shape[0]
    spec_a, spec_b, ta = _two_sources(xa.shape[0], xb.shape[0], d)
    vec = pl.BlockSpec((1, d), lambda i: (0, 0))
    return pl.pallas_call(
        functools.partial(_ln_kernel, tiles_a=ta), grid=(n // ROW_TILE,), in_specs=[spec_a, spec_b, vec, vec],
        out_specs=pl.BlockSpec((ROW_TILE, d), lambda i: (i, 0)),
        out_shape=jax.ShapeDtypeStruct((n, d), F32), compiler_params=_params("parallel"),
        name="ln_in")(xa, xb, g.reshape(1, d), b.reshape(1, d))


def _combine_ln_kernel(resid_ref, y0_ref, y1_ref, y2_ref, y3_ref, gate_ref, g_ref, beta_ref, o_ref):
    acc = resid_ref[...]
    for k, y_ref in enumerate((y0_ref, y1_ref, y2_ref, y3_ref)):
        acc = acc + gate_ref[:, k:k + 1] * y_ref[...]
    o_ref[...] = _layernorm_rows(acc, g_ref[...], beta_ref[...])


def _combine_ln(resid, y_k, gates, g, beta):
    n, d = resid.shape
    nt = n // ROW_TILE
    row = lambda w: pl.BlockSpec((ROW_TILE, w), lambda i: (i, 0))
    yk = lambda k: pl.BlockSpec((ROW_TILE, d), lambda i: (k * nt + i, 0))
    vec = pl.BlockSpec((1, d), lambda i: (0, 0))
    return pl.pallas_call(
        _combine_ln_kernel, grid=(nt,),
        in_specs=[row(d)] + [yk(k) for k in range(TOP_K)] + [row(LANES), vec, vec],
        out_specs=row(d), out_shape=jax.ShapeDtypeStruct((n, d), F32), compiler_params=_params("parallel"),
        name="combine_ln2")(resid, y_k, y_k, y_k, y_k, gates, g.reshape(1, d), beta.reshape(1, d))


def _inproj_kernel(x_ref, w_ref, o_ref):
    o_ref[...] = jnp.dot(x_ref[...].astype(BF16), w_ref[...], preferred_element_type=F32)


def _inproj(x, w, row0, n_rows):
    d = x.shape[1]
    t0 = row0 // ROW_TILE
    return pl.pallas_call(
        _inproj_kernel, grid=(n_rows // ROW_TILE,),
        in_specs=[pl.BlockSpec((ROW_TILE, d), lambda i: (t0 + i, 0)),
                  pl.BlockSpec((d, N_PROJ), lambda i: (0, 0))],
        out_specs=pl.BlockSpec((ROW_TILE, N_PROJ), lambda i: (i, 0)),
        out_shape=jax.ShapeDtypeStruct((n_rows, N_PROJ), F32), compiler_params=_params("parallel"),
        name="in_proj")(x, w)


def _scan_kernel(*refs, NB, L, last, has_state):
    it = iter(refs)
    proj_ref, gbias_ref, lb_ref, nga_ref, ngb_ref = (next(it) for _ in range(5))
    if has_state:
        C0_ref, n0_ref, m0_ref, S0_ref = (next(it) for _ in range(4))
    for _ in range(4):
        next(it)
    h_ref, C_out, n_out, m_out, S_out = (next(it) for _ in range(5))
    C_s, n_s, m_s, St_s = (next(it) for _ in range(4))

    c = pl.program_id(1)
    nc = pl.num_programs(1)
    H = range(N_HEADS)
    N = range(NB)
    R = NB * L
    assert R == LANES

    @pl.when(c == 0)
    def _init():
        if has_state:
            for n in N:
                for h in H:
                    C_s[n, h, 0:DK_A, :] = C0_ref[n, h]
                    C_s[n, h, DK_A:D_HEAD, :] = jnp.zeros((D_HEAD - DK_A, D_HEAD), F32)
                    St_s[n, h] = S0_ref[n, h].T
            n_s[...] = n0_ref[...]
            m_s[...] = m0_ref[...]
        else:
            C_s[...] = jnp.zeros(C_s.shape, F32)
            n_s[...] = jnp.zeros(n_s.shape, F32)
            m_s[...] = jnp.zeros(m_s.shape, F32)
            St_s[...] = jnp.zeros(St_s.shape, F32)

    row = lax.broadcasted_iota(jnp.int32, (R, R), 0)
    col = lax.broadcasted_iota(jnp.int32, (R, R), 1)
    causal = jnp.logical_and(row // L == col // L, row >= col)
    tril = causal.astype(BF16)
    ones_r = jnp.ones((R, LANES), BF16)
    lane = lax.broadcasted_iota(jnp.int32, (R, LANES), 1)
    step = lax.broadcasted_iota(jnp.int32, (R, 1), 0) % L
    valid = step <= last
    padded = last < L - 1
    tile = lambda base, h: slice(base + LANES * h, base + LANES * (h + 1))
    seq = lambda n: slice(n * L, (n + 1) * L)
    cols = lambda base, width: proj_ref[:, :, base:base + width].reshape(R, width)
    per_seq = lambda vals: jnp.concatenate([jnp.broadcast_to(x, (L, x.shape[-1])) for x in vals], axis=0)
    last_rows = lambda x: [x[n * L + last:n * L + last + 1] for n in N]

    i_t = cols(IG0, LANES) + gbias_ref[0:1, :]
    f_t = _log_sigmoid(cols(FG0, LANES) + gbias_ref[1:2, :])
    zf = cols(FB0, MIX)
    lb = lb_ref[...]
    la = jnp.log(jnp.maximum(lb, LB_FLOOR))
    bb = jnp.log1p(-lb) + _log_sigmoid(zf)
    f_log = jnp.maximum(la, bb) + jnp.log(1.0 + jnp.exp(-jnp.abs(la - bb)))
    b_t = _dot01(tril, f_t)
    g = _dot01(tril, f_log)
    m_all = [m_s[n] for n in N]
    C_prev = [[C_s[n, h] for h in H] for n in N]
    n_prev = [[n_s[n, h] for h in H] for n in N]
    St = [[St_s[n, h] for h in H] for n in N]
    qf = [cols(QA0 + LANES * h, LANES) for h in H]
    kf = [cols(KA0 + LANES * h, LANES) for h in H]
    q = [x.astype(BF16) for x in qf]
    k = [x.astype(BF16) for x in kf]
    v = [cols(VA0 + LANES * h, LANES).astype(BF16) for h in H]
    kb = (1.0 - lb) * jax.nn.sigmoid(-zf)
    qr = cols(QB0, MIX)
    qb = qr * jax.nn.sigmoid(qr)
    vb = [cols(IB0 + LANES * h, LANES).astype(BF16) for h in H]
    qb16 = qb.astype(BF16)
    kb16 = kb.astype(BF16)
    r_t = i_t - b_t
    zero16 = jnp.zeros((R, LANES), BF16)
    r_all = sum(
        lax.dot_general(ones_r, jnp.concatenate([jnp.where(lane == h, term, zero16) for h in H], axis=0),
                        NT_DIMS, preferred_element_type=F32)
        for term in _split3(r_t))
    r_m = [r_all[:, R * h:R * (h + 1)] for h in H]
    qk = [lax.dot_general(q[h], k[h], NT_DIMS, preferred_element_type=F32) for h in H]
    qC = [jnp.concatenate([jnp.dot(q[h][seq(n)], C_prev[n][h].astype(BF16), preferred_element_type=F32)
                           for n in N], axis=0) for h in H]
    levels = []
    bs = 1
    while bs < L:
        levels.append(bs)
        bs *= 2
    small = [bs for bs in levels if bs < SUBLANES]
    g_anchor = {}
    if small:
        sel = jnp.concatenate(
            [(col == (row // (2 * bs)) * (2 * bs) + bs).astype(BF16) for bs in small], axis=0)
        picked = _dot01(sel, g)
        for i, bs in enumerate(small):
            g_anchor[bs] = picked[i * R:(i + 1) * R]
    for bs in levels:
        if bs >= SUBLANES:
            g_anchor[bs] = jnp.concatenate(
                [jnp.broadcast_to(g[p0 + bs:p0 + bs + 1], (2 * bs, MIX)) for p0 in range(0, R, 2 * bs)], axis=0)
    qg = (qb * jnp.exp(g)).astype(BF16)
    o_inter = [jnp.concatenate([lax.dot_general(qg[seq(n), tile(0, h)], St[n][h].astype(BF16), NT_DIMS,
                                                preferred_element_type=F32) for n in N], axis=0) for h in H]
    a = [jnp.where(row == col, lax.dot_general(qb16[:, tile(0, h)], kb16[:, tile(0, h)], NT_DIMS,
                                                preferred_element_type=F32), 0.0) for h in H]
    qn = [jnp.sum(qf[h] * per_seq([n_prev[n][h] for n in N]), axis=1, keepdims=True) for h in H]
    b_col = [b_t[:, h:h + 1] for h in H]
    i_col = [i_t[:, h:h + 1] for h in H]
    m_prev = [per_seq([m_all[n][:, h:h + 1] for n in N]) for h in H]
    cm = [jnp.max(jnp.where(causal, r_m[h], -jnp.inf), axis=1, keepdims=True) for h in H]
    m_t = [b_col[h] + jnp.maximum(m_prev[h], cm[h]) for h in H]
    inter = [jnp.exp(b_col[h] + m_prev[h] - m_t[h]) for h in H]
    d = [jnp.exp(jnp.where(causal, (b_col[h] - m_t[h]) + r_m[h], -1e30)) for h in H]
    s = [qk[h] * d[h] for h in H]
    log2e = 1.4426950408889634
    e = [jnp.exp2((g - g_anchor[bs]) * jnp.where((step // bs) % 2 == 1, log2e, -log2e)) for bs in levels]
    qe = [(qb * x).astype(BF16) for x in e]
    ke = [(kb * x).astype(BF16) for x in e]
    sv = [jnp.dot(s[h].astype(BF16), v[h], preferred_element_type=F32) for h in H]
    sc = [[lax.dot_general(qe[i][:, tile(0, h)], ke[i][:, tile(0, h)], NT_DIMS, preferred_element_type=F32)
           for h in H] for i in range(len(levels))]
    den = [jnp.sum(s[h], axis=1, keepdims=True) + inter[h] * qn[h] for h in H]
    m_new = [last_rows(m_t[h]) for h in H]
    b_last = [last_rows(b_col[h]) for h in H]
    w_arg = [per_seq(b_last[h]) - b_col[h] + i_col[h] - per_seq(m_new[h]) for h in H]
    if padded:
        w_arg = [jnp.where(valid, x, -1e30) for x in w_arg]
    kw = [kf[h] * jnp.exp(w_arg[h]) for h in H]
    kw16 = [x.astype(BF16) for x in kw]
    decay = [[jnp.exp(b_last[h][n] + m_all[n][:, h:h + 1] - m_new[h][n]) for h in H] for n in N]
    kv = [[lax.dot_general(kw16[h][seq(n)], v[h][seq(n)], TN_DIMS, preferred_element_type=F32) for h in H]
          for n in N]
    for i, bs in enumerate(levels):
        pair = jnp.logical_and(row // (2 * bs) == col // (2 * bs),
                               jnp.logical_and((row // bs) % 2 == 1, (col // bs) % 2 == 0))
        for h in H:
            a[h] = jnp.where(pair, sc[i][h], a[h])
    o = [o_inter[h] + jnp.dot(a[h].astype(BF16), vb[h], preferred_element_type=F32) for h in H]
    g_last = last_rows(g)
    dec_arg = per_seq(g_last) - g
    if padded:
        dec_arg = jnp.where(valid, dec_arg, -1e30)
    kdec = (kb * jnp.exp(dec_arg)).astype(BF16)
    eg_last = [jnp.exp(x) for x in g_last]
    vk = [[lax.dot_general(vb[h][seq(n)], kdec[seq(n), tile(0, h)], TN_DIMS, preferred_element_type=F32)
           for h in H] for n in N]
    hh = [(sv[h] + inter[h] * qC[h]) / jnp.maximum(jnp.abs(den[h]), jnp.exp(-m_t[h])) for h in H]
    for h in H:
        hn = hh[h] * lax.rsqrt(jnp.mean(hh[h] * hh[h], axis=-1, keepdims=True) + NORM_EPS) * nga_ref[:, tile(0, h)]
        out_a = jax.nn.sigmoid(cols(OA0 + LANES * h, LANES)) * hn
        h_ref[:, :, tile(0, h)] = out_a.reshape(NB, L, LANES)
    for h in H:
        on = o[h] * lax.rsqrt(jnp.mean(o[h] * o[h], axis=-1, keepdims=True) + NORM_EPS) * ngb_ref[:, tile(0, h)]
        gr = cols(GB0 + LANES * h, LANES)
        h_ref[:, :, tile(MIX, h)] = (gr * jax.nn.sigmoid(gr) * on).reshape(NB, L, LANES)
    lane1 = lax.broadcasted_iota(jnp.int32, (1, LANES), 1)
    for n in N:
        for h in H:
            C_s[n, h] = decay[n][h] * C_prev[n][h] + kv[n][h]
            n_s[n, h] = decay[n][h] * n_prev[n][h] + jnp.sum(kw[h][seq(n)], axis=0, keepdims=True)
            St_s[n, h] = eg_last[n][:, tile(0, h)] * St[n][h] + vk[n][h]
        m_upd = m_all[n]
        for h in H:
            m_upd = jnp.where(lane1 == h, m_new[h][n], m_upd)
        m_s[n] = m_upd

    @pl.when(c == nc - 1)
    def _finish():
        for n in N:
            for h in H:
                C_out[n, h] = C_s[n, h, 0:DK_A, :]
                S_out[n, h] = St_s[n, h].T
        n_out[...] = n_s[...]
        m_out[...] = m_s[...]


def _scan(proj, gbias, lb, nga, ngb, state, stacked, *, layer, NB, n_chunks, L, last):
    batch = proj.shape[0]
    has_state = state is not None
    const2 = lambda b, c: (0, 0)
    in_specs = [
        pl.BlockSpec((NB, L, N_PROJ), lambda b, c: (b, c, 0)),
        pl.BlockSpec((2, LANES), const2),
        pl.BlockSpec((1, MIX), const2),
        pl.BlockSpec((1, MIX), const2),
        pl.BlockSpec((1, MIX), const2),
    ]
    args = [proj, gbias, lb, nga, ngb]
    st4 = lambda b, c: (b, 0, 0, 0)
    st3 = lambda b, c: (b, 0, 0)
    if has_state:
        in_specs += [
            pl.BlockSpec((NB, N_HEADS, DK_A, D_HEAD), st4),
            pl.BlockSpec((NB, N_HEADS, 1, LANES), st4),
            pl.BlockSpec((NB, 1, LANES), st3),
            pl.BlockSpec((NB, N_HEADS, D_HEAD, D_HEAD), st4),
        ]
        args += list(state)
    aliases = {len(args) + i: 1 + i for i in range(4)}
    in_specs += [pl.BlockSpec(memory_space=pl.ANY)] * 4
    args += list(stacked)
    lst4 = lambda b, c: (layer, b, 0, 0, 0)
    lst3 = lambda b, c: (layer, b, 0, 0)
    out_specs = [
        pl.BlockSpec((NB, L, D_MODEL), lambda b, c: (b, c, 0)),
        pl.BlockSpec((None, NB, N_HEADS, DK_A, D_HEAD), lst4),
        pl.BlockSpec((None, NB, N_HEADS, 1, LANES), lst4),
        pl.BlockSpec((None, NB, 1, LANES), lst3),
        pl.BlockSpec((None, NB, N_HEADS, D_HEAD, D_HEAD), lst4),
    ]
    out_shape = [jax.ShapeDtypeStruct((batch, n_chunks * L, D_MODEL), F32)] + [
        jax.ShapeDtypeStruct(a.shape, a.dtype) for a in stacked]
    scratch = [
        pltpu.VMEM((NB, N_HEADS, D_HEAD, D_HEAD), F32),
        pltpu.VMEM((NB, N_HEADS, 1, LANES), F32),
        pltpu.VMEM((NB, 1, LANES), F32),
        pltpu.VMEM((NB, N_HEADS, D_HEAD, D_HEAD), F32),
    ]
    kern = functools.partial(_scan_kernel, NB=NB, L=L, last=last, has_state=has_state)
    res = pl.pallas_call(
        kern, grid=(batch // NB, n_chunks), in_specs=in_specs, out_specs=out_specs, out_shape=out_shape,
        scratch_shapes=scratch, input_output_aliases=aliases,
        compiler_params=_params("parallel", "arbitrary"),
        name="scan_state" if has_state else "scan_prompt")(*args)
    return res[0], tuple(res[1:])


def _stacked_states(batch):
    return (jnp.zeros((DEPTH, batch, N_HEADS, DK_A, D_HEAD), F32),
            jnp.zeros((DEPTH, batch, N_HEADS, 1, LANES), F32),
            jnp.zeros((DEPTH, batch, 1, LANES), F32),
            jnp.zeros((DEPTH, batch, N_HEADS, D_HEAD, D_HEAD), F32))


def _post_mixer_kernel(x_ref, ha_ref, hb_ref, pa_ref, pb_ref, wout_ref, g1_ref, b1_ref, wpg_ref, wpp_ref, wr2_ref,
                       wrh_ref, br_ref, x1_ref, resid_ref, idx_ref, rank_ref, gate_ref, cnt_ref, carry_s, *, tiles_a):
    i = pl.program_id(0)

    @pl.when(i == 0)
    def _init():
        carry_s[...] = jnp.zeros(carry_s.shape, F32)

    from_a = i < tiles_a
    h = jnp.where(from_a, ha_ref[...], hb_ref[...])
    p = jnp.where(from_a, pa_ref[...], pb_ref[...])
    mix = jnp.dot(h.astype(BF16), wout_ref[...], preferred_element_type=F32)
    x1 = _layernorm_rows(DEEPNORM_ALPHA * x_ref[...] + mix, g1_ref[...], b1_ref[...])
    x1b = x1.astype(BF16)
    x1_ref[...] = x1
    x1_lo = (x1 - x1b.astype(F32)).astype(BF16)
    l2 = jnp.dot(x1b, wr2_ref[...], preferred_element_type=F32)
    logits = (l2[:, :LANES] + l2[:, LANES:]
              + jnp.dot(x1_lo, wrh_ref[...], preferred_element_type=F32) + br_ref[...])

    tm = logits.shape[0]
    lane = lax.broadcasted_iota(jnp.int32, (tm, LANES), 1)
    lane_f = lane.astype(F32)
    l = jnp.where(lane < N_EXPERTS, logits, -jnp.inf)
    vals, idxs, onehots = [], [], []
    for _ in range(TOP_K):
        mx = jnp.max(l, axis=1, keepdims=True)
        ix = jnp.min(jnp.where(l == mx, lane_f, float(LANES)), axis=1, keepdims=True)
        sel = lane_f == ix
        vals.append(mx)
        idxs.append(ix)
        onehots.append(sel.astype(F32))
        l = jnp.where(sel, -jnp.inf, l)
    gate = jax.nn.sigmoid(jnp.dot(x1b, wpg_ref[...], preferred_element_type=F32))
    pp = jnp.dot(p.astype(BF16), wpp_ref[...], preferred_element_type=F32)
    resid_ref[...] = DEEPNORM_ALPHA * x1 + gate * pp

    w = [jnp.exp(val - vals[0]) for val in vals]
    tot = w[0] + w[1] + w[2] + w[3]
    oh_all = onehots[0] + onehots[1] + onehots[2] + onehots[3]
    row = lax.broadcasted_iota(jnp.int32, (tm, tm), 0)
    col = lax.broadcasted_iota(jnp.int32, (tm, tm), 1)
    earlier = (row > col).astype(BF16)
    prefix = jnp.dot(earlier, oh_all.astype(BF16), preferred_element_type=F32) + carry_s[...]
    idx_o = jnp.zeros((tm, LANES), F32)
    rank_o = jnp.zeros((tm, LANES), F32)
    gate_o = jnp.zeros((tm, LANES), F32)
    for k in range(TOP_K):
        rank_k = jnp.sum(onehots[k] * prefix, axis=1, keepdims=True)
        idx_o = jnp.where(lane == k, idxs[k], idx_o)
        rank_o = jnp.where(lane == k, rank_k, rank_o)
        gate_o = jnp.where(lane == k, w[k] / tot, gate_o)
    idx_ref[...] = idx_o.astype(jnp.int32)
    rank_ref[...] = rank_o.astype(jnp.int32)
    gate_ref[...] = gate_o
    carry_s[...] = carry_s[...] + jnp.sum(oh_all, axis=0, keepdims=True)
    cnt_ref[...] = carry_s[...].astype(jnp.int32)


def _post_mixer(layer, x, ha, hb, pa, pb, wout, g1, b1, wpg, wpp, wr2, wrh, br):
    n, d = x.shape
    row = lambda w: pl.BlockSpec((ROW_TILE, w), lambda i: (i, 0))
    full = lambda a: pl.BlockSpec(a.shape, lambda i: (0, 0))
    ha_spec, hb_spec, ta = _two_sources(ha.shape[0], hb.shape[0], d)
    pa_spec, pb_spec, _ = _two_sources(pa.shape[1], pb.shape[1], D_PLE, lead=(layer,))
    g1 = g1.reshape(1, d)
    b1 = b1.reshape(1, d)
    return pl.pallas_call(
        functools.partial(_post_mixer_kernel, tiles_a=ta), grid=(n // ROW_TILE,),
        in_specs=[row(d), ha_spec, hb_spec, pa_spec, pb_spec, full(wout), full(g1), full(b1), full(wpg), full(wpp),
                  full(wr2), full(wrh), full(br)],
        out_specs=[row(d), row(d), row(LANES), row(LANES), row(LANES), pl.BlockSpec((1, LANES), lambda i: (0, 0))],
        out_shape=[jax.ShapeDtypeStruct((n, d), F32), jax.ShapeDtypeStruct((n, d), F32),
                   jax.ShapeDtypeStruct((n, LANES), jnp.int32), jax.ShapeDtypeStruct((n, LANES), jnp.int32),
                   jax.ShapeDtypeStruct((n, LANES), F32), jax.ShapeDtypeStruct((1, LANES), jnp.int32)],
        scratch_shapes=[pltpu.VMEM((1, LANES), F32)],
        compiler_params=_params("arbitrary"), name="post_mixer")(
            x, ha, hb, pa, pb, wout, g1, b1, wpg, wpp, wr2, wrh, br)


def _expert_kernel(blk_ref, e_ref, lo_ref, hi_ref, nv_ref, nxt_ref, slot_ref, x_ref, wg_hbm, bg_ref, wu_hbm, bu_ref,
                   wd_hbm, bd_ref, o_ref, w32_s, wg_s, wu_s, wd_s, sem, *, layer):
    v = pl.program_id(0)
    prev = jnp.maximum(v - 1, 0)
    valid = v < nv_ref[0]
    new_expert = jnp.logical_or(v == 0, e_ref[v] != e_ref[prev])
    new_block = jnp.logical_or(v == 0, blk_ref[v] != blk_ref[prev])
    slot = slot_ref[v]

    def weight_copies(expert, to_slot):
        return [pltpu.make_async_copy(w_hbm.at[layer, expert], w32_s.at[to_slot, j], sem.at[to_slot, j])
                for j, w_hbm in enumerate((wg_hbm, wu_hbm, wd_hbm))]

    @pl.when(v == 0)
    def _prime():
        for cp in weight_copies(e_ref[0], slot):
            cp.start()

    @pl.when(jnp.logical_and(valid, new_expert))
    def _next_weights():
        for cp in weight_copies(e_ref[v], slot):
            cp.wait()

        @pl.when(nxt_ref[v] != e_ref[v])
        def _prefetch():
            for cp in weight_copies(nxt_ref[v], 1 - slot):
                cp.start()

        wg_s[...] = w32_s[slot, 0].astype(BF16)
        wu_s[...] = w32_s[slot, 1].astype(BF16)
        wd_s[...] = w32_s[slot, 2].astype(BF16)

    @pl.when(valid)
    def _compute():
        xb = x_ref[...].astype(BF16)
        gate = jnp.dot(xb, wg_s[...], preferred_element_type=F32) + bg_ref[0]
        up = jnp.dot(xb, wu_s[...], preferred_element_type=F32) + bu_ref[0]
        gate = jnp.minimum(gate, SWIGLU_LIMIT)
        up = jnp.clip(up, -SWIGLU_LIMIT, SWIGLU_LIMIT)
        glu = gate * jax.nn.sigmoid(gate * SWIGLU_ALPHA)
        act = ((up + 1.0) * glu).astype(BF16)
        y = jnp.dot(act, wd_s[...], preferred_element_type=F32) + bd_ref[0]
        rows = blk_ref[v] * EXPERT_ROWS + lax.broadcasted_iota(jnp.int32, (EXPERT_ROWS, 1), 0)
        mine = jnp.logical_and(rows >= lo_ref[v], rows < hi_ref[v])

        @pl.when(new_block)
        def _first():
            o_ref[...] = jnp.where(mine, y, 0.0)

        @pl.when(jnp.logical_not(new_block))
        def _again():
            o_ref[...] = jnp.where(mine, y, o_ref[...])


def _experts(layer, blk_v, e_v, lo_v, hi_v, n_vis, nxt_v, slot_v, xin, wg, bg, wu, bu, wd, bd):
    n_rows = xin.shape[0]
    d = wg.shape[-1]
    n_visits = blk_v.shape[0]
    xspec = pl.BlockSpec((EXPERT_ROWS, d), lambda v, blk, e, *_: (blk[v], 0))
    ospec = pl.BlockSpec((EXPERT_ROWS, d), lambda v, blk, e, *_: (blk[v], 0))
    wspec = pl.BlockSpec(memory_space=pl.ANY)
    bspec = pl.BlockSpec((None, 1, 1, d), lambda v, blk, e, *_: (layer, e[v], 0, 0))
    grid_spec = pltpu.PrefetchScalarGridSpec(
        num_scalar_prefetch=7, grid=(n_visits,),
        in_specs=[xspec, wspec, bspec, wspec, bspec, wspec, bspec], out_specs=ospec,
        scratch_shapes=[pltpu.VMEM((2, 3, d, d), F32)] + [pltpu.VMEM((d, d), BF16)] * 3
        + [pltpu.SemaphoreType.DMA((2, 3))])
    return pl.pallas_call(
        functools.partial(_expert_kernel, layer=layer), grid_spec=grid_spec,
        out_shape=jax.ShapeDtypeStruct((n_rows, d), F32),
        compiler_params=_params("arbitrary"), name="experts")(
            blk_v, e_v, lo_v, hi_v, n_vis, nxt_v, slot_v, xin, wg, bg, wu, bu, wd, bd)


def _dispatch_rows(x1, pos):
    info = plsc.get_sparse_core_info()
    n_workers = info.num_cores * info.num_subcores
    n_tok, d = x1.shape
    per_worker = n_tok // n_workers
    n_chunks = per_worker // DISPATCH_ROWS
    assert per_worker * n_workers == n_tok and n_chunks * DISPATCH_ROWS == per_worker
    idx = pos.T.reshape(TOP_K, n_workers, n_chunks, DISPATCH_ROWS).transpose(1, 2, 0, 3).reshape(
        n_workers, n_chunks * TOP_K, DISPATCH_ROWS)
    mesh = plsc.VectorSubcoreMesh(core_axis_name="core", subcore_axis_name="subcore")

    @functools.partial(
        pl.kernel, out_type=jax.ShapeDtypeStruct((n_tok * TOP_K, d), x1.dtype), mesh=mesh,
        scratch_types=[pltpu.VMEM((n_chunks * TOP_K, DISPATCH_ROWS), jnp.int32),
                       pltpu.VMEM((DISPATCH_ROWS, d), x1.dtype), pltpu.SemaphoreType.DMA],
        name="dispatch_rows")
    def scatter_rows(x_hbm, idx_hbm, out_hbm, idx_v, rows_v, sem):
        worker = lax.axis_index("subcore") * info.num_cores + lax.axis_index("core")
        pltpu.sync_copy(idx_hbm.at[worker], idx_v)
        base = worker * per_worker
        for c in range(n_chunks):
            pltpu.sync_copy(x_hbm.at[pl.ds(base + c * DISPATCH_ROWS, DISPATCH_ROWS)], rows_v)
            copies = [pltpu.async_copy(rows_v, out_hbm.at[idx_v.at[c * TOP_K + k]], sem) for k in range(TOP_K)]
            for cp in copies:
                cp.wait()

    return scatter_rows(x1, idx)


def _moe(layer, x1, routing, wg, bg, wu, bu, wd, bd):
    n_tok = x1.shape[0]
    n_assign = n_tok * TOP_K
    n_blocks = n_assign // EXPERT_ROWS
    idx_o, rank_o, gate_o, cnt = routing
    counts = cnt[0, :N_EXPERTS]
    ends = jnp.cumsum(counts)
    offs = ends - counts
    experts = jnp.arange(N_EXPERTS, dtype=jnp.int32)
    top_i = idx_o[:, :TOP_K]
    off_tok = jnp.sum(jnp.where(top_i[:, :, None] == experts, offs, 0), axis=-1)
    pos = (off_tok + rank_o[:, :TOP_K]).reshape(n_assign)
    xin = _dispatch_rows(x1, pos.reshape(n_tok, TOP_K))

    first_blk = offs // EXPERT_ROWS
    last_blk = (ends - 1) // EXPERT_ROWS
    nvis = jnp.where(counts > 0, last_blk - first_blk + 1, 0)
    vend = jnp.cumsum(nvis)
    vstart = vend - nvis
    n_visits = n_blocks + N_EXPERTS - 1
    v = jnp.arange(n_visits, dtype=jnp.int32)
    e_v = jnp.sum((vend[None, :] <= v[:, None]).astype(jnp.int32), axis=1)
    e_last = jnp.max(jnp.where(counts > 0, experts, 0))
    e_v = jnp.minimum(e_v, e_last)
    pick = lambda tab: jnp.sum(jnp.where(e_v[:, None] == experts, tab, 0), axis=-1)
    blk_v = jnp.minimum(pick(first_blk) + v - pick(vstart), n_blocks - 1)
    active = counts > 0
    later = jnp.where(jnp.logical_and(active[None, :], experts[None, :] > experts[:, None]), experts[None, :], N_EXPERTS)
    nxt_e = jnp.min(later, axis=1)
    nxt_e = jnp.where(nxt_e == N_EXPERTS, experts, nxt_e)
    order_e = jnp.cumsum(active.astype(jnp.int32)) - 1
    y_rows = _experts(layer, blk_v.astype(jnp.int32), e_v.astype(jnp.int32), pick(offs).astype(jnp.int32),
                      pick(ends).astype(jnp.int32), vend[-1:].astype(jnp.int32),
                      pick(nxt_e).astype(jnp.int32), (pick(order_e) % 2).astype(jnp.int32),
                      xin, wg, bg, wu, bu, wd, bd)
    return y_rows[pos.reshape(n_tok, TOP_K).T.reshape(n_assign)], gate_o


def _pack_w_in(w_in):
    qa, ka, va, oa, ia, fa, qb, fb, ib, gb = jnp.split(w_in, SPLIT_POINTS, axis=-1)

    def pad_heads(w):
        w = w.reshape(w.shape[:-1] + (N_HEADS, DK_A))
        w = jnp.pad(w, ((0, 0),) * (w.ndim - 1) + ((0, D_HEAD - DK_A),))
        return w.reshape(w.shape[:-2] + (N_HEADS * D_HEAD,))

    def pad_gate(w):
        return jnp.pad(w, ((0, 0),) * (w.ndim - 1) + ((0, LANES - N_HEADS),))

    cols = [pad_heads(qa), pad_heads(ka * (DK_A ** -0.5)), va, oa, qb, fb, ib, gb, pad_gate(ia), pad_gate(fa)]
    return jnp.concatenate(cols, axis=-1).astype(BF16)


def kernel(x_prompt, x_sample, state_mlstm_C, state_mlstm_n, state_mlstm_m, state_hgrn_S, p_prompt, p_sample,
           ln_in_g, ln_in_b, w_in, mlstm_ig_bias, mlstm_fg_bias, mlstm_norm_g, hgrn_lb_logits, hgrn_norm_g,
           w_out, ln1_g, ln1_b, w_router, b_router, w_gate, b_gate, w_up, b_up, w_down, b_down,
           w_ple_gate, w_ple_proj, ln2_g, ln2_b):
    bp, tp, d = x_prompt.shape
    bs, ts, _ = x_sample.shape
    n_p = bp * tp
    n_s = bs * ts
    ts_pad = SUBLANES
    assert tp % CHUNK == 0 and ts <= ts_pad and ts % CHUNK != 0

    lb_soft = jax.nn.softmax(hgrn_lb_logits.astype(F32), axis=0)
    lower_bounds = jnp.cumsum(lb_soft, axis=0) - lb_soft[0]

    w_in_p = _pack_w_in(w_in)
    gbias = jnp.stack([jnp.pad(mlstm_ig_bias, ((0, 0), (0, LANES - N_HEADS))),
                       jnp.pad(mlstm_fg_bias, ((0, 0), (0, LANES - N_HEADS)))], axis=1)
    w_out_b = w_out.astype(BF16)
    w_pg_b = w_ple_gate.astype(BF16)
    w_pp_b = w_ple_proj.astype(BF16)
    w_r = jnp.pad(w_router, ((0, 0), (0, 0), (0, LANES - N_EXPERTS)))
    w_r_hi = w_r.astype(BF16)
    w_r2 = jnp.concatenate([w_r_hi, (w_r - w_r_hi.astype(F32)).astype(BF16)], axis=-1)
    b_r = jnp.pad(b_router, ((0, 0), (0, LANES - N_EXPERTS))).reshape(DEPTH, 1, LANES)
    n0_pad = jnp.pad(state_mlstm_n, ((0, 0), (0, 0), (0, 0), (0, D_HEAD - DK_A))).reshape(
        DEPTH, bs, N_HEADS, 1, D_HEAD)
    m0_pad = jnp.pad(state_mlstm_m, ((0, 0), (0, 0), (0, LANES - N_HEADS))).reshape(DEPTH, bs, 1, LANES)

    p_p = p_prompt.reshape(DEPTH, n_p, D_PLE)
    p_s = p_sample.reshape(DEPTH, n_s, D_PLE)
    x = _ln(x_prompt.reshape(n_p, d), x_sample.reshape(n_s, d), ln_in_g, ln_in_b)

    assert n_p % ROW_TILE == 0 and n_s % ROW_TILE == 0
    nb_p = LANES // CHUNK
    nb_s = LANES // ts_pad
    st_p = _stacked_states(bp)
    st_s = _stacked_states(bs)
    for l in range(DEPTH):
        lb = lower_bounds[l].reshape(1, MIX)
        nga = mlstm_norm_g[l].reshape(1, MIX)
        ngb = hgrn_norm_g[l].reshape(1, MIX)
        proj_p = _inproj(x, w_in_p[l], 0, n_p).reshape(bp, tp, N_PROJ)
        h_p, st_p = _scan(proj_p, gbias[l], lb, nga, ngb, None, st_p, layer=l,
                          NB=nb_p, n_chunks=tp // CHUNK, L=CHUNK, last=CHUNK - 1)
        proj_s = jnp.pad(_inproj(x, w_in_p[l], n_p, n_s).reshape(bs, ts, N_PROJ), ((0, 0), (0, ts_pad - ts), (0, 0)))
        h_s, st_s = _scan(proj_s, gbias[l], lb, nga, ngb,
                          (state_mlstm_C[l], n0_pad[l], m0_pad[l], state_hgrn_S[l]), st_s, layer=l,
                          NB=nb_s, n_chunks=1, L=ts_pad, last=ts - 1)
        x1, resid, *routing = _post_mixer(l, x, h_p.reshape(n_p, d), h_s[:, :ts].reshape(n_s, d), p_p, p_s,
                                          w_out_b[l], ln1_g[l], ln1_b[l], w_pg_b[l], w_pp_b[l],
                                          w_r2[l], w_r_hi[l], b_r[l])
        y_k, gates = _moe(l, x1, routing, w_gate, b_gate.reshape(DEPTH, N_EXPERTS, 1, d),
                          w_up, b_up.reshape(DEPTH, N_EXPERTS, 1, d),
                          w_down, b_down.reshape(DEPTH, N_EXPERTS, 1, d))
        x = _combine_ln(resid, y_k, gates, ln2_g[l], ln2_b[l])

    y_prompt = x[:n_p].reshape(bp, tp, d)
    y_sample = x[n_p:].reshape(bs, ts, d)
    unpad = lambda st: (st[0], st[1][:, :, :, 0, :DK_A], st[2][:, :, 0, :N_HEADS], st[3])
    return (y_prompt, y_sample) + unpad(st_p) + unpad(st_s)
```

```python
import functools

import jax
import jax.numpy as jnp
import numpy as np
from jax import lax
from jax.experimental import pallas as pl
from jax.experimental.pallas import tpu as pltpu
from jax.experimental.pallas import tpu_sc as plsc

F32 = jnp.float32
BF16 = jnp.bfloat16

D_MODEL = 1024
DEPTH = 4
D_PLE = 256
N_HEADS = 4
DK_A = 64
D_HEAD = 128
MIX = 512
N_EXPERTS = 32
TOP_K = 4
SWIGLU_LIMIT = 7.0
SWIGLU_ALPHA = 1.702
CHUNK = 64
LN_EPS = 1e-5
NORM_EPS = 1e-6
LB_FLOOR = 1e-20
DEEPNORM_ALPHA = (2 * DEPTH) ** 0.25
SPLIT_SIZES = (256, 256, 512, 512, 4, 4, 512, 512, 512, 512)
SPLIT_POINTS = tuple(int(s) for s in np.cumsum(SPLIT_SIZES)[:-1])

LANES = 128
SUBLANES = 8
VMEM_LIMIT = 56 * 1024 * 1024

QA0, KA0, VA0, OA0 = 0, 512, 1024, 1536
QB0, FB0, IB0, GB0 = 2048, 2560, 3072, 3584
IG0, FG0 = 4096, 4224
N_PROJ = 4352

ROW_TILE = 256
EXPERT_ROWS = 256

NT_DIMS = (((1,), (1,)), ((), ()))
TN_DIMS = (((0,), (0,)), ((), ()))


def _params(*sem):
    return pltpu.CompilerParams(dimension_semantics=sem, vmem_limit_bytes=VMEM_LIMIT)


def _log_sigmoid(x):
    return jnp.minimum(x, 0.0) - jnp.log(1.0 + jnp.exp(-jnp.abs(x)))


def _split3(x):
    x1 = x.astype(BF16)
    r1 = x - x1.astype(F32)
    x2 = r1.astype(BF16)
    x3 = (r1 - x2.astype(F32)).astype(BF16)
    return x1, x2, x3


def _dot01(sel, x):
    return sum(jnp.dot(sel, t, preferred_element_type=F32) for t in _split3(x))


def _pack_pairs(x):
    w = x.shape[1] // 2
    bits = lax.bitcast_convert_type(x.astype(BF16).astype(F32), jnp.uint32)
    return lax.bitcast_convert_type(bits[:, :w] | (bits[:, w:] >> 16), jnp.int32)


def _unpack_pairs(words):
    u = lax.bitcast_convert_type(words, jnp.uint32)
    hi = lax.bitcast_convert_type(u & jnp.uint32(0xFFFF0000), F32)
    lo = lax.bitcast_convert_type(u << 16, F32)
    return hi, lo


def _layernorm_rows(x, g, b):
    mu = jnp.mean(x, axis=-1, keepdims=True)
    xc = x - mu
    var = jnp.mean(xc * xc, axis=-1, keepdims=True)
    return xc * lax.rsqrt(var + LN_EPS) * g + b


def _two_sources(n_a, n_b, width, lead=()):
    ta = n_a // ROW_TILE
    assert n_a % ROW_TILE == 0 and n_b % ROW_TILE == 0
    block = (None,) * len(lead) + (ROW_TILE, width)
    spec_a = pl.BlockSpec(block, lambda i: lead + (jnp.minimum(i, ta - 1), 0))
    spec_b = pl.BlockSpec(block, lambda i: lead + (jnp.maximum(i - ta, 0), 0))
    return spec_a, spec_b, ta


def _ln_kernel(xa_ref, xb_ref, g_ref, b_ref, o_ref, *, tiles_a):
    x = jnp.where(pl.program_id(0) < tiles_a, xa_ref[...], xb_ref[...])
    o_ref[...] = _layernorm_rows(x, g_ref[...], b_ref[...])


def _ln(xa, xb, g, b):
    d = xa.shape[1]
    n = xa.shape[0] + xb.shape[0]
    spec_a, spec_b, ta = _two_sources(xa.shape[0], xb.shape[0], d)
    vec = pl.BlockSpec((1, d), lambda i: (0, 0))
    return pl.pallas_call(
        functools.partial(_ln_kernel, tiles_a=ta), grid=(n // ROW_TILE,), in_specs=[spec_a, spec_b, vec, vec],
        out_specs=pl.BlockSpec((ROW_TILE, d), lambda i: (i, 0)),
        out_shape=jax.ShapeDtypeStruct((n, d), F32), compiler_params=_params("parallel"),
        name="ln_in")(xa, xb, g.reshape(1, d), b.reshape(1, d))


def _combine_ln_kernel(resid_ref, y0_ref, y1_ref, y2_ref, y3_ref, gate_ref, g_ref, beta_ref, o_ref):
    half = resid_ref.shape[1] // 2
    acc_hi = resid_ref[:, :half]
    acc_lo = resid_ref[:, half:]
    for k, y_ref in enumerate((y0_ref, y1_ref, y2_ref, y3_ref)):
        hi, lo = _unpack_pairs(y_ref[...])
        gk = gate_ref[:, k:k + 1]
        acc_hi = acc_hi + gk * hi
        acc_lo = acc_lo + gk * lo
    o_ref[...] = _layernorm_rows(jnp.concatenate([acc_hi, acc_lo], axis=1), g_ref[...], beta_ref[...])


def _combine_ln(resid, y_k, gates, g, beta):
    n, d = resid.shape
    nt = n // ROW_TILE
    row = lambda w: pl.BlockSpec((ROW_TILE, w), lambda i: (i, 0))
    yk = lambda k: pl.BlockSpec((ROW_TILE, d // 2), lambda i: (k * nt + i, 0))
    vec = pl.BlockSpec((1, d), lambda i: (0, 0))
    return pl.pallas_call(
        _combine_ln_kernel, grid=(nt,),
        in_specs=[row(d)] + [yk(k) for k in range(TOP_K)] + [row(LANES), vec, vec],
        out_specs=row(d), out_shape=jax.ShapeDtypeStruct((n, d), F32), compiler_params=_params("parallel"),
        name="combine_ln2")(resid, y_k, y_k, y_k, y_k, gates, g.reshape(1, d), beta.reshape(1, d))


def _inproj_kernel(x_ref, w_ref, o_ref):
    o_ref[...] = jnp.dot(x_ref[...].astype(BF16), w_ref[...], preferred_element_type=F32)


def _inproj(x, w, row0, n_rows):
    d = x.shape[1]
    t0 = row0 // ROW_TILE
    return pl.pallas_call(
        _inproj_kernel, grid=(n_rows // ROW_TILE,),
        in_specs=[pl.BlockSpec((ROW_TILE, d), lambda i: (t0 + i, 0)),
                  pl.BlockSpec((d, N_PROJ), lambda i: (0, 0))],
        out_specs=pl.BlockSpec((ROW_TILE, N_PROJ), lambda i: (i, 0)),
        out_shape=jax.ShapeDtypeStruct((n_rows, N_PROJ), F32), compiler_params=_params("parallel"),
        name="in_proj")(x, w)


def _scan_kernel(*refs, NB, L, last, has_state):
    it = iter(refs)
    proj_ref, gbias_ref, lb_ref, nga_ref, ngb_ref = (next(it) for _ in range(5))
    if has_state:
        C0_ref, n0_ref, m0_ref, S0_ref = (next(it) for _ in range(4))
    for _ in range(4):
        next(it)
    h_ref, C_out, n_out, m_out, S_out = (next(it) for _ in range(5))
    C_s, n_s, m_s, St_s = (next(it) for _ in range(4))

    c = pl.program_id(1)
    nc = pl.num_programs(1)
    H = range(N_HEADS)
    N = range(NB)
    R = NB * L
    assert R == LANES

    @pl.when(c == 0)
    def _init():
        if has_state:
            for n in N:
                for h in H:
                    C_s[n, h, 0:DK_A, :] = C0_ref[n, h]
                    C_s[n, h, DK_A:D_HEAD, :] = jnp.zeros((D_HEAD - DK_A, D_HEAD), F32)
                    St_s[n, h] = S0_ref[n, h].T
            n_s[...] = n0_ref[...]
            m_s[...] = m0_ref[...]
        else:
            C_s[...] = jnp.zeros(C_s.shape, F32)
            n_s[...] = jnp.zeros(n_s.shape, F32)
            m_s[...] = jnp.zeros(m_s.shape, F32)
            St_s[...] = jnp.zeros(St_s.shape, F32)

    row = lax.broadcasted_iota(jnp.int32, (R, R), 0)
    col = lax.broadcasted_iota(jnp.int32, (R, R), 1)
    causal = jnp.logical_and(row // L == col // L, row >= col)
    tril = causal.astype(BF16)
    ones_r = jnp.ones((R, LANES), BF16)
    lane = lax.broadcasted_iota(jnp.int32, (R, LANES), 1)
    step = lax.broadcasted_iota(jnp.int32, (R, 1), 0) % L
    valid = step <= last
    padded = last < L - 1
    tile = lambda base, h: slice(base + LANES * h, base + LANES * (h + 1))
    seq = lambda n: slice(n * L, (n + 1) * L)
    cols = lambda base, width: proj_ref[:, :, base:base + width].reshape(R, width)
    per_seq = lambda vals: jnp.concatenate([jnp.broadcast_to(x, (L, x.shape[-1])) for x in vals], axis=0)
    last_rows = lambda x: [x[n * L + last:n * L + last + 1] for n in N]

    i_t = cols(IG0, LANES) + gbias_ref[0:1, :]
    f_t = _log_sigmoid(cols(FG0, LANES) + gbias_ref[1:2, :])
    zf = cols(FB0, MIX)
    lb = lb_ref[...]
    la = jnp.log(jnp.maximum(lb, LB_FLOOR))
    bb = jnp.log1p(-lb) + _log_sigmoid(zf)
    f_log = jnp.maximum(la, bb) + jnp.log(1.0 + jnp.exp(-jnp.abs(la - bb)))
    b_t = _dot01(tril, f_t)
    g = _dot01(tril, f_log)
    m_all = [m_s[n] for n in N]
    C_prev = [[C_s[n, h] for h in H] for n in N]
    n_prev = [[n_s[n, h] for h in H] for n in N]
    St = [[St_s[n, h] for h in H] for n in N]
    qf = [cols(QA0 + LANES * h, LANES) for h in H]
    kf = [cols(KA0 + LANES * h, LANES) for h in H]
    q = [x.astype(BF16) for x in qf]
    k = [x.astype(BF16) for x in kf]
    v = [cols(VA0 + LANES * h, LANES).astype(BF16) for h in H]
    kb = (1.0 - lb) * jax.nn.sigmoid(-zf)
    qr = cols(QB0, MIX)
    qb = qr * jax.nn.sigmoid(qr)
    vb = [cols(IB0 + LANES * h, LANES).astype(BF16) for h in H]
    qb16 = qb.astype(BF16)
    kb16 = kb.astype(BF16)
    r_t = i_t - b_t
    zero16 = jnp.zeros((R, LANES), BF16)
    r_all = sum(
        lax.dot_general(ones_r, jnp.concatenate([jnp.where(lane == h, term, zero16) for h in H], axis=0),
                        NT_DIMS, preferred_element_type=F32)
        for term in _split3(r_t))
    r_m = [r_all[:, R * h:R * (h + 1)] for h in H]
    qk = [lax.dot_general(q[h], k[h], NT_DIMS, preferred_element_type=F32) for h in H]
    qC = [jnp.concatenate([jnp.dot(q[h][seq(n)], C_prev[n][h].astype(BF16), preferred_element_type=F32)
                           for n in N], axis=0) for h in H]
    levels = []
    bs = 1
    while bs < L:
        levels.append(bs)
        bs *= 2
    small = [bs for bs in levels if bs < SUBLANES]
    g_anchor = {}
    if small:
        sel = jnp.concatenate(
            [(col == (row // (2 * bs)) * (2 * bs) + bs).astype(BF16) for bs in small], axis=0)
        picked = _dot01(sel, g)
        for i, bs in enumerate(small):
            g_anchor[bs] = picked[i * R:(i + 1) * R]
    for bs in levels:
        if bs >= SUBLANES:
            g_anchor[bs] = jnp.concatenate(
                [jnp.broadcast_to(g[p0 + bs:p0 + bs + 1], (2 * bs, MIX)) for p0 in range(0, R, 2 * bs)], axis=0)
    qg = (qb * jnp.exp(g)).astype(BF16)
    o_inter = [jnp.concatenate([lax.dot_general(qg[seq(n), tile(0, h)], St[n][h].astype(BF16), NT_DIMS,
                                                preferred_element_type=F32) for n in N], axis=0) for h in H]
    a = [jnp.where(row == col, lax.dot_general(qb16[:, tile(0, h)], kb16[:, tile(0, h)], NT_DIMS,
                                                preferred_element_type=F32), 0.0) for h in H]
    qn = [jnp.sum(qf[h] * per_seq([n_prev[n][h] for n in N]), axis=1, keepdims=True) for h in H]
    b_col = [b_t[:, h:h + 1] for h in H]
    i_col = [i_t[:, h:h + 1] for h in H]
    m_prev = [per_seq([m_all[n][:, h:h + 1] for n in N]) for h in H]
    cm = [jnp.max(jnp.where(causal, r_m[h], -jnp.inf), axis=1, keepdims=True) for h in H]
    m_t = [b_col[h] + jnp.maximum(m_prev[h], cm[h]) for h in H]
    inter = [jnp.exp(b_col[h] + m_prev[h] - m_t[h]) for h in H]
    d = [jnp.exp(jnp.where(causal, (b_col[h] - m_t[h]) + r_m[h], -1e30)) for h in H]
    s = [qk[h] * d[h] for h in H]
    log2e = 1.4426950408889634
    e = [jnp.exp2((g - g_anchor[bs]) * jnp.where((step // bs) % 2 == 1, log2e, -log2e)) for bs in levels]
    qe = [(qb * x).astype(BF16) for x in e]
    ke = [(kb * x).astype(BF16) for x in e]
    sv = [jnp.dot(s[h].astype(BF16), v[h], preferred_element_type=F32) for h in H]
    sc = [[lax.dot_general(qe[i][:, tile(0, h)], ke[i][:, tile(0, h)], NT_DIMS, preferred_element_type=F32)
           for h in H] for i in range(len(levels))]
    den = [jnp.sum(s[h], axis=1, keepdims=True) + inter[h] * qn[h] for h in H]
    m_new = [last_rows(m_t[h]) for h in H]
    b_last = [last_rows(b_col[h]) for h in H]
    w_arg = [per_seq(b_last[h]) - b_col[h] + i_col[h] - per_seq(m_new[h]) for h in H]
    if padded:
        w_arg = [jnp.where(valid, x, -1e30) for x in w_arg]
    kw = [kf[h] * jnp.exp(w_arg[h]) for h in H]
    kw16 = [x.astype(BF16) for x in kw]
    decay = [[jnp.exp(b_last[h][n] + m_all[n][:, h:h + 1] - m_new[h][n]) for h in H] for n in N]
    kv = [[lax.dot_general(kw16[h][seq(n)], v[h][seq(n)], TN_DIMS, preferred_element_type=F32) for h in H]
          for n in N]
    for i, bs in enumerate(levels):
        pair = jnp.logical_and(row // (2 * bs) == col // (2 * bs),
                               jnp.logical_and((row // bs) % 2 == 1, (col // bs) % 2 == 0))
        for h in H:
            a[h] = jnp.where(pair, sc[i][h], a[h])
    o = [o_inter[h] + jnp.dot(a[h].astype(BF16), vb[h], preferred_element_type=F32) for h in H]
    g_last = last_rows(g)
    dec_arg = per_seq(g_last) - g
    if padded:
        dec_arg = jnp.where(valid, dec_arg, -1e30)
    kdec = (kb * jnp.exp(dec_arg)).astype(BF16)
    eg_last = [jnp.exp(x) for x in g_last]
    vk = [[lax.dot_general(vb[h][seq(n)], kdec[seq(n), tile(0, h)], TN_DIMS, preferred_element_type=F32)
           for h in H] for n in N]
    hh = [(sv[h] + inter[h] * qC[h]) / jnp.maximum(jnp.abs(den[h]), jnp.exp(-m_t[h])) for h in H]
    for h in H:
        hn = hh[h] * lax.rsqrt(jnp.mean(hh[h] * hh[h], axis=-1, keepdims=True) + NORM_EPS) * nga_ref[:, tile(0, h)]
        out_a = jax.nn.sigmoid(cols(OA0 + LANES * h, LANES)) * hn
        h_ref[:, :, tile(0, h)] = out_a.reshape(NB, L, LANES)
    for h in H:
        on = o[h] * lax.rsqrt(jnp.mean(o[h] * o[h], axis=-1, keepdims=True) + NORM_EPS) * ngb_ref[:, tile(0, h)]
        gr = cols(GB0 + LANES * h, LANES)
        h_ref[:, :, tile(MIX, h)] = (gr * jax.nn.sigmoid(gr) * on).reshape(NB, L, LANES)
    lane1 = lax.broadcasted_iota(jnp.int32, (1, LANES), 1)
    for n in N:
        for h in H:
            C_s[n, h] = decay[n][h] * C_prev[n][h] + kv[n][h]
            n_s[n, h] = decay[n][h] * n_prev[n][h] + jnp.sum(kw[h][seq(n)], axis=0, keepdims=True)
            St_s[n, h] = eg_last[n][:, tile(0, h)] * St[n][h] + vk[n][h]
        m_upd = m_all[n]
        for h in H:
            m_upd = jnp.where(lane1 == h, m_new[h][n], m_upd)
        m_s[n] = m_upd

    @pl.when(c == nc - 1)
    def _finish():
        for n in N:
            for h in H:
                C_out[n, h] = C_s[n, h, 0:DK_A, :]
                S_out[n, h] = St_s[n, h].T
        n_out[...] = n_s[...]
        m_out[...] = m_s[...]


def _scan(proj, gbias, lb, nga, ngb, state, stacked, *, layer, NB, n_chunks, L, last):
    batch = proj.shape[0]
    has_state = state is not None
    const2 = lambda b, c: (0, 0)
    in_specs = [
        pl.BlockSpec((NB, L, N_PROJ), lambda b, c: (b, c, 0)),
        pl.BlockSpec((2, LANES), const2),
        pl.BlockSpec((1, MIX), const2),
        pl.BlockSpec((1, MIX), const2),
        pl.BlockSpec((1, MIX), const2),
    ]
    args = [proj, gbias, lb, nga, ngb]
    st4 = lambda b, c: (b, 0, 0, 0)
    st3 = lambda b, c: (b, 0, 0)
    if has_state:
        in_specs += [
            pl.BlockSpec((NB, N_HEADS, DK_A, D_HEAD), st4),
            pl.BlockSpec((NB, N_HEADS, 1, LANES), st4),
            pl.BlockSpec((NB, 1, LANES), st3),
            pl.BlockSpec((NB, N_HEADS, D_HEAD, D_HEAD), st4),
        ]
        args += list(state)
    aliases = {len(args) + i: 1 + i for i in range(4)}
    in_specs += [pl.BlockSpec(memory_space=pl.ANY)] * 4
    args += list(stacked)
    lst4 = lambda b, c: (layer, b, 0, 0, 0)
    lst3 = lambda b, c: (layer, b, 0, 0)
    out_specs = [
        pl.BlockSpec((NB, L, D_MODEL), lambda b, c: (b, c, 0)),
        pl.BlockSpec((None, NB, N_HEADS, DK_A, D_HEAD), lst4),
        pl.BlockSpec((None, NB, N_HEADS, 1, LANES), lst4),
        pl.BlockSpec((None, NB, 1, LANES), lst3),
        pl.BlockSpec((None, NB, N_HEADS, D_HEAD, D_HEAD), lst4),
    ]
    out_shape = [jax.ShapeDtypeStruct((batch, n_chunks * L, D_MODEL), F32)] + [
        jax.ShapeDtypeStruct(a.shape, a.dtype) for a in stacked]
    scratch = [
        pltpu.VMEM((NB, N_HEADS, D_HEAD, D_HEAD), F32),
        pltpu.VMEM((NB, N_HEADS, 1, LANES), F32),
        pltpu.VMEM((NB, 1, LANES), F32),
        pltpu.VMEM((NB, N_HEADS, D_HEAD, D_HEAD), F32),
    ]
    kern = functools.partial(_scan_kernel, NB=NB, L=L, last=last, has_state=has_state)
    res = pl.pallas_call(
        kern, grid=(batch // NB, n_chunks), in_specs=in_specs, out_specs=out_specs, out_shape=out_shape,
        scratch_shapes=scratch, input_output_aliases=aliases,
        compiler_params=_params("parallel", "arbitrary"),
        name="scan_state" if has_state else "scan_prompt")(*args)
    return res[0], tuple(res[1:])


def _stacked_states(batch):
    return (jnp.zeros((DEPTH, batch, N_HEADS, DK_A, D_HEAD), F32),
            jnp.zeros((DEPTH, batch, N_HEADS, 1, LANES), F32),
            jnp.zeros((DEPTH, batch, 1, LANES), F32),
            jnp.zeros((DEPTH, batch, N_HEADS, D_HEAD, D_HEAD), F32))


def _post_mixer_kernel(x_ref, ha_ref, hb_ref, pa_ref, pb_ref, wout_ref, g1_ref, b1_ref, wpg_ref, wpp_ref, wr2_ref,
                       wrh_ref, br_ref, x1_ref, resid_ref, idx_ref, rank_ref, gate_ref, cnt_ref, carry_s, *, tiles_a):
    i = pl.program_id(0)

    @pl.when(i == 0)
    def _init():
        carry_s[...] = jnp.zeros(carry_s.shape, F32)

    from_a = i < tiles_a
    h = jnp.where(from_a, ha_ref[...], hb_ref[...])
    p = jnp.where(from_a, pa_ref[...], pb_ref[...])
    mix = jnp.dot(h.astype(BF16), wout_ref[...], preferred_element_type=F32)
    x1 = _layernorm_rows(DEEPNORM_ALPHA * x_ref[...] + mix, g1_ref[...], b1_ref[...])
    x1b = x1.astype(BF16)
    x1_ref[...] = _pack_pairs(x1)
    x1_lo = (x1 - x1b.astype(F32)).astype(BF16)
    l2 = jnp.dot(x1b, wr2_ref[...], preferred_element_type=F32)
    logits = (l2[:, :LANES] + l2[:, LANES:]
              + jnp.dot(x1_lo, wrh_ref[...], preferred_element_type=F32) + br_ref[...])

    tm = logits.shape[0]
    lane = lax.broadcasted_iota(jnp.int32, (tm, LANES), 1)
    lane_f = lane.astype(F32)
    l = jnp.where(lane < N_EXPERTS, logits, -jnp.inf)
    vals, idxs, onehots = [], [], []
    for _ in range(TOP_K):
        mx = jnp.max(l, axis=1, keepdims=True)
        ix = jnp.min(jnp.where(l == mx, lane_f, float(LANES)), axis=1, keepdims=True)
        sel = lane_f == ix
        vals.append(mx)
        idxs.append(ix)
        onehots.append(sel.astype(F32))
        l = jnp.where(sel, -jnp.inf, l)
    gate = jax.nn.sigmoid(jnp.dot(x1b, wpg_ref[...], preferred_element_type=F32))
    pp = jnp.dot(p.astype(BF16), wpp_ref[...], preferred_element_type=F32)
    resid_ref[...] = DEEPNORM_ALPHA * x1 + gate * pp

    w = [jnp.exp(val - vals[0]) for val in vals]
    tot = w[0] + w[1] + w[2] + w[3]
    oh_all = onehots[0] + onehots[1] + onehots[2] + onehots[3]
    row = lax.broadcasted_iota(jnp.int32, (tm, tm), 0)
    col = lax.broadcasted_iota(jnp.int32, (tm, tm), 1)
    earlier = (row > col).astype(BF16)
    prefix = jnp.dot(earlier, oh_all.astype(BF16), preferred_element_type=F32) + carry_s[...]
    idx_o = jnp.zeros((tm, LANES), F32)
    rank_o = jnp.zeros((tm, LANES), F32)
    gate_o = jnp.zeros((tm, LANES), F32)
    for k in range(TOP_K):
        rank_k = jnp.sum(onehots[k] * prefix, axis=1, keepdims=True)
        idx_o = jnp.where(lane == k, idxs[k], idx_o)
        rank_o = jnp.where(lane == k, rank_k, rank_o)
        gate_o = jnp.where(lane == k, w[k] / tot, gate_o)
    idx_ref[...] = idx_o.astype(jnp.int32)
    rank_ref[...] = rank_o.astype(jnp.int32)
    gate_ref[...] = gate_o
    carry_s[...] = carry_s[...] + jnp.sum(oh_all, axis=0, keepdims=True)
    cnt_ref[...] = carry_s[...].astype(jnp.int32)


def _post_mixer(layer, x, ha, hb, pa, pb, wout, g1, b1, wpg, wpp, wr2, wrh, br):
    n, d = x.shape
    row = lambda w: pl.BlockSpec((ROW_TILE, w), lambda i: (i, 0))
    full = lambda a: pl.BlockSpec(a.shape, lambda i: (0, 0))
    ha_spec, hb_spec, ta = _two_sources(ha.shape[0], hb.shape[0], d)
    pa_spec, pb_spec, _ = _two_sources(pa.shape[1], pb.shape[1], D_PLE, lead=(layer,))
    g1 = g1.reshape(1, d)
    b1 = b1.reshape(1, d)
    return pl.pallas_call(
        functools.partial(_post_mixer_kernel, tiles_a=ta), grid=(n // ROW_TILE,),
        in_specs=[row(d), ha_spec, hb_spec, pa_spec, pb_spec, full(wout), full(g1), full(b1), full(wpg), full(wpp),
                  full(wr2), full(wrh), full(br)],
        out_specs=[row(d // 2), row(d), row(LANES), row(LANES), row(LANES), pl.BlockSpec((1, LANES), lambda i: (0, 0))],
        out_shape=[jax.ShapeDtypeStruct((n, d // 2), jnp.int32), jax.ShapeDtypeStruct((n, d), F32),
                   jax.ShapeDtypeStruct((n, LANES), jnp.int32), jax.ShapeDtypeStruct((n, LANES), jnp.int32),
                   jax.ShapeDtypeStruct((n, LANES), F32), jax.ShapeDtypeStruct((1, LANES), jnp.int32)],
        scratch_shapes=[pltpu.VMEM((1, LANES), F32)],
        compiler_params=_params("arbitrary"), name="post_mixer")(
            x, ha, hb, pa, pb, wout, g1, b1, wpg, wpp, wr2, wrh, br)


def _expert_kernel(blk_ref, e_ref, lo_ref, hi_ref, nv_ref, nxt_ref, slot_ref, x_ref, wg_hbm, bg_ref, wu_hbm, bu_ref,
                   wd_hbm, bd_ref, o_ref, w32_s, wg_s, wu_s, wd_s, sem, *, layer):
    v = pl.program_id(0)
    prev = jnp.maximum(v - 1, 0)
    valid = v < nv_ref[0]
    new_expert = jnp.logical_or(v == 0, e_ref[v] != e_ref[prev])
    new_block = jnp.logical_or(v == 0, blk_ref[v] != blk_ref[prev])
    slot = slot_ref[v]

    def weight_copies(expert, to_slot):
        return [pltpu.make_async_copy(w_hbm.at[layer, expert], w32_s.at[to_slot, j], sem.at[to_slot, j])
                for j, w_hbm in enumerate((wg_hbm, wu_hbm, wd_hbm))]

    @pl.when(v == 0)
    def _prime():
        for cp in weight_copies(e_ref[0], slot):
            cp.start()

    @pl.when(jnp.logical_and(valid, new_expert))
    def _next_weights():
        for cp in weight_copies(e_ref[v], slot):
            cp.wait()

        @pl.when(nxt_ref[v] != e_ref[v])
        def _prefetch():
            for cp in weight_copies(nxt_ref[v], 1 - slot):
                cp.start()

        wg_s[...] = w32_s[slot, 0].astype(BF16)
        wu_s[...] = w32_s[slot, 1].astype(BF16)
        wd_s[...] = w32_s[slot, 2].astype(BF16)

    @pl.when(valid)
    def _compute():
        half = wg_s.shape[0] // 2
        x_hi, x_lo = (t.astype(BF16) for t in _unpack_pairs(x_ref[...]))
        gate = (jnp.dot(x_hi, wg_s[0:half, :], preferred_element_type=F32)
                + jnp.dot(x_lo, wg_s[half:, :], preferred_element_type=F32) + bg_ref[0])
        up = (jnp.dot(x_hi, wu_s[0:half, :], preferred_element_type=F32)
              + jnp.dot(x_lo, wu_s[half:, :], preferred_element_type=F32) + bu_ref[0])
        gate = jnp.minimum(gate, SWIGLU_LIMIT)
        up = jnp.clip(up, -SWIGLU_LIMIT, SWIGLU_LIMIT)
        glu = gate * jax.nn.sigmoid(gate * SWIGLU_ALPHA)
        act = ((up + 1.0) * glu).astype(BF16)
        y = _pack_pairs(jnp.dot(act, wd_s[...], preferred_element_type=F32) + bd_ref[0])
        rows = blk_ref[v] * EXPERT_ROWS + lax.broadcasted_iota(jnp.int32, (EXPERT_ROWS, 1), 0)
        mine = jnp.logical_and(rows >= lo_ref[v], rows < hi_ref[v])

        @pl.when(new_block)
        def _first():
            o_ref[...] = jnp.where(mine, y, 0)

        @pl.when(jnp.logical_not(new_block))
        def _again():
            o_ref[...] = jnp.where(mine, y, o_ref[...])


def _experts(layer, blk_v, e_v, lo_v, hi_v, n_vis, nxt_v, slot_v, xin, wg, bg, wu, bu, wd, bd):
    n_rows = xin.shape[0]
    d = wg.shape[-1]
    n_visits = blk_v.shape[0]
    xspec = pl.BlockSpec((EXPERT_ROWS, d // 2), lambda v, blk, e, *_: (blk[v], 0))
    ospec = pl.BlockSpec((EXPERT_ROWS, d // 2), lambda v, blk, e, *_: (blk[v], 0))
    wspec = pl.BlockSpec(memory_space=pl.ANY)
    bspec = pl.BlockSpec((None, 1, 1, d), lambda v, blk, e, *_: (layer, e[v], 0, 0))
    grid_spec = pltpu.PrefetchScalarGridSpec(
        num_scalar_prefetch=7, grid=(n_visits,),
        in_specs=[xspec, wspec, bspec, wspec, bspec, wspec, bspec], out_specs=ospec,
        scratch_shapes=[pltpu.VMEM((2, 3, d, d), F32)] + [pltpu.VMEM((d, d), BF16)] * 3
        + [pltpu.SemaphoreType.DMA((2, 3))])
    return pl.pallas_call(
        functools.partial(_expert_kernel, layer=layer), grid_spec=grid_spec,
        out_shape=jax.ShapeDtypeStruct((n_rows, d // 2), jnp.int32),
        compiler_params=_params("arbitrary"), name="experts")(
            blk_v, e_v, lo_v, hi_v, n_vis, nxt_v, slot_v, xin, wg, bg, wu, bu, wd, bd)


DISPATCH_ROWS = 48


def _sc_layout(n_tok):
    info = plsc.get_sparse_core_info()
    n_workers = info.num_cores * info.num_subcores
    per_worker = n_tok // n_workers
    n_chunks = per_worker // DISPATCH_ROWS
    assert per_worker * n_workers == n_tok and n_chunks * DISPATCH_ROWS == per_worker
    return info, n_workers, per_worker, n_chunks


def _sc_index(pos, n_workers, n_chunks):
    return pos.T.reshape(TOP_K, n_workers, n_chunks, DISPATCH_ROWS).transpose(1, 2, 0, 3).reshape(
        n_workers, n_chunks * TOP_K, DISPATCH_ROWS)


def _dispatch_rows(x1, pos):
    n_tok, w = x1.shape
    info, n_workers, per_worker, n_chunks = _sc_layout(n_tok)
    mesh = plsc.VectorSubcoreMesh(core_axis_name="core", subcore_axis_name="subcore")

    @functools.partial(
        pl.kernel, out_type=jax.ShapeDtypeStruct((n_tok * TOP_K, w), x1.dtype), mesh=mesh,
        scratch_types=[pltpu.VMEM((n_chunks * TOP_K, DISPATCH_ROWS), jnp.int32),
                       pltpu.VMEM((DISPATCH_ROWS, w), x1.dtype), pltpu.SemaphoreType.DMA],
        name="dispatch_rows")
    def scatter_rows(x_hbm, idx_hbm, out_hbm, idx_v, rows_v, sem):
        worker = lax.axis_index("subcore") * info.num_cores + lax.axis_index("core")
        pltpu.sync_copy(idx_hbm.at[worker], idx_v)
        base = worker * per_worker
        for c in range(n_chunks):
            pltpu.sync_copy(x_hbm.at[pl.ds(base + c * DISPATCH_ROWS, DISPATCH_ROWS)], rows_v)
            copies = [pltpu.async_copy(rows_v, out_hbm.at[idx_v.at[c * TOP_K + k]], sem) for k in range(TOP_K)]
            for cp in copies:
                cp.wait()

    return scatter_rows(x1, _sc_index(pos, n_workers, n_chunks))


def _collect_rows(y_rows, pos):
    n_tok = pos.shape[0]
    w = y_rows.shape[1]
    info, n_workers, per_worker, n_chunks = _sc_layout(n_tok)
    mesh = plsc.VectorSubcoreMesh(core_axis_name="core", subcore_axis_name="subcore")

    @functools.partial(
        pl.kernel, out_type=jax.ShapeDtypeStruct((n_tok * TOP_K, w), y_rows.dtype), mesh=mesh,
        scratch_types=[pltpu.VMEM((n_chunks * TOP_K, DISPATCH_ROWS), jnp.int32),
                       pltpu.VMEM((TOP_K, DISPATCH_ROWS, w), y_rows.dtype), pltpu.SemaphoreType.DMA],
        name="collect_rows")
    def gather_rows(y_hbm, idx_hbm, out_hbm, idx_v, rows_v, sem):
        worker = lax.axis_index("subcore") * info.num_cores + lax.axis_index("core")
        pltpu.sync_copy(idx_hbm.at[worker], idx_v)
        base = worker * per_worker
        for c in range(n_chunks):
            copies = [pltpu.async_copy(y_hbm.at[idx_v.at[c * TOP_K + k]], rows_v.at[k], sem) for k in range(TOP_K)]
            for cp in copies:
                cp.wait()
            for k in range(TOP_K):
                pltpu.sync_copy(rows_v.at[k], out_hbm.at[pl.ds(k * n_tok + base + c * DISPATCH_ROWS, DISPATCH_ROWS)])

    return gather_rows(y_rows, _sc_index(pos, n_workers, n_chunks))


def _moe(layer, x1, routing, wg, bg, wu, bu, wd, bd):
    n_tok = x1.shape[0]
    n_assign = n_tok * TOP_K
    n_blocks = n_assign // EXPERT_ROWS
    idx_o, rank_o, gate_o, cnt = routing
    counts = cnt[0, :N_EXPERTS]
    ends = jnp.cumsum(counts)
    offs = ends - counts
    experts = jnp.arange(N_EXPERTS, dtype=jnp.int32)
    top_i = idx_o[:, :TOP_K]
    off_tok = jnp.sum(jnp.where(top_i[:, :, None] == experts, offs, 0), axis=-1)
    pos = (off_tok + rank_o[:, :TOP_K]).reshape(n_assign)
    pos = pos.reshape(n_tok, TOP_K)
    xin = _dispatch_rows(x1, pos)

    first_blk = offs // EXPERT_ROWS
    last_blk = (ends - 1) // EXPERT_ROWS
    nvis = jnp.where(counts > 0, last_blk - first_blk + 1, 0)
    vend = jnp.cumsum(nvis)
    vstart = vend - nvis
    n_visits = n_blocks + N_EXPERTS - 1
    v = jnp.arange(n_visits, dtype=jnp.int32)
    e_v = jnp.sum((vend[None, :] <= v[:, None]).astype(jnp.int32), axis=1)
    e_last = jnp.max(jnp.where(counts > 0, experts, 0))
    e_v = jnp.minimum(e_v, e_last)
    pick = lambda tab: jnp.sum(jnp.where(e_v[:, None] == experts, tab, 0), axis=-1)
    blk_v = jnp.minimum(pick(first_blk) + v - pick(vstart), n_blocks - 1)
    active = counts > 0
    later = jnp.where(jnp.logical_and(active[None, :], experts[None, :] > experts[:, None]), experts[None, :], N_EXPERTS)
    nxt_e = jnp.min(later, axis=1)
    nxt_e = jnp.where(nxt_e == N_EXPERTS, experts, nxt_e)
    order_e = jnp.cumsum(active.astype(jnp.int32)) - 1
    y_rows = _experts(layer, blk_v.astype(jnp.int32), e_v.astype(jnp.int32), pick(offs).astype(jnp.int32),
                      pick(ends).astype(jnp.int32), vend[-1:].astype(jnp.int32),
                      pick(nxt_e).astype(jnp.int32), (pick(order_e) % 2).astype(jnp.int32),
                      xin, wg, bg, wu, bu, wd, bd)
    return _collect_rows(y_rows, pos), gate_o


def _pack_w_in(w_in):
    qa, ka, va, oa, ia, fa, qb, fb, ib, gb = jnp.split(w_in, SPLIT_POINTS, axis=-1)

    def pad_heads(w):
        w = w.reshape(w.shape[:-1] + (N_HEADS, DK_A))
        w = jnp.pad(w, ((0, 0),) * (w.ndim - 1) + ((0, D_HEAD - DK_A),))
        return w.reshape(w.shape[:-2] + (N_HEADS * D_HEAD,))

    def pad_gate(w):
        return jnp.pad(w, ((0, 0),) * (w.ndim - 1) + ((0, LANES - N_HEADS),))

    cols = [pad_heads(qa), pad_heads(ka * (DK_A ** -0.5)), va, oa, qb, fb, ib, gb, pad_gate(ia), pad_gate(fa)]
    return jnp.concatenate(cols, axis=-1).astype(BF16)


def kernel(x_prompt, x_sample, state_mlstm_C, state_mlstm_n, state_mlstm_m, state_hgrn_S, p_prompt, p_sample,
           ln_in_g, ln_in_b, w_in, mlstm_ig_bias, mlstm_fg_bias, mlstm_norm_g, hgrn_lb_logits, hgrn_norm_g,
           w_out, ln1_g, ln1_b, w_router, b_router, w_gate, b_gate, w_up, b_up, w_down, b_down,
           w_ple_gate, w_ple_proj, ln2_g, ln2_b):
    bp, tp, d = x_prompt.shape
    bs, ts, _ = x_sample.shape
    n_p = bp * tp
    n_s = bs * ts
    ts_pad = SUBLANES
    assert tp % CHUNK == 0 and ts <= ts_pad and ts % CHUNK != 0

    lb_soft = jax.nn.softmax(hgrn_lb_logits.astype(F32), axis=0)
    lower_bounds = jnp.cumsum(lb_soft, axis=0) - lb_soft[0]

    w_in_p = _pack_w_in(w_in)
    gbias = jnp.stack([jnp.pad(mlstm_ig_bias, ((0, 0), (0, LANES - N_HEADS))),
                       jnp.pad(mlstm_fg_bias, ((0, 0), (0, LANES - N_HEADS)))], axis=1)
    w_out_b = w_out.astype(BF16)
    w_pg_b = w_ple_gate.astype(BF16)
    w_pp_b = w_ple_proj.astype(BF16)
    w_r = jnp.pad(w_router, ((0, 0), (0, 0), (0, LANES - N_EXPERTS)))
    w_r_hi = w_r.astype(BF16)
    w_r2 = jnp.concatenate([w_r_hi, (w_r - w_r_hi.astype(F32)).astype(BF16)], axis=-1)
    b_r = jnp.pad(b_router, ((0, 0), (0, LANES - N_EXPERTS))).reshape(DEPTH, 1, LANES)
    n0_pad = jnp.pad(state_mlstm_n, ((0, 0), (0, 0), (0, 0), (0, D_HEAD - DK_A))).reshape(
        DEPTH, bs, N_HEADS, 1, D_HEAD)
    m0_pad = jnp.pad(state_mlstm_m, ((0, 0), (0, 0), (0, LANES - N_HEADS))).reshape(DEPTH, bs, 1, LANES)

    p_p = p_prompt.reshape(DEPTH, n_p, D_PLE)
    p_s = p_sample.reshape(DEPTH, n_s, D_PLE)
    x = _ln(x_prompt.reshape(n_p, d), x_sample.reshape(n_s, d), ln_in_g, ln_in_b)

    assert n_p % ROW_TILE == 0 and n_s % ROW_TILE == 0
    nb_p = LANES // CHUNK
    nb_s = LANES // ts_pad
    st_p = _stacked_states(bp)
    st_s = _stacked_states(bs)
    for l in range(DEPTH):
        lb = lower_bounds[l].reshape(1, MIX)
        nga = mlstm_norm_g[l].reshape(1, MIX)
        ngb = hgrn_norm_g[l].reshape(1, MIX)
        proj_p = _inproj(x, w_in_p[l], 0, n_p).reshape(bp, tp, N_PROJ)
        h_p, st_p = _scan(proj_p, gbias[l], lb, nga, ngb, None, st_p, layer=l,
                          NB=nb_p, n_chunks=tp // CHUNK, L=CHUNK, last=CHUNK - 1)
        proj_s = jnp.pad(_inproj(x, w_in_p[l], n_p, n_s).reshape(bs, ts, N_PROJ), ((0, 0), (0, ts_pad - ts), (0, 0)))
        h_s, st_s = _scan(proj_s, gbias[l], lb, nga, ngb,
                          (state_mlstm_C[l], n0_pad[l], m0_pad[l], state_hgrn_S[l]), st_s, layer=l,
                          NB=nb_s, n_chunks=1, L=ts_pad, last=ts - 1)
        x1, resid, *routing = _post_mixer(l, x, h_p.reshape(n_p, d), h_s[:, :ts].reshape(n_s, d), p_p, p_s,
                                          w_out_b[l], ln1_g[l], ln1_b[l], w_pg_b[l], w_pp_b[l],
                                          w_r2[l], w_r_hi[l], b_r[l])
        y_k, gates = _moe(l, x1, routing, w_gate, b_gate.reshape(DEPTH, N_EXPERTS, 1, d),
                          w_up, b_up.reshape(DEPTH, N_EXPERTS, 1, d),
                          w_down, b_down.reshape(DEPTH, N_EXPERTS, 1, d))
        x = _combine_ln(resid, y_k, gates, ln2_g[l], ln2_b[l])

    y_prompt = x[:n_p].reshape(bp, tp, d)
    y_sample = x[n_p:].reshape(bs, ts, d)
    unpad = lambda st: (st[0], st[1][:, :, :, 0, :DK_A], st[2][:, :, 0, :N_HEADS], st[3])
    return (y_prompt, y_sample) + unpad(st_p) + unpad(st_s)
```

```python
import functools

import jax
import jax.numpy as jnp
import numpy as np
from jax import lax
from jax.experimental import pallas as pl
from jax.experimental.pallas import tpu as pltpu
from jax.experimental.pallas import tpu_sc as plsc

F32 = jnp.float32
BF16 = jnp.bfloat16

D_MODEL = 1024
DEPTH = 4
D_PLE = 256
N_HEADS = 4
DK_A = 64
D_HEAD = 128
MIX = 512
N_EXPERTS = 32
TOP_K = 4
SWIGLU_LIMIT = 7.0
SWIGLU_ALPHA = 1.702
CHUNK = 64
LN_EPS = 1e-5
NORM_EPS = 1e-6
LB_FLOOR = 1e-20
DEEPNORM_ALPHA = (2 * DEPTH) ** 0.25
SPLIT_SIZES = (256, 256, 512, 512, 4, 4, 512, 512, 512, 512)
SPLIT_POINTS = tuple(int(s) for s in np.cumsum(SPLIT_SIZES)[:-1])

LANES = 128
SUBLANES = 8
VMEM_LIMIT = 56 * 1024 * 1024

QA0, KA0, VA0, OA0 = 0, 512, 1024, 1536
QB0, FB0, IB0, GB0 = 2048, 2560, 3072, 3584
IG0, FG0 = 4096, 4224
N_PROJ = 4352

ROW_TILE = 256
EXPERT_ROWS = 256

NT_DIMS = (((1,), (1,)), ((), ()))
TN_DIMS = (((0,), (0,)), ((), ()))


def _params(*sem):
    return pltpu.CompilerParams(dimension_semantics=sem, vmem_limit_bytes=VMEM_LIMIT)


def _log_sigmoid(x):
    return jnp.minimum(x, 0.0) - jnp.log(1.0 + jnp.exp(-jnp.abs(x)))


def _split3(x):
    x1 = x.astype(BF16)
    r1 = x - x1.astype(F32)
    x2 = r1.astype(BF16)
    x3 = (r1 - x2.astype(F32)).astype(BF16)
    return x1, x2, x3


def _dot01(sel, x):
    return sum(jnp.dot(sel, t, preferred_element_type=F32) for t in _split3(x))


def _pack_pairs(x):
    w = x.shape[1] // 2
    bits = lax.bitcast_convert_type(x.astype(BF16).astype(F32), jnp.uint32)
    return lax.bitcast_convert_type(bits[:, :w] | (bits[:, w:] >> 16), jnp.int32)


def _unpack_pairs(words):
    u = lax.bitcast_convert_type(words, jnp.uint32)
    hi = lax.bitcast_convert_type(u & jnp.uint32(0xFFFF0000), F32)
    lo = lax.bitcast_convert_type(u << 16, F32)
    return hi, lo


def _layernorm_rows(x, g, b):
    mu = jnp.mean(x, axis=-1, keepdims=True)
    xc = x - mu
    var = jnp.mean(xc * xc, axis=-1, keepdims=True)
    return xc * lax.rsqrt(var + LN_EPS) * g + b


def _two_sources(n_a, n_b, width, lead=()):
    ta = n_a // ROW_TILE
    assert n_a % ROW_TILE == 0 and n_b % ROW_TILE == 0
    block = (None,) * len(lead) + (ROW_TILE, width)
    spec_a = pl.BlockSpec(block, lambda i: lead + (jnp.minimum(i, ta - 1), 0))
    spec_b = pl.BlockSpec(block, lambda i: lead + (jnp.maximum(i - ta, 0), 0))
    return spec_a, spec_b, ta


def _ln_kernel(xa_ref, xb_ref, g_ref, b_ref, o_ref, *, tiles_a):
    x = jnp.where(pl.program_id(0) < tiles_a, xa_ref[...], xb_ref[...])
    o_ref[...] = _layernorm_rows(x, g_ref[...], b_ref[...])


def _ln(xa, xb, g, b):
    d = xa.shape[1]
    n = xa.shape[0] + xb.shape[0]
    spec_a, spec_b, ta = _two_sources(xa.shape[0], xb.shape[0], d)
    vec = pl.BlockSpec((1, d), lambda i: (0, 0))
    return pl.pallas_call(
        functools.partial(_ln_kernel, tiles_a=ta), grid=(n // ROW_TILE,), in_specs=[spec_a, spec_b, vec, vec],
        out_specs=pl.BlockSpec((ROW_TILE, d), lambda i: (i, 0)),
        out_shape=jax.ShapeDtypeStruct((n, d), F32), compiler_params=_params("parallel"),
        name="ln_in")(xa, xb, g.reshape(1, d), b.reshape(1, d))


def _combine_ln_kernel(resid_ref, y0_ref, y1_ref, y2_ref, y3_ref, gate_ref, g_ref, beta_ref, o_ref):
    half = resid_ref.shape[1] // 2
    acc_hi = resid_ref[:, :half]
    acc_lo = resid_ref[:, half:]
    for k, y_ref in enumerate((y0_ref, y1_ref, y2_ref, y3_ref)):
        hi, lo = _unpack_pairs(y_ref[...])
        gk = gate_ref[:, k:k + 1]
        acc_hi = acc_hi + gk * hi
        acc_lo = acc_lo + gk * lo
    o_ref[...] = _layernorm_rows(jnp.concatenate([acc_hi, acc_lo], axis=1), g_ref[...], beta_ref[...])


def _combine_ln(resid, y_k, gates, g, beta):
    n, d = resid.shape
    nt = n // ROW_TILE
    row = lambda w: pl.BlockSpec((ROW_TILE, w), lambda i: (i, 0))
    yk = lambda k: pl.BlockSpec((ROW_TILE, d // 2), lambda i: (k * nt + i, 0))
    vec = pl.BlockSpec((1, d), lambda i: (0, 0))
    return pl.pallas_call(
        _combine_ln_kernel, grid=(nt,),
        in_specs=[row(d)] + [yk(k) for k in range(TOP_K)] + [row(LANES), vec, vec],
        out_specs=row(d), out_shape=jax.ShapeDtypeStruct((n, d), F32), compiler_params=_params("parallel"),
        name="combine_ln2")(resid, y_k, y_k, y_k, y_k, gates, g.reshape(1, d), beta.reshape(1, d))


def _inproj_kernel(x_ref, w_ref, o_ref):
    o_ref[...] = jnp.dot(x_ref[...].astype(BF16), w_ref[...], preferred_element_type=F32)


def _inproj(x, w, layer, row0, n_rows):
    d = x.shape[1]
    t0 = row0 // ROW_TILE
    return pl.pallas_call(
        _inproj_kernel, grid=(n_rows // ROW_TILE,),
        in_specs=[pl.BlockSpec((ROW_TILE, d), lambda i: (t0 + i, 0)),
                  pl.BlockSpec((None, d, N_PROJ), lambda i: (layer, 0, 0))],
        out_specs=pl.BlockSpec((ROW_TILE, N_PROJ), lambda i: (i, 0)),
        out_shape=jax.ShapeDtypeStruct((n_rows, N_PROJ), F32), compiler_params=_params("parallel"),
        name="in_proj")(x, w)


def _scan_kernel(*refs, NB, L, last, has_state):
    it = iter(refs)
    proj_ref, gbias_ref, lb_ref, nga_ref, ngb_ref = (next(it) for _ in range(5))
    if has_state:
        C0_ref, n0_ref, m0_ref, S0_ref = (next(it) for _ in range(4))
    for _ in range(4):
        next(it)
    h_ref, C_out, n_out, m_out, S_out = (next(it) for _ in range(5))
    C_s, n_s, m_s, St_s = (next(it) for _ in range(4))

    c = pl.program_id(1)
    nc = pl.num_programs(1)
    H = range(N_HEADS)
    N = range(NB)
    R = NB * L
    assert R == LANES

    @pl.when(c == 0)
    def _init():
        if has_state:
            for n in N:
                for h in H:
                    C_s[n, h, 0:DK_A, :] = C0_ref[n, h]
                    C_s[n, h, DK_A:D_HEAD, :] = jnp.zeros((D_HEAD - DK_A, D_HEAD), F32)
                    St_s[n, h] = S0_ref[n, h].T
            n_s[...] = n0_ref[...]
            m_s[...] = m0_ref[...]
        else:
            C_s[...] = jnp.zeros(C_s.shape, F32)
            n_s[...] = jnp.zeros(n_s.shape, F32)
            m_s[...] = jnp.zeros(m_s.shape, F32)
            St_s[...] = jnp.zeros(St_s.shape, F32)

    row = lax.broadcasted_iota(jnp.int32, (R, R), 0)
    col = lax.broadcasted_iota(jnp.int32, (R, R), 1)
    causal = jnp.logical_and(row // L == col // L, row >= col)
    tril = causal.astype(BF16)
    ones_r = jnp.ones((R, LANES), BF16)
    lane = lax.broadcasted_iota(jnp.int32, (R, LANES), 1)
    step = lax.broadcasted_iota(jnp.int32, (R, 1), 0) % L
    valid = step <= last
    padded = last < L - 1
    tile = lambda base, h: slice(base + LANES * h, base + LANES * (h + 1))
    seq = lambda n: slice(n * L, (n + 1) * L)
    cols = lambda base, width: proj_ref[:, :, base:base + width].reshape(R, width)
    per_seq = lambda vals: jnp.concatenate([jnp.broadcast_to(x, (L, x.shape[-1])) for x in vals], axis=0)
    last_rows = lambda x: [x[n * L + last:n * L + last + 1] for n in N]

    i_t = cols(IG0, LANES) + gbias_ref[0:1, :]
    f_t = _log_sigmoid(cols(FG0, LANES) + gbias_ref[1:2, :])
    zf = cols(FB0, MIX)
    lb = lb_ref[...]
    la = jnp.log(jnp.maximum(lb, LB_FLOOR))
    bb = jnp.log1p(-lb) + _log_sigmoid(zf)
    f_log = jnp.maximum(la, bb) + jnp.log(1.0 + jnp.exp(-jnp.abs(la - bb)))
    b_t = _dot01(tril, f_t)
    g = _dot01(tril, f_log)
    m_all = [m_s[n] for n in N]
    C_prev = [[C_s[n, h] for h in H] for n in N]
    n_prev = [[n_s[n, h] for h in H] for n in N]
    St = [[St_s[n, h] for h in H] for n in N]
    qf = [cols(QA0 + LANES * h, LANES) for h in H]
    kf = [cols(KA0 + LANES * h, LANES) for h in H]
    q = [x.astype(BF16) for x in qf]
    k = [x.astype(BF16) for x in kf]
    v = [cols(VA0 + LANES * h, LANES).astype(BF16) for h in H]
    kb = (1.0 - lb) * jax.nn.sigmoid(-zf)
    qr = cols(QB0, MIX)
    qb = qr * jax.nn.sigmoid(qr)
    vb = [cols(IB0 + LANES * h, LANES).astype(BF16) for h in H]
    qb16 = qb.astype(BF16)
    kb16 = kb.astype(BF16)
    r_t = i_t - b_t
    zero16 = jnp.zeros((R, LANES), BF16)
    r_all = sum(
        lax.dot_general(ones_r, jnp.concatenate([jnp.where(lane == h, term, zero16) for h in H], axis=0),
                        NT_DIMS, preferred_element_type=F32)
        for term in _split3(r_t))
    r_m = [r_all[:, R * h:R * (h + 1)] for h in H]
    qk = [lax.dot_general(q[h], k[h], NT_DIMS, preferred_element_type=F32) for h in H]
    qC = [jnp.concatenate([jnp.dot(q[h][seq(n)], C_prev[n][h].astype(BF16), preferred_element_type=F32)
                           for n in N], axis=0) for h in H]
    levels = []
    bs = 1
    while bs < L:
        levels.append(bs)
        bs *= 2
    small = [bs for bs in levels if bs < SUBLANES]
    g_anchor = {}
    if small:
        sel = jnp.concatenate(
            [(col == (row // (2 * bs)) * (2 * bs) + bs).astype(BF16) for bs in small], axis=0)
        picked = _dot01(sel, g)
        for i, bs in enumerate(small):
            g_anchor[bs] = picked[i * R:(i + 1) * R]
    for bs in levels:
        if bs >= SUBLANES:
            g_anchor[bs] = jnp.concatenate(
                [jnp.broadcast_to(g[p0 + bs:p0 + bs + 1], (2 * bs, MIX)) for p0 in range(0, R, 2 * bs)], axis=0)
    qg = (qb * jnp.exp(g)).astype(BF16)
    o_inter = [jnp.concatenate([lax.dot_general(qg[seq(n), tile(0, h)], St[n][h].astype(BF16), NT_DIMS,
                                                preferred_element_type=F32) for n in N], axis=0) for h in H]
    a = [jnp.where(row == col, lax.dot_general(qb16[:, tile(0, h)], kb16[:, tile(0, h)], NT_DIMS,
                                                preferred_element_type=F32), 0.0) for h in H]
    qn = [jnp.sum(qf[h] * per_seq([n_prev[n][h] for n in N]), axis=1, keepdims=True) for h in H]
    b_col = [b_t[:, h:h + 1] for h in H]
    i_col = [i_t[:, h:h + 1] for h in H]
    m_prev = [per_seq([m_all[n][:, h:h + 1] for n in N]) for h in H]
    cm = [jnp.max(jnp.where(causal, r_m[h], -jnp.inf), axis=1, keepdims=True) for h in H]
    m_t = [b_col[h] + jnp.maximum(m_prev[h], cm[h]) for h in H]
    inter = [jnp.exp(b_col[h] + m_prev[h] - m_t[h]) for h in H]
    d = [jnp.exp(jnp.where(causal, (b_col[h] - m_t[h]) + r_m[h], -1e30)) for h in H]
    s = [qk[h] * d[h] for h in H]
    log2e = 1.4426950408889634
    e = [jnp.exp2((g - g_anchor[bs]) * jnp.where((step // bs) % 2 == 1, log2e, -log2e)) for bs in levels]
    qe = [(qb * x).astype(BF16) for x in e]
    ke = [(kb * x).astype(BF16) for x in e]
    sv = [jnp.dot(s[h].astype(BF16), v[h], preferred_element_type=F32) for h in H]
    sc = [[lax.dot_general(qe[i][:, tile(0, h)], ke[i][:, tile(0, h)], NT_DIMS, preferred_element_type=F32)
           for h in H] for i in range(len(levels))]
    den = [jnp.sum(s[h], axis=1, keepdims=True) + inter[h] * qn[h] for h in H]
    m_new = [last_rows(m_t[h]) for h in H]
    b_last = [last_rows(b_col[h]) for h in H]
    w_arg = [per_seq(b_last[h]) - b_col[h] + i_col[h] - per_seq(m_new[h]) for h in H]
    if padded:
        w_arg = [jnp.where(valid, x, -1e30) for x in w_arg]
    kw = [kf[h] * jnp.exp(w_arg[h]) for h in H]
    kw16 = [x.astype(BF16) for x in kw]
    decay = [[jnp.exp(b_last[h][n] + m_all[n][:, h:h + 1] - m_new[h][n]) for h in H] for n in N]
    kv = [[lax.dot_general(kw16[h][seq(n)], v[h][seq(n)], TN_DIMS, preferred_element_type=F32) for h in H]
          for n in N]
    for i, bs in enumerate(levels):
        pair = jnp.logical_and(row // (2 * bs) == col // (2 * bs),
                               jnp.logical_and((row // bs) % 2 == 1, (col // bs) % 2 == 0))
        for h in H:
            a[h] = jnp.where(pair, sc[i][h], a[h])
    o = [o_inter[h] + jnp.dot(a[h].astype(BF16), vb[h], preferred_element_type=F32) for h in H]
    g_last = last_rows(g)
    dec_arg = per_seq(g_last) - g
    if padded:
        dec_arg = jnp.where(valid, dec_arg, -1e30)
    kdec = (kb * jnp.exp(dec_arg)).astype(BF16)
    eg_last = [jnp.exp(x) for x in g_last]
    vk = [[lax.dot_general(vb[h][seq(n)], kdec[seq(n), tile(0, h)], TN_DIMS, preferred_element_type=F32)
           for h in H] for n in N]
    hh = [(sv[h] + inter[h] * qC[h]) / jnp.maximum(jnp.abs(den[h]), jnp.exp(-m_t[h])) for h in H]
    for h in H:
        hn = hh[h] * lax.rsqrt(jnp.mean(hh[h] * hh[h], axis=-1, keepdims=True) + NORM_EPS) * nga_ref[:, tile(0, h)]
        out_a = jax.nn.sigmoid(cols(OA0 + LANES * h, LANES)) * hn
        h_ref[:, :, tile(0, h)] = out_a.reshape(NB, L, LANES)
    for h in H:
        on = o[h] * lax.rsqrt(jnp.mean(o[h] * o[h], axis=-1, keepdims=True) + NORM_EPS) * ngb_ref[:, tile(0, h)]
        gr = cols(GB0 + LANES * h, LANES)
        h_ref[:, :, tile(MIX, h)] = (gr * jax.nn.sigmoid(gr) * on).reshape(NB, L, LANES)
    lane1 = lax.broadcasted_iota(jnp.int32, (1, LANES), 1)
    for n in N:
        for h in H:
            C_s[n, h] = decay[n][h] * C_prev[n][h] + kv[n][h]
            n_s[n, h] = decay[n][h] * n_prev[n][h] + jnp.sum(kw[h][seq(n)], axis=0, keepdims=True)
            St_s[n, h] = eg_last[n][:, tile(0, h)] * St[n][h] + vk[n][h]
        m_upd = m_all[n]
        for h in H:
            m_upd = jnp.where(lane1 == h, m_new[h][n], m_upd)
        m_s[n] = m_upd

    @pl.when(c == nc - 1)
    def _finish():
        for n in N:
            for h in H:
                C_out[n, h] = C_s[n, h, 0:DK_A, :]
                S_out[n, h] = St_s[n, h].T
        n_out[...] = n_s[...]
        m_out[...] = m_s[...]


def _scan(proj, gbias, lb, nga, ngb, state, stacked, *, layer, NB, n_chunks, L, last):
    batch = proj.shape[0]
    has_state = state is not None
    const2 = lambda b, c: (0, 0)
    in_specs = [
        pl.BlockSpec((NB, L, N_PROJ), lambda b, c: (b, c, 0)),
        pl.BlockSpec((2, LANES), const2),
        pl.BlockSpec((1, MIX), const2),
        pl.BlockSpec((1, MIX), const2),
        pl.BlockSpec((1, MIX), const2),
    ]
    args = [proj, gbias, lb, nga, ngb]
    lst4 = lambda b, c: (layer, b, 0, 0, 0)
    lst3 = lambda b, c: (layer, b, 0, 0)
    if has_state:
        in_specs += [
            pl.BlockSpec((None, NB, N_HEADS, DK_A, D_HEAD), lst4),
            pl.BlockSpec((None, NB, N_HEADS, 1, LANES), lst4),
            pl.BlockSpec((None, NB, 1, LANES), lst3),
            pl.BlockSpec((None, NB, N_HEADS, D_HEAD, D_HEAD), lst4),
        ]
        args += list(state)
    aliases = {len(args) + i: 1 + i for i in range(4)}
    in_specs += [pl.BlockSpec(memory_space=pl.ANY)] * 4
    args += list(stacked)
    out_specs = [
        pl.BlockSpec((NB, L, D_MODEL), lambda b, c: (b, c, 0)),
        pl.BlockSpec((None, NB, N_HEADS, DK_A, D_HEAD), lst4),
        pl.BlockSpec((None, NB, N_HEADS, 1, LANES), lst4),
        pl.BlockSpec((None, NB, 1, LANES), lst3),
        pl.BlockSpec((None, NB, N_HEADS, D_HEAD, D_HEAD), lst4),
    ]
    out_shape = [jax.ShapeDtypeStruct((batch, n_chunks * L, D_MODEL), F32)] + [
        jax.ShapeDtypeStruct(a.shape, a.dtype) for a in stacked]
    scratch = [
        pltpu.VMEM((NB, N_HEADS, D_HEAD, D_HEAD), F32),
        pltpu.VMEM((NB, N_HEADS, 1, LANES), F32),
        pltpu.VMEM((NB, 1, LANES), F32),
        pltpu.VMEM((NB, N_HEADS, D_HEAD, D_HEAD), F32),
    ]
    kern = functools.partial(_scan_kernel, NB=NB, L=L, last=last, has_state=has_state)
    res = pl.pallas_call(
        kern, grid=(batch // NB, n_chunks), in_specs=in_specs, out_specs=out_specs, out_shape=out_shape,
        scratch_shapes=scratch, input_output_aliases=aliases,
        compiler_params=_params("parallel", "arbitrary"),
        name="scan_state" if has_state else "scan_prompt")(*args)
    return res[0], tuple(res[1:])


def _stacked_states(batch):
    return (jnp.zeros((DEPTH, batch, N_HEADS, DK_A, D_HEAD), F32),
            jnp.zeros((DEPTH, batch, N_HEADS, 1, LANES), F32),
            jnp.zeros((DEPTH, batch, 1, LANES), F32),
            jnp.zeros((DEPTH, batch, N_HEADS, D_HEAD, D_HEAD), F32))


def _post_mixer_kernel(x_ref, ha_ref, hb_ref, pa_ref, pb_ref, wout_ref, g1_ref, b1_ref, wpg_ref, wpp_ref, wr2_ref,
                       wrh_ref, br_ref, x1_ref, resid_ref, idx_ref, rank_ref, gate_ref, cnt_ref, carry_s, *, tiles_a):
    i = pl.program_id(0)

    @pl.when(i == 0)
    def _init():
        carry_s[...] = jnp.zeros(carry_s.shape, F32)

    from_a = i < tiles_a
    h = jnp.where(from_a, ha_ref[...], hb_ref[...])
    p = jnp.where(from_a, pa_ref[...], pb_ref[...])
    mix = jnp.dot(h.astype(BF16), wout_ref[...], preferred_element_type=F32)
    x1 = _layernorm_rows(DEEPNORM_ALPHA * x_ref[...] + mix, g1_ref[...], b1_ref[...])
    x1b = x1.astype(BF16)
    x1_ref[...] = _pack_pairs(x1)
    x1_lo = (x1 - x1b.astype(F32)).astype(BF16)
    l2 = jnp.dot(x1b, wr2_ref[...], preferred_element_type=F32)
    logits = (l2[:, :LANES] + l2[:, LANES:]
              + jnp.dot(x1_lo, wrh_ref[...], preferred_element_type=F32) + br_ref[...])

    tm = logits.shape[0]
    lane = lax.broadcasted_iota(jnp.int32, (tm, LANES), 1)
    lane_f = lane.astype(F32)
    l = jnp.where(lane < N_EXPERTS, logits, -jnp.inf)
    vals, idxs, onehots = [], [], []
    for _ in range(TOP_K):
        mx = jnp.max(l, axis=1, keepdims=True)
        ix = jnp.min(jnp.where(l == mx, lane_f, float(LANES)), axis=1, keepdims=True)
        sel = lane_f == ix
        vals.append(mx)
        idxs.append(ix)
        onehots.append(sel.astype(F32))
        l = jnp.where(sel, -jnp.inf, l)
    gate = jax.nn.sigmoid(jnp.dot(x1b, wpg_ref[...], preferred_element_type=F32))
    pp = jnp.dot(p.astype(BF16), wpp_ref[...], preferred_element_type=F32)
    resid_ref[...] = DEEPNORM_ALPHA * x1 + gate * pp

    w = [jnp.exp(val - vals[0]) for val in vals]
    tot = w[0] + w[1] + w[2] + w[3]
    oh_all = onehots[0] + onehots[1] + onehots[2] + onehots[3]
    row = lax.broadcasted_iota(jnp.int32, (tm, tm), 0)
    col = lax.broadcasted_iota(jnp.int32, (tm, tm), 1)
    earlier = (row > col).astype(BF16)
    prefix = jnp.dot(earlier, oh_all.astype(BF16), preferred_element_type=F32) + carry_s[...]
    idx_o = jnp.zeros((tm, LANES), F32)
    rank_o = jnp.zeros((tm, LANES), F32)
    gate_o = jnp.zeros((tm, LANES), F32)
    for k in range(TOP_K):
        rank_k = jnp.sum(onehots[k] * prefix, axis=1, keepdims=True)
        idx_o = jnp.where(lane == k, idxs[k], idx_o)
        rank_o = jnp.where(lane == k, rank_k, rank_o)
        gate_o = jnp.where(lane == k, w[k] / tot, gate_o)
    idx_ref[...] = idx_o.astype(jnp.int32)
    rank_ref[...] = rank_o.astype(jnp.int32)
    gate_ref[...] = gate_o
    carry_s[...] = carry_s[...] + jnp.sum(oh_all, axis=0, keepdims=True)
    cnt_ref[...] = carry_s[...].astype(jnp.int32)


def _post_mixer(layer, x, ha, hb, pa, pb, wout, g1, b1, wpg, wpp, wr2, wrh, br):
    n, d = x.shape
    row = lambda w: pl.BlockSpec((ROW_TILE, w), lambda i: (i, 0))
    full = lambda a: pl.BlockSpec(a.shape, lambda i: (0, 0))
    ha_spec, hb_spec, ta = _two_sources(ha.shape[0], hb.shape[0], d)
    pa_spec, pb_spec, _ = _two_sources(pa.shape[1], pb.shape[1], D_PLE, lead=(layer,))
    g1 = g1.reshape(1, d)
    b1 = b1.reshape(1, d)
    return pl.pallas_call(
        functools.partial(_post_mixer_kernel, tiles_a=ta), grid=(n // ROW_TILE,),
        in_specs=[row(d), ha_spec, hb_spec, pa_spec, pb_spec, full(wout), full(g1), full(b1), full(wpg), full(wpp),
                  full(wr2), full(wrh), full(br)],
        out_specs=[row(d // 2), row(d), row(LANES), row(LANES), row(LANES), pl.BlockSpec((1, LANES), lambda i: (0, 0))],
        out_shape=[jax.ShapeDtypeStruct((n, d // 2), jnp.int32), jax.ShapeDtypeStruct((n, d), F32),
                   jax.ShapeDtypeStruct((n, LANES), jnp.int32), jax.ShapeDtypeStruct((n, LANES), jnp.int32),
                   jax.ShapeDtypeStruct((n, LANES), F32), jax.ShapeDtypeStruct((1, LANES), jnp.int32)],
        scratch_shapes=[pltpu.VMEM((1, LANES), F32)],
        compiler_params=_params("arbitrary"), name="post_mixer")(
            x, ha, hb, pa, pb, wout, g1, b1, wpg, wpp, wr2, wrh, br)


def _expert_kernel(blk_ref, e_ref, lo_ref, hi_ref, nv_ref, nxt_ref, slot_ref, x_ref, wg_hbm, bg_ref, wu_hbm, bu_ref,
                   wd_hbm, bd_ref, o_ref, w32_s, wg_s, wu_s, wd_s, sem, *, layer):
    v = pl.program_id(0)
    prev = jnp.maximum(v - 1, 0)
    valid = v < nv_ref[0]
    new_expert = jnp.logical_or(v == 0, e_ref[v] != e_ref[prev])
    new_block = jnp.logical_or(v == 0, blk_ref[v] != blk_ref[prev])
    slot = slot_ref[v]

    def weight_copies(expert, to_slot):
        return [pltpu.make_async_copy(w_hbm.at[layer, expert], w32_s.at[to_slot, j], sem.at[to_slot, j])
                for j, w_hbm in enumerate((wg_hbm, wu_hbm, wd_hbm))]

    @pl.when(v == 0)
    def _prime():
        for cp in weight_copies(e_ref[0], slot):
            cp.start()

    @pl.when(jnp.logical_and(valid, new_expert))
    def _next_weights():
        for cp in weight_copies(e_ref[v], slot):
            cp.wait()

        @pl.when(nxt_ref[v] != e_ref[v])
        def _prefetch():
            for cp in weight_copies(nxt_ref[v], 1 - slot):
                cp.start()

        wg_s[...] = w32_s[slot, 0].astype(BF16)
        wu_s[...] = w32_s[slot, 1].astype(BF16)
        wd_s[...] = w32_s[slot, 2].astype(BF16)

    @pl.when(valid)
    def _compute():
        half = wg_s.shape[0] // 2
        x_hi, x_lo = (t.astype(BF16) for t in _unpack_pairs(x_ref[...]))
        gate = (jnp.dot(x_hi, wg_s[0:half, :], preferred_element_type=F32)
                + jnp.dot(x_lo, wg_s[half:, :], preferred_element_type=F32) + bg_ref[0])
        up = (jnp.dot(x_hi, wu_s[0:half, :], preferred_element_type=F32)
              + jnp.dot(x_lo, wu_s[half:, :], preferred_element_type=F32) + bu_ref[0])
        gate = jnp.minimum(gate, SWIGLU_LIMIT)
        up = jnp.clip(up, -SWIGLU_LIMIT, SWIGLU_LIMIT)
        glu = gate * jax.nn.sigmoid(gate * SWIGLU_ALPHA)
        act = ((up + 1.0) * glu).astype(BF16)
        y = _pack_pairs(jnp.dot(act, wd_s[...], preferred_element_type=F32) + bd_ref[0])
        rows = blk_ref[v] * EXPERT_ROWS + lax.broadcasted_iota(jnp.int32, (EXPERT_ROWS, 1), 0)
        mine = jnp.logical_and(rows >= lo_ref[v], rows < hi_ref[v])

        @pl.when(new_block)
        def _first():
            o_ref[...] = jnp.where(mine, y, 0)

        @pl.when(jnp.logical_not(new_block))
        def _again():
            o_ref[...] = jnp.where(mine, y, o_ref[...])


def _experts(layer, blk_v, e_v, lo_v, hi_v, n_vis, nxt_v, slot_v, xin, wg, bg, wu, bu, wd, bd):
    n_rows = xin.shape[0]
    d = wg.shape[-1]
    n_visits = blk_v.shape[0]
    xspec = pl.BlockSpec((EXPERT_ROWS, d // 2), lambda v, blk, e, *_: (blk[v], 0))
    ospec = pl.BlockSpec((EXPERT_ROWS, d // 2), lambda v, blk, e, *_: (blk[v], 0))
    wspec = pl.BlockSpec(memory_space=pl.ANY)
    bspec = pl.BlockSpec((None, 1, 1, d), lambda v, blk, e, *_: (layer, e[v], 0, 0))
    grid_spec = pltpu.PrefetchScalarGridSpec(
        num_scalar_prefetch=7, grid=(n_visits,),
        in_specs=[xspec, wspec, bspec, wspec, bspec, wspec, bspec], out_specs=ospec,
        scratch_shapes=[pltpu.VMEM((2, 3, d, d), F32)] + [pltpu.VMEM((d, d), BF16)] * 3
        + [pltpu.SemaphoreType.DMA((2, 3))])
    return pl.pallas_call(
        functools.partial(_expert_kernel, layer=layer), grid_spec=grid_spec,
        out_shape=jax.ShapeDtypeStruct((n_rows, d // 2), jnp.int32),
        compiler_params=_params("arbitrary"), name="experts")(
            blk_v, e_v, lo_v, hi_v, n_vis, nxt_v, slot_v, xin, wg, bg, wu, bu, wd, bd)


DISPATCH_ROWS = 48
COLLECT_ROWS = 24


def _sc_layout(n_tok, chunk_rows):
    info = plsc.get_sparse_core_info()
    n_workers = info.num_cores * info.num_subcores
    per_worker = n_tok // n_workers
    n_chunks = per_worker // chunk_rows
    assert per_worker * n_workers == n_tok and n_chunks * chunk_rows == per_worker
    return info, n_workers, per_worker, n_chunks


def _sc_index(pos, n_workers, n_chunks, chunk_rows):
    return pos.T.reshape(TOP_K, n_workers, n_chunks, chunk_rows).transpose(1, 2, 0, 3).reshape(
        n_workers, n_chunks * TOP_K, chunk_rows)


def _dispatch_rows(x1, pos):
    n_tok, w = x1.shape
    rows = DISPATCH_ROWS
    info, n_workers, per_worker, n_chunks = _sc_layout(n_tok, rows)
    mesh = plsc.VectorSubcoreMesh(core_axis_name="core", subcore_axis_name="subcore")

    @functools.partial(
        pl.kernel, out_type=jax.ShapeDtypeStruct((n_tok * TOP_K, w), x1.dtype), mesh=mesh,
        scratch_types=[pltpu.VMEM((n_chunks * TOP_K, rows), jnp.int32), pltpu.VMEM((2, rows, w), x1.dtype),
                       pltpu.SemaphoreType.DMA, pltpu.SemaphoreType.DMA, pltpu.SemaphoreType.DMA],
        name="dispatch_rows")
    def scatter_rows(x_hbm, idx_hbm, out_hbm, idx_v, rows_v, load_sem0, load_sem1, scatter_sem):
        worker = lax.axis_index("subcore") * info.num_cores + lax.axis_index("core")
        pltpu.sync_copy(idx_hbm.at[worker], idx_v)
        base = worker * per_worker
        load_sems = (load_sem0, load_sem1)

        def load(c):
            return pltpu.async_copy(x_hbm.at[pl.ds(base + c * rows, rows)], rows_v.at[c % 2], load_sems[c % 2])

        loading = load(0)
        for c in range(n_chunks):
            loading.wait()
            if c + 1 < n_chunks:
                loading = load(c + 1)
            copies = [pltpu.async_copy(rows_v.at[c % 2], out_hbm.at[idx_v.at[c * TOP_K + k]], scatter_sem)
                      for k in range(TOP_K)]
            for cp in copies:
                cp.wait()

    return scatter_rows(x1, _sc_index(pos, n_workers, n_chunks, rows))


def _collect_rows(y_rows, pos):
    n_tok = pos.shape[0]
    w = y_rows.shape[1]
    rows = COLLECT_ROWS
    info, n_workers, per_worker, n_chunks = _sc_layout(n_tok, rows)
    mesh = plsc.VectorSubcoreMesh(core_axis_name="core", subcore_axis_name="subcore")

    @functools.partial(
        pl.kernel, out_type=jax.ShapeDtypeStruct((n_tok * TOP_K, w), y_rows.dtype), mesh=mesh,
        scratch_types=[pltpu.VMEM((n_chunks * TOP_K, rows), jnp.int32), pltpu.VMEM((2, TOP_K, rows, w), y_rows.dtype),
                       pltpu.SemaphoreType.DMA, pltpu.SemaphoreType.DMA],
        name="collect_rows")
    def gather_rows(y_hbm, idx_hbm, out_hbm, idx_v, rows_v, sem0, sem1):
        worker = lax.axis_index("subcore") * info.num_cores + lax.axis_index("core")
        pltpu.sync_copy(idx_hbm.at[worker], idx_v)
        base = worker * per_worker
        sems = (sem0, sem1)

        def gather(c):
            return [pltpu.async_copy(y_hbm.at[idx_v.at[c * TOP_K + k]], rows_v.at[c % 2, k], sems[c % 2])
                    for k in range(TOP_K)]

        pending = gather(0)
        for c in range(n_chunks):
            following = gather(c + 1) if c + 1 < n_chunks else []
            for cp in pending:
                cp.wait()
            for k in range(TOP_K):
                pltpu.sync_copy(rows_v.at[c % 2, k], out_hbm.at[pl.ds(k * n_tok + base + c * rows, rows)])
            pending = following

    return gather_rows(y_rows, _sc_index(pos, n_workers, n_chunks, rows))


def _moe(layer, x1, routing, wg, bg, wu, bu, wd, bd):
    n_tok = x1.shape[0]
    n_assign = n_tok * TOP_K
    n_blocks = n_assign // EXPERT_ROWS
    idx_o, rank_o, gate_o, cnt = routing
    counts = cnt[0, :N_EXPERTS]
    ends = jnp.cumsum(counts)
    offs = ends - counts
    experts = jnp.arange(N_EXPERTS, dtype=jnp.int32)
    top_i = idx_o[:, :TOP_K]
    off_tok = jnp.sum(jnp.where(top_i[:, :, None] == experts, offs, 0), axis=-1)
    pos = (off_tok + rank_o[:, :TOP_K]).reshape(n_assign)
    pos = pos.reshape(n_tok, TOP_K)
    xin = _dispatch_rows(x1, pos)

    first_blk = offs // EXPERT_ROWS
    last_blk = (ends - 1) // EXPERT_ROWS
    nvis = jnp.where(counts > 0, last_blk - first_blk + 1, 0)
    vend = jnp.cumsum(nvis)
    vstart = vend - nvis
    n_visits = n_blocks + N_EXPERTS - 1
    v = jnp.arange(n_visits, dtype=jnp.int32)
    e_v = jnp.sum((vend[None, :] <= v[:, None]).astype(jnp.int32), axis=1)
    e_last = jnp.max(jnp.where(counts > 0, experts, 0))
    e_v = jnp.minimum(e_v, e_last)
    pick = lambda tab: jnp.sum(jnp.where(e_v[:, None] == experts, tab, 0), axis=-1)
    blk_v = jnp.minimum(pick(first_blk) + v - pick(vstart), n_blocks - 1)
    active = counts > 0
    later = jnp.where(jnp.logical_and(active[None, :], experts[None, :] > experts[:, None]), experts[None, :], N_EXPERTS)
    nxt_e = jnp.min(later, axis=1)
    nxt_e = jnp.where(nxt_e == N_EXPERTS, experts, nxt_e)
    order_e = jnp.cumsum(active.astype(jnp.int32)) - 1
    y_rows = _experts(layer, blk_v.astype(jnp.int32), e_v.astype(jnp.int32), pick(offs).astype(jnp.int32),
                      pick(ends).astype(jnp.int32), vend[-1:].astype(jnp.int32),
                      pick(nxt_e).astype(jnp.int32), (pick(order_e) % 2).astype(jnp.int32),
                      xin, wg, bg, wu, bu, wd, bd)
    return _collect_rows(y_rows, pos), gate_o


def _pack_w_in(w_in):
    qa, ka, va, oa, ia, fa, qb, fb, ib, gb = jnp.split(w_in, SPLIT_POINTS, axis=-1)

    def pad_heads(w):
        w = w.reshape(w.shape[:-1] + (N_HEADS, DK_A))
        w = jnp.pad(w, ((0, 0),) * (w.ndim - 1) + ((0, D_HEAD - DK_A),))
        return w.reshape(w.shape[:-2] + (N_HEADS * D_HEAD,))

    def pad_gate(w):
        return jnp.pad(w, ((0, 0),) * (w.ndim - 1) + ((0, LANES - N_HEADS),))

    cols = [pad_heads(qa), pad_heads(ka * (DK_A ** -0.5)), va, oa, qb, fb, ib, gb, pad_gate(ia), pad_gate(fa)]
    return jnp.concatenate(cols, axis=-1).astype(BF16)


def kernel(x_prompt, x_sample, state_mlstm_C, state_mlstm_n, state_mlstm_m, state_hgrn_S, p_prompt, p_sample,
           ln_in_g, ln_in_b, w_in, mlstm_ig_bias, mlstm_fg_bias, mlstm_norm_g, hgrn_lb_logits, hgrn_norm_g,
           w_out, ln1_g, ln1_b, w_router, b_router, w_gate, b_gate, w_up, b_up, w_down, b_down,
           w_ple_gate, w_ple_proj, ln2_g, ln2_b):
    bp, tp, d = x_prompt.shape
    bs, ts, _ = x_sample.shape
    n_p = bp * tp
    n_s = bs * ts
    ts_pad = SUBLANES
    assert tp % CHUNK == 0 and ts <= ts_pad and ts % CHUNK != 0

    lb_soft = jax.nn.softmax(hgrn_lb_logits.astype(F32), axis=0)
    lower_bounds = jnp.cumsum(lb_soft, axis=0) - lb_soft[0]

    w_in_p = _pack_w_in(w_in)
    gbias = jnp.stack([jnp.pad(mlstm_ig_bias, ((0, 0), (0, LANES - N_HEADS))),
                       jnp.pad(mlstm_fg_bias, ((0, 0), (0, LANES - N_HEADS)))], axis=1)
    w_out_b = w_out.astype(BF16)
    w_pg_b = w_ple_gate.astype(BF16)
    w_pp_b = w_ple_proj.astype(BF16)
    w_r = jnp.pad(w_router, ((0, 0), (0, 0), (0, LANES - N_EXPERTS)))
    w_r_hi = w_r.astype(BF16)
    w_r2 = jnp.concatenate([w_r_hi, (w_r - w_r_hi.astype(F32)).astype(BF16)], axis=-1)
    b_r = jnp.pad(b_router, ((0, 0), (0, LANES - N_EXPERTS))).reshape(DEPTH, 1, LANES)
    n0_pad = jnp.pad(state_mlstm_n, ((0, 0), (0, 0), (0, 0), (0, D_HEAD - DK_A))).reshape(
        DEPTH, bs, N_HEADS, 1, D_HEAD)
    m0_pad = jnp.pad(state_mlstm_m, ((0, 0), (0, 0), (0, LANES - N_HEADS))).reshape(DEPTH, bs, 1, LANES)

    p_p = p_prompt.reshape(DEPTH, n_p, D_PLE)
    p_s = p_sample.reshape(DEPTH, n_s, D_PLE)
    x = _ln(x_prompt.reshape(n_p, d), x_sample.reshape(n_s, d), ln_in_g, ln_in_b)

    assert n_p % ROW_TILE == 0 and n_s % ROW_TILE == 0
    nb_p = LANES // CHUNK
    nb_s = LANES // ts_pad
    st_p = _stacked_states(bp)
    st_s = _stacked_states(bs)
    for l in range(DEPTH):
        lb = lower_bounds[l].reshape(1, MIX)
        nga = mlstm_norm_g[l].reshape(1, MIX)
        ngb = hgrn_norm_g[l].reshape(1, MIX)
        proj_p = _inproj(x, w_in_p, l, 0, n_p).reshape(bp, tp, N_PROJ)
        h_p, st_p = _scan(proj_p, gbias[l], lb, nga, ngb, None, st_p, layer=l,
                          NB=nb_p, n_chunks=tp // CHUNK, L=CHUNK, last=CHUNK - 1)
        proj_s = jnp.pad(_inproj(x, w_in_p, l, n_p, n_s).reshape(bs, ts, N_PROJ), ((0, 0), (0, ts_pad - ts), (0, 0)))
        h_s, st_s = _scan(proj_s, gbias[l], lb, nga, ngb,
                          (state_mlstm_C, n0_pad, m0_pad, state_hgrn_S), st_s, layer=l,
                          NB=nb_s, n_chunks=1, L=ts_pad, last=ts - 1)
        x1, resid, *routing = _post_mixer(l, x, h_p.reshape(n_p, d), h_s[:, :ts].reshape(n_s, d), p_p, p_s,
                                          w_out_b[l], ln1_g[l], ln1_b[l], w_pg_b[l], w_pp_b[l],
                                          w_r2[l], w_r_hi[l], b_r[l])
        y_k, gates = _moe(l, x1, routing, w_gate, b_gate.reshape(DEPTH, N_EXPERTS, 1, d),
                          w_up, b_up.reshape(DEPTH, N_EXPERTS, 1, d),
                          w_down, b_down.reshape(DEPTH, N_EXPERTS, 1, d))
        x = _combine_ln(resid, y_k, gates, ln2_g[l], ln2_b[l])

    y_prompt = x[:n_p].reshape(bp, tp, d)
    y_sample = x[n_p:].reshape(bs, ts, d)
    unpad = lambda st: (st[0], st[1][:, :, :, 0, :DK_A], st[2][:, :, 0, :N_HEADS], st[3])
    return (y_prompt, y_sample) + unpad(st_p) + unpad(st_s)
```

```python
import functools

import jax
import jax.numpy as jnp
import numpy as np
from jax import lax
from jax.experimental import pallas as pl
from jax.experimental.pallas import tpu as pltpu
from jax.experimental.pallas import tpu_sc as plsc

F32 = jnp.float32
BF16 = jnp.bfloat16

D_MODEL = 1024
DEPTH = 4
D_PLE = 256
N_HEADS = 4
DK_A = 64
D_HEAD = 128
MIX = 512
N_EXPERTS = 32
TOP_K = 4
SWIGLU_LIMIT = 7.0
SWIGLU_ALPHA = 1.702
CHUNK = 64
LN_EPS = 1e-5
NORM_EPS = 1e-6
LB_FLOOR = 1e-20
DEEPNORM_ALPHA = (2 * DEPTH) ** 0.25
SPLIT_SIZES = (256, 256, 512, 512, 4, 4, 512, 512, 512, 512)
SPLIT_POINTS = tuple(int(s) for s in np.cumsum(SPLIT_SIZES)[:-1])

LANES = 128
SUBLANES = 8
VMEM_LIMIT = 56 * 1024 * 1024

QA0, KA0, VA0, OA0 = 0, 512, 1024, 1536
QB0, FB0, IB0, GB0 = 2048, 2560, 3072, 3584
IG0, FG0 = 4096, 4224
N_PROJ = 4352

ROW_TILE = 256
EXPERT_ROWS = 256

NT_DIMS = (((1,), (1,)), ((), ()))
TN_DIMS = (((0,), (0,)), ((), ()))


def _params(*sem):
    return pltpu.CompilerParams(dimension_semantics=sem, vmem_limit_bytes=VMEM_LIMIT)


def _log_sigmoid(x):
    return jnp.minimum(x, 0.0) - jnp.log(1.0 + jnp.exp(-jnp.abs(x)))


def _split3(x):
    x1 = x.astype(BF16)
    r1 = x - x1.astype(F32)
    x2 = r1.astype(BF16)
    x3 = (r1 - x2.astype(F32)).astype(BF16)
    return x1, x2, x3


def _dot01(sel, x):
    return sum(jnp.dot(sel, t, preferred_element_type=F32) for t in _split3(x))


def _pack_pairs(x):
    w = x.shape[1] // 2
    bits = lax.bitcast_convert_type(x.astype(BF16).astype(F32), jnp.uint32)
    return lax.bitcast_convert_type(bits[:, :w] | (bits[:, w:] >> 16), jnp.int32)


def _unpack_pairs(words):
    u = lax.bitcast_convert_type(words, jnp.uint32)
    hi = lax.bitcast_convert_type(u & jnp.uint32(0xFFFF0000), F32)
    lo = lax.bitcast_convert_type(u << 16, F32)
    return hi, lo


def _layernorm_rows(x, g, b):
    mu = jnp.mean(x, axis=-1, keepdims=True)
    xc = x - mu
    var = jnp.mean(xc * xc, axis=-1, keepdims=True)
    return xc * lax.rsqrt(var + LN_EPS) * g + b


def _two_sources(n_a, n_b, width, lead=()):
    ta = n_a // ROW_TILE
    assert n_a % ROW_TILE == 0 and n_b % ROW_TILE == 0
    block = (None,) * len(lead) + (ROW_TILE, width)
    spec_a = pl.BlockSpec(block, lambda i: lead + (jnp.minimum(i, ta - 1), 0))
    spec_b = pl.BlockSpec(block, lambda i: lead + (jnp.maximum(i - ta, 0), 0))
    return spec_a, spec_b, ta


def _ln_kernel(xa_ref, xb_ref, g_ref, b_ref, o_ref, *, tiles_a):
    x = jnp.where(pl.program_id(0) < tiles_a, xa_ref[...], xb_ref[...])
    o_ref[...] = _layernorm_rows(x, g_ref[...], b_ref[...])


def _ln(xa, xb, g, b):
    d = xa.shape[1]
    n = xa.shape[0] + xb.shape[0]
    spec_a, spec_b, ta = _two_sources(xa.shape[0], xb.shape[0], d)
    vec = pl.BlockSpec((1, d), lambda i: (0, 0))
    return pl.pallas_call(
        functools.partial(_ln_kernel, tiles_a=ta), grid=(n // ROW_TILE,), in_specs=[spec_a, spec_b, vec, vec],
        out_specs=pl.BlockSpec((ROW_TILE, d), lambda i: (i, 0)),
        out_shape=jax.ShapeDtypeStruct((n, d), F32), compiler_params=_params("parallel"),
        name="ln_in")(xa, xb, g.reshape(1, d), b.reshape(1, d))


def _combine_ln_kernel(resid_ref, y0_ref, y1_ref, y2_ref, y3_ref, gate_ref, g_ref, beta_ref, buf_ref, o_ref):
    half = resid_ref.shape[1] // 2
    acc_hi = resid_ref[:, :half]
    acc_lo = resid_ref[:, half:]
    for k, y_ref in enumerate((y0_ref, y1_ref, y2_ref, y3_ref)):
        hi, lo = _unpack_pairs(y_ref[...])
        gk = gate_ref[:, k:k + 1]
        acc_hi = acc_hi + gk * hi
        acc_lo = acc_lo + gk * lo
    o_ref[...] = _layernorm_rows(jnp.concatenate([acc_hi, acc_lo], axis=1), g_ref[...], beta_ref[...])


def _combine_ln(resid, y_k, gates, g, beta, buf, row0):
    n, d = resid.shape
    n_part = y_k.shape[0] // TOP_K
    nt = n_part // ROW_TILE
    t0 = row0 // ROW_TILE
    row = lambda w: pl.BlockSpec((ROW_TILE, w), lambda i: (t0 + i, 0))
    yk = lambda k: pl.BlockSpec((ROW_TILE, d // 2), lambda i: (k * nt + i, 0))
    vec = pl.BlockSpec((1, d), lambda i: (0, 0))
    return pl.pallas_call(
        _combine_ln_kernel, grid=(nt,),
        in_specs=[row(d)] + [yk(k) for k in range(TOP_K)] + [row(LANES), vec, vec, pl.BlockSpec(memory_space=pl.ANY)],
        out_specs=row(d), out_shape=jax.ShapeDtypeStruct((n, d), F32), input_output_aliases={8: 0},
        compiler_params=_params("parallel"),
        name="combine_ln2")(resid, y_k, y_k, y_k, y_k, gates, g.reshape(1, d), beta.reshape(1, d), buf)


def _inproj_kernel(x_ref, w_ref, o_ref):
    o_ref[...] = jnp.dot(x_ref[...].astype(BF16), w_ref[...], preferred_element_type=F32)


def _inproj(x, w, layer, row0, n_rows):
    d = x.shape[1]
    t0 = row0 // ROW_TILE
    return pl.pallas_call(
        _inproj_kernel, grid=(n_rows // ROW_TILE,),
        in_specs=[pl.BlockSpec((ROW_TILE, d), lambda i: (t0 + i, 0)),
                  pl.BlockSpec((None, d, N_PROJ), lambda i: (layer, 0, 0))],
        out_specs=pl.BlockSpec((ROW_TILE, N_PROJ), lambda i: (i, 0)),
        out_shape=jax.ShapeDtypeStruct((n_rows, N_PROJ), F32), compiler_params=_params("parallel"),
        name="in_proj")(x, w)


def _scan_kernel(*refs, NB, L, last, has_state):
    it = iter(refs)
    proj_ref, gbias_ref, lb_ref, nga_ref, ngb_ref = (next(it) for _ in range(5))
    if has_state:
        C0_ref, n0_ref, m0_ref, S0_ref = (next(it) for _ in range(4))
    for _ in range(4):
        next(it)
    h_ref, C_out, n_out, m_out, S_out = (next(it) for _ in range(5))
    C_s, n_s, m_s, St_s = (next(it) for _ in range(4))

    c = pl.program_id(1)
    nc = pl.num_programs(1)
    H = range(N_HEADS)
    N = range(NB)
    R = NB * L
    assert R == LANES

    @pl.when(c == 0)
    def _init():
        if has_state:
            for n in N:
                for h in H:
                    C_s[n, h, 0:DK_A, :] = C0_ref[n, h]
                    C_s[n, h, DK_A:D_HEAD, :] = jnp.zeros((D_HEAD - DK_A, D_HEAD), F32)
                    St_s[n, h] = S0_ref[n, h].T
            n_s[...] = n0_ref[...]
            m_s[...] = m0_ref[...]
        else:
            C_s[...] = jnp.zeros(C_s.shape, F32)
            n_s[...] = jnp.zeros(n_s.shape, F32)
            m_s[...] = jnp.zeros(m_s.shape, F32)
            St_s[...] = jnp.zeros(St_s.shape, F32)

    row = lax.broadcasted_iota(jnp.int32, (R, R), 0)
    col = lax.broadcasted_iota(jnp.int32, (R, R), 1)
    causal = jnp.logical_and(row // L == col // L, row >= col)
    tril = causal.astype(BF16)
    ones_r = jnp.ones((R, LANES), BF16)
    lane = lax.broadcasted_iota(jnp.int32, (R, LANES), 1)
    step = lax.broadcasted_iota(jnp.int32, (R, 1), 0) % L
    valid = step <= last
    padded = last < L - 1
    tile = lambda base, h: slice(base + LANES * h, base + LANES * (h + 1))
    seq = lambda n: slice(n * L, (n + 1) * L)
    cols = lambda base, width: proj_ref[:, :, base:base + width].reshape(R, width)
    per_seq = lambda vals: jnp.concatenate([jnp.broadcast_to(x, (L, x.shape[-1])) for x in vals], axis=0)
    last_rows = lambda x: [x[n * L + last:n * L + last + 1] for n in N]

    i_t = cols(IG0, LANES) + gbias_ref[0:1, :]
    f_t = _log_sigmoid(cols(FG0, LANES) + gbias_ref[1:2, :])
    zf = cols(FB0, MIX)
    lb = lb_ref[...]
    la = jnp.log(jnp.maximum(lb, LB_FLOOR))
    bb = jnp.log1p(-lb) + _log_sigmoid(zf)
    f_log = jnp.maximum(la, bb) + jnp.log(1.0 + jnp.exp(-jnp.abs(la - bb)))
    b_t = _dot01(tril, f_t)
    g = _dot01(tril, f_log)
    m_all = [m_s[n] for n in N]
    C_prev = [[C_s[n, h] for h in H] for n in N]
    n_prev = [[n_s[n, h] for h in H] for n in N]
    St = [[St_s[n, h] for h in H] for n in N]
    qf = [cols(QA0 + LANES * h, LANES) for h in H]
    kf = [cols(KA0 + LANES * h, LANES) for h in H]
    q = [x.astype(BF16) for x in qf]
    k = [x.astype(BF16) for x in kf]
    v = [cols(VA0 + LANES * h, LANES).astype(BF16) for h in H]
    kb = (1.0 - lb) * jax.nn.sigmoid(-zf)
    qr = cols(QB0, MIX)
    qb = qr * jax.nn.sigmoid(qr)
    vb = [cols(IB0 + LANES * h, LANES).astype(BF16) for h in H]
    qb16 = qb.astype(BF16)
    kb16 = kb.astype(BF16)
    r_t = i_t - b_t
    zero16 = jnp.zeros((R, LANES), BF16)
    r_all = sum(
        lax.dot_general(ones_r, jnp.concatenate([jnp.where(lane == h, term, zero16) for h in H], axis=0),
                        NT_DIMS, preferred_element_type=F32)
        for term in _split3(r_t))
    r_m = [r_all[:, R * h:R * (h + 1)] for h in H]
    qk = [lax.dot_general(q[h], k[h], NT_DIMS, preferred_element_type=F32) for h in H]
    qC = [jnp.concatenate([jnp.dot(q[h][seq(n)], C_prev[n][h].astype(BF16), preferred_element_type=F32)
                           for n in N], axis=0) for h in H]
    levels = []
    bs = 1
    while bs < L:
        levels.append(bs)
        bs *= 2
    small = [bs for bs in levels if bs < SUBLANES]
    g_anchor = {}
    if small:
        sel = jnp.concatenate(
            [(col == (row // (2 * bs)) * (2 * bs) + bs).astype(BF16) for bs in small], axis=0)
        picked = _dot01(sel, g)
        for i, bs in enumerate(small):
            g_anchor[bs] = picked[i * R:(i + 1) * R]
    for bs in levels:
        if bs >= SUBLANES:
            g_anchor[bs] = jnp.concatenate(
                [jnp.broadcast_to(g[p0 + bs:p0 + bs + 1], (2 * bs, MIX)) for p0 in range(0, R, 2 * bs)], axis=0)
    qg = (qb * jnp.exp(g)).astype(BF16)
    o_inter = [jnp.concatenate([lax.dot_general(qg[seq(n), tile(0, h)], St[n][h].astype(BF16), NT_DIMS,
                                                preferred_element_type=F32) for n in N], axis=0) for h in H]
    a = [jnp.where(row == col, lax.dot_general(qb16[:, tile(0, h)], kb16[:, tile(0, h)], NT_DIMS,
                                                preferred_element_type=F32), 0.0) for h in H]
    qn = [jnp.sum(qf[h] * per_seq([n_prev[n][h] for n in N]), axis=1, keepdims=True) for h in H]
    b_col = [b_t[:, h:h + 1] for h in H]
    i_col = [i_t[:, h:h + 1] for h in H]
    m_prev = [per_seq([m_all[n][:, h:h + 1] for n in N]) for h in H]
    cm = [jnp.max(jnp.where(causal, r_m[h], -jnp.inf), axis=1, keepdims=True) for h in H]
    m_t = [b_col[h] + jnp.maximum(m_prev[h], cm[h]) for h in H]
    inter = [jnp.exp(b_col[h] + m_prev[h] - m_t[h]) for h in H]
    d = [jnp.exp(jnp.where(causal, (b_col[h] - m_t[h]) + r_m[h], -1e30)) for h in H]
    s = [qk[h] * d[h] for h in H]
    log2e = 1.4426950408889634
    e = [jnp.exp2((g - g_anchor[bs]) * jnp.where((step // bs) % 2 == 1, log2e, -log2e)) for bs in levels]
    qe = [(qb * x).astype(BF16) for x in e]
    ke = [(kb * x).astype(BF16) for x in e]
    sv = [jnp.dot(s[h].astype(BF16), v[h], preferred_element_type=F32) for h in H]
    sc = [[lax.dot_general(qe[i][:, tile(0, h)], ke[i][:, tile(0, h)], NT_DIMS, preferred_element_type=F32)
           for h in H] for i in range(len(levels))]
    den = [jnp.sum(s[h], axis=1, keepdims=True) + inter[h] * qn[h] for h in H]
    m_new = [last_rows(m_t[h]) for h in H]
    b_last = [last_rows(b_col[h]) for h in H]
    w_arg = [per_seq(b_last[h]) - b_col[h] + i_col[h] - per_seq(m_new[h]) for h in H]
    if padded:
        w_arg = [jnp.where(valid, x, -1e30) for x in w_arg]
    kw = [kf[h] * jnp.exp(w_arg[h]) for h in H]
    kw16 = [x.astype(BF16) for x in kw]
    decay = [[jnp.exp(b_last[h][n] + m_all[n][:, h:h + 1] - m_new[h][n]) for h in H] for n in N]
    kv = [[lax.dot_general(kw16[h][seq(n)], v[h][seq(n)], TN_DIMS, preferred_element_type=F32) for h in H]
          for n in N]
    for i, bs in enumerate(levels):
        pair = jnp.logical_and(row // (2 * bs) == col // (2 * bs),
                               jnp.logical_and((row // bs) % 2 == 1, (col // bs) % 2 == 0))
        for h in H:
            a[h] = jnp.where(pair, sc[i][h], a[h])
    o = [o_inter[h] + jnp.dot(a[h].astype(BF16), vb[h], preferred_element_type=F32) for h in H]
    g_last = last_rows(g)
    dec_arg = per_seq(g_last) - g
    if padded:
        dec_arg = jnp.where(valid, dec_arg, -1e30)
    kdec = (kb * jnp.exp(dec_arg)).astype(BF16)
    eg_last = [jnp.exp(x) for x in g_last]
    vk = [[lax.dot_general(vb[h][seq(n)], kdec[seq(n), tile(0, h)], TN_DIMS, preferred_element_type=F32)
           for h in H] for n in N]
    hh = [(sv[h] + inter[h] * qC[h]) / jnp.maximum(jnp.abs(den[h]), jnp.exp(-m_t[h])) for h in H]
    for h in H:
        hn = hh[h] * lax.rsqrt(jnp.mean(hh[h] * hh[h], axis=-1, keepdims=True) + NORM_EPS) * nga_ref[:, tile(0, h)]
        out_a = jax.nn.sigmoid(cols(OA0 + LANES * h, LANES)) * hn
        h_ref[:, :, tile(0, h)] = out_a.reshape(NB, L, LANES)
    for h in H:
        on = o[h] * lax.rsqrt(jnp.mean(o[h] * o[h], axis=-1, keepdims=True) + NORM_EPS) * ngb_ref[:, tile(0, h)]
        gr = cols(GB0 + LANES * h, LANES)
        h_ref[:, :, tile(MIX, h)] = (gr * jax.nn.sigmoid(gr) * on).reshape(NB, L, LANES)
    lane1 = lax.broadcasted_iota(jnp.int32, (1, LANES), 1)
    for n in N:
        for h in H:
            C_s[n, h] = decay[n][h] * C_prev[n][h] + kv[n][h]
            n_s[n, h] = decay[n][h] * n_prev[n][h] + jnp.sum(kw[h][seq(n)], axis=0, keepdims=True)
            St_s[n, h] = eg_last[n][:, tile(0, h)] * St[n][h] + vk[n][h]
        m_upd = m_all[n]
        for h in H:
            m_upd = jnp.where(lane1 == h, m_new[h][n], m_upd)
        m_s[n] = m_upd

    @pl.when(c == nc - 1)
    def _finish():
        for n in N:
            for h in H:
                C_out[n, h] = C_s[n, h, 0:DK_A, :]
                S_out[n, h] = St_s[n, h].T
        n_out[...] = n_s[...]
        m_out[...] = m_s[...]


def _scan(proj, gbias, lb, nga, ngb, state, stacked, *, layer, NB, n_chunks, L, last):
    batch = proj.shape[0]
    has_state = state is not None
    const2 = lambda b, c: (0, 0)
    in_specs = [
        pl.BlockSpec((NB, L, N_PROJ), lambda b, c: (b, c, 0)),
        pl.BlockSpec((2, LANES), const2),
        pl.BlockSpec((1, MIX), const2),
        pl.BlockSpec((1, MIX), const2),
        pl.BlockSpec((1, MIX), const2),
    ]
    args = [proj, gbias, lb, nga, ngb]
    lst4 = lambda b, c: (layer, b, 0, 0, 0)
    lst3 = lambda b, c: (layer, b, 0, 0)
    if has_state:
        in_specs += [
            pl.BlockSpec((None, NB, N_HEADS, DK_A, D_HEAD), lst4),
            pl.BlockSpec((None, NB, N_HEADS, 1, LANES), lst4),
            pl.BlockSpec((None, NB, 1, LANES), lst3),
            pl.BlockSpec((None, NB, N_HEADS, D_HEAD, D_HEAD), lst4),
        ]
        args += list(state)
    aliases = {len(args) + i: 1 + i for i in range(4)}
    in_specs += [pl.BlockSpec(memory_space=pl.ANY)] * 4
    args += list(stacked)
    out_specs = [
        pl.BlockSpec((NB, L, D_MODEL), lambda b, c: (b, c, 0)),
        pl.BlockSpec((None, NB, N_HEADS, DK_A, D_HEAD), lst4),
        pl.BlockSpec((None, NB, N_HEADS, 1, LANES), lst4),
        pl.BlockSpec((None, NB, 1, LANES), lst3),
        pl.BlockSpec((None, NB, N_HEADS, D_HEAD, D_HEAD), lst4),
    ]
    out_shape = [jax.ShapeDtypeStruct((batch, n_chunks * L, D_MODEL), F32)] + [
        jax.ShapeDtypeStruct(a.shape, a.dtype) for a in stacked]
    scratch = [
        pltpu.VMEM((NB, N_HEADS, D_HEAD, D_HEAD), F32),
        pltpu.VMEM((NB, N_HEADS, 1, LANES), F32),
        pltpu.VMEM((NB, 1, LANES), F32),
        pltpu.VMEM((NB, N_HEADS, D_HEAD, D_HEAD), F32),
    ]
    kern = functools.partial(_scan_kernel, NB=NB, L=L, last=last, has_state=has_state)
    res = pl.pallas_call(
        kern, grid=(batch // NB, n_chunks), in_specs=in_specs, out_specs=out_specs, out_shape=out_shape,
        scratch_shapes=scratch, input_output_aliases=aliases,
        compiler_params=_params("parallel", "arbitrary"),
        name="scan_state" if has_state else "scan_prompt")(*args)
    return res[0], tuple(res[1:])


def _stacked_states(batch):
    return (jnp.zeros((DEPTH, batch, N_HEADS, DK_A, D_HEAD), F32),
            jnp.zeros((DEPTH, batch, N_HEADS, 1, LANES), F32),
            jnp.zeros((DEPTH, batch, 1, LANES), F32),
            jnp.zeros((DEPTH, batch, N_HEADS, D_HEAD, D_HEAD), F32))


def _post_mixer_kernel(x_ref, ha_ref, hb_ref, pa_ref, pb_ref, wout_ref, g1_ref, b1_ref, wpg_ref, wpp_ref, wr2_ref,
                       wrh_ref, br_ref, x1_ref, resid_ref, idx_ref, rank_ref, gate_ref, cnt_ref, carry_s, *, tiles_a):
    i = pl.program_id(0)

    @pl.when(i == 0)
    def _init():
        carry_s[...] = jnp.zeros(carry_s.shape, F32)

    from_a = i < tiles_a
    h = jnp.where(from_a, ha_ref[...], hb_ref[...])
    p = jnp.where(from_a, pa_ref[...], pb_ref[...])
    mix = jnp.dot(h.astype(BF16), wout_ref[...], preferred_element_type=F32)
    x1 = _layernorm_rows(DEEPNORM_ALPHA * x_ref[...] + mix, g1_ref[...], b1_ref[...])
    x1b = x1.astype(BF16)
    x1_ref[...] = _pack_pairs(x1)
    x1_lo = (x1 - x1b.astype(F32)).astype(BF16)
    l2 = jnp.dot(x1b, wr2_ref[...], preferred_element_type=F32)
    logits = (l2[:, :LANES] + l2[:, LANES:]
              + jnp.dot(x1_lo, wrh_ref[...], preferred_element_type=F32) + br_ref[...])

    tm = logits.shape[0]
    lane = lax.broadcasted_iota(jnp.int32, (tm, LANES), 1)
    lane_f = lane.astype(F32)
    l = jnp.where(lane < N_EXPERTS, logits, -jnp.inf)
    vals, idxs, onehots = [], [], []
    for _ in range(TOP_K):
        mx = jnp.max(l, axis=1, keepdims=True)
        ix = jnp.min(jnp.where(l == mx, lane_f, float(LANES)), axis=1, keepdims=True)
        sel = lane_f == ix
        vals.append(mx)
        idxs.append(ix)
        onehots.append(sel.astype(F32))
        l = jnp.where(sel, -jnp.inf, l)
    gate = jax.nn.sigmoid(jnp.dot(x1b, wpg_ref[...], preferred_element_type=F32))
    pp = jnp.dot(p.astype(BF16), wpp_ref[...], preferred_element_type=F32)
    resid_ref[...] = DEEPNORM_ALPHA * x1 + gate * pp

    w = [jnp.exp(val - vals[0]) for val in vals]
    tot = w[0] + w[1] + w[2] + w[3]
    oh_all = onehots[0] + onehots[1] + onehots[2] + onehots[3]
    row = lax.broadcasted_iota(jnp.int32, (tm, tm), 0)
    col = lax.broadcasted_iota(jnp.int32, (tm, tm), 1)
    earlier = (row > col).astype(BF16)
    prefix = jnp.dot(earlier, oh_all.astype(BF16), preferred_element_type=F32) + carry_s[...]
    idx_o = jnp.zeros((tm, LANES), F32)
    rank_o = jnp.zeros((tm, LANES), F32)
    gate_o = jnp.zeros((tm, LANES), F32)
    for k in range(TOP_K):
        rank_k = jnp.sum(onehots[k] * prefix, axis=1, keepdims=True)
        idx_o = jnp.where(lane == k, idxs[k], idx_o)
        rank_o = jnp.where(lane == k, rank_k, rank_o)
        gate_o = jnp.where(lane == k, w[k] / tot, gate_o)
    idx_ref[...] = idx_o.astype(jnp.int32)
    rank_ref[...] = rank_o.astype(jnp.int32)
    gate_ref[...] = gate_o
    carry_s[...] = carry_s[...] + jnp.sum(oh_all, axis=0, keepdims=True)
    cnt_ref[...] = carry_s[...].astype(jnp.int32)


def _post_mixer(layer, x, ha, hb, pa, pb, wout, g1, b1, wpg, wpp, wr2, wrh, br):
    n, d = x.shape
    row = lambda w: pl.BlockSpec((ROW_TILE, w), lambda i: (i, 0))
    full = lambda a: pl.BlockSpec(a.shape, lambda i: (0, 0))
    ha_spec, hb_spec, ta = _two_sources(ha.shape[0], hb.shape[0], d)
    pa_spec, pb_spec, _ = _two_sources(pa.shape[1], pb.shape[1], D_PLE, lead=(layer,))
    g1 = g1.reshape(1, d)
    b1 = b1.reshape(1, d)
    return pl.pallas_call(
        functools.partial(_post_mixer_kernel, tiles_a=ta), grid=(n // ROW_TILE,),
        in_specs=[row(d), ha_spec, hb_spec, pa_spec, pb_spec, full(wout), full(g1), full(b1), full(wpg), full(wpp),
                  full(wr2), full(wrh), full(br)],
        out_specs=[row(d // 2), row(d), row(LANES), row(LANES), row(LANES), pl.BlockSpec((1, LANES), lambda i: (0, 0))],
        out_shape=[jax.ShapeDtypeStruct((n, d // 2), jnp.int32), jax.ShapeDtypeStruct((n, d), F32),
                   jax.ShapeDtypeStruct((n, LANES), jnp.int32), jax.ShapeDtypeStruct((n, LANES), jnp.int32),
                   jax.ShapeDtypeStruct((n, LANES), F32), jax.ShapeDtypeStruct((1, LANES), jnp.int32)],
        scratch_shapes=[pltpu.VMEM((1, LANES), F32)],
        compiler_params=_params("arbitrary"), name="post_mixer")(
            x, ha, hb, pa, pb, wout, g1, b1, wpg, wpp, wr2, wrh, br)


def _expert_kernel(blk_ref, e_ref, lo_ref, hi_ref, nv_ref, nxt_ref, slot_ref, x_ref, wg_hbm, bg_ref, wu_hbm, bu_ref,
                   wd_hbm, bd_ref, o_ref, w32_s, wg_s, wu_s, wd_s, sem, *, layer):
    v = pl.program_id(0)
    prev = jnp.maximum(v - 1, 0)
    valid = v < nv_ref[0]
    new_expert = jnp.logical_or(v == 0, e_ref[v] != e_ref[prev])
    new_block = jnp.logical_or(v == 0, blk_ref[v] != blk_ref[prev])
    slot = slot_ref[v]

    def weight_copies(expert, to_slot):
        return [pltpu.make_async_copy(w_hbm.at[layer, expert], w32_s.at[to_slot, j], sem.at[to_slot, j])
                for j, w_hbm in enumerate((wg_hbm, wu_hbm, wd_hbm))]

    @pl.when(v == 0)
    def _prime():
        for cp in weight_copies(e_ref[0], slot):
            cp.start()

    @pl.when(jnp.logical_and(valid, new_expert))
    def _next_weights():
        for cp in weight_copies(e_ref[v], slot):
            cp.wait()

        @pl.when(nxt_ref[v] != e_ref[v])
        def _prefetch():
            for cp in weight_copies(nxt_ref[v], 1 - slot):
                cp.start()

        wg_s[...] = w32_s[slot, 0].astype(BF16)
        wu_s[...] = w32_s[slot, 1].astype(BF16)
        wd_s[...] = w32_s[slot, 2].astype(BF16)

    @pl.when(valid)
    def _compute():
        half = wg_s.shape[0] // 2
        x_hi, x_lo = (t.astype(BF16) for t in _unpack_pairs(x_ref[...]))
        gate = (jnp.dot(x_hi, wg_s[0:half, :], preferred_element_type=F32)
                + jnp.dot(x_lo, wg_s[half:, :], preferred_element_type=F32) + bg_ref[0])
        up = (jnp.dot(x_hi, wu_s[0:half, :], preferred_element_type=F32)
              + jnp.dot(x_lo, wu_s[half:, :], preferred_element_type=F32) + bu_ref[0])
        gate = jnp.minimum(gate, SWIGLU_LIMIT)
        up = jnp.clip(up, -SWIGLU_LIMIT, SWIGLU_LIMIT)
        glu = gate * jax.nn.sigmoid(gate * SWIGLU_ALPHA)
        act = ((up + 1.0) * glu).astype(BF16)
        y = _pack_pairs(jnp.dot(act, wd_s[...], preferred_element_type=F32) + bd_ref[0])
        rows = blk_ref[v] * EXPERT_ROWS + lax.broadcasted_iota(jnp.int32, (EXPERT_ROWS, 1), 0)
        mine = jnp.logical_and(rows >= lo_ref[v], rows < hi_ref[v])

        @pl.when(new_block)
        def _first():
            o_ref[...] = jnp.where(mine, y, 0)

        @pl.when(jnp.logical_not(new_block))
        def _again():
            o_ref[...] = jnp.where(mine, y, o_ref[...])


def _experts(layer, blk_v, e_v, lo_v, hi_v, n_vis, nxt_v, slot_v, xin, wg, bg, wu, bu, wd, bd):
    n_rows = xin.shape[0]
    d = wg.shape[-1]
    n_visits = blk_v.shape[0]
    xspec = pl.BlockSpec((EXPERT_ROWS, d // 2), lambda v, blk, e, *_: (blk[v], 0))
    ospec = pl.BlockSpec((EXPERT_ROWS, d // 2), lambda v, blk, e, *_: (blk[v], 0))
    wspec = pl.BlockSpec(memory_space=pl.ANY)
    bspec = pl.BlockSpec((None, 1, 1, d), lambda v, blk, e, *_: (layer, e[v], 0, 0))
    grid_spec = pltpu.PrefetchScalarGridSpec(
        num_scalar_prefetch=7, grid=(n_visits,),
        in_specs=[xspec, wspec, bspec, wspec, bspec, wspec, bspec], out_specs=ospec,
        scratch_shapes=[pltpu.VMEM((2, 3, d, d), F32)] + [pltpu.VMEM((d, d), BF16)] * 3
        + [pltpu.SemaphoreType.DMA((2, 3))])
    return pl.pallas_call(
        functools.partial(_expert_kernel, layer=layer), grid_spec=grid_spec,
        out_shape=jax.ShapeDtypeStruct((n_rows, d // 2), jnp.int32),
        compiler_params=_params("arbitrary"), name="experts")(
            blk_v, e_v, lo_v, hi_v, n_vis, nxt_v, slot_v, xin, wg, bg, wu, bu, wd, bd)


DISPATCH_ROWS = 48
COLLECT_ROWS = 24
COMBINE_PARTS = 2


def _sc_layout(n_tok, chunk_rows):
    info = plsc.get_sparse_core_info()
    n_workers = info.num_cores * info.num_subcores
    per_worker = n_tok // n_workers
    n_chunks = per_worker // chunk_rows
    assert per_worker * n_workers == n_tok and n_chunks * chunk_rows == per_worker
    return info, n_workers, per_worker, n_chunks


def _sc_index(pos, n_workers, n_chunks, chunk_rows):
    return pos.T.reshape(TOP_K, n_workers, n_chunks, chunk_rows).transpose(1, 2, 0, 3).reshape(
        n_workers, n_chunks * TOP_K, chunk_rows)


def _dispatch_rows(x1, pos):
    n_tok, w = x1.shape
    rows = DISPATCH_ROWS
    info, n_workers, per_worker, n_chunks = _sc_layout(n_tok, rows)
    mesh = plsc.VectorSubcoreMesh(core_axis_name="core", subcore_axis_name="subcore")

    @functools.partial(
        pl.kernel, out_type=jax.ShapeDtypeStruct((n_tok * TOP_K, w), x1.dtype), mesh=mesh,
        scratch_types=[pltpu.VMEM((n_chunks * TOP_K, rows), jnp.int32), pltpu.VMEM((2, rows, w), x1.dtype),
                       pltpu.SemaphoreType.DMA, pltpu.SemaphoreType.DMA, pltpu.SemaphoreType.DMA],
        name="dispatch_rows")
    def scatter_rows(x_hbm, idx_hbm, out_hbm, idx_v, rows_v, load_sem0, load_sem1, scatter_sem):
        worker = lax.axis_index("subcore") * info.num_cores + lax.axis_index("core")
        pltpu.sync_copy(idx_hbm.at[worker], idx_v)
        base = worker * per_worker
        load_sems = (load_sem0, load_sem1)

        def load(c):
            return pltpu.async_copy(x_hbm.at[pl.ds(base + c * rows, rows)], rows_v.at[c % 2], load_sems[c % 2])

        loading = load(0)
        for c in range(n_chunks):
            loading.wait()
            if c + 1 < n_chunks:
                loading = load(c + 1)
            copies = [pltpu.async_copy(rows_v.at[c % 2], out_hbm.at[idx_v.at[c * TOP_K + k]], scatter_sem)
                      for k in range(TOP_K)]
            for cp in copies:
                cp.wait()

    return scatter_rows(x1, _sc_index(pos, n_workers, n_chunks, rows))


def _collect_rows(y_rows, pos):
    n_tok = pos.shape[0]
    w = y_rows.shape[1]
    rows = COLLECT_ROWS
    info, n_workers, per_worker, n_chunks = _sc_layout(n_tok, rows)
    mesh = plsc.VectorSubcoreMesh(core_axis_name="core", subcore_axis_name="subcore")

    @functools.partial(
        pl.kernel, out_type=jax.ShapeDtypeStruct((n_tok * TOP_K, w), y_rows.dtype), mesh=mesh,
        scratch_types=[pltpu.VMEM((n_chunks * TOP_K, rows), jnp.int32), pltpu.VMEM((2, TOP_K, rows, w), y_rows.dtype),
                       pltpu.SemaphoreType.DMA, pltpu.SemaphoreType.DMA],
        name="collect_rows")
    def gather_rows(y_hbm, idx_hbm, out_hbm, idx_v, rows_v, sem0, sem1):
        worker = lax.axis_index("subcore") * info.num_cores + lax.axis_index("core")
        pltpu.sync_copy(idx_hbm.at[worker], idx_v)
        base = worker * per_worker
        sems = (sem0, sem1)

        def gather(c):
            return [pltpu.async_copy(y_hbm.at[idx_v.at[c * TOP_K + k]], rows_v.at[c % 2, k], sems[c % 2])
                    for k in range(TOP_K)]

        pending = gather(0)
        for c in range(n_chunks):
            following = gather(c + 1) if c + 1 < n_chunks else []
            for cp in pending:
                cp.wait()
            for k in range(TOP_K):
                pltpu.sync_copy(rows_v.at[c % 2, k], out_hbm.at[pl.ds(k * n_tok + base + c * rows, rows)])
            pending = following

    return gather_rows(y_rows, _sc_index(pos, n_workers, n_chunks, rows))


def _moe(layer, x1, routing, wg, bg, wu, bu, wd, bd):
    n_tok = x1.shape[0]
    n_assign = n_tok * TOP_K
    n_blocks = n_assign // EXPERT_ROWS
    idx_o, rank_o, gate_o, cnt = routing
    counts = cnt[0, :N_EXPERTS]
    ends = jnp.cumsum(counts)
    offs = ends - counts
    experts = jnp.arange(N_EXPERTS, dtype=jnp.int32)
    top_i = idx_o[:, :TOP_K]
    off_tok = jnp.sum(jnp.where(top_i[:, :, None] == experts, offs, 0), axis=-1)
    pos = (off_tok + rank_o[:, :TOP_K]).reshape(n_assign)
    pos = pos.reshape(n_tok, TOP_K)
    xin = _dispatch_rows(x1, pos)

    first_blk = offs // EXPERT_ROWS
    last_blk = (ends - 1) // EXPERT_ROWS
    nvis = jnp.where(counts > 0, last_blk - first_blk + 1, 0)
    vend = jnp.cumsum(nvis)
    vstart = vend - nvis
    n_visits = n_blocks + N_EXPERTS - 1
    v = jnp.arange(n_visits, dtype=jnp.int32)
    e_v = jnp.sum((vend[None, :] <= v[:, None]).astype(jnp.int32), axis=1)
    e_last = jnp.max(jnp.where(counts > 0, experts, 0))
    e_v = jnp.minimum(e_v, e_last)
    pick = lambda tab: jnp.sum(jnp.where(e_v[:, None] == experts, tab, 0), axis=-1)
    blk_v = jnp.minimum(pick(first_blk) + v - pick(vstart), n_blocks - 1)
    active = counts > 0
    later = jnp.where(jnp.logical_and(active[None, :], experts[None, :] > experts[:, None]), experts[None, :], N_EXPERTS)
    nxt_e = jnp.min(later, axis=1)
    nxt_e = jnp.where(nxt_e == N_EXPERTS, experts, nxt_e)
    order_e = jnp.cumsum(active.astype(jnp.int32)) - 1
    y_rows = _experts(layer, blk_v.astype(jnp.int32), e_v.astype(jnp.int32), pick(offs).astype(jnp.int32),
                      pick(ends).astype(jnp.int32), vend[-1:].astype(jnp.int32),
                      pick(nxt_e).astype(jnp.int32), (pick(order_e) % 2).astype(jnp.int32),
                      xin, wg, bg, wu, bu, wd, bd)
    return y_rows, pos, gate_o


def _pack_w_in(w_in):
    qa, ka, va, oa, ia, fa, qb, fb, ib, gb = jnp.split(w_in, SPLIT_POINTS, axis=-1)

    def pad_heads(w):
        w = w.reshape(w.shape[:-1] + (N_HEADS, DK_A))
        w = jnp.pad(w, ((0, 0),) * (w.ndim - 1) + ((0, D_HEAD - DK_A),))
        return w.reshape(w.shape[:-2] + (N_HEADS * D_HEAD,))

    def pad_gate(w):
        return jnp.pad(w, ((0, 0),) * (w.ndim - 1) + ((0, LANES - N_HEADS),))

    cols = [pad_heads(qa), pad_heads(ka * (DK_A ** -0.5)), va, oa, qb, fb, ib, gb, pad_gate(ia), pad_gate(fa)]
    return jnp.concatenate(cols, axis=-1).astype(BF16)


def kernel(x_prompt, x_sample, state_mlstm_C, state_mlstm_n, state_mlstm_m, state_hgrn_S, p_prompt, p_sample,
           ln_in_g, ln_in_b, w_in, mlstm_ig_bias, mlstm_fg_bias, mlstm_norm_g, hgrn_lb_logits, hgrn_norm_g,
           w_out, ln1_g, ln1_b, w_router, b_router, w_gate, b_gate, w_up, b_up, w_down, b_down,
           w_ple_gate, w_ple_proj, ln2_g, ln2_b):
    bp, tp, d = x_prompt.shape
    bs, ts, _ = x_sample.shape
    n_p = bp * tp
    n_s = bs * ts
    ts_pad = SUBLANES
    assert tp % CHUNK == 0 and ts <= ts_pad and ts % CHUNK != 0

    lb_soft = jax.nn.softmax(hgrn_lb_logits.astype(F32), axis=0)
    lower_bounds = jnp.cumsum(lb_soft, axis=0) - lb_soft[0]

    w_in_p = _pack_w_in(w_in)
    gbias = jnp.stack([jnp.pad(mlstm_ig_bias, ((0, 0), (0, LANES - N_HEADS))),
                       jnp.pad(mlstm_fg_bias, ((0, 0), (0, LANES - N_HEADS)))], axis=1)
    w_out_b = w_out.astype(BF16)
    w_pg_b = w_ple_gate.astype(BF16)
    w_pp_b = w_ple_proj.astype(BF16)
    w_r = jnp.pad(w_router, ((0, 0), (0, 0), (0, LANES - N_EXPERTS)))
    w_r_hi = w_r.astype(BF16)
    w_r2 = jnp.concatenate([w_r_hi, (w_r - w_r_hi.astype(F32)).astype(BF16)], axis=-1)
    b_r = jnp.pad(b_router, ((0, 0), (0, LANES - N_EXPERTS))).reshape(DEPTH, 1, LANES)
    n0_pad = jnp.pad(state_mlstm_n, ((0, 0), (0, 0), (0, 0), (0, D_HEAD - DK_A))).reshape(
        DEPTH, bs, N_HEADS, 1, D_HEAD)
    m0_pad = jnp.pad(state_mlstm_m, ((0, 0), (0, 0), (0, LANES - N_HEADS))).reshape(DEPTH, bs, 1, LANES)

    p_p = p_prompt.reshape(DEPTH, n_p, D_PLE)
    p_s = p_sample.reshape(DEPTH, n_s, D_PLE)
    x = _ln(x_prompt.reshape(n_p, d), x_sample.reshape(n_s, d), ln_in_g, ln_in_b)

    assert n_p % ROW_TILE == 0 and n_s % ROW_TILE == 0
    nb_p = LANES // CHUNK
    nb_s = LANES // ts_pad
    st_p = _stacked_states(bp)
    st_s = _stacked_states(bs)
    for l in range(DEPTH):
        lb = lower_bounds[l].reshape(1, MIX)
        nga = mlstm_norm_g[l].reshape(1, MIX)
        ngb = hgrn_norm_g[l].reshape(1, MIX)
        proj_p = _inproj(x, w_in_p, l, 0, n_p).reshape(bp, tp, N_PROJ)
        h_p, st_p = _scan(proj_p, gbias[l], lb, nga, ngb, None, st_p, layer=l,
                          NB=nb_p, n_chunks=tp // CHUNK, L=CHUNK, last=CHUNK - 1)
        proj_s = jnp.pad(_inproj(x, w_in_p, l, n_p, n_s).reshape(bs, ts, N_PROJ), ((0, 0), (0, ts_pad - ts), (0, 0)))
        h_s, st_s = _scan(proj_s, gbias[l], lb, nga, ngb,
                          (state_mlstm_C, n0_pad, m0_pad, state_hgrn_S), st_s, layer=l,
                          NB=nb_s, n_chunks=1, L=ts_pad, last=ts - 1)
        x1, resid, *routing = _post_mixer(l, x, h_p.reshape(n_p, d), h_s[:, :ts].reshape(n_s, d), p_p, p_s,
                                          w_out_b[l], ln1_g[l], ln1_b[l], w_pg_b[l], w_pp_b[l],
                                          w_r2[l], w_r_hi[l], b_r[l])
        y_rows, pos, gates = _moe(l, x1, routing, w_gate, b_gate.reshape(DEPTH, N_EXPERTS, 1, d),
                                  w_up, b_up.reshape(DEPTH, N_EXPERTS, 1, d),
                                  w_down, b_down.reshape(DEPTH, N_EXPERTS, 1, d))
        n_part = (n_p + n_s) // COMBINE_PARTS
        assert n_part * COMBINE_PARTS == n_p + n_s and n_part % ROW_TILE == 0
        for part in range(COMBINE_PARTS):
            y_k = _collect_rows(y_rows, pos[part * n_part:(part + 1) * n_part])
            x = _combine_ln(resid, y_k, gates, ln2_g[l], ln2_b[l], x, part * n_part)

    y_prompt = x[:n_p].reshape(bp, tp, d)
    y_sample = x[n_p:].reshape(bs, ts, d)
    unpad = lambda st: (st[0], st[1][:, :, :, 0, :DK_A], st[2][:, :, 0, :N_HEADS], st[3])
    return (y_prompt, y_sample) + unpad(st_p) + unpad(st_s)
```

```python
import functools

import jax
import jax.numpy as jnp
import numpy as np
from jax import lax
from jax.experimental import pallas as pl
from jax.experimental.pallas import tpu as pltpu
from jax.experimental.pallas import tpu_sc as plsc

F32 = jnp.float32
BF16 = jnp.bfloat16

D_MODEL = 1024
DEPTH = 4
D_PLE = 256
N_HEADS = 4
DK_A = 64
D_HEAD = 128
MIX = 512
N_EXPERTS = 32
TOP_K = 4
SWIGLU_LIMIT = 7.0
SWIGLU_ALPHA = 1.702
CHUNK = 128
LN_EPS = 1e-5
NORM_EPS = 1e-6
LB_FLOOR = 1e-20
DEEPNORM_ALPHA = (2 * DEPTH) ** 0.25
SPLIT_SIZES = (256, 256, 512, 512, 4, 4, 512, 512, 512, 512)
SPLIT_POINTS = tuple(int(s) for s in np.cumsum(SPLIT_SIZES)[:-1])

LANES = 128
SUBLANES = 8
VMEM_LIMIT = 56 * 1024 * 1024

QA0, KA0, VA0, OA0 = 0, 512, 1024, 1536
QB0, FB0, IB0, GB0 = 2048, 2560, 3072, 3584
IG0, FG0 = 4096, 4224
N_PROJ = 4352

ROW_TILE = 256
EXPERT_ROWS = 256

NT_DIMS = (((1,), (1,)), ((), ()))
TN_DIMS = (((0,), (0,)), ((), ()))


def _params(*sem):
    return pltpu.CompilerParams(dimension_semantics=sem, vmem_limit_bytes=VMEM_LIMIT)


def _log_sigmoid(x):
    return jnp.minimum(x, 0.0) - jnp.log(1.0 + jnp.exp(-jnp.abs(x)))


def _split3(x):
    x1 = x.astype(BF16)
    r1 = x - x1.astype(F32)
    x2 = r1.astype(BF16)
    x3 = (r1 - x2.astype(F32)).astype(BF16)
    return x1, x2, x3


def _dot01(sel, x):
    return sum(jnp.dot(sel, t, preferred_element_type=F32) for t in _split3(x))


def _pack_pairs(x):
    w = x.shape[1] // 2
    bits = lax.bitcast_convert_type(x.astype(BF16).astype(F32), jnp.uint32)
    return lax.bitcast_convert_type(bits[:, :w] | (bits[:, w:] >> 16), jnp.int32)


def _unpack_pairs(words):
    u = lax.bitcast_convert_type(words, jnp.uint32)
    hi = lax.bitcast_convert_type(u & jnp.uint32(0xFFFF0000), F32)
    lo = lax.bitcast_convert_type(u << 16, F32)
    return hi, lo


def _layernorm_rows(x, g, b):
    mu = jnp.mean(x, axis=-1, keepdims=True)
    xc = x - mu
    var = jnp.mean(xc * xc, axis=-1, keepdims=True)
    return xc * lax.rsqrt(var + LN_EPS) * g + b


def _two_sources(n_a, n_b, width, lead=()):
    ta = n_a // ROW_TILE
    assert n_a % ROW_TILE == 0 and n_b % ROW_TILE == 0
    block = (None,) * len(lead) + (ROW_TILE, width)
    spec_a = pl.BlockSpec(block, lambda i: lead + (jnp.minimum(i, ta - 1), 0))
    spec_b = pl.BlockSpec(block, lambda i: lead + (jnp.maximum(i - ta, 0), 0))
    return spec_a, spec_b, ta


def _ln_kernel(xa_ref, xb_ref, g_ref, b_ref, o_ref, *, tiles_a):
    x = jnp.where(pl.program_id(0) < tiles_a, xa_ref[...], xb_ref[...])
    o_ref[...] = _layernorm_rows(x, g_ref[...], b_ref[...])


def _ln(xa, xb, g, b):
    d = xa.shape[1]
    n = xa.shape[0] + xb.shape[0]
    spec_a, spec_b, ta = _two_sources(xa.shape[0], xb.shape[0], d)
    vec = pl.BlockSpec((1, d), lambda i: (0, 0))
    return pl.pallas_call(
        functools.partial(_ln_kernel, tiles_a=ta), grid=(n // ROW_TILE,), in_specs=[spec_a, spec_b, vec, vec],
        out_specs=pl.BlockSpec((ROW_TILE, d), lambda i: (i, 0)),
        out_shape=jax.ShapeDtypeStruct((n, d), F32), compiler_params=_params("parallel"),
        name="ln_in")(xa, xb, g.reshape(1, d), b.reshape(1, d))


def _combine_ln_kernel(resid_ref, y0_ref, y1_ref, y2_ref, y3_ref, gate_ref, g_ref, beta_ref, buf_ref, o_ref):
    half = resid_ref.shape[1] // 2
    acc_hi = resid_ref[:, :half]
    acc_lo = resid_ref[:, half:]
    for k, y_ref in enumerate((y0_ref, y1_ref, y2_ref, y3_ref)):
        hi, lo = _unpack_pairs(y_ref[...])
        gk = gate_ref[:, k:k + 1]
        acc_hi = acc_hi + gk * hi
        acc_lo = acc_lo + gk * lo
    o_ref[...] = _layernorm_rows(jnp.concatenate([acc_hi, acc_lo], axis=1), g_ref[...], beta_ref[...])


def _combine_ln(resid, y_k, gates, g, beta, buf, row0):
    n, d = resid.shape
    n_part = y_k.shape[0] // TOP_K
    nt = n_part // ROW_TILE
    t0 = row0 // ROW_TILE
    row = lambda w: pl.BlockSpec((ROW_TILE, w), lambda i: (t0 + i, 0))
    yk = lambda k: pl.BlockSpec((ROW_TILE, d // 2), lambda i: (k * nt + i, 0))
    vec = pl.BlockSpec((1, d), lambda i: (0, 0))
    return pl.pallas_call(
        _combine_ln_kernel, grid=(nt,),
        in_specs=[row(d)] + [yk(k) for k in range(TOP_K)] + [row(LANES), vec, vec, pl.BlockSpec(memory_space=pl.ANY)],
        out_specs=row(d), out_shape=jax.ShapeDtypeStruct((n, d), F32), input_output_aliases={8: 0},
        compiler_params=_params("parallel"),
        name="combine_ln2")(resid, y_k, y_k, y_k, y_k, gates, g.reshape(1, d), beta.reshape(1, d), buf)


def _inproj_kernel(x_ref, w_ref, o_ref):
    o_ref[...] = jnp.dot(x_ref[...].astype(BF16), w_ref[...], preferred_element_type=F32)


def _inproj(x, w, layer, row0, n_rows):
    d = x.shape[1]
    t0 = row0 // ROW_TILE
    return pl.pallas_call(
        _inproj_kernel, grid=(n_rows // ROW_TILE,),
        in_specs=[pl.BlockSpec((ROW_TILE, d), lambda i: (t0 + i, 0)),
                  pl.BlockSpec((None, d, N_PROJ), lambda i: (layer, 0, 0))],
        out_specs=pl.BlockSpec((ROW_TILE, N_PROJ), lambda i: (i, 0)),
        out_shape=jax.ShapeDtypeStruct((n_rows, N_PROJ), F32), compiler_params=_params("parallel"),
        name="in_proj")(x, w)


def _scan_kernel(*refs, NB, L, last, has_state):
    it = iter(refs)
    proj_ref, gbias_ref, lb_ref, nga_ref, ngb_ref = (next(it) for _ in range(5))
    if has_state:
        C0_ref, n0_ref, m0_ref, S0_ref = (next(it) for _ in range(4))
    for _ in range(4):
        next(it)
    h_ref, C_out, n_out, m_out, S_out = (next(it) for _ in range(5))
    C_s, n_s, m_s, St_s = (next(it) for _ in range(4))

    c = pl.program_id(1)
    nc = pl.num_programs(1)
    H = range(N_HEADS)
    N = range(NB)
    R = NB * L
    assert R == LANES

    @pl.when(c == 0)
    def _init():
        if has_state:
            for n in N:
                for h in H:
                    C_s[n, h, 0:DK_A, :] = C0_ref[n, h]
                    C_s[n, h, DK_A:D_HEAD, :] = jnp.zeros((D_HEAD - DK_A, D_HEAD), F32)
                    St_s[n, h] = S0_ref[n, h].T
            n_s[...] = n0_ref[...]
            m_s[...] = m0_ref[...]
        else:
            C_s[...] = jnp.zeros(C_s.shape, F32)
            n_s[...] = jnp.zeros(n_s.shape, F32)
            m_s[...] = jnp.zeros(m_s.shape, F32)
            St_s[...] = jnp.zeros(St_s.shape, F32)

    row = lax.broadcasted_iota(jnp.int32, (R, R), 0)
    col = lax.broadcasted_iota(jnp.int32, (R, R), 1)
    causal = jnp.logical_and(row // L == col // L, row >= col)
    tril = causal.astype(BF16)
    ones_r = jnp.ones((R, LANES), BF16)
    lane = lax.broadcasted_iota(jnp.int32, (R, LANES), 1)
    step = lax.broadcasted_iota(jnp.int32, (R, 1), 0) % L
    valid = step <= last
    padded = last < L - 1
    tile = lambda base, h: slice(base + LANES * h, base + LANES * (h + 1))
    seq = lambda n: slice(n * L, (n + 1) * L)
    cols = lambda base, width: proj_ref[:, :, base:base + width].reshape(R, width)
    per_seq = lambda vals: jnp.concatenate([jnp.broadcast_to(x, (L, x.shape[-1])) for x in vals], axis=0)
    last_rows = lambda x: [x[n * L + last:n * L + last + 1] for n in N]

    i_t = cols(IG0, LANES) + gbias_ref[0:1, :]
    f_t = _log_sigmoid(cols(FG0, LANES) + gbias_ref[1:2, :])
    zf = cols(FB0, MIX)
    lb = lb_ref[...]
    la = jnp.log(jnp.maximum(lb, LB_FLOOR))
    bb = jnp.log1p(-lb) + _log_sigmoid(zf)
    f_log = jnp.maximum(la, bb) + jnp.log(1.0 + jnp.exp(-jnp.abs(la - bb)))
    b_t = _dot01(tril, f_t)
    g = _dot01(tril, f_log)
    m_all = [m_s[n] for n in N]
    C_prev = [[C_s[n, h] for h in H] for n in N]
    n_prev = [[n_s[n, h] for h in H] for n in N]
    St = [[St_s[n, h] for h in H] for n in N]
    qf = [cols(QA0 + LANES * h, LANES) for h in H]
    kf = [cols(KA0 + LANES * h, LANES) for h in H]
    q = [x.astype(BF16) for x in qf]
    k = [x.astype(BF16) for x in kf]
    v = [cols(VA0 + LANES * h, LANES).astype(BF16) for h in H]
    kb = (1.0 - lb) * jax.nn.sigmoid(-zf)
    qr = cols(QB0, MIX)
    qb = qr * jax.nn.sigmoid(qr)
    vb = [cols(IB0 + LANES * h, LANES).astype(BF16) for h in H]
    qb16 = qb.astype(BF16)
    kb16 = kb.astype(BF16)
    r_t = i_t - b_t
    zero16 = jnp.zeros((R, LANES), BF16)
    r_all = sum(
        lax.dot_general(ones_r, jnp.concatenate([jnp.where(lane == h, term, zero16) for h in H], axis=0),
                        NT_DIMS, preferred_element_type=F32)
        for term in _split3(r_t))
    r_m = [r_all[:, R * h:R * (h + 1)] for h in H]
    qk = [lax.dot_general(q[h], k[h], NT_DIMS, preferred_element_type=F32) for h in H]
    qC = [jnp.concatenate([jnp.dot(q[h][seq(n)], C_prev[n][h].astype(BF16), preferred_element_type=F32)
                           for n in N], axis=0) for h in H]
    levels = []
    bs = 1
    while bs < L:
        levels.append(bs)
        bs *= 2
    small = [bs for bs in levels if bs < SUBLANES]
    g_anchor = {}
    if small:
        sel = jnp.concatenate(
            [(col == (row // (2 * bs)) * (2 * bs) + bs).astype(BF16) for bs in small], axis=0)
        picked = _dot01(sel, g)
        for i, bs in enumerate(small):
            g_anchor[bs] = picked[i * R:(i + 1) * R]
    for bs in levels:
        if bs >= SUBLANES:
            g_anchor[bs] = jnp.concatenate(
                [jnp.broadcast_to(g[p0 + bs:p0 + bs + 1], (2 * bs, MIX)) for p0 in range(0, R, 2 * bs)], axis=0)
    qg = (qb * jnp.exp(g)).astype(BF16)
    o_inter = [jnp.concatenate([lax.dot_general(qg[seq(n), tile(0, h)], St[n][h].astype(BF16), NT_DIMS,
                                                preferred_element_type=F32) for n in N], axis=0) for h in H]
    a = [jnp.where(row == col, lax.dot_general(qb16[:, tile(0, h)], kb16[:, tile(0, h)], NT_DIMS,
                                                preferred_element_type=F32), 0.0) for h in H]
    qn = [jnp.sum(qf[h] * per_seq([n_prev[n][h] for n in N]), axis=1, keepdims=True) for h in H]
    b_col = [b_t[:, h:h + 1] for h in H]
    i_col = [i_t[:, h:h + 1] for h in H]
    m_prev = [per_seq([m_all[n][:, h:h + 1] for n in N]) for h in H]
    cm = [jnp.max(jnp.where(causal, r_m[h], -jnp.inf), axis=1, keepdims=True) for h in H]
    m_t = [b_col[h] + jnp.maximum(m_prev[h], cm[h]) for h in H]
    inter = [jnp.exp(b_col[h] + m_prev[h] - m_t[h]) for h in H]
    d = [jnp.exp(jnp.where(causal, (b_col[h] - m_t[h]) + r_m[h], -1e30)) for h in H]
    s = [qk[h] * d[h] for h in H]
    log2e = 1.4426950408889634
    e = [jnp.exp2((g - g_anchor[bs]) * jnp.where((step // bs) % 2 == 1, log2e, -log2e)) for bs in levels]
    qe = [(qb * x).astype(BF16) for x in e]
    ke = [(kb * x).astype(BF16) for x in e]
    sv = [jnp.dot(s[h].astype(BF16), v[h], preferred_element_type=F32) for h in H]
    sc = [[lax.dot_general(qe[i][:, tile(0, h)], ke[i][:, tile(0, h)], NT_DIMS, preferred_element_type=F32)
           for h in H] for i in range(len(levels))]
    den = [jnp.sum(s[h], axis=1, keepdims=True) + inter[h] * qn[h] for h in H]
    m_new = [last_rows(m_t[h]) for h in H]
    b_last = [last_rows(b_col[h]) for h in H]
    w_arg = [per_seq(b_last[h]) - b_col[h] + i_col[h] - per_seq(m_new[h]) for h in H]
    if padded:
        w_arg = [jnp.where(valid, x, -1e30) for x in w_arg]
    kw = [kf[h] * jnp.exp(w_arg[h]) for h in H]
    kw16 = [x.astype(BF16) for x in kw]
    decay = [[jnp.exp(b_last[h][n] + m_all[n][:, h:h + 1] - m_new[h][n]) for h in H] for n in N]
    kv = [[lax.dot_general(kw16[h][seq(n)], v[h][seq(n)], TN_DIMS, preferred_element_type=F32) for h in H]
          for n in N]
    for i, bs in enumerate(levels):
        pair = jnp.logical_and(row // (2 * bs) == col // (2 * bs),
                               jnp.logical_and((row // bs) % 2 == 1, (col // bs) % 2 == 0))
        for h in H:
            a[h] = jnp.where(pair, sc[i][h], a[h])
    o = [o_inter[h] + jnp.dot(a[h].astype(BF16), vb[h], preferred_element_type=F32) for h in H]
    g_last = last_rows(g)
    dec_arg = per_seq(g_last) - g
    if padded:
        dec_arg = jnp.where(valid, dec_arg, -1e30)
    kdec = (kb * jnp.exp(dec_arg)).astype(BF16)
    eg_last = [jnp.exp(x) for x in g_last]
    vk = [[lax.dot_general(vb[h][seq(n)], kdec[seq(n), tile(0, h)], TN_DIMS, preferred_element_type=F32)
           for h in H] for n in N]
    hh = [(sv[h] + inter[h] * qC[h]) / jnp.maximum(jnp.abs(den[h]), jnp.exp(-m_t[h])) for h in H]
    for h in H:
        hn = hh[h] * lax.rsqrt(jnp.mean(hh[h] * hh[h], axis=-1, keepdims=True) + NORM_EPS) * nga_ref[:, tile(0, h)]
        out_a = jax.nn.sigmoid(cols(OA0 + LANES * h, LANES)) * hn
        h_ref[:, :, tile(0, h)] = out_a.reshape(NB, L, LANES)
    for h in H:
        on = o[h] * lax.rsqrt(jnp.mean(o[h] * o[h], axis=-1, keepdims=True) + NORM_EPS) * ngb_ref[:, tile(0, h)]
        gr = cols(GB0 + LANES * h, LANES)
        h_ref[:, :, tile(MIX, h)] = (gr * jax.nn.sigmoid(gr) * on).reshape(NB, L, LANES)
    lane1 = lax.broadcasted_iota(jnp.int32, (1, LANES), 1)
    for n in N:
        for h in H:
            C_s[n, h] = decay[n][h] * C_prev[n][h] + kv[n][h]
            n_s[n, h] = decay[n][h] * n_prev[n][h] + jnp.sum(kw[h][seq(n)], axis=0, keepdims=True)
            St_s[n, h] = eg_last[n][:, tile(0, h)] * St[n][h] + vk[n][h]
        m_upd = m_all[n]
        for h in H:
            m_upd = jnp.where(lane1 == h, m_new[h][n], m_upd)
        m_s[n] = m_upd

    @pl.when(c == nc - 1)
    def _finish():
        for n in N:
            for h in H:
                C_out[n, h] = C_s[n, h, 0:DK_A, :]
                S_out[n, h] = St_s[n, h].T
        n_out[...] = n_s[...]
        m_out[...] = m_s[...]


def _scan(proj, gbias, lb, nga, ngb, state, stacked, *, layer, NB, n_chunks, L, last):
    batch = proj.shape[0]
    has_state = state is not None
    const2 = lambda b, c: (0, 0)
    in_specs = [
        pl.BlockSpec((NB, L, N_PROJ), lambda b, c: (b, c, 0)),
        pl.BlockSpec((2, LANES), const2),
        pl.BlockSpec((1, MIX), const2),
        pl.BlockSpec((1, MIX), const2),
        pl.BlockSpec((1, MIX), const2),
    ]
    args = [proj, gbias, lb, nga, ngb]
    lst4 = lambda b, c: (layer, b, 0, 0, 0)
    lst3 = lambda b, c: (layer, b, 0, 0)
    if has_state:
        in_specs += [
            pl.BlockSpec((None, NB, N_HEADS, DK_A, D_HEAD), lst4),
            pl.BlockSpec((None, NB, N_HEADS, 1, LANES), lst4),
            pl.BlockSpec((None, NB, 1, LANES), lst3),
            pl.BlockSpec((None, NB, N_HEADS, D_HEAD, D_HEAD), lst4),
        ]
        args += list(state)
    aliases = {len(args) + i: 1 + i for i in range(4)}
    in_specs += [pl.BlockSpec(memory_space=pl.ANY)] * 4
    args += list(stacked)
    out_specs = [
        pl.BlockSpec((NB, L, D_MODEL), lambda b, c: (b, c, 0)),
        pl.BlockSpec((None, NB, N_HEADS, DK_A, D_HEAD), lst4),
        pl.BlockSpec((None, NB, N_HEADS, 1, LANES), lst4),
        pl.BlockSpec((None, NB, 1, LANES), lst3),
        pl.BlockSpec((None, NB, N_HEADS, D_HEAD, D_HEAD), lst4),
    ]
    out_shape = [jax.ShapeDtypeStruct((batch, n_chunks * L, D_MODEL), F32)] + [
        jax.ShapeDtypeStruct(a.shape, a.dtype) for a in stacked]
    scratch = [
        pltpu.VMEM((NB, N_HEADS, D_HEAD, D_HEAD), F32),
        pltpu.VMEM((NB, N_HEADS, 1, LANES), F32),
        pltpu.VMEM((NB, 1, LANES), F32),
        pltpu.VMEM((NB, N_HEADS, D_HEAD, D_HEAD), F32),
    ]
    kern = functools.partial(_scan_kernel, NB=NB, L=L, last=last, has_state=has_state)
    res = pl.pallas_call(
        kern, grid=(batch // NB, n_chunks), in_specs=in_specs, out_specs=out_specs, out_shape=out_shape,
        scratch_shapes=scratch, input_output_aliases=aliases,
        compiler_params=_params("parallel", "arbitrary"),
        name="scan_state" if has_state else "scan_prompt")(*args)
    return res[0], tuple(res[1:])


def _stacked_states(batch):
    return (jnp.zeros((DEPTH, batch, N_HEADS, DK_A, D_HEAD), F32),
            jnp.zeros((DEPTH, batch, N_HEADS, 1, LANES), F32),
            jnp.zeros((DEPTH, batch, 1, LANES), F32),
            jnp.zeros((DEPTH, batch, N_HEADS, D_HEAD, D_HEAD), F32))


def _post_mixer_kernel(x_ref, ha_ref, hb_ref, pa_ref, pb_ref, wout_ref, g1_ref, b1_ref, wpg_ref, wpp_ref, wr2_ref,
                       wrh_ref, br_ref, x1_ref, resid_ref, idx_ref, rank_ref, gate_ref, cnt_ref, carry_s, *, tiles_a):
    i = pl.program_id(0)

    @pl.when(i == 0)
    def _init():
        carry_s[...] = jnp.zeros(carry_s.shape, F32)

    from_a = i < tiles_a
    h = jnp.where(from_a, ha_ref[...], hb_ref[...])
    p = jnp.where(from_a, pa_ref[...], pb_ref[...])
    mix = jnp.dot(h.astype(BF16), wout_ref[...], preferred_element_type=F32)
    x1 = _layernorm_rows(DEEPNORM_ALPHA * x_ref[...] + mix, g1_ref[...], b1_ref[...])
    x1b = x1.astype(BF16)
    x1_ref[...] = _pack_pairs(x1)
    x1_lo = (x1 - x1b.astype(F32)).astype(BF16)
    l2 = jnp.dot(x1b, wr2_ref[...], preferred_element_type=F32)
    logits = (l2[:, :LANES] + l2[:, LANES:]
              + jnp.dot(x1_lo, wrh_ref[...], preferred_element_type=F32) + br_ref[...])

    tm = logits.shape[0]
    lane = lax.broadcasted_iota(jnp.int32, (tm, LANES), 1)
    lane_f = lane.astype(F32)
    l = jnp.where(lane < N_EXPERTS, logits, -jnp.inf)
    vals, idxs, onehots = [], [], []
    for _ in range(TOP_K):
        mx = jnp.max(l, axis=1, keepdims=True)
        ix = jnp.min(jnp.where(l == mx, lane_f, float(LANES)), axis=1, keepdims=True)
        sel = lane_f == ix
        vals.append(mx)
        idxs.append(ix)
        onehots.append(sel.astype(F32))
        l = jnp.where(sel, -jnp.inf, l)
    gate = jax.nn.sigmoid(jnp.dot(x1b, wpg_ref[...], preferred_element_type=F32))
    pp = jnp.dot(p.astype(BF16), wpp_ref[...], preferred_element_type=F32)
    resid_ref[...] = DEEPNORM_ALPHA * x1 + gate * pp

    w = [jnp.exp(val - vals[0]) for val in vals]
    tot = w[0] + w[1] + w[2] + w[3]
    oh_all = onehots[0] + onehots[1] + onehots[2] + onehots[3]
    row = lax.broadcasted_iota(jnp.int32, (tm, tm), 0)
    col = lax.broadcasted_iota(jnp.int32, (tm, tm), 1)
    earlier = (row > col).astype(BF16)
    prefix = jnp.dot(earlier, oh_all.astype(BF16), preferred_element_type=F32) + carry_s[...]
    idx_o = jnp.zeros((tm, LANES), F32)
    rank_o = jnp.zeros((tm, LANES), F32)
    gate_o = jnp.zeros((tm, LANES), F32)
    for k in range(TOP_K):
        rank_k = jnp.sum(onehots[k] * prefix, axis=1, keepdims=True)
        idx_o = jnp.where(lane == k, idxs[k], idx_o)
        rank_o = jnp.where(lane == k, rank_k, rank_o)
        gate_o = jnp.where(lane == k, w[k] / tot, gate_o)
    idx_ref[...] = idx_o.astype(jnp.int32)
    rank_ref[...] = rank_o.astype(jnp.int32)
    gate_ref[...] = gate_o
    carry_s[...] = carry_s[...] + jnp.sum(oh_all, axis=0, keepdims=True)
    cnt_ref[...] = carry_s[...].astype(jnp.int32)


def _post_mixer(layer, x, ha, hb, pa, pb, wout, g1, b1, wpg, wpp, wr2, wrh, br):
    n, d = x.shape
    row = lambda w: pl.BlockSpec((ROW_TILE, w), lambda i: (i, 0))
    full = lambda a: pl.BlockSpec(a.shape, lambda i: (0, 0))
    ha_spec, hb_spec, ta = _two_sources(ha.shape[0], hb.shape[0], d)
    pa_spec, pb_spec, _ = _two_sources(pa.shape[1], pb.shape[1], D_PLE, lead=(layer,))
    g1 = g1.reshape(1, d)
    b1 = b1.reshape(1, d)
    return pl.pallas_call(
        functools.partial(_post_mixer_kernel, tiles_a=ta), grid=(n // ROW_TILE,),
        in_specs=[row(d), ha_spec, hb_spec, pa_spec, pb_spec, full(wout), full(g1), full(b1), full(wpg), full(wpp),
                  full(wr2), full(wrh), full(br)],
        out_specs=[row(d // 2), row(d), row(LANES), row(LANES), row(LANES), pl.BlockSpec((1, LANES), lambda i: (0, 0))],
        out_shape=[jax.ShapeDtypeStruct((n, d // 2), jnp.int32), jax.ShapeDtypeStruct((n, d), F32),
                   jax.ShapeDtypeStruct((n, LANES), jnp.int32), jax.ShapeDtypeStruct((n, LANES), jnp.int32),
                   jax.ShapeDtypeStruct((n, LANES), F32), jax.ShapeDtypeStruct((1, LANES), jnp.int32)],
        scratch_shapes=[pltpu.VMEM((1, LANES), F32)],
        compiler_params=_params("arbitrary"), name="post_mixer")(
            x, ha, hb, pa, pb, wout, g1, b1, wpg, wpp, wr2, wrh, br)


def _expert_kernel(blk_ref, e_ref, lo_ref, hi_ref, nv_ref, nxt_ref, slot_ref, x_ref, wg_hbm, bg_ref, wu_hbm, bu_ref,
                   wd_hbm, bd_ref, o_ref, w32_s, wg_s, wu_s, wd_s, sem, *, layer):
    v = pl.program_id(0)
    prev = jnp.maximum(v - 1, 0)
    valid = v < nv_ref[0]
    new_expert = jnp.logical_or(v == 0, e_ref[v] != e_ref[prev])
    new_block = jnp.logical_or(v == 0, blk_ref[v] != blk_ref[prev])
    slot = slot_ref[v]

    def weight_copies(expert, to_slot):
        return [pltpu.make_async_copy(w_hbm.at[layer, expert], w32_s.at[to_slot, j], sem.at[to_slot, j])
                for j, w_hbm in enumerate((wg_hbm, wu_hbm, wd_hbm))]

    @pl.when(v == 0)
    def _prime():
        for cp in weight_copies(e_ref[0], slot):
            cp.start()

    @pl.when(jnp.logical_and(valid, new_expert))
    def _next_weights():
        for cp in weight_copies(e_ref[v], slot):
            cp.wait()

        @pl.when(nxt_ref[v] != e_ref[v])
        def _prefetch():
            for cp in weight_copies(nxt_ref[v], 1 - slot):
                cp.start()

        wg_s[...] = w32_s[slot, 0].astype(BF16)
        wu_s[...] = w32_s[slot, 1].astype(BF16)
        wd_s[...] = w32_s[slot, 2].astype(BF16)

    @pl.when(valid)
    def _compute():
        half = wg_s.shape[0] // 2
        x_hi, x_lo = (t.astype(BF16) for t in _unpack_pairs(x_ref[...]))
        gate = (jnp.dot(x_hi, wg_s[0:half, :], preferred_element_type=F32)
                + jnp.dot(x_lo, wg_s[half:, :], preferred_element_type=F32) + bg_ref[0])
        up = (jnp.dot(x_hi, wu_s[0:half, :], preferred_element_type=F32)
              + jnp.dot(x_lo, wu_s[half:, :], preferred_element_type=F32) + bu_ref[0])
        gate = jnp.minimum(gate, SWIGLU_LIMIT)
        up = jnp.clip(up, -SWIGLU_LIMIT, SWIGLU_LIMIT)
        glu = gate * jax.nn.sigmoid(gate * SWIGLU_ALPHA)
        act = ((up + 1.0) * glu).astype(BF16)
        y = _pack_pairs(jnp.dot(act, wd_s[...], preferred_element_type=F32) + bd_ref[0])
        rows = blk_ref[v] * EXPERT_ROWS + lax.broadcasted_iota(jnp.int32, (EXPERT_ROWS, 1), 0)
        mine = jnp.logical_and(rows >= lo_ref[v], rows < hi_ref[v])

        @pl.when(new_block)
        def _first():
            o_ref[...] = jnp.where(mine, y, 0)

        @pl.when(jnp.logical_not(new_block))
        def _again():
            o_ref[...] = jnp.where(mine, y, o_ref[...])


def _experts(layer, blk_v, e_v, lo_v, hi_v, n_vis, nxt_v, slot_v, xin, wg, bg, wu, bu, wd, bd):
    n_rows = xin.shape[0]
    d = wg.shape[-1]
    n_visits = blk_v.shape[0]
    xspec = pl.BlockSpec((EXPERT_ROWS, d // 2), lambda v, blk, e, *_: (blk[v], 0))
    ospec = pl.BlockSpec((EXPERT_ROWS, d // 2), lambda v, blk, e, *_: (blk[v], 0))
    wspec = pl.BlockSpec(memory_space=pl.ANY)
    bspec = pl.BlockSpec((None, 1, 1, d), lambda v, blk, e, *_: (layer, e[v], 0, 0))
    grid_spec = pltpu.PrefetchScalarGridSpec(
        num_scalar_prefetch=7, grid=(n_visits,),
        in_specs=[xspec, wspec, bspec, wspec, bspec, wspec, bspec], out_specs=ospec,
        scratch_shapes=[pltpu.VMEM((2, 3, d, d), F32)] + [pltpu.VMEM((d, d), BF16)] * 3
        + [pltpu.SemaphoreType.DMA((2, 3))])
    return pl.pallas_call(
        functools.partial(_expert_kernel, layer=layer), grid_spec=grid_spec,
        out_shape=jax.ShapeDtypeStruct((n_rows, d // 2), jnp.int32),
        compiler_params=_params("arbitrary"), name="experts")(
            blk_v, e_v, lo_v, hi_v, n_vis, nxt_v, slot_v, xin, wg, bg, wu, bu, wd, bd)


DISPATCH_ROWS = 48
COLLECT_ROWS = 24
COMBINE_PARTS = 2


def _sc_layout(n_tok, chunk_rows):
    info = plsc.get_sparse_core_info()
    n_workers = info.num_cores * info.num_subcores
    per_worker = n_tok // n_workers
    n_chunks = per_worker // chunk_rows
    assert per_worker * n_workers == n_tok and n_chunks * chunk_rows == per_worker
    return info, n_workers, per_worker, n_chunks


def _sc_index(pos, n_workers, n_chunks, chunk_rows):
    return pos.T.reshape(TOP_K, n_workers, n_chunks, chunk_rows).transpose(1, 2, 0, 3).reshape(
        n_workers, n_chunks * TOP_K, chunk_rows)


def _dispatch_rows(x1, pos):
    n_tok, w = x1.shape
    rows = DISPATCH_ROWS
    info, n_workers, per_worker, n_chunks = _sc_layout(n_tok, rows)
    mesh = plsc.VectorSubcoreMesh(core_axis_name="core", subcore_axis_name="subcore")

    @functools.partial(
        pl.kernel, out_type=jax.ShapeDtypeStruct((n_tok * TOP_K, w), x1.dtype), mesh=mesh,
        scratch_types=[pltpu.VMEM((n_chunks * TOP_K, rows), jnp.int32), pltpu.VMEM((2, rows, w), x1.dtype),
                       pltpu.SemaphoreType.DMA, pltpu.SemaphoreType.DMA, pltpu.SemaphoreType.DMA],
        name="dispatch_rows")
    def scatter_rows(x_hbm, idx_hbm, out_hbm, idx_v, rows_v, load_sem0, load_sem1, scatter_sem):
        worker = lax.axis_index("subcore") * info.num_cores + lax.axis_index("core")
        pltpu.sync_copy(idx_hbm.at[worker], idx_v)
        base = worker * per_worker
        load_sems = (load_sem0, load_sem1)

        def load(c):
            return pltpu.async_copy(x_hbm.at[pl.ds(base + c * rows, rows)], rows_v.at[c % 2], load_sems[c % 2])

        loading = load(0)
        for c in range(n_chunks):
            loading.wait()
            if c + 1 < n_chunks:
                loading = load(c + 1)
            copies = [pltpu.async_copy(rows_v.at[c % 2], out_hbm.at[idx_v.at[c * TOP_K + k]], scatter_sem)
                      for k in range(TOP_K)]
            for cp in copies:
                cp.wait()

    return scatter_rows(x1, _sc_index(pos, n_workers, n_chunks, rows))


def _collect_rows(y_rows, pos):
    n_tok = pos.shape[0]
    w = y_rows.shape[1]
    rows = COLLECT_ROWS
    info, n_workers, per_worker, n_chunks = _sc_layout(n_tok, rows)
    mesh = plsc.VectorSubcoreMesh(core_axis_name="core", subcore_axis_name="subcore")

    @functools.partial(
        pl.kernel, out_type=jax.ShapeDtypeStruct((n_tok * TOP_K, w), y_rows.dtype), mesh=mesh,
        scratch_types=[pltpu.VMEM((n_chunks * TOP_K, rows), jnp.int32), pltpu.VMEM((2, TOP_K, rows, w), y_rows.dtype),
                       pltpu.SemaphoreType.DMA, pltpu.SemaphoreType.DMA],
        name="collect_rows")
    def gather_rows(y_hbm, idx_hbm, out_hbm, idx_v, rows_v, sem0, sem1):
        worker = lax.axis_index("subcore") * info.num_cores + lax.axis_index("core")
        pltpu.sync_copy(idx_hbm.at[worker], idx_v)
        base = worker * per_worker
        sems = (sem0, sem1)

        def gather(c):
            return [pltpu.async_copy(y_hbm.at[idx_v.at[c * TOP_K + k]], rows_v.at[c % 2, k], sems[c % 2])
                    for k in range(TOP_K)]

        pending = gather(0)
        for c in range(n_chunks):
            following = gather(c + 1) if c + 1 < n_chunks else []
            for cp in pending:
                cp.wait()
            for k in range(TOP_K):
                pltpu.sync_copy(rows_v.at[c % 2, k], out_hbm.at[pl.ds(k * n_tok + base + c * rows, rows)])
            pending = following

    return gather_rows(y_rows, _sc_index(pos, n_workers, n_chunks, rows))


def _moe(layer, x1, routing, wg, bg, wu, bu, wd, bd):
    n_tok = x1.shape[0]
    n_assign = n_tok * TOP_K
    n_blocks = n_assign // EXPERT_ROWS
    idx_o, rank_o, gate_o, cnt = routing
    counts = cnt[0, :N_EXPERTS]
    ends = jnp.cumsum(counts)
    offs = ends - counts
    experts = jnp.arange(N_EXPERTS, dtype=jnp.int32)
    top_i = idx_o[:, :TOP_K]
    off_tok = jnp.sum(jnp.where(top_i[:, :, None] == experts, offs, 0), axis=-1)
    pos = (off_tok + rank_o[:, :TOP_K]).reshape(n_assign)
    pos = pos.reshape(n_tok, TOP_K)
    xin = _dispatch_rows(x1, pos)

    first_blk = offs // EXPERT_ROWS
    last_blk = (ends - 1) // EXPERT_ROWS
    nvis = jnp.where(counts > 0, last_blk - first_blk + 1, 0)
    vend = jnp.cumsum(nvis)
    vstart = vend - nvis
    n_visits = n_blocks + N_EXPERTS - 1
    v = jnp.arange(n_visits, dtype=jnp.int32)
    e_v = jnp.sum((vend[None, :] <= v[:, None]).astype(jnp.int32), axis=1)
    e_last = jnp.max(jnp.where(counts > 0, experts, 0))
    e_v = jnp.minimum(e_v, e_last)
    pick = lambda tab: jnp.sum(jnp.where(e_v[:, None] == experts, tab, 0), axis=-1)
    blk_v = jnp.minimum(pick(first_blk) + v - pick(vstart), n_blocks - 1)
    active = counts > 0
    later = jnp.where(jnp.logical_and(active[None, :], experts[None, :] > experts[:, None]), experts[None, :], N_EXPERTS)
    nxt_e = jnp.min(later, axis=1)
    nxt_e = jnp.where(nxt_e == N_EXPERTS, experts, nxt_e)
    order_e = jnp.cumsum(active.astype(jnp.int32)) - 1
    y_rows = _experts(layer, blk_v.astype(jnp.int32), e_v.astype(jnp.int32), pick(offs).astype(jnp.int32),
                      pick(ends).astype(jnp.int32), vend[-1:].astype(jnp.int32),
                      pick(nxt_e).astype(jnp.int32), (pick(order_e) % 2).astype(jnp.int32),
                      xin, wg, bg, wu, bu, wd, bd)
    return y_rows, pos, gate_o


def _pack_w_in(w_in):
    qa, ka, va, oa, ia, fa, qb, fb, ib, gb = jnp.split(w_in, SPLIT_POINTS, axis=-1)

    def pad_heads(w):
        w = w.reshape(w.shape[:-1] + (N_HEADS, DK_A))
        w = jnp.pad(w, ((0, 0),) * (w.ndim - 1) + ((0, D_HEAD - DK_A),))
        return w.reshape(w.shape[:-2] + (N_HEADS * D_HEAD,))

    def pad_gate(w):
        return jnp.pad(w, ((0, 0),) * (w.ndim - 1) + ((0, LANES - N_HEADS),))

    cols = [pad_heads(qa), pad_heads(ka * (DK_A ** -0.5)), va, oa, qb, fb, ib, gb, pad_gate(ia), pad_gate(fa)]
    return jnp.concatenate(cols, axis=-1).astype(BF16)


def kernel(x_prompt, x_sample, state_mlstm_C, state_mlstm_n, state_mlstm_m, state_hgrn_S, p_prompt, p_sample,
           ln_in_g, ln_in_b, w_in, mlstm_ig_bias, mlstm_fg_bias, mlstm_norm_g, hgrn_lb_logits, hgrn_norm_g,
           w_out, ln1_g, ln1_b, w_router, b_router, w_gate, b_gate, w_up, b_up, w_down, b_down,
           w_ple_gate, w_ple_proj, ln2_g, ln2_b):
    bp, tp, d = x_prompt.shape
    bs, ts, _ = x_sample.shape
    n_p = bp * tp
    n_s = bs * ts
    ts_pad = SUBLANES
    assert tp % CHUNK == 0 and ts <= ts_pad and ts % CHUNK != 0

    lb_soft = jax.nn.softmax(hgrn_lb_logits.astype(F32), axis=0)
    lower_bounds = jnp.cumsum(lb_soft, axis=0) - lb_soft[0]

    w_in_p = _pack_w_in(w_in)
    gbias = jnp.stack([jnp.pad(mlstm_ig_bias, ((0, 0), (0, LANES - N_HEADS))),
                       jnp.pad(mlstm_fg_bias, ((0, 0), (0, LANES - N_HEADS)))], axis=1)
    w_out_b = w_out.astype(BF16)
    w_pg_b = w_ple_gate.astype(BF16)
    w_pp_b = w_ple_proj.astype(BF16)
    w_r = jnp.pad(w_router, ((0, 0), (0, 0), (0, LANES - N_EXPERTS)))
    w_r_hi = w_r.astype(BF16)
    w_r2 = jnp.concatenate([w_r_hi, (w_r - w_r_hi.astype(F32)).astype(BF16)], axis=-1)
    b_r = jnp.pad(b_router, ((0, 0), (0, LANES - N_EXPERTS))).reshape(DEPTH, 1, LANES)
    n0_pad = jnp.pad(state_mlstm_n, ((0, 0), (0, 0), (0, 0), (0, D_HEAD - DK_A))).reshape(
        DEPTH, bs, N_HEADS, 1, D_HEAD)
    m0_pad = jnp.pad(state_mlstm_m, ((0, 0), (0, 0), (0, LANES - N_HEADS))).reshape(DEPTH, bs, 1, LANES)

    p_p = p_prompt.reshape(DEPTH, n_p, D_PLE)
    p_s = p_sample.reshape(DEPTH, n_s, D_PLE)
    x = _ln(x_prompt.reshape(n_p, d), x_sample.reshape(n_s, d), ln_in_g, ln_in_b)

    assert n_p % ROW_TILE == 0 and n_s % ROW_TILE == 0
    nb_p = LANES // CHUNK
    nb_s = LANES // ts_pad
    st_p = _stacked_states(bp)
    st_s = _stacked_states(bs)
    for l in range(DEPTH):
        lb = lower_bounds[l].reshape(1, MIX)
        nga = mlstm_norm_g[l].reshape(1, MIX)
        ngb = hgrn_norm_g[l].reshape(1, MIX)
        proj_p = _inproj(x, w_in_p, l, 0, n_p).reshape(bp, tp, N_PROJ)
        h_p, st_p = _scan(proj_p, gbias[l], lb, nga, ngb, None, st_p, layer=l,
                          NB=nb_p, n_chunks=tp // CHUNK, L=CHUNK, last=CHUNK - 1)
        proj_s = jnp.pad(_inproj(x, w_in_p, l, n_p, n_s).reshape(bs, ts, N_PROJ), ((0, 0), (0, ts_pad - ts), (0, 0)))
        h_s, st_s = _scan(proj_s, gbias[l], lb, nga, ngb,
                          (state_mlstm_C, n0_pad, m0_pad, state_hgrn_S), st_s, layer=l,
                          NB=nb_s, n_chunks=1, L=ts_pad, last=ts - 1)
        x1, resid, *routing = _post_mixer(l, x, h_p.reshape(n_p, d), h_s[:, :ts].reshape(n_s, d), p_p, p_s,
                                          w_out_b[l], ln1_g[l], ln1_b[l], w_pg_b[l], w_pp_b[l],
                                          w_r2[l], w_r_hi[l], b_r[l])
        y_rows, pos, gates = _moe(l, x1, routing, w_gate, b_gate.reshape(DEPTH, N_EXPERTS, 1, d),
                                  w_up, b_up.reshape(DEPTH, N_EXPERTS, 1, d),
                                  w_down, b_down.reshape(DEPTH, N_EXPERTS, 1, d))
        n_part = (n_p + n_s) // COMBINE_PARTS
        assert n_part * COMBINE_PARTS == n_p + n_s and n_part % ROW_TILE == 0
        for part in range(COMBINE_PARTS):
            y_k = _collect_rows(y_rows, pos[part * n_part:(part + 1) * n_part])
            x = _combine_ln(resid, y_k, gates, ln2_g[l], ln2_b[l], x, part * n_part)

    y_prompt = x[:n_p].reshape(bp, tp, d)
    y_sample = x[n_p:].reshape(bs, ts, d)
    unpad = lambda st: (st[0], st[1][:, :, :, 0, :DK_A], st[2][:, :, 0, :N_HEADS], st[3])
    return (y_prompt, y_sample) + unpad(st_p) + unpad(st_s)
```

```python
import functools

import jax
import jax.numpy as jnp
import numpy as np
from jax import lax
from jax.experimental import pallas as pl
from jax.experimental.pallas import tpu as pltpu
from jax.experimental.pallas import tpu_sc as plsc

F32 = jnp.float32
BF16 = jnp.bfloat16

D_MODEL = 1024
DEPTH = 4
D_PLE = 256
N_HEADS = 4
DK_A = 64
D_HEAD = 128
MIX = 512
N_EXPERTS = 32
TOP_K = 4
SWIGLU_LIMIT = 7.0
SWIGLU_ALPHA = 1.702
CHUNK = 64
LN_EPS = 1e-5
NORM_EPS = 1e-6
LB_FLOOR = 1e-20
DEEPNORM_ALPHA = (2 * DEPTH) ** 0.25
SPLIT_SIZES = (256, 256, 512, 512, 4, 4, 512, 512, 512, 512)
SPLIT_POINTS = tuple(int(s) for s in np.cumsum(SPLIT_SIZES)[:-1])

LANES = 128
SUBLANES = 8
VMEM_LIMIT = 56 * 1024 * 1024

QA0, KA0, VA0, OA0 = 0, 512, 1024, 1536
QB0, FB0, IB0, GB0 = 2048, 2560, 3072, 3584
IG0, FG0 = 4096, 4224
N_PROJ = 4352

ROW_TILE = 512
EXPERT_ROWS = 256

NT_DIMS = (((1,), (1,)), ((), ()))
TN_DIMS = (((0,), (0,)), ((), ()))


def _params(*sem):
    return pltpu.CompilerParams(dimension_semantics=sem, vmem_limit_bytes=VMEM_LIMIT)


def _log_sigmoid(x):
    return jnp.minimum(x, 0.0) - jnp.log(1.0 + jnp.exp(-jnp.abs(x)))


def _split3(x):
    x1 = x.astype(BF16)
    r1 = x - x1.astype(F32)
    x2 = r1.astype(BF16)
    x3 = (r1 - x2.astype(F32)).astype(BF16)
    return x1, x2, x3


def _dot01(sel, x):
    return sum(jnp.dot(sel, t, preferred_element_type=F32) for t in _split3(x))


def _pack_pairs(x):
    w = x.shape[1] // 2
    bits = lax.bitcast_convert_type(x.astype(BF16).astype(F32), jnp.uint32)
    return lax.bitcast_convert_type(bits[:, :w] | (bits[:, w:] >> 16), jnp.int32)


def _unpack_pairs(words):
    u = lax.bitcast_convert_type(words, jnp.uint32)
    hi = lax.bitcast_convert_type(u & jnp.uint32(0xFFFF0000), F32)
    lo = lax.bitcast_convert_type(u << 16, F32)
    return hi, lo


def _layernorm_rows(x, g, b):
    mu = jnp.mean(x, axis=-1, keepdims=True)
    xc = x - mu
    var = jnp.mean(xc * xc, axis=-1, keepdims=True)
    return xc * lax.rsqrt(var + LN_EPS) * g + b


def _two_sources(n_a, n_b, width, lead=()):
    ta = n_a // ROW_TILE
    assert n_a % ROW_TILE == 0 and n_b % ROW_TILE == 0
    block = (None,) * len(lead) + (ROW_TILE, width)
    spec_a = pl.BlockSpec(block, lambda i: lead + (jnp.minimum(i, ta - 1), 0))
    spec_b = pl.BlockSpec(block, lambda i: lead + (jnp.maximum(i - ta, 0), 0))
    return spec_a, spec_b, ta


def _ln_kernel(xa_ref, xb_ref, g_ref, b_ref, o_ref, *, tiles_a):
    x = jnp.where(pl.program_id(0) < tiles_a, xa_ref[...], xb_ref[...])
    o_ref[...] = _layernorm_rows(x, g_ref[...], b_ref[...])


def _ln(xa, xb, g, b):
    d = xa.shape[1]
    n = xa.shape[0] + xb.shape[0]
    spec_a, spec_b, ta = _two_sources(xa.shape[0], xb.shape[0], d)
    vec = pl.BlockSpec((1, d), lambda i: (0, 0))
    return pl.pallas_call(
        functools.partial(_ln_kernel, tiles_a=ta), grid=(n // ROW_TILE,), in_specs=[spec_a, spec_b, vec, vec],
        out_specs=pl.BlockSpec((ROW_TILE, d), lambda i: (i, 0)),
        out_shape=jax.ShapeDtypeStruct((n, d), F32), compiler_params=_params("parallel"),
        name="ln_in")(xa, xb, g.reshape(1, d), b.reshape(1, d))


def _combine_ln_kernel(resid_ref, y0_ref, y1_ref, y2_ref, y3_ref, gate_ref, g_ref, beta_ref, buf_ref, o_ref):
    half = resid_ref.shape[1] // 2
    acc_hi = resid_ref[:, :half]
    acc_lo = resid_ref[:, half:]
    for k, y_ref in enumerate((y0_ref, y1_ref, y2_ref, y3_ref)):
        hi, lo = _unpack_pairs(y_ref[...])
        gk = gate_ref[:, k:k + 1]
        acc_hi = acc_hi + gk * hi
        acc_lo = acc_lo + gk * lo
    o_ref[...] = _layernorm_rows(jnp.concatenate([acc_hi, acc_lo], axis=1), g_ref[...], beta_ref[...])


def _combine_ln(resid, y_k, gates, g, beta, buf, row0):
    n, d = resid.shape
    n_part = y_k.shape[0] // TOP_K
    nt = n_part // ROW_TILE
    t0 = row0 // ROW_TILE
    row = lambda w: pl.BlockSpec((ROW_TILE, w), lambda i: (t0 + i, 0))
    yk = lambda k: pl.BlockSpec((ROW_TILE, d // 2), lambda i: (k * nt + i, 0))
    vec = pl.BlockSpec((1, d), lambda i: (0, 0))
    return pl.pallas_call(
        _combine_ln_kernel, grid=(nt,),
        in_specs=[row(d)] + [yk(k) for k in range(TOP_K)] + [row(LANES), vec, vec, pl.BlockSpec(memory_space=pl.ANY)],
        out_specs=row(d), out_shape=jax.ShapeDtypeStruct((n, d), F32), input_output_aliases={8: 0},
        compiler_params=_params("parallel"),
        name="combine_ln2")(resid, y_k, y_k, y_k, y_k, gates, g.reshape(1, d), beta.reshape(1, d), buf)


def _inproj_kernel(x_ref, w_ref, o_ref):
    o_ref[...] = jnp.dot(x_ref[...].astype(BF16), w_ref[...], preferred_element_type=F32)


def _inproj(x, w, layer, row0, n_rows):
    d = x.shape[1]
    t0 = row0 // ROW_TILE
    return pl.pallas_call(
        _inproj_kernel, grid=(n_rows // ROW_TILE,),
        in_specs=[pl.BlockSpec((ROW_TILE, d), lambda i: (t0 + i, 0)),
                  pl.BlockSpec((None, d, N_PROJ), lambda i: (layer, 0, 0))],
        out_specs=pl.BlockSpec((ROW_TILE, N_PROJ), lambda i: (i, 0)),
        out_shape=jax.ShapeDtypeStruct((n_rows, N_PROJ), F32), compiler_params=_params("parallel"),
        name="in_proj")(x, w)


def _scan_kernel(*refs, NB, L, last, has_state):
    it = iter(refs)
    proj_ref, gbias_ref, lb_ref, nga_ref, ngb_ref = (next(it) for _ in range(5))
    if has_state:
        C0_ref, n0_ref, m0_ref, S0_ref = (next(it) for _ in range(4))
    for _ in range(4):
        next(it)
    h_ref, C_out, n_out, m_out, S_out = (next(it) for _ in range(5))
    C_s, n_s, m_s, St_s = (next(it) for _ in range(4))

    c = pl.program_id(1)
    nc = pl.num_programs(1)
    H = range(N_HEADS)
    N = range(NB)
    R = NB * L
    assert R == LANES

    @pl.when(c == 0)
    def _init():
        if has_state:
            for n in N:
                for h in H:
                    C_s[n, h, 0:DK_A, :] = C0_ref[n, h]
                    C_s[n, h, DK_A:D_HEAD, :] = jnp.zeros((D_HEAD - DK_A, D_HEAD), F32)
                    St_s[n, h] = S0_ref[n, h].T
            n_s[...] = n0_ref[...]
            m_s[...] = m0_ref[...]
        else:
            C_s[...] = jnp.zeros(C_s.shape, F32)
            n_s[...] = jnp.zeros(n_s.shape, F32)
            m_s[...] = jnp.zeros(m_s.shape, F32)
            St_s[...] = jnp.zeros(St_s.shape, F32)

    row = lax.broadcasted_iota(jnp.int32, (R, R), 0)
    col = lax.broadcasted_iota(jnp.int32, (R, R), 1)
    causal = jnp.logical_and(row // L == col // L, row >= col)
    tril = causal.astype(BF16)
    ones_r = jnp.ones((R, LANES), BF16)
    lane = lax.broadcasted_iota(jnp.int32, (R, LANES), 1)
    step = lax.broadcasted_iota(jnp.int32, (R, 1), 0) % L
    valid = step <= last
    padded = last < L - 1
    tile = lambda base, h: slice(base + LANES * h, base + LANES * (h + 1))
    seq = lambda n: slice(n * L, (n + 1) * L)
    cols = lambda base, width: proj_ref[:, :, base:base + width].reshape(R, width)
    per_seq = lambda vals: jnp.concatenate([jnp.broadcast_to(x, (L, x.shape[-1])) for x in vals], axis=0)
    last_rows = lambda x: [x[n * L + last:n * L + last + 1] for n in N]

    i_t = cols(IG0, LANES) + gbias_ref[0:1, :]
    f_t = _log_sigmoid(cols(FG0, LANES) + gbias_ref[1:2, :])
    zf = cols(FB0, MIX)
    lb = lb_ref[...]
    la = jnp.log(jnp.maximum(lb, LB_FLOOR))
    bb = jnp.log1p(-lb) + _log_sigmoid(zf)
    f_log = jnp.maximum(la, bb) + jnp.log(1.0 + jnp.exp(-jnp.abs(la - bb)))
    b_t = _dot01(tril, f_t)
    g = _dot01(tril, f_log)
    m_all = [m_s[n] for n in N]
    C_prev = [[C_s[n, h] for h in H] for n in N]
    n_prev = [[n_s[n, h] for h in H] for n in N]
    St = [[St_s[n, h] for h in H] for n in N]
    qf = [cols(QA0 + LANES * h, LANES) for h in H]
    kf = [cols(KA0 + LANES * h, LANES) for h in H]
    q = [x.astype(BF16) for x in qf]
    k = [x.astype(BF16) for x in kf]
    v = [cols(VA0 + LANES * h, LANES).astype(BF16) for h in H]
    kb = (1.0 - lb) * jax.nn.sigmoid(-zf)
    qr = cols(QB0, MIX)
    qb = qr * jax.nn.sigmoid(qr)
    vb = [cols(IB0 + LANES * h, LANES).astype(BF16) for h in H]
    qb16 = qb.astype(BF16)
    kb16 = kb.astype(BF16)
    r_t = i_t - b_t
    zero16 = jnp.zeros((R, LANES), BF16)
    r_all = sum(
        lax.dot_general(ones_r, jnp.concatenate([jnp.where(lane == h, term, zero16) for h in H], axis=0),
                        NT_DIMS, preferred_element_type=F32)
        for term in _split3(r_t))
    r_m = [r_all[:, R * h:R * (h + 1)] for h in H]
    qk = [lax.dot_general(q[h], k[h], NT_DIMS, preferred_element_type=F32) for h in H]
    qC = [jnp.concatenate([jnp.dot(q[h][seq(n)], C_prev[n][h].astype(BF16), preferred_element_type=F32)
                           for n in N], axis=0) for h in H]
    levels = []
    bs = 1
    while bs < L:
        levels.append(bs)
        bs *= 2
    small = [bs for bs in levels if bs < SUBLANES]
    g_anchor = {}
    if small:
        sel = jnp.concatenate(
            [(col == (row // (2 * bs)) * (2 * bs) + bs).astype(BF16) for bs in small], axis=0)
        picked = _dot01(sel, g)
        for i, bs in enumerate(small):
            g_anchor[bs] = picked[i * R:(i + 1) * R]
    for bs in levels:
        if bs >= SUBLANES:
            g_anchor[bs] = jnp.concatenate(
                [jnp.broadcast_to(g[p0 + bs:p0 + bs + 1], (2 * bs, MIX)) for p0 in range(0, R, 2 * bs)], axis=0)
    qg = (qb * jnp.exp(g)).astype(BF16)
    o_inter = [jnp.concatenate([lax.dot_general(qg[seq(n), tile(0, h)], St[n][h].astype(BF16), NT_DIMS,
                                                preferred_element_type=F32) for n in N], axis=0) for h in H]
    a = [jnp.where(row == col, lax.dot_general(qb16[:, tile(0, h)], kb16[:, tile(0, h)], NT_DIMS,
                                                preferred_element_type=F32), 0.0) for h in H]
    qn = [jnp.sum(qf[h] * per_seq([n_prev[n][h] for n in N]), axis=1, keepdims=True) for h in H]
    b_col = [b_t[:, h:h + 1] for h in H]
    i_col = [i_t[:, h:h + 1] for h in H]
    m_prev = [per_seq([m_all[n][:, h:h + 1] for n in N]) for h in H]
    cm = [jnp.max(jnp.where(causal, r_m[h], -jnp.inf), axis=1, keepdims=True) for h in H]
    m_t = [b_col[h] + jnp.maximum(m_prev[h], cm[h]) for h in H]
    inter = [jnp.exp(b_col[h] + m_prev[h] - m_t[h]) for h in H]
    d = [jnp.exp(jnp.where(causal, (b_col[h] - m_t[h]) + r_m[h], -1e30)) for h in H]
    s = [qk[h] * d[h] for h in H]
    log2e = 1.4426950408889634
    e = [jnp.exp2((g - g_anchor[bs]) * jnp.where((step // bs) % 2 == 1, log2e, -log2e)) for bs in levels]
    qe = [(qb * x).astype(BF16) for x in e]
    ke = [(kb * x).astype(BF16) for x in e]
    sv = [jnp.dot(s[h].astype(BF16), v[h], preferred_element_type=F32) for h in H]
    sc = [[lax.dot_general(qe[i][:, tile(0, h)], ke[i][:, tile(0, h)], NT_DIMS, preferred_element_type=F32)
           for h in H] for i in range(len(levels))]
    den = [jnp.sum(s[h], axis=1, keepdims=True) + inter[h] * qn[h] for h in H]
    m_new = [last_rows(m_t[h]) for h in H]
    b_last = [last_rows(b_col[h]) for h in H]
    w_arg = [per_seq(b_last[h]) - b_col[h] + i_col[h] - per_seq(m_new[h]) for h in H]
    if padded:
        w_arg = [jnp.where(valid, x, -1e30) for x in w_arg]
    kw = [kf[h] * jnp.exp(w_arg[h]) for h in H]
    kw16 = [x.astype(BF16) for x in kw]
    decay = [[jnp.exp(b_last[h][n] + m_all[n][:, h:h + 1] - m_new[h][n]) for h in H] for n in N]
    kv = [[lax.dot_general(kw16[h][seq(n)], v[h][seq(n)], TN_DIMS, preferred_element_type=F32) for h in H]
          for n in N]
    for i, bs in enumerate(levels):
        pair = jnp.logical_and(row // (2 * bs) == col // (2 * bs),
                               jnp.logical_and((row // bs) % 2 == 1, (col // bs) % 2 == 0))
        for h in H:
            a[h] = jnp.where(pair, sc[i][h], a[h])
    o = [o_inter[h] + jnp.dot(a[h].astype(BF16), vb[h], preferred_element_type=F32) for h in H]
    g_last = last_rows(g)
    dec_arg = per_seq(g_last) - g
    if padded:
        dec_arg = jnp.where(valid, dec_arg, -1e30)
    kdec = (kb * jnp.exp(dec_arg)).astype(BF16)
    eg_last = [jnp.exp(x) for x in g_last]
    vk = [[lax.dot_general(vb[h][seq(n)], kdec[seq(n), tile(0, h)], TN_DIMS, preferred_element_type=F32)
           for h in H] for n in N]
    hh = [(sv[h] + inter[h] * qC[h]) / jnp.maximum(jnp.abs(den[h]), jnp.exp(-m_t[h])) for h in H]
    for h in H:
        hn = hh[h] * lax.rsqrt(jnp.mean(hh[h] * hh[h], axis=-1, keepdims=True) + NORM_EPS) * nga_ref[:, tile(0, h)]
        out_a = jax.nn.sigmoid(cols(OA0 + LANES * h, LANES)) * hn
        h_ref[:, :, tile(0, h)] = out_a.reshape(NB, L, LANES)
    for h in H:
        on = o[h] * lax.rsqrt(jnp.mean(o[h] * o[h], axis=-1, keepdims=True) + NORM_EPS) * ngb_ref[:, tile(0, h)]
        gr = cols(GB0 + LANES * h, LANES)
        h_ref[:, :, tile(MIX, h)] = (gr * jax.nn.sigmoid(gr) * on).reshape(NB, L, LANES)
    lane1 = lax.broadcasted_iota(jnp.int32, (1, LANES), 1)
    for n in N:
        for h in H:
            C_s[n, h] = decay[n][h] * C_prev[n][h] + kv[n][h]
            n_s[n, h] = decay[n][h] * n_prev[n][h] + jnp.sum(kw[h][seq(n)], axis=0, keepdims=True)
            St_s[n, h] = eg_last[n][:, tile(0, h)] * St[n][h] + vk[n][h]
        m_upd = m_all[n]
        for h in H:
            m_upd = jnp.where(lane1 == h, m_new[h][n], m_upd)
        m_s[n] = m_upd

    @pl.when(c == nc - 1)
    def _finish():
        for n in N:
            for h in H:
                C_out[n, h] = C_s[n, h, 0:DK_A, :]
                S_out[n, h] = St_s[n, h].T
        n_out[...] = n_s[...]
        m_out[...] = m_s[...]


def _scan(proj, gbias, lb, nga, ngb, state, stacked, *, layer, NB, n_chunks, L, last):
    batch = proj.shape[0]
    has_state = state is not None
    const2 = lambda b, c: (0, 0)
    in_specs = [
        pl.BlockSpec((NB, L, N_PROJ), lambda b, c: (b, c, 0)),
        pl.BlockSpec((2, LANES), const2),
        pl.BlockSpec((1, MIX), const2),
        pl.BlockSpec((1, MIX), const2),
        pl.BlockSpec((1, MIX), const2),
    ]
    args = [proj, gbias, lb, nga, ngb]
    lst4 = lambda b, c: (layer, b, 0, 0, 0)
    lst3 = lambda b, c: (layer, b, 0, 0)
    if has_state:
        in_specs += [
            pl.BlockSpec((None, NB, N_HEADS, DK_A, D_HEAD), lst4),
            pl.BlockSpec((None, NB, N_HEADS, 1, LANES), lst4),
            pl.BlockSpec((None, NB, 1, LANES), lst3),
            pl.BlockSpec((None, NB, N_HEADS, D_HEAD, D_HEAD), lst4),
        ]
        args += list(state)
    aliases = {len(args) + i: 1 + i for i in range(4)}
    in_specs += [pl.BlockSpec(memory_space=pl.ANY)] * 4
    args += list(stacked)
    out_specs = [
        pl.BlockSpec((NB, L, D_MODEL), lambda b, c: (b, c, 0)),
        pl.BlockSpec((None, NB, N_HEADS, DK_A, D_HEAD), lst4),
        pl.BlockSpec((None, NB, N_HEADS, 1, LANES), lst4),
        pl.BlockSpec((None, NB, 1, LANES), lst3),
        pl.BlockSpec((None, NB, N_HEADS, D_HEAD, D_HEAD), lst4),
    ]
    out_shape = [jax.ShapeDtypeStruct((batch, n_chunks * L, D_MODEL), F32)] + [
        jax.ShapeDtypeStruct(a.shape, a.dtype) for a in stacked]
    scratch = [
        pltpu.VMEM((NB, N_HEADS, D_HEAD, D_HEAD), F32),
        pltpu.VMEM((NB, N_HEADS, 1, LANES), F32),
        pltpu.VMEM((NB, 1, LANES), F32),
        pltpu.VMEM((NB, N_HEADS, D_HEAD, D_HEAD), F32),
    ]
    kern = functools.partial(_scan_kernel, NB=NB, L=L, last=last, has_state=has_state)
    res = pl.pallas_call(
        kern, grid=(batch // NB, n_chunks), in_specs=in_specs, out_specs=out_specs, out_shape=out_shape,
        scratch_shapes=scratch, input_output_aliases=aliases,
        compiler_params=_params("parallel", "arbitrary"),
        name="scan_state" if has_state else "scan_prompt")(*args)
    return res[0], tuple(res[1:])


def _stacked_states(batch):
    return (jnp.zeros((DEPTH, batch, N_HEADS, DK_A, D_HEAD), F32),
            jnp.zeros((DEPTH, batch, N_HEADS, 1, LANES), F32),
            jnp.zeros((DEPTH, batch, 1, LANES), F32),
            jnp.zeros((DEPTH, batch, N_HEADS, D_HEAD, D_HEAD), F32))


def _post_mixer_kernel(x_ref, ha_ref, hb_ref, pa_ref, pb_ref, wout_ref, g1_ref, b1_ref, wpg_ref, wpp_ref, wr2_ref,
                       wrh_ref, br_ref, x1_ref, resid_ref, idx_ref, rank_ref, gate_ref, cnt_ref, carry_s, *, tiles_a):
    i = pl.program_id(0)

    @pl.when(i == 0)
    def _init():
        carry_s[...] = jnp.zeros(carry_s.shape, F32)

    from_a = i < tiles_a
    h = jnp.where(from_a, ha_ref[...], hb_ref[...])
    p = jnp.where(from_a, pa_ref[...], pb_ref[...])
    mix = jnp.dot(h.astype(BF16), wout_ref[...], preferred_element_type=F32)
    x1 = _layernorm_rows(DEEPNORM_ALPHA * x_ref[...] + mix, g1_ref[...], b1_ref[...])
    x1b = x1.astype(BF16)
    x1_ref[...] = _pack_pairs(x1)
    x1_lo = (x1 - x1b.astype(F32)).astype(BF16)
    l2 = jnp.dot(x1b, wr2_ref[...], preferred_element_type=F32)
    logits = (l2[:, :LANES] + l2[:, LANES:]
              + jnp.dot(x1_lo, wrh_ref[...], preferred_element_type=F32) + br_ref[...])

    tm = logits.shape[0]
    lane = lax.broadcasted_iota(jnp.int32, (tm, LANES), 1)
    lane_f = lane.astype(F32)
    l = jnp.where(lane < N_EXPERTS, logits, -jnp.inf)
    vals, idxs, onehots = [], [], []
    for _ in range(TOP_K):
        mx = jnp.max(l, axis=1, keepdims=True)
        ix = jnp.min(jnp.where(l == mx, lane_f, float(LANES)), axis=1, keepdims=True)
        sel = lane_f == ix
        vals.append(mx)
        idxs.append(ix)
        onehots.append(sel.astype(F32))
        l = jnp.where(sel, -jnp.inf, l)
    gate = jax.nn.sigmoid(jnp.dot(x1b, wpg_ref[...], preferred_element_type=F32))
    pp = jnp.dot(p.astype(BF16), wpp_ref[...], preferred_element_type=F32)
    resid_ref[...] = DEEPNORM_ALPHA * x1 + gate * pp

    w = [jnp.exp(val - vals[0]) for val in vals]
    tot = w[0] + w[1] + w[2] + w[3]
    oh_all = onehots[0] + onehots[1] + onehots[2] + onehots[3]
    row = lax.broadcasted_iota(jnp.int32, (tm, tm), 0)
    col = lax.broadcasted_iota(jnp.int32, (tm, tm), 1)
    earlier = (row > col).astype(BF16)
    prefix = jnp.dot(earlier, oh_all.astype(BF16), preferred_element_type=F32) + carry_s[...]
    idx_o = jnp.zeros((tm, LANES), F32)
    rank_o = jnp.zeros((tm, LANES), F32)
    gate_o = jnp.zeros((tm, LANES), F32)
    for k in range(TOP_K):
        rank_k = jnp.sum(onehots[k] * prefix, axis=1, keepdims=True)
        idx_o = jnp.where(lane == k, idxs[k], idx_o)
        rank_o = jnp.where(lane == k, rank_k, rank_o)
        gate_o = jnp.where(lane == k, w[k] / tot, gate_o)
    idx_ref[...] = idx_o.astype(jnp.int32)
    rank_ref[...] = rank_o.astype(jnp.int32)
    gate_ref[...] = gate_o
    carry_s[...] = carry_s[...] + jnp.sum(oh_all, axis=0, keepdims=True)
    cnt_ref[...] = carry_s[...].astype(jnp.int32)


def _post_mixer(layer, x, ha, hb, pa, pb, wout, g1, b1, wpg, wpp, wr2, wrh, br):
    n, d = x.shape
    row = lambda w: pl.BlockSpec((ROW_TILE, w), lambda i: (i, 0))
    full = lambda a: pl.BlockSpec(a.shape, lambda i: (0, 0))
    ha_spec, hb_spec, ta = _two_sources(ha.shape[0], hb.shape[0], d)
    pa_spec, pb_spec, _ = _two_sources(pa.shape[1], pb.shape[1], D_PLE, lead=(layer,))
    g1 = g1.reshape(1, d)
    b1 = b1.reshape(1, d)
    return pl.pallas_call(
        functools.partial(_post_mixer_kernel, tiles_a=ta), grid=(n // ROW_TILE,),
        in_specs=[row(d), ha_spec, hb_spec, pa_spec, pb_spec, full(wout), full(g1), full(b1), full(wpg), full(wpp),
                  full(wr2), full(wrh), full(br)],
        out_specs=[row(d // 2), row(d), row(LANES), row(LANES), row(LANES), pl.BlockSpec((1, LANES), lambda i: (0, 0))],
        out_shape=[jax.ShapeDtypeStruct((n, d // 2), jnp.int32), jax.ShapeDtypeStruct((n, d), F32),
                   jax.ShapeDtypeStruct((n, LANES), jnp.int32), jax.ShapeDtypeStruct((n, LANES), jnp.int32),
                   jax.ShapeDtypeStruct((n, LANES), F32), jax.ShapeDtypeStruct((1, LANES), jnp.int32)],
        scratch_shapes=[pltpu.VMEM((1, LANES), F32)],
        compiler_params=_params("arbitrary"), name="post_mixer")(
            x, ha, hb, pa, pb, wout, g1, b1, wpg, wpp, wr2, wrh, br)


def _expert_kernel(blk_ref, e_ref, lo_ref, hi_ref, nv_ref, nxt_ref, slot_ref, x_ref, wg_hbm, bg_ref, wu_hbm, bu_ref,
                   wd_hbm, bd_ref, o_ref, w32_s, wg_s, wu_s, wd_s, sem, *, layer):
    v = pl.program_id(0)
    prev = jnp.maximum(v - 1, 0)
    valid = v < nv_ref[0]
    new_expert = jnp.logical_or(v == 0, e_ref[v] != e_ref[prev])
    new_block = jnp.logical_or(v == 0, blk_ref[v] != blk_ref[prev])
    slot = slot_ref[v]

    def weight_copies(expert, to_slot):
        return [pltpu.make_async_copy(w_hbm.at[layer, expert], w32_s.at[to_slot, j], sem.at[to_slot, j])
                for j, w_hbm in enumerate((wg_hbm, wu_hbm, wd_hbm))]

    @pl.when(v == 0)
    def _prime():
        for cp in weight_copies(e_ref[0], slot):
            cp.start()

    @pl.when(jnp.logical_and(valid, new_expert))
    def _next_weights():
        for cp in weight_copies(e_ref[v], slot):
            cp.wait()

        @pl.when(nxt_ref[v] != e_ref[v])
        def _prefetch():
            for cp in weight_copies(nxt_ref[v], 1 - slot):
                cp.start()

        wg_s[...] = w32_s[slot, 0].astype(BF16)
        wu_s[...] = w32_s[slot, 1].astype(BF16)
        wd_s[...] = w32_s[slot, 2].astype(BF16)

    @pl.when(valid)
    def _compute():
        half = wg_s.shape[0] // 2
        x_hi, x_lo = (t.astype(BF16) for t in _unpack_pairs(x_ref[...]))
        gate = (jnp.dot(x_hi, wg_s[0:half, :], preferred_element_type=F32)
                + jnp.dot(x_lo, wg_s[half:, :], preferred_element_type=F32) + bg_ref[0])
        up = (jnp.dot(x_hi, wu_s[0:half, :], preferred_element_type=F32)
              + jnp.dot(x_lo, wu_s[half:, :], preferred_element_type=F32) + bu_ref[0])
        gate = jnp.minimum(gate, SWIGLU_LIMIT)
        up = jnp.clip(up, -SWIGLU_LIMIT, SWIGLU_LIMIT)
        glu = gate * jax.nn.sigmoid(gate * SWIGLU_ALPHA)
        act = ((up + 1.0) * glu).astype(BF16)
        y = _pack_pairs(jnp.dot(act, wd_s[...], preferred_element_type=F32) + bd_ref[0])
        rows = blk_ref[v] * EXPERT_ROWS + lax.broadcasted_iota(jnp.int32, (EXPERT_ROWS, 1), 0)
        mine = jnp.logical_and(rows >= lo_ref[v], rows < hi_ref[v])

        @pl.when(new_block)
        def _first():
            o_ref[...] = jnp.where(mine, y, 0)

        @pl.when(jnp.logical_not(new_block))
        def _again():
            o_ref[...] = jnp.where(mine, y, o_ref[...])


def _experts(layer, blk_v, e_v, lo_v, hi_v, n_vis, nxt_v, slot_v, xin, wg, bg, wu, bu, wd, bd):
    n_rows = xin.shape[0]
    d = wg.shape[-1]
    n_visits = blk_v.shape[0]
    xspec = pl.BlockSpec((EXPERT_ROWS, d // 2), lambda v, blk, e, *_: (blk[v], 0))
    ospec = pl.BlockSpec((EXPERT_ROWS, d // 2), lambda v, blk, e, *_: (blk[v], 0))
    wspec = pl.BlockSpec(memory_space=pl.ANY)
    bspec = pl.BlockSpec((None, 1, 1, d), lambda v, blk, e, *_: (layer, e[v], 0, 0))
    grid_spec = pltpu.PrefetchScalarGridSpec(
        num_scalar_prefetch=7, grid=(n_visits,),
        in_specs=[xspec, wspec, bspec, wspec, bspec, wspec, bspec], out_specs=ospec,
        scratch_shapes=[pltpu.VMEM((2, 3, d, d), F32)] + [pltpu.VMEM((d, d), BF16)] * 3
        + [pltpu.SemaphoreType.DMA((2, 3))])
    return pl.pallas_call(
        functools.partial(_expert_kernel, layer=layer), grid_spec=grid_spec,
        out_shape=jax.ShapeDtypeStruct((n_rows, d // 2), jnp.int32),
        compiler_params=_params("arbitrary"), name="experts")(
            blk_v, e_v, lo_v, hi_v, n_vis, nxt_v, slot_v, xin, wg, bg, wu, bu, wd, bd)


DISPATCH_ROWS = 48
COLLECT_ROWS = 24
COMBINE_PARTS = 1


def _sc_layout(n_tok, chunk_rows):
    info = plsc.get_sparse_core_info()
    n_workers = info.num_cores * info.num_subcores
    per_worker = n_tok // n_workers
    n_chunks = per_worker // chunk_rows
    assert per_worker * n_workers == n_tok and n_chunks * chunk_rows == per_worker
    return info, n_workers, per_worker, n_chunks


def _sc_index(pos, n_workers, n_chunks, chunk_rows):
    return pos.T.reshape(TOP_K, n_workers, n_chunks, chunk_rows).transpose(1, 2, 0, 3).reshape(
        n_workers, n_chunks * TOP_K, chunk_rows)


def _dispatch_rows(x1, pos):
    n_tok, w = x1.shape
    rows = DISPATCH_ROWS
    info, n_workers, per_worker, n_chunks = _sc_layout(n_tok, rows)
    mesh = plsc.VectorSubcoreMesh(core_axis_name="core", subcore_axis_name="subcore")

    @functools.partial(
        pl.kernel, out_type=jax.ShapeDtypeStruct((n_tok * TOP_K, w), x1.dtype), mesh=mesh,
        scratch_types=[pltpu.VMEM((n_chunks * TOP_K, rows), jnp.int32), pltpu.VMEM((2, rows, w), x1.dtype),
                       pltpu.SemaphoreType.DMA, pltpu.SemaphoreType.DMA, pltpu.SemaphoreType.DMA],
        name="dispatch_rows")
    def scatter_rows(x_hbm, idx_hbm, out_hbm, idx_v, rows_v, load_sem0, load_sem1, scatter_sem):
        worker = lax.axis_index("subcore") * info.num_cores + lax.axis_index("core")
        pltpu.sync_copy(idx_hbm.at[worker], idx_v)
        base = worker * per_worker
        load_sems = (load_sem0, load_sem1)

        def load(c):
            return pltpu.async_copy(x_hbm.at[pl.ds(base + c * rows, rows)], rows_v.at[c % 2], load_sems[c % 2])

        loading = load(0)
        for c in range(n_chunks):
            loading.wait()
            if c + 1 < n_chunks:
                loading = load(c + 1)
            copies = [pltpu.async_copy(rows_v.at[c % 2], out_hbm.at[idx_v.at[c * TOP_K + k]], scatter_sem)
                      for k in range(TOP_K)]
            for cp in copies:
                cp.wait()

    return scatter_rows(x1, _sc_index(pos, n_workers, n_chunks, rows))


def _collect_rows(y_rows, pos):
    n_tok = pos.shape[0]
    w = y_rows.shape[1]
    rows = COLLECT_ROWS
    info, n_workers, per_worker, n_chunks = _sc_layout(n_tok, rows)
    mesh = plsc.VectorSubcoreMesh(core_axis_name="core", subcore_axis_name="subcore")

    @functools.partial(
        pl.kernel, out_type=jax.ShapeDtypeStruct((n_tok * TOP_K, w), y_rows.dtype), mesh=mesh,
        scratch_types=[pltpu.VMEM((n_chunks * TOP_K, rows), jnp.int32), pltpu.VMEM((2, TOP_K, rows, w), y_rows.dtype),
                       pltpu.SemaphoreType.DMA, pltpu.SemaphoreType.DMA],
        name="collect_rows")
    def gather_rows(y_hbm, idx_hbm, out_hbm, idx_v, rows_v, sem0, sem1):
        worker = lax.axis_index("subcore") * info.num_cores + lax.axis_index("core")
        pltpu.sync_copy(idx_hbm.at[worker], idx_v)
        base = worker * per_worker
        sems = (sem0, sem1)

        def gather(c):
            return [pltpu.async_copy(y_hbm.at[idx_v.at[c * TOP_K + k]], rows_v.at[c % 2, k], sems[c % 2])
                    for k in range(TOP_K)]

        pending = gather(0)
        for c in range(n_chunks):
            following = gather(c + 1) if c + 1 < n_chunks else []
            for cp in pending:
                cp.wait()
            for k in range(TOP_K):
                pltpu.sync_copy(rows_v.at[c % 2, k], out_hbm.at[pl.ds(k * n_tok + base + c * rows, rows)])
            pending = following

    return gather_rows(y_rows, _sc_index(pos, n_workers, n_chunks, rows))


def _moe(layer, x1, routing, wg, bg, wu, bu, wd, bd):
    n_tok = x1.shape[0]
    n_assign = n_tok * TOP_K
    n_blocks = n_assign // EXPERT_ROWS
    idx_o, rank_o, gate_o, cnt = routing
    counts = cnt[0, :N_EXPERTS]
    ends = jnp.cumsum(counts)
    offs = ends - counts
    experts = jnp.arange(N_EXPERTS, dtype=jnp.int32)
    top_i = idx_o[:, :TOP_K]
    off_tok = jnp.sum(jnp.where(top_i[:, :, None] == experts, offs, 0), axis=-1)
    pos = (off_tok + rank_o[:, :TOP_K]).reshape(n_assign)
    pos = pos.reshape(n_tok, TOP_K)
    xin = _dispatch_rows(x1, pos)

    first_blk = offs // EXPERT_ROWS
    last_blk = (ends - 1) // EXPERT_ROWS
    nvis = jnp.where(counts > 0, last_blk - first_blk + 1, 0)
    vend = jnp.cumsum(nvis)
    vstart = vend - nvis
    n_visits = n_blocks + N_EXPERTS - 1
    v = jnp.arange(n_visits, dtype=jnp.int32)
    e_v = jnp.sum((vend[None, :] <= v[:, None]).astype(jnp.int32), axis=1)
    e_last = jnp.max(jnp.where(counts > 0, experts, 0))
    e_v = jnp.minimum(e_v, e_last)
    pick = lambda tab: jnp.sum(jnp.where(e_v[:, None] == experts, tab, 0), axis=-1)
    blk_v = jnp.minimum(pick(first_blk) + v - pick(vstart), n_blocks - 1)
    active = counts > 0
    later = jnp.where(jnp.logical_and(active[None, :], experts[None, :] > experts[:, None]), experts[None, :], N_EXPERTS)
    nxt_e = jnp.min(later, axis=1)
    nxt_e = jnp.where(nxt_e == N_EXPERTS, experts, nxt_e)
    order_e = jnp.cumsum(active.astype(jnp.int32)) - 1
    y_rows = _experts(layer, blk_v.astype(jnp.int32), e_v.astype(jnp.int32), pick(offs).astype(jnp.int32),
                      pick(ends).astype(jnp.int32), vend[-1:].astype(jnp.int32),
                      pick(nxt_e).astype(jnp.int32), (pick(order_e) % 2).astype(jnp.int32),
                      xin, wg, bg, wu, bu, wd, bd)
    return y_rows, pos, gate_o


def _pack_w_in(w_in):
    qa, ka, va, oa, ia, fa, qb, fb, ib, gb = jnp.split(w_in, SPLIT_POINTS, axis=-1)

    def pad_heads(w):
        w = w.reshape(w.shape[:-1] + (N_HEADS, DK_A))
        w = jnp.pad(w, ((0, 0),) * (w.ndim - 1) + ((0, D_HEAD - DK_A),))
        return w.reshape(w.shape[:-2] + (N_HEADS * D_HEAD,))

    def pad_gate(w):
        return jnp.pad(w, ((0, 0),) * (w.ndim - 1) + ((0, LANES - N_HEADS),))

    cols = [pad_heads(qa), pad_heads(ka * (DK_A ** -0.5)), va, oa, qb, fb, ib, gb, pad_gate(ia), pad_gate(fa)]
    return jnp.concatenate(cols, axis=-1).astype(BF16)


def kernel(x_prompt, x_sample, state_mlstm_C, state_mlstm_n, state_mlstm_m, state_hgrn_S, p_prompt, p_sample,
           ln_in_g, ln_in_b, w_in, mlstm_ig_bias, mlstm_fg_bias, mlstm_norm_g, hgrn_lb_logits, hgrn_norm_g,
           w_out, ln1_g, ln1_b, w_router, b_router, w_gate, b_gate, w_up, b_up, w_down, b_down,
           w_ple_gate, w_ple_proj, ln2_g, ln2_b):
    bp, tp, d = x_prompt.shape
    bs, ts, _ = x_sample.shape
    n_p = bp * tp
    n_s = bs * ts
    ts_pad = SUBLANES
    assert tp % CHUNK == 0 and ts <= ts_pad and ts % CHUNK != 0

    lb_soft = jax.nn.softmax(hgrn_lb_logits.astype(F32), axis=0)
    lower_bounds = jnp.cumsum(lb_soft, axis=0) - lb_soft[0]

    w_in_p = _pack_w_in(w_in)
    gbias = jnp.stack([jnp.pad(mlstm_ig_bias, ((0, 0), (0, LANES - N_HEADS))),
                       jnp.pad(mlstm_fg_bias, ((0, 0), (0, LANES - N_HEADS)))], axis=1)
    w_out_b = w_out.astype(BF16)
    w_pg_b = w_ple_gate.astype(BF16)
    w_pp_b = w_ple_proj.astype(BF16)
    w_r = jnp.pad(w_router, ((0, 0), (0, 0), (0, LANES - N_EXPERTS)))
    w_r_hi = w_r.astype(BF16)
    w_r2 = jnp.concatenate([w_r_hi, (w_r - w_r_hi.astype(F32)).astype(BF16)], axis=-1)
    b_r = jnp.pad(b_router, ((0, 0), (0, LANES - N_EXPERTS))).reshape(DEPTH, 1, LANES)
    n0_pad = jnp.pad(state_mlstm_n, ((0, 0), (0, 0), (0, 0), (0, D_HEAD - DK_A))).reshape(
        DEPTH, bs, N_HEADS, 1, D_HEAD)
    m0_pad = jnp.pad(state_mlstm_m, ((0, 0), (0, 0), (0, LANES - N_HEADS))).reshape(DEPTH, bs, 1, LANES)

    p_p = p_prompt.reshape(DEPTH, n_p, D_PLE)
    p_s = p_sample.reshape(DEPTH, n_s, D_PLE)
    x = _ln(x_prompt.reshape(n_p, d), x_sample.reshape(n_s, d), ln_in_g, ln_in_b)

    assert n_p % ROW_TILE == 0 and n_s % ROW_TILE == 0
    nb_p = LANES // CHUNK
    nb_s = LANES // ts_pad
    st_p = _stacked_states(bp)
    st_s = _stacked_states(bs)
    for l in range(DEPTH):
        lb = lower_bounds[l].reshape(1, MIX)
        nga = mlstm_norm_g[l].reshape(1, MIX)
        ngb = hgrn_norm_g[l].reshape(1, MIX)
        proj_p = _inproj(x, w_in_p, l, 0, n_p).reshape(bp, tp, N_PROJ)
        h_p, st_p = _scan(proj_p, gbias[l], lb, nga, ngb, None, st_p, layer=l,
                          NB=nb_p, n_chunks=tp // CHUNK, L=CHUNK, last=CHUNK - 1)
        proj_s = jnp.pad(_inproj(x, w_in_p, l, n_p, n_s).reshape(bs, ts, N_PROJ), ((0, 0), (0, ts_pad - ts), (0, 0)))
        h_s, st_s = _scan(proj_s, gbias[l], lb, nga, ngb,
                          (state_mlstm_C, n0_pad, m0_pad, state_hgrn_S), st_s, layer=l,
                          NB=nb_s, n_chunks=1, L=ts_pad, last=ts - 1)
        x1, resid, *routing = _post_mixer(l, x, h_p.reshape(n_p, d), h_s[:, :ts].reshape(n_s, d), p_p, p_s,
                                          w_out_b[l], ln1_g[l], ln1_b[l], w_pg_b[l], w_pp_b[l],
                                          w_r2[l], w_r_hi[l], b_r[l])
        y_rows, pos, gates = _moe(l, x1, routing, w_gate, b_gate.reshape(DEPTH, N_EXPERTS, 1, d),
                                  w_up, b_up.reshape(DEPTH, N_EXPERTS, 1, d),
                                  w_down, b_down.reshape(DEPTH, N_EXPERTS, 1, d))
        n_part = (n_p + n_s) // COMBINE_PARTS
        assert n_part * COMBINE_PARTS == n_p + n_s and n_part % ROW_TILE == 0
        for part in range(COMBINE_PARTS):
            y_k = _collect_rows(y_rows, pos[part * n_part:(part + 1) * n_part])
            x = _combine_ln(resid, y_k, gates, ln2_g[l], ln2_b[l], x, part * n_part)

    y_prompt = x[:n_p].reshape(bp, tp, d)
    y_sample = x[n_p:].reshape(bs, ts, d)
    unpad = lambda st: (st[0], st[1][:, :, :, 0, :DK_A], st[2][:, :, 0, :N_HEADS], st[3])
    return (y_prompt, y_sample) + unpad(st_p) + unpad(st_s)
```

```python
import functools

import jax
import jax.numpy as jnp
import numpy as np
from jax import lax
from jax.experimental import pallas as pl
from jax.experimental.pallas import tpu as pltpu
from jax.experimental.pallas import tpu_sc as plsc

F32 = jnp.float32
BF16 = jnp.bfloat16

D_MODEL = 1024
DEPTH = 4
D_PLE = 256
N_HEADS = 4
DK_A = 64
D_HEAD = 128
MIX = 512
N_EXPERTS = 32
TOP_K = 4
SWIGLU_LIMIT = 7.0
SWIGLU_ALPHA = 1.702
CHUNK = 64
LN_EPS = 1e-5
NORM_EPS = 1e-6
LB_FLOOR = 1e-20
DEEPNORM_ALPHA = (2 * DEPTH) ** 0.25
SPLIT_SIZES = (256, 256, 512, 512, 4, 4, 512, 512, 512, 512)
SPLIT_POINTS = tuple(int(s) for s in np.cumsum(SPLIT_SIZES)[:-1])

LANES = 128
SUBLANES = 8
VMEM_LIMIT = 56 * 1024 * 1024

QA0, KA0, VA0, OA0 = 0, 512, 1024, 1536
QB0, FB0, IB0, GB0 = 2048, 2560, 3072, 3584
IG0, FG0 = 4096, 4224
N_PROJ = 4352

ROW_TILE = 512
EXPERT_ROWS = 256

NT_DIMS = (((1,), (1,)), ((), ()))
TN_DIMS = (((0,), (0,)), ((), ()))


def _params(*sem):
    return pltpu.CompilerParams(dimension_semantics=sem, vmem_limit_bytes=VMEM_LIMIT)


def _log_sigmoid(x):
    return jnp.minimum(x, 0.0) - jnp.log(1.0 + jnp.exp(-jnp.abs(x)))


def _split3(x):
    x1 = x.astype(BF16)
    r1 = x - x1.astype(F32)
    x2 = r1.astype(BF16)
    x3 = (r1 - x2.astype(F32)).astype(BF16)
    return x1, x2, x3


def _dot01(sel, x):
    return sum(jnp.dot(sel, t, preferred_element_type=F32) for t in _split3(x))


def _pack_pairs(x):
    w = x.shape[1] // 2
    bits = lax.bitcast_convert_type(x.astype(BF16).astype(F32), jnp.uint32)
    return lax.bitcast_convert_type(bits[:, :w] | (bits[:, w:] >> 16), jnp.int32)


def _unpack_pairs(words):
    u = lax.bitcast_convert_type(words, jnp.uint32)
    hi = lax.bitcast_convert_type(u & jnp.uint32(0xFFFF0000), F32)
    lo = lax.bitcast_convert_type(u << 16, F32)
    return hi, lo


def _layernorm_rows(x, g, b):
    mu = jnp.mean(x, axis=-1, keepdims=True)
    xc = x - mu
    var = jnp.mean(xc * xc, axis=-1, keepdims=True)
    return xc * lax.rsqrt(var + LN_EPS) * g + b


def _two_sources(n_a, n_b, width, lead=()):
    ta = n_a // ROW_TILE
    assert n_a % ROW_TILE == 0 and n_b % ROW_TILE == 0
    block = (None,) * len(lead) + (ROW_TILE, width)
    spec_a = pl.BlockSpec(block, lambda i: lead + (jnp.minimum(i, ta - 1), 0))
    spec_b = pl.BlockSpec(block, lambda i: lead + (jnp.maximum(i - ta, 0), 0))
    return spec_a, spec_b, ta


def _ln_kernel(xa_ref, xb_ref, g_ref, b_ref, o_ref, *, tiles_a):
    x = jnp.where(pl.program_id(0) < tiles_a, xa_ref[...], xb_ref[...])
    o_ref[...] = _layernorm_rows(x, g_ref[...], b_ref[...])


def _ln(xa, xb, g, b):
    d = xa.shape[1]
    n = xa.shape[0] + xb.shape[0]
    spec_a, spec_b, ta = _two_sources(xa.shape[0], xb.shape[0], d)
    vec = pl.BlockSpec((1, d), lambda i: (0, 0))
    return pl.pallas_call(
        functools.partial(_ln_kernel, tiles_a=ta), grid=(n // ROW_TILE,), in_specs=[spec_a, spec_b, vec, vec],
        out_specs=pl.BlockSpec((ROW_TILE, d), lambda i: (i, 0)),
        out_shape=jax.ShapeDtypeStruct((n, d), F32), compiler_params=_params("parallel"),
        name="ln_in")(xa, xb, g.reshape(1, d), b.reshape(1, d))


def _combine_ln_kernel(resid_ref, y0_ref, y1_ref, y2_ref, y3_ref, gate_ref, g_ref, beta_ref, buf_ref, o_ref):
    half = resid_ref.shape[1] // 2
    acc_hi = resid_ref[:, :half]
    acc_lo = resid_ref[:, half:]
    for k, y_ref in enumerate((y0_ref, y1_ref, y2_ref, y3_ref)):
        hi, lo = _unpack_pairs(y_ref[...])
        gk = gate_ref[:, k:k + 1]
        acc_hi = acc_hi + gk * hi
        acc_lo = acc_lo + gk * lo
    o_ref[...] = _layernorm_rows(jnp.concatenate([acc_hi, acc_lo], axis=1), g_ref[...], beta_ref[...])


def _combine_ln(resid, y_k, gates, g, beta, buf, row0):
    n, d = resid.shape
    n_part = y_k.shape[0] // TOP_K
    nt = n_part // ROW_TILE
    t0 = row0 // ROW_TILE
    row = lambda w: pl.BlockSpec((ROW_TILE, w), lambda i: (t0 + i, 0))
    yk = lambda k: pl.BlockSpec((ROW_TILE, d // 2), lambda i: (k * nt + i, 0))
    vec = pl.BlockSpec((1, d), lambda i: (0, 0))
    return pl.pallas_call(
        _combine_ln_kernel, grid=(nt,),
        in_specs=[row(d)] + [yk(k) for k in range(TOP_K)] + [row(LANES), vec, vec, pl.BlockSpec(memory_space=pl.ANY)],
        out_specs=row(d), out_shape=jax.ShapeDtypeStruct((n, d), F32), input_output_aliases={8: 0},
        compiler_params=_params("parallel"),
        name="combine_ln2")(resid, y_k, y_k, y_k, y_k, gates, g.reshape(1, d), beta.reshape(1, d), buf)


def _inproj_kernel(x_ref, w_ref, o_ref):
    o_ref[...] = jnp.dot(x_ref[...].astype(BF16), w_ref[...], preferred_element_type=F32)


def _inproj(x, w, layer, row0, n_rows):
    d = x.shape[1]
    t0 = row0 // ROW_TILE
    return pl.pallas_call(
        _inproj_kernel, grid=(n_rows // ROW_TILE,),
        in_specs=[pl.BlockSpec((ROW_TILE, d), lambda i: (t0 + i, 0)),
                  pl.BlockSpec((None, d, N_PROJ), lambda i: (layer, 0, 0))],
        out_specs=pl.BlockSpec((ROW_TILE, N_PROJ), lambda i: (i, 0)),
        out_shape=jax.ShapeDtypeStruct((n_rows, N_PROJ), F32), compiler_params=_params("parallel"),
        name="in_proj")(x, w)


def _scan_kernel(*refs, NB, L, last, has_state):
    it = iter(refs)
    proj_ref, gbias_ref, lb_ref, nga_ref, ngb_ref = (next(it) for _ in range(5))
    if has_state:
        C0_ref, n0_ref, m0_ref, S0_ref = (next(it) for _ in range(4))
    for _ in range(4):
        next(it)
    h_ref, C_out, n_out, m_out, S_out = (next(it) for _ in range(5))
    C_s, n_s, m_s, St_s = (next(it) for _ in range(4))

    c = pl.program_id(1)
    nc = pl.num_programs(1)
    H = range(N_HEADS)
    N = range(NB)
    R = NB * L
    assert R == LANES

    @pl.when(c == 0)
    def _init():
        if has_state:
            for n in N:
                for h in H:
                    C_s[n, h, 0:DK_A, :] = C0_ref[n, h]
                    C_s[n, h, DK_A:D_HEAD, :] = jnp.zeros((D_HEAD - DK_A, D_HEAD), F32)
                    St_s[n, h] = S0_ref[n, h].T
            n_s[...] = n0_ref[...]
            m_s[...] = m0_ref[...]
        else:
            C_s[...] = jnp.zeros(C_s.shape, F32)
            n_s[...] = jnp.zeros(n_s.shape, F32)
            m_s[...] = jnp.zeros(m_s.shape, F32)
            St_s[...] = jnp.zeros(St_s.shape, F32)

    row = lax.broadcasted_iota(jnp.int32, (R, R), 0)
    col = lax.broadcasted_iota(jnp.int32, (R, R), 1)
    causal = jnp.logical_and(row // L == col // L, row >= col)
    tril = causal.astype(BF16)
    ones_r = jnp.ones((R, LANES), BF16)
    lane = lax.broadcasted_iota(jnp.int32, (R, LANES), 1)
    step = lax.broadcasted_iota(jnp.int32, (R, 1), 0) % L
    valid = step <= last
    padded = last < L - 1
    tile = lambda base, h: slice(base + LANES * h, base + LANES * (h + 1))
    seq = lambda n: slice(n * L, (n + 1) * L)
    cols = lambda base, width: proj_ref[:, :, base:base + width].reshape(R, width)
    per_seq = lambda vals: jnp.concatenate([jnp.broadcast_to(x, (L, x.shape[-1])) for x in vals], axis=0)
    last_rows = lambda x: [x[n * L + last:n * L + last + 1] for n in N]

    i_t = cols(IG0, LANES) + gbias_ref[0:1, :]
    f_t = _log_sigmoid(cols(FG0, LANES) + gbias_ref[1:2, :])
    zf = cols(FB0, MIX)
    lb = lb_ref[...]
    la = jnp.log(jnp.maximum(lb, LB_FLOOR))
    bb = jnp.log1p(-lb) + _log_sigmoid(zf)
    f_log = jnp.maximum(la, bb) + jnp.log(1.0 + jnp.exp(-jnp.abs(la - bb)))
    b_t = _dot01(tril, f_t)
    g = _dot01(tril, f_log)
    m_all = [m_s[n] for n in N]
    C_prev = [[C_s[n, h] for h in H] for n in N]
    n_prev = [[n_s[n, h] for h in H] for n in N]
    St = [[St_s[n, h] for h in H] for n in N]
    qf = [cols(QA0 + LANES * h, LANES) for h in H]
    kf = [cols(KA0 + LANES * h, LANES) for h in H]
    q = [x.astype(BF16) for x in qf]
    k = [x.astype(BF16) for x in kf]
    v = [cols(VA0 + LANES * h, LANES).astype(BF16) for h in H]
    kb = (1.0 - lb) * jax.nn.sigmoid(-zf)
    qr = cols(QB0, MIX)
    qb = qr * jax.nn.sigmoid(qr)
    vb = [cols(IB0 + LANES * h, LANES).astype(BF16) for h in H]
    qb16 = qb.astype(BF16)
    kb16 = kb.astype(BF16)
    r_t = i_t - b_t
    zero16 = jnp.zeros((R, LANES), BF16)
    r_all = sum(
        lax.dot_general(ones_r, jnp.concatenate([jnp.where(lane == h, term, zero16) for h in H], axis=0),
                        NT_DIMS, preferred_element_type=F32)
        for term in _split3(r_t))
    r_m = [r_all[:, R * h:R * (h + 1)] for h in H]
    qk = [lax.dot_general(q[h], k[h], NT_DIMS, preferred_element_type=F32) for h in H]
    qC = [jnp.concatenate([jnp.dot(q[h][seq(n)], C_prev[n][h].astype(BF16), preferred_element_type=F32)
                           for n in N], axis=0) for h in H]
    levels = []
    bs = 1
    while bs < L:
        levels.append(bs)
        bs *= 2
    small = [bs for bs in levels if bs < SUBLANES]
    g_anchor = {}
    if small:
        sel = jnp.concatenate(
            [(col == (row // (2 * bs)) * (2 * bs) + bs).astype(BF16) for bs in small], axis=0)
        picked = _dot01(sel, g)
        for i, bs in enumerate(small):
            g_anchor[bs] = picked[i * R:(i + 1) * R]
    for bs in levels:
        if bs >= SUBLANES:
            g_anchor[bs] = jnp.concatenate(
                [jnp.broadcast_to(g[p0 + bs:p0 + bs + 1], (2 * bs, MIX)) for p0 in range(0, R, 2 * bs)], axis=0)
    qg = (qb * jnp.exp(g)).astype(BF16)
    o_inter = [jnp.concatenate([lax.dot_general(qg[seq(n), tile(0, h)], St[n][h].astype(BF16), NT_DIMS,
                                                preferred_element_type=F32) for n in N], axis=0) for h in H]
    a = [jnp.where(row == col, lax.dot_general(qb16[:, tile(0, h)], kb16[:, tile(0, h)], NT_DIMS,
                                                preferred_element_type=F32), 0.0) for h in H]
    qn = [jnp.sum(qf[h] * per_seq([n_prev[n][h] for n in N]), axis=1, keepdims=True) for h in H]
    b_col = [b_t[:, h:h + 1] for h in H]
    i_col = [i_t[:, h:h + 1] for h in H]
    m_prev = [per_seq([m_all[n][:, h:h + 1] for n in N]) for h in H]
    cm = [jnp.max(jnp.where(causal, r_m[h], -jnp.inf), axis=1, keepdims=True) for h in H]
    m_t = [b_col[h] + jnp.maximum(m_prev[h], cm[h]) for h in H]
    inter = [jnp.exp(b_col[h] + m_prev[h] - m_t[h]) for h in H]
    d = [jnp.exp(jnp.where(causal, (b_col[h] - m_t[h]) + r_m[h], -1e30)) for h in H]
    s = [qk[h] * d[h] for h in H]
    log2e = 1.4426950408889634
    e = [jnp.exp2((g - g_anchor[bs]) * jnp.where((step // bs) % 2 == 1, log2e, -log2e)) for bs in levels]
    qe = [(qb * x).astype(BF16) for x in e]
    ke = [(kb * x).astype(BF16) for x in e]
    sv = [jnp.dot(s[h].astype(BF16), v[h], preferred_element_type=F32) for h in H]
    sc = [[lax.dot_general(qe[i][:, tile(0, h)], ke[i][:, tile(0, h)], NT_DIMS, preferred_element_type=F32)
           for h in H] for i in range(len(levels))]
    den = [jnp.sum(s[h], axis=1, keepdims=True) + inter[h] * qn[h] for h in H]
    m_new = [last_rows(m_t[h]) for h in H]
    b_last = [last_rows(b_col[h]) for h in H]
    w_arg = [per_seq(b_last[h]) - b_col[h] + i_col[h] - per_seq(m_new[h]) for h in H]
    if padded:
        w_arg = [jnp.where(valid, x, -1e30) for x in w_arg]
    kw = [kf[h] * jnp.exp(w_arg[h]) for h in H]
    kw16 = [x.astype(BF16) for x in kw]
    decay = [[jnp.exp(b_last[h][n] + m_all[n][:, h:h + 1] - m_new[h][n]) for h in H] for n in N]
    kv = [[lax.dot_general(kw16[h][seq(n)], v[h][seq(n)], TN_DIMS, preferred_element_type=F32) for h in H]
          for n in N]
    for i, bs in enumerate(levels):
        pair = jnp.logical_and(row // (2 * bs) == col // (2 * bs),
                               jnp.logical_and((row // bs) % 2 == 1, (col // bs) % 2 == 0))
        for h in H:
            a[h] = jnp.where(pair, sc[i][h], a[h])
    o = [o_inter[h] + jnp.dot(a[h].astype(BF16), vb[h], preferred_element_type=F32) for h in H]
    g_last = last_rows(g)
    dec_arg = per_seq(g_last) - g
    if padded:
        dec_arg = jnp.where(valid, dec_arg, -1e30)
    kdec = (kb * jnp.exp(dec_arg)).astype(BF16)
    eg_last = [jnp.exp(x) for x in g_last]
    vk = [[lax.dot_general(vb[h][seq(n)], kdec[seq(n), tile(0, h)], TN_DIMS, preferred_element_type=F32)
           for h in H] for n in N]
    hh = [(sv[h] + inter[h] * qC[h]) / jnp.maximum(jnp.abs(den[h]), jnp.exp(-m_t[h])) for h in H]
    for h in H:
        hn = hh[h] * lax.rsqrt(jnp.mean(hh[h] * hh[h], axis=-1, keepdims=True) + NORM_EPS) * nga_ref[:, tile(0, h)]
        out_a = jax.nn.sigmoid(cols(OA0 + LANES * h, LANES)) * hn
        h_ref[:, :, tile(0, h)] = out_a.reshape(NB, L, LANES)
    for h in H:
        on = o[h] * lax.rsqrt(jnp.mean(o[h] * o[h], axis=-1, keepdims=True) + NORM_EPS) * ngb_ref[:, tile(0, h)]
        gr = cols(GB0 + LANES * h, LANES)
        h_ref[:, :, tile(MIX, h)] = (gr * jax.nn.sigmoid(gr) * on).reshape(NB, L, LANES)
    lane1 = lax.broadcasted_iota(jnp.int32, (1, LANES), 1)
    for n in N:
        for h in H:
            C_s[n, h] = decay[n][h] * C_prev[n][h] + kv[n][h]
            n_s[n, h] = decay[n][h] * n_prev[n][h] + jnp.sum(kw[h][seq(n)], axis=0, keepdims=True)
            St_s[n, h] = eg_last[n][:, tile(0, h)] * St[n][h] + vk[n][h]
        m_upd = m_all[n]
        for h in H:
            m_upd = jnp.where(lane1 == h, m_new[h][n], m_upd)
        m_s[n] = m_upd

    @pl.when(c == nc - 1)
    def _finish():
        for n in N:
            for h in H:
                C_out[n, h] = C_s[n, h, 0:DK_A, :]
                S_out[n, h] = St_s[n, h].T
        n_out[...] = n_s[...]
        m_out[...] = m_s[...]


def _scan(proj, gbias, lb, nga, ngb, state, stacked, *, layer, NB, n_chunks, L, last):
    batch = proj.shape[0]
    has_state = state is not None
    const2 = lambda b, c: (0, 0)
    in_specs = [
        pl.BlockSpec((NB, L, N_PROJ), lambda b, c: (b, c, 0)),
        pl.BlockSpec((2, LANES), const2),
        pl.BlockSpec((1, MIX), const2),
        pl.BlockSpec((1, MIX), const2),
        pl.BlockSpec((1, MIX), const2),
    ]
    args = [proj, gbias, lb, nga, ngb]
    lst4 = lambda b, c: (layer, b, 0, 0, 0)
    lst3 = lambda b, c: (layer, b, 0, 0)
    if has_state:
        in_specs += [
            pl.BlockSpec((None, NB, N_HEADS, DK_A, D_HEAD), lst4),
            pl.BlockSpec((None, NB, N_HEADS, 1, LANES), lst4),
            pl.BlockSpec((None, NB, 1, LANES), lst3),
            pl.BlockSpec((None, NB, N_HEADS, D_HEAD, D_HEAD), lst4),
        ]
        args += list(state)
    aliases = {len(args) + i: 1 + i for i in range(4)}
    in_specs += [pl.BlockSpec(memory_space=pl.ANY)] * 4
    args += list(stacked)
    out_specs = [
        pl.BlockSpec((NB, L, D_MODEL), lambda b, c: (b, c, 0)),
        pl.BlockSpec((None, NB, N_HEADS, DK_A, D_HEAD), lst4),
        pl.BlockSpec((None, NB, N_HEADS, 1, LANES), lst4),
        pl.BlockSpec((None, NB, 1, LANES), lst3),
        pl.BlockSpec((None, NB, N_HEADS, D_HEAD, D_HEAD), lst4),
    ]
    out_shape = [jax.ShapeDtypeStruct((batch, n_chunks * L, D_MODEL), F32)] + [
        jax.ShapeDtypeStruct(a.shape, a.dtype) for a in stacked]
    scratch = [
        pltpu.VMEM((NB, N_HEADS, D_HEAD, D_HEAD), F32),
        pltpu.VMEM((NB, N_HEADS, 1, LANES), F32),
        pltpu.VMEM((NB, 1, LANES), F32),
        pltpu.VMEM((NB, N_HEADS, D_HEAD, D_HEAD), F32),
    ]
    kern = functools.partial(_scan_kernel, NB=NB, L=L, last=last, has_state=has_state)
    res = pl.pallas_call(
        kern, grid=(batch // NB, n_chunks), in_specs=in_specs, out_specs=out_specs, out_shape=out_shape,
        scratch_shapes=scratch, input_output_aliases=aliases,
        compiler_params=_params("parallel", "arbitrary"),
        name="scan_state" if has_state else "scan_prompt")(*args)
    return res[0], tuple(res[1:])


def _stacked_states(batch):
    return (jnp.zeros((DEPTH, batch, N_HEADS, DK_A, D_HEAD), F32),
            jnp.zeros((DEPTH, batch, N_HEADS, 1, LANES), F32),
            jnp.zeros((DEPTH, batch, 1, LANES), F32),
            jnp.zeros((DEPTH, batch, N_HEADS, D_HEAD, D_HEAD), F32))


def _post_mixer_kernel(x_ref, ha_ref, hb_ref, pa_ref, pb_ref, wout_ref, g1_ref, b1_ref, wpg_ref, wpp_ref, wr2_ref,
                       wrh_ref, br_ref, x1_ref, resid_ref, idx_ref, rank_ref, gate_ref, cnt_ref, carry_s, *, tiles_a):
    i = pl.program_id(0)

    @pl.when(i == 0)
    def _init():
        carry_s[...] = jnp.zeros(carry_s.shape, F32)

    from_a = i < tiles_a
    h = jnp.where(from_a, ha_ref[...], hb_ref[...])
    p = jnp.where(from_a, pa_ref[...], pb_ref[...])
    mix = jnp.dot(h.astype(BF16), wout_ref[...], preferred_element_type=F32)
    x1 = _layernorm_rows(DEEPNORM_ALPHA * x_ref[...] + mix, g1_ref[...], b1_ref[...])
    x1b = x1.astype(BF16)
    x1_ref[...] = _pack_pairs(x1)
    x1_lo = (x1 - x1b.astype(F32)).astype(BF16)
    l2 = jnp.dot(x1b, wr2_ref[...], preferred_element_type=F32)
    logits = (l2[:, :LANES] + l2[:, LANES:]
              + jnp.dot(x1_lo, wrh_ref[...], preferred_element_type=F32) + br_ref[...])

    tm = logits.shape[0]
    lane = lax.broadcasted_iota(jnp.int32, (tm, LANES), 1)
    lane_f = lane.astype(F32)
    l = jnp.where(lane < N_EXPERTS, logits, -jnp.inf)
    vals, idxs, onehots = [], [], []
    for _ in range(TOP_K):
        mx = jnp.max(l, axis=1, keepdims=True)
        ix = jnp.min(jnp.where(l == mx, lane_f, float(LANES)), axis=1, keepdims=True)
        sel = lane_f == ix
        vals.append(mx)
        idxs.append(ix)
        onehots.append(sel.astype(F32))
        l = jnp.where(sel, -jnp.inf, l)
    gate = jax.nn.sigmoid(jnp.dot(x1b, wpg_ref[...], preferred_element_type=F32))
    pp = jnp.dot(p.astype(BF16), wpp_ref[...], preferred_element_type=F32)
    resid_ref[...] = DEEPNORM_ALPHA * x1 + gate * pp

    w = [jnp.exp(val - vals[0]) for val in vals]
    tot = w[0] + w[1] + w[2] + w[3]
    oh_all = onehots[0] + onehots[1] + onehots[2] + onehots[3]
    row = lax.broadcasted_iota(jnp.int32, (tm, tm), 0)
    col = lax.broadcasted_iota(jnp.int32, (tm, tm), 1)
    earlier = (row > col).astype(BF16)
    prefix = jnp.dot(earlier, oh_all.astype(BF16), preferred_element_type=F32) + carry_s[...]
    idx_o = jnp.zeros((tm, LANES), F32)
    rank_o = jnp.zeros((tm, LANES), F32)
    gate_o = jnp.zeros((tm, LANES), F32)
    for k in range(TOP_K):
        rank_k = jnp.sum(onehots[k] * prefix, axis=1, keepdims=True)
        idx_o = jnp.where(lane == k, idxs[k], idx_o)
        rank_o = jnp.where(lane == k, rank_k, rank_o)
        gate_o = jnp.where(lane == k, w[k] / tot, gate_o)
    idx_ref[...] = idx_o.astype(jnp.int32)
    rank_ref[...] = rank_o.astype(jnp.int32)
    gate_ref[...] = gate_o
    carry_s[...] = carry_s[...] + jnp.sum(oh_all, axis=0, keepdims=True)
    cnt_ref[...] = carry_s[...].astype(jnp.int32)


def _post_mixer(layer, x, ha, hb, pa, pb, wout, g1, b1, wpg, wpp, wr2, wrh, br):
    n, d = x.shape
    row = lambda w: pl.BlockSpec((ROW_TILE, w), lambda i: (i, 0))
    full = lambda a: pl.BlockSpec(a.shape, lambda i: (0, 0))
    ha_spec, hb_spec, ta = _two_sources(ha.shape[0], hb.shape[0], d)
    pa_spec, pb_spec, _ = _two_sources(pa.shape[1], pb.shape[1], D_PLE, lead=(layer,))
    g1 = g1.reshape(1, d)
    b1 = b1.reshape(1, d)
    return pl.pallas_call(
        functools.partial(_post_mixer_kernel, tiles_a=ta), grid=(n // ROW_TILE,),
        in_specs=[row(d), ha_spec, hb_spec, pa_spec, pb_spec, full(wout), full(g1), full(b1), full(wpg), full(wpp),
                  full(wr2), full(wrh), full(br)],
        out_specs=[row(d // 2), row(d), row(LANES), row(LANES), row(LANES), pl.BlockSpec((1, LANES), lambda i: (0, 0))],
        out_shape=[jax.ShapeDtypeStruct((n, d // 2), jnp.int32), jax.ShapeDtypeStruct((n, d), F32),
                   jax.ShapeDtypeStruct((n, LANES), jnp.int32), jax.ShapeDtypeStruct((n, LANES), jnp.int32),
                   jax.ShapeDtypeStruct((n, LANES), F32), jax.ShapeDtypeStruct((1, LANES), jnp.int32)],
        scratch_shapes=[pltpu.VMEM((1, LANES), F32)],
        compiler_params=_params("arbitrary"), name="post_mixer")(
            x, ha, hb, pa, pb, wout, g1, b1, wpg, wpp, wr2, wrh, br)


def _expert_kernel(blk_ref, e_ref, lo_ref, hi_ref, nv_ref, nxt_ref, slot_ref, x_ref, wg_hbm, bg_ref, wu_hbm, bu_ref,
                   wd_hbm, bd_ref, o_ref, w32_s, wg_s, wu_s, wd_s, sem, *, layer):
    v = pl.program_id(0)
    prev = jnp.maximum(v - 1, 0)
    valid = v < nv_ref[0]
    new_expert = jnp.logical_or(v == 0, e_ref[v] != e_ref[prev])
    new_block = jnp.logical_or(v == 0, blk_ref[v] != blk_ref[prev])
    slot = slot_ref[v]

    def weight_copies(expert, to_slot):
        return [pltpu.make_async_copy(w_hbm.at[layer, expert], w32_s.at[to_slot, j], sem.at[to_slot, j])
                for j, w_hbm in enumerate((wg_hbm, wu_hbm, wd_hbm))]

    @pl.when(v == 0)
    def _prime():
        for cp in weight_copies(e_ref[0], slot):
            cp.start()

    @pl.when(jnp.logical_and(valid, new_expert))
    def _next_weights():
        for cp in weight_copies(e_ref[v], slot):
            cp.wait()

        @pl.when(nxt_ref[v] != e_ref[v])
        def _prefetch():
            for cp in weight_copies(nxt_ref[v], 1 - slot):
                cp.start()

        wg_s[...] = w32_s[slot, 0].astype(BF16)
        wu_s[...] = w32_s[slot, 1].astype(BF16)
        wd_s[...] = w32_s[slot, 2].astype(BF16)

    @pl.when(valid)
    def _compute():
        half = wg_s.shape[0] // 2
        x_hi, x_lo = (t.astype(BF16) for t in _unpack_pairs(x_ref[...]))
        gate = (jnp.dot(x_hi, wg_s[0:half, :], preferred_element_type=F32)
                + jnp.dot(x_lo, wg_s[half:, :], preferred_element_type=F32) + bg_ref[0])
        up = (jnp.dot(x_hi, wu_s[0:half, :], preferred_element_type=F32)
              + jnp.dot(x_lo, wu_s[half:, :], preferred_element_type=F32) + bu_ref[0])
        gate = jnp.minimum(gate, SWIGLU_LIMIT)
        up = jnp.clip(up, -SWIGLU_LIMIT, SWIGLU_LIMIT)
        glu = gate * jax.nn.sigmoid(gate * SWIGLU_ALPHA)
        act = ((up + 1.0) * glu).astype(BF16)
        y = _pack_pairs(jnp.dot(act, wd_s[...], preferred_element_type=F32) + bd_ref[0])
        rows = blk_ref[v] * EXPERT_ROWS + lax.broadcasted_iota(jnp.int32, (EXPERT_ROWS, 1), 0)
        mine = jnp.logical_and(rows >= lo_ref[v], rows < hi_ref[v])

        @pl.when(new_block)
        def _first():
            o_ref[...] = jnp.where(mine, y, 0)

        @pl.when(jnp.logical_not(new_block))
        def _again():
            o_ref[...] = jnp.where(mine, y, o_ref[...])


def _experts(layer, blk_v, e_v, lo_v, hi_v, n_vis, nxt_v, slot_v, xin, wg, bg, wu, bu, wd, bd):
    n_rows = xin.shape[0]
    d = wg.shape[-1]
    n_visits = blk_v.shape[0]
    xspec = pl.BlockSpec((EXPERT_ROWS, d // 2), lambda v, blk, e, *_: (blk[v], 0))
    ospec = pl.BlockSpec((EXPERT_ROWS, d // 2), lambda v, blk, e, *_: (blk[v], 0))
    wspec = pl.BlockSpec(memory_space=pl.ANY)
    bspec = pl.BlockSpec((None, 1, 1, d), lambda v, blk, e, *_: (layer, e[v], 0, 0))
    grid_spec = pltpu.PrefetchScalarGridSpec(
        num_scalar_prefetch=7, grid=(n_visits,),
        in_specs=[xspec, wspec, bspec, wspec, bspec, wspec, bspec], out_specs=ospec,
        scratch_shapes=[pltpu.VMEM((2, 3, d, d), F32)] + [pltpu.VMEM((d, d), BF16)] * 3
        + [pltpu.SemaphoreType.DMA((2, 3))])
    return pl.pallas_call(
        functools.partial(_expert_kernel, layer=layer), grid_spec=grid_spec,
        out_shape=jax.ShapeDtypeStruct((n_rows, d // 2), jnp.int32),
        compiler_params=_params("arbitrary"), name="experts")(
            blk_v, e_v, lo_v, hi_v, n_vis, nxt_v, slot_v, xin, wg, bg, wu, bu, wd, bd)


DISPATCH_ROWS = 48
COLLECT_ROWS = 16
COMBINE_PARTS = 3


def _sc_layout(n_tok, chunk_rows):
    info = plsc.get_sparse_core_info()
    n_workers = info.num_cores * info.num_subcores
    per_worker = n_tok // n_workers
    n_chunks = per_worker // chunk_rows
    assert per_worker * n_workers == n_tok and n_chunks * chunk_rows == per_worker
    return info, n_workers, per_worker, n_chunks


def _sc_index(pos, n_workers, n_chunks, chunk_rows):
    return pos.T.reshape(TOP_K, n_workers, n_chunks, chunk_rows).transpose(1, 2, 0, 3).reshape(
        n_workers, n_chunks * TOP_K, chunk_rows)


def _dispatch_rows(x1, pos):
    n_tok, w = x1.shape
    rows = DISPATCH_ROWS
    info, n_workers, per_worker, n_chunks = _sc_layout(n_tok, rows)
    mesh = plsc.VectorSubcoreMesh(core_axis_name="core", subcore_axis_name="subcore")

    @functools.partial(
        pl.kernel, out_type=jax.ShapeDtypeStruct((n_tok * TOP_K, w), x1.dtype), mesh=mesh,
        scratch_types=[pltpu.VMEM((n_chunks * TOP_K, rows), jnp.int32), pltpu.VMEM((2, rows, w), x1.dtype),
                       pltpu.SemaphoreType.DMA, pltpu.SemaphoreType.DMA, pltpu.SemaphoreType.DMA],
        name="dispatch_rows")
    def scatter_rows(x_hbm, idx_hbm, out_hbm, idx_v, rows_v, load_sem0, load_sem1, scatter_sem):
        worker = lax.axis_index("subcore") * info.num_cores + lax.axis_index("core")
        pltpu.sync_copy(idx_hbm.at[worker], idx_v)
        base = worker * per_worker
        load_sems = (load_sem0, load_sem1)

        def load(c):
            return pltpu.async_copy(x_hbm.at[pl.ds(base + c * rows, rows)], rows_v.at[c % 2], load_sems[c % 2])

        loading = load(0)
        for c in range(n_chunks):
            loading.wait()
            if c + 1 < n_chunks:
                loading = load(c + 1)
            copies = [pltpu.async_copy(rows_v.at[c % 2], out_hbm.at[idx_v.at[c * TOP_K + k]], scatter_sem)
                      for k in range(TOP_K)]
            for cp in copies:
                cp.wait()

    return scatter_rows(x1, _sc_index(pos, n_workers, n_chunks, rows))


def _collect_rows(y_rows, pos):
    n_tok = pos.shape[0]
    w = y_rows.shape[1]
    rows = COLLECT_ROWS
    info, n_workers, per_worker, n_chunks = _sc_layout(n_tok, rows)
    mesh = plsc.VectorSubcoreMesh(core_axis_name="core", subcore_axis_name="subcore")

    @functools.partial(
        pl.kernel, out_type=jax.ShapeDtypeStruct((n_tok * TOP_K, w), y_rows.dtype), mesh=mesh,
        scratch_types=[pltpu.VMEM((n_chunks * TOP_K, rows), jnp.int32), pltpu.VMEM((2, TOP_K, rows, w), y_rows.dtype),
                       pltpu.SemaphoreType.DMA, pltpu.SemaphoreType.DMA],
        name="collect_rows")
    def gather_rows(y_hbm, idx_hbm, out_hbm, idx_v, rows_v, sem0, sem1):
        worker = lax.axis_index("subcore") * info.num_cores + lax.axis_index("core")
        pltpu.sync_copy(idx_hbm.at[worker], idx_v)
        base = worker * per_worker
        sems = (sem0, sem1)

        def gather(c):
            return [pltpu.async_copy(y_hbm.at[idx_v.at[c * TOP_K + k]], rows_v.at[c % 2, k], sems[c % 2])
                    for k in range(TOP_K)]

        pending = gather(0)
        for c in range(n_chunks):
            following = gather(c + 1) if c + 1 < n_chunks else []
            for cp in pending:
                cp.wait()
            for k in range(TOP_K):
                pltpu.sync_copy(rows_v.at[c % 2, k], out_hbm.at[pl.ds(k * n_tok + base + c * rows, rows)])
            pending = following

    return gather_rows(y_rows, _sc_index(pos, n_workers, n_chunks, rows))


def _moe(layer, x1, routing, wg, bg, wu, bu, wd, bd):
    n_tok = x1.shape[0]
    n_assign = n_tok * TOP_K
    n_blocks = n_assign // EXPERT_ROWS
    idx_o, rank_o, gate_o, cnt = routing
    counts = cnt[0, :N_EXPERTS]
    ends = jnp.cumsum(counts)
    offs = ends - counts
    experts = jnp.arange(N_EXPERTS, dtype=jnp.int32)
    top_i = idx_o[:, :TOP_K]
    off_tok = jnp.sum(jnp.where(top_i[:, :, None] == experts, offs, 0), axis=-1)
    pos = (off_tok + rank_o[:, :TOP_K]).reshape(n_assign)
    pos = pos.reshape(n_tok, TOP_K)
    xin = _dispatch_rows(x1, pos)

    first_blk = offs // EXPERT_ROWS
    last_blk = (ends - 1) // EXPERT_ROWS
    nvis = jnp.where(counts > 0, last_blk - first_blk + 1, 0)
    vend = jnp.cumsum(nvis)
    vstart = vend - nvis
    n_visits = n_blocks + N_EXPERTS - 1
    v = jnp.arange(n_visits, dtype=jnp.int32)
    e_v = jnp.sum((vend[None, :] <= v[:, None]).astype(jnp.int32), axis=1)
    e_last = jnp.max(jnp.where(counts > 0, experts, 0))
    e_v = jnp.minimum(e_v, e_last)
    pick = lambda tab: jnp.sum(jnp.where(e_v[:, None] == experts, tab, 0), axis=-1)
    blk_v = jnp.minimum(pick(first_blk) + v - pick(vstart), n_blocks - 1)
    active = counts > 0
    later = jnp.where(jnp.logical_and(active[None, :], experts[None, :] > experts[:, None]), experts[None, :], N_EXPERTS)
    nxt_e = jnp.min(later, axis=1)
    nxt_e = jnp.where(nxt_e == N_EXPERTS, experts, nxt_e)
    order_e = jnp.cumsum(active.astype(jnp.int32)) - 1
    y_rows = _experts(layer, blk_v.astype(jnp.int32), e_v.astype(jnp.int32), pick(offs).astype(jnp.int32),
                      pick(ends).astype(jnp.int32), vend[-1:].astype(jnp.int32),
                      pick(nxt_e).astype(jnp.int32), (pick(order_e) % 2).astype(jnp.int32),
                      xin, wg, bg, wu, bu, wd, bd)
    return y_rows, pos, gate_o


def _pack_w_in(w_in):
    qa, ka, va, oa, ia, fa, qb, fb, ib, gb = jnp.split(w_in, SPLIT_POINTS, axis=-1)

    def pad_heads(w):
        w = w.reshape(w.shape[:-1] + (N_HEADS, DK_A))
        w = jnp.pad(w, ((0, 0),) * (w.ndim - 1) + ((0, D_HEAD - DK_A),))
        return w.reshape(w.shape[:-2] + (N_HEADS * D_HEAD,))

    def pad_gate(w):
        return jnp.pad(w, ((0, 0),) * (w.ndim - 1) + ((0, LANES - N_HEADS),))

    cols = [pad_heads(qa), pad_heads(ka * (DK_A ** -0.5)), va, oa, qb, fb, ib, gb, pad_gate(ia), pad_gate(fa)]
    return jnp.concatenate(cols, axis=-1).astype(BF16)


def kernel(x_prompt, x_sample, state_mlstm_C, state_mlstm_n, state_mlstm_m, state_hgrn_S, p_prompt, p_sample,
           ln_in_g, ln_in_b, w_in, mlstm_ig_bias, mlstm_fg_bias, mlstm_norm_g, hgrn_lb_logits, hgrn_norm_g,
           w_out, ln1_g, ln1_b, w_router, b_router, w_gate, b_gate, w_up, b_up, w_down, b_down,
           w_ple_gate, w_ple_proj, ln2_g, ln2_b):
    bp, tp, d = x_prompt.shape
    bs, ts, _ = x_sample.shape
    n_p = bp * tp
    n_s = bs * ts
    ts_pad = SUBLANES
    assert tp % CHUNK == 0 and ts <= ts_pad and ts % CHUNK != 0

    lb_soft = jax.nn.softmax(hgrn_lb_logits.astype(F32), axis=0)
    lower_bounds = jnp.cumsum(lb_soft, axis=0) - lb_soft[0]

    w_in_p = _pack_w_in(w_in)
    gbias = jnp.stack([jnp.pad(mlstm_ig_bias, ((0, 0), (0, LANES - N_HEADS))),
                       jnp.pad(mlstm_fg_bias, ((0, 0), (0, LANES - N_HEADS)))], axis=1)
    w_out_b = w_out.astype(BF16)
    w_pg_b = w_ple_gate.astype(BF16)
    w_pp_b = w_ple_proj.astype(BF16)
    w_r = jnp.pad(w_router, ((0, 0), (0, 0), (0, LANES - N_EXPERTS)))
    w_r_hi = w_r.astype(BF16)
    w_r2 = jnp.concatenate([w_r_hi, (w_r - w_r_hi.astype(F32)).astype(BF16)], axis=-1)
    b_r = jnp.pad(b_router, ((0, 0), (0, LANES - N_EXPERTS))).reshape(DEPTH, 1, LANES)
    n0_pad = jnp.pad(state_mlstm_n, ((0, 0), (0, 0), (0, 0), (0, D_HEAD - DK_A))).reshape(
        DEPTH, bs, N_HEADS, 1, D_HEAD)
    m0_pad = jnp.pad(state_mlstm_m, ((0, 0), (0, 0), (0, LANES - N_HEADS))).reshape(DEPTH, bs, 1, LANES)

    p_p = p_prompt.reshape(DEPTH, n_p, D_PLE)
    p_s = p_sample.reshape(DEPTH, n_s, D_PLE)
    x = _ln(x_prompt.reshape(n_p, d), x_sample.reshape(n_s, d), ln_in_g, ln_in_b)

    assert n_p % ROW_TILE == 0 and n_s % ROW_TILE == 0
    nb_p = LANES // CHUNK
    nb_s = LANES // ts_pad
    st_p = _stacked_states(bp)
    st_s = _stacked_states(bs)
    for l in range(DEPTH):
        lb = lower_bounds[l].reshape(1, MIX)
        nga = mlstm_norm_g[l].reshape(1, MIX)
        ngb = hgrn_norm_g[l].reshape(1, MIX)
        proj_p = _inproj(x, w_in_p, l, 0, n_p).reshape(bp, tp, N_PROJ)
        h_p, st_p = _scan(proj_p, gbias[l], lb, nga, ngb, None, st_p, layer=l,
                          NB=nb_p, n_chunks=tp // CHUNK, L=CHUNK, last=CHUNK - 1)
        proj_s = jnp.pad(_inproj(x, w_in_p, l, n_p, n_s).reshape(bs, ts, N_PROJ), ((0, 0), (0, ts_pad - ts), (0, 0)))
        h_s, st_s = _scan(proj_s, gbias[l], lb, nga, ngb,
                          (state_mlstm_C, n0_pad, m0_pad, state_hgrn_S), st_s, layer=l,
                          NB=nb_s, n_chunks=1, L=ts_pad, last=ts - 1)
        x1, resid, *routing = _post_mixer(l, x, h_p.reshape(n_p, d), h_s[:, :ts].reshape(n_s, d), p_p, p_s,
                                          w_out_b[l], ln1_g[l], ln1_b[l], w_pg_b[l], w_pp_b[l],
                                          w_r2[l], w_r_hi[l], b_r[l])
        y_rows, pos, gates = _moe(l, x1, routing, w_gate, b_gate.reshape(DEPTH, N_EXPERTS, 1, d),
                                  w_up, b_up.reshape(DEPTH, N_EXPERTS, 1, d),
                                  w_down, b_down.reshape(DEPTH, N_EXPERTS, 1, d))
        n_part = (n_p + n_s) // COMBINE_PARTS
        assert n_part * COMBINE_PARTS == n_p + n_s and n_part % ROW_TILE == 0
        for part in range(COMBINE_PARTS):
            y_k = _collect_rows(y_rows, pos[part * n_part:(part + 1) * n_part])
            x = _combine_ln(resid, y_k, gates, ln2_g[l], ln2_b[l], x, part * n_part)

    y_prompt = x[:n_p].reshape(bp, tp, d)
    y_sample = x[n_p:].reshape(bs, ts, d)
    unpad = lambda st: (st[0], st[1][:, :, :, 0, :DK_A], st[2][:, :, 0, :N_HEADS], st[3])
    return (y_prompt, y_sample) + unpad(st_p) + unpad(st_s)
```

```python
import functools

import jax
import jax.numpy as jnp
import numpy as np
from jax import lax
from jax.experimental import pallas as pl
from jax.experimental.pallas import tpu as pltpu
from jax.experimental.pallas import tpu_sc as plsc

F32 = jnp.float32
BF16 = jnp.bfloat16

D_MODEL = 1024
DEPTH = 4
D_PLE = 256
N_HEADS = 4
DK_A = 64
D_HEAD = 128
MIX = 512
N_EXPERTS = 32
TOP_K = 4
SWIGLU_LIMIT = 7.0
SWIGLU_ALPHA = 1.702
CHUNK = 64
LN_EPS = 1e-5
NORM_EPS = 1e-6
LB_FLOOR = 1e-20
DEEPNORM_ALPHA = (2 * DEPTH) ** 0.25
SPLIT_SIZES = (256, 256, 512, 512, 4, 4, 512, 512, 512, 512)
SPLIT_POINTS = tuple(int(s) for s in np.cumsum(SPLIT_SIZES)[:-1])

LANES = 128
SUBLANES = 8
VMEM_LIMIT = 56 * 1024 * 1024

QA0, KA0, VA0, OA0 = 0, 512, 1024, 1536
QB0, FB0, IB0, GB0 = 2048, 2560, 3072, 3584
IG0, FG0 = 4096, 4224
N_PROJ = 4352

ROW_TILE = 512
EXPERT_ROWS = 256

NT_DIMS = (((1,), (1,)), ((), ()))
TN_DIMS = (((0,), (0,)), ((), ()))


def _params(*sem):
    return pltpu.CompilerParams(dimension_semantics=sem, vmem_limit_bytes=VMEM_LIMIT)


def _log_sigmoid(x):
    return jnp.minimum(x, 0.0) - jnp.log(1.0 + jnp.exp(-jnp.abs(x)))


def _split3(x):
    x1 = x.astype(BF16)
    r1 = x - x1.astype(F32)
    x2 = r1.astype(BF16)
    x3 = (r1 - x2.astype(F32)).astype(BF16)
    return x1, x2, x3


def _dot01(sel, x):
    return sum(jnp.dot(sel, t, preferred_element_type=F32) for t in _split3(x))


def _pack_pairs(x):
    w = x.shape[1] // 2
    bits = lax.bitcast_convert_type(x.astype(BF16).astype(F32), jnp.uint32)
    return lax.bitcast_convert_type(bits[:, :w] | (bits[:, w:] >> 16), jnp.int32)


def _unpack_pairs(words):
    u = lax.bitcast_convert_type(words, jnp.uint32)
    hi = lax.bitcast_convert_type(u & jnp.uint32(0xFFFF0000), F32)
    lo = lax.bitcast_convert_type(u << 16, F32)
    return hi, lo


def _layernorm_rows(x, g, b):
    mu = jnp.mean(x, axis=-1, keepdims=True)
    xc = x - mu
    var = jnp.mean(xc * xc, axis=-1, keepdims=True)
    return xc * lax.rsqrt(var + LN_EPS) * g + b


def _two_sources(n_a, n_b, width, lead=()):
    ta = n_a // ROW_TILE
    assert n_a % ROW_TILE == 0 and n_b % ROW_TILE == 0
    block = (None,) * len(lead) + (ROW_TILE, width)
    spec_a = pl.BlockSpec(block, lambda i: lead + (jnp.minimum(i, ta - 1), 0))
    spec_b = pl.BlockSpec(block, lambda i: lead + (jnp.maximum(i - ta, 0), 0))
    return spec_a, spec_b, ta


def _ln_kernel(xa_ref, xb_ref, g_ref, b_ref, o_ref, *, tiles_a):
    x = jnp.where(pl.program_id(0) < tiles_a, xa_ref[...], xb_ref[...])
    o_ref[...] = _layernorm_rows(x, g_ref[...], b_ref[...])


def _ln(xa, xb, g, b):
    d = xa.shape[1]
    n = xa.shape[0] + xb.shape[0]
    spec_a, spec_b, ta = _two_sources(xa.shape[0], xb.shape[0], d)
    vec = pl.BlockSpec((1, d), lambda i: (0, 0))
    return pl.pallas_call(
        functools.partial(_ln_kernel, tiles_a=ta), grid=(n // ROW_TILE,), in_specs=[spec_a, spec_b, vec, vec],
        out_specs=pl.BlockSpec((ROW_TILE, d), lambda i: (i, 0)),
        out_shape=jax.ShapeDtypeStruct((n, d), F32), compiler_params=_params("parallel"),
        name="ln_in")(xa, xb, g.reshape(1, d), b.reshape(1, d))


def _combine_ln_kernel(resid_ref, y0_ref, y1_ref, y2_ref, y3_ref, gate_ref, g_ref, beta_ref, buf_ref, o_ref):
    half = resid_ref.shape[1] // 2
    acc_hi = resid_ref[:, :half]
    acc_lo = resid_ref[:, half:]
    for k, y_ref in enumerate((y0_ref, y1_ref, y2_ref, y3_ref)):
        hi, lo = _unpack_pairs(y_ref[...])
        gk = gate_ref[:, k:k + 1]
        acc_hi = acc_hi + gk * hi
        acc_lo = acc_lo + gk * lo
    o_ref[...] = _layernorm_rows(jnp.concatenate([acc_hi, acc_lo], axis=1), g_ref[...], beta_ref[...])


def _combine_ln(resid, y_k, gates, g, beta, buf, row0):
    n, d = resid.shape
    n_part = y_k.shape[0] // TOP_K
    nt = n_part // ROW_TILE
    t0 = row0 // ROW_TILE
    row = lambda w: pl.BlockSpec((ROW_TILE, w), lambda i: (t0 + i, 0))
    yk = lambda k: pl.BlockSpec((ROW_TILE, d // 2), lambda i: (k * nt + i, 0))
    vec = pl.BlockSpec((1, d), lambda i: (0, 0))
    return pl.pallas_call(
        _combine_ln_kernel, grid=(nt,),
        in_specs=[row(d)] + [yk(k) for k in range(TOP_K)] + [row(LANES), vec, vec, pl.BlockSpec(memory_space=pl.ANY)],
        out_specs=row(d), out_shape=jax.ShapeDtypeStruct((n, d), F32), input_output_aliases={8: 0},
        compiler_params=_params("parallel"),
        name="combine_ln2")(resid, y_k, y_k, y_k, y_k, gates, g.reshape(1, d), beta.reshape(1, d), buf)


def _inproj_kernel(x_ref, w_ref, o_ref):
    o_ref[...] = jnp.dot(x_ref[...].astype(BF16), w_ref[...], preferred_element_type=F32)


def _inproj(x, w, layer, row0, n_rows):
    d = x.shape[1]
    t0 = row0 // ROW_TILE
    return pl.pallas_call(
        _inproj_kernel, grid=(n_rows // ROW_TILE,),
        in_specs=[pl.BlockSpec((ROW_TILE, d), lambda i: (t0 + i, 0)),
                  pl.BlockSpec((None, d, N_PROJ), lambda i: (layer, 0, 0))],
        out_specs=pl.BlockSpec((ROW_TILE, N_PROJ), lambda i: (i, 0)),
        out_shape=jax.ShapeDtypeStruct((n_rows, N_PROJ), F32), compiler_params=_params("parallel"),
        name="in_proj")(x, w)


def _scan_kernel(*refs, NB, L, last, has_state):
    it = iter(refs)
    proj_ref, gbias_ref, lb_ref, nga_ref, ngb_ref = (next(it) for _ in range(5))
    if has_state:
        C0_ref, n0_ref, m0_ref, S0_ref = (next(it) for _ in range(4))
    for _ in range(4):
        next(it)
    h_ref, C_out, n_out, m_out, S_out = (next(it) for _ in range(5))
    C_s, n_s, m_s, St_s = (next(it) for _ in range(4))

    c = pl.program_id(1)
    nc = pl.num_programs(1)
    H = range(N_HEADS)
    N = range(NB)
    R = NB * L
    assert R == LANES

    @pl.when(c == 0)
    def _init():
        if has_state:
            for n in N:
                for h in H:
                    C_s[n, h, 0:DK_A, :] = C0_ref[n, h]
                    C_s[n, h, DK_A:D_HEAD, :] = jnp.zeros((D_HEAD - DK_A, D_HEAD), F32)
                    St_s[n, h] = S0_ref[n, h].T
            n_s[...] = n0_ref[...]
            m_s[...] = m0_ref[...]
        else:
            C_s[...] = jnp.zeros(C_s.shape, F32)
            n_s[...] = jnp.zeros(n_s.shape, F32)
            m_s[...] = jnp.zeros(m_s.shape, F32)
            St_s[...] = jnp.zeros(St_s.shape, F32)

    row = lax.broadcasted_iota(jnp.int32, (R, R), 0)
    col = lax.broadcasted_iota(jnp.int32, (R, R), 1)
    causal = jnp.logical_and(row // L == col // L, row >= col)
    tril = causal.astype(BF16)
    ones_r = jnp.ones((R, LANES), BF16)
    lane = lax.broadcasted_iota(jnp.int32, (R, LANES), 1)
    step = lax.broadcasted_iota(jnp.int32, (R, 1), 0) % L
    valid = step <= last
    padded = last < L - 1
    tile = lambda base, h: slice(base + LANES * h, base + LANES * (h + 1))
    seq = lambda n: slice(n * L, (n + 1) * L)
    cols = lambda base, width: proj_ref[:, :, base:base + width].reshape(R, width)
    per_seq = lambda vals: jnp.concatenate([jnp.broadcast_to(x, (L, x.shape[-1])) for x in vals], axis=0)
    last_rows = lambda x: [x[n * L + last:n * L + last + 1] for n in N]

    i_t = cols(IG0, LANES) + gbias_ref[0:1, :]
    f_t = _log_sigmoid(cols(FG0, LANES) + gbias_ref[1:2, :])
    zf = cols(FB0, MIX)
    lb = lb_ref[...]
    la = jnp.log(jnp.maximum(lb, LB_FLOOR))
    bb = jnp.log1p(-lb) + _log_sigmoid(zf)
    f_log = jnp.maximum(la, bb) + jnp.log(1.0 + jnp.exp(-jnp.abs(la - bb)))
    b_t = _dot01(tril, f_t)
    g = _dot01(tril, f_log)
    m_all = [m_s[n] for n in N]
    C_prev = [[C_s[n, h] for h in H] for n in N]
    n_prev = [[n_s[n, h] for h in H] for n in N]
    St = [[St_s[n, h] for h in H] for n in N]
    qf = [cols(QA0 + LANES * h, LANES) for h in H]
    kf = [cols(KA0 + LANES * h, LANES) for h in H]
    q = [x.astype(BF16) for x in qf]
    k = [x.astype(BF16) for x in kf]
    v = [cols(VA0 + LANES * h, LANES).astype(BF16) for h in H]
    kb = (1.0 - lb) * jax.nn.sigmoid(-zf)
    qr = cols(QB0, MIX)
    qb = qr * jax.nn.sigmoid(qr)
    vb = [cols(IB0 + LANES * h, LANES).astype(BF16) for h in H]
    qb16 = qb.astype(BF16)
    kb16 = kb.astype(BF16)
    r_t = i_t - b_t
    zero16 = jnp.zeros((R, LANES), BF16)
    r_all = sum(
        lax.dot_general(ones_r, jnp.concatenate([jnp.where(lane == h, term, zero16) for h in H], axis=0),
                        NT_DIMS, preferred_element_type=F32)
        for term in _split3(r_t))
    r_m = [r_all[:, R * h:R * (h + 1)] for h in H]
    qk = [lax.dot_general(q[h], k[h], NT_DIMS, preferred_element_type=F32) for h in H]
    qC = [jnp.concatenate([jnp.dot(q[h][seq(n)], C_prev[n][h].astype(BF16), preferred_element_type=F32)
                           for n in N], axis=0) for h in H]
    levels = []
    bs = 1
    while bs < L:
        levels.append(bs)
        bs *= 2
    small = [bs for bs in levels if bs < SUBLANES]
    g_anchor = {}
    if small:
        sel = jnp.concatenate(
            [(col == (row // (2 * bs)) * (2 * bs) + bs).astype(BF16) for bs in small], axis=0)
        picked = _dot01(sel, g)
        for i, bs in enumerate(small):
            g_anchor[bs] = picked[i * R:(i + 1) * R]
    for bs in levels:
        if bs >= SUBLANES:
            g_anchor[bs] = jnp.concatenate(
                [jnp.broadcast_to(g[p0 + bs:p0 + bs + 1], (2 * bs, MIX)) for p0 in range(0, R, 2 * bs)], axis=0)
    qg = (qb * jnp.exp(g)).astype(BF16)
    o_inter = [jnp.concatenate([lax.dot_general(qg[seq(n), tile(0, h)], St[n][h].astype(BF16), NT_DIMS,
                                                preferred_element_type=F32) for n in N], axis=0) for h in H]
    a = [jnp.where(row == col, lax.dot_general(qb16[:, tile(0, h)], kb16[:, tile(0, h)], NT_DIMS,
                                                preferred_element_type=F32), 0.0) for h in H]
    qn = [jnp.sum(qf[h] * per_seq([n_prev[n][h] for n in N]), axis=1, keepdims=True) for h in H]
    b_col = [b_t[:, h:h + 1] for h in H]
    i_col = [i_t[:, h:h + 1] for h in H]
    m_prev = [per_seq([m_all[n][:, h:h + 1] for n in N]) for h in H]
    cm = [jnp.max(jnp.where(causal, r_m[h], -jnp.inf), axis=1, keepdims=True) for h in H]
    m_t = [b_col[h] + jnp.maximum(m_prev[h], cm[h]) for h in H]
    inter = [jnp.exp(b_col[h] + m_prev[h] - m_t[h]) for h in H]
    d = [jnp.exp(jnp.where(causal, (b_col[h] - m_t[h]) + r_m[h], -1e30)) for h in H]
    s = [qk[h] * d[h] for h in H]
    log2e = 1.4426950408889634
    e = [jnp.exp2((g - g_anchor[bs]) * jnp.where((step // bs) % 2 == 1, log2e, -log2e)) for bs in levels]
    qe = [(qb * x).astype(BF16) for x in e]
    ke = [(kb * x).astype(BF16) for x in e]
    sv = [jnp.dot(s[h].astype(BF16), v[h], preferred_element_type=F32) for h in H]
    sc = [[lax.dot_general(qe[i][:, tile(0, h)], ke[i][:, tile(0, h)], NT_DIMS, preferred_element_type=F32)
           for h in H] for i in range(len(levels))]
    den = [jnp.sum(s[h], axis=1, keepdims=True) + inter[h] * qn[h] for h in H]
    m_new = [last_rows(m_t[h]) for h in H]
    b_last = [last_rows(b_col[h]) for h in H]
    w_arg = [per_seq(b_last[h]) - b_col[h] + i_col[h] - per_seq(m_new[h]) for h in H]
    if padded:
        w_arg = [jnp.where(valid, x, -1e30) for x in w_arg]
    kw = [kf[h] * jnp.exp(w_arg[h]) for h in H]
    kw16 = [x.astype(BF16) for x in kw]
    decay = [[jnp.exp(b_last[h][n] + m_all[n][:, h:h + 1] - m_new[h][n]) for h in H] for n in N]
    kv = [[lax.dot_general(kw16[h][seq(n)], v[h][seq(n)], TN_DIMS, preferred_element_type=F32) for h in H]
          for n in N]
    for i, bs in enumerate(levels):
        pair = jnp.logical_and(row // (2 * bs) == col // (2 * bs),
                               jnp.logical_and((row // bs) % 2 == 1, (col // bs) % 2 == 0))
        for h in H:
            a[h] = jnp.where(pair, sc[i][h], a[h])
    o = [o_inter[h] + jnp.dot(a[h].astype(BF16), vb[h], preferred_element_type=F32) for h in H]
    g_last = last_rows(g)
    dec_arg = per_seq(g_last) - g
    if padded:
        dec_arg = jnp.where(valid, dec_arg, -1e30)
    kdec = (kb * jnp.exp(dec_arg)).astype(BF16)
    eg_last = [jnp.exp(x) for x in g_last]
    vk = [[lax.dot_general(vb[h][seq(n)], kdec[seq(n), tile(0, h)], TN_DIMS, preferred_element_type=F32)
           for h in H] for n in N]
    hh = [(sv[h] + inter[h] * qC[h]) / jnp.maximum(jnp.abs(den[h]), jnp.exp(-m_t[h])) for h in H]
    for h in H:
        hn = hh[h] * lax.rsqrt(jnp.mean(hh[h] * hh[h], axis=-1, keepdims=True) + NORM_EPS) * nga_ref[:, tile(0, h)]
        out_a = jax.nn.sigmoid(cols(OA0 + LANES * h, LANES)) * hn
        h_ref[:, :, tile(0, h)] = out_a.reshape(NB, L, LANES).astype(h_ref.dtype)
    for h in H:
        on = o[h] * lax.rsqrt(jnp.mean(o[h] * o[h], axis=-1, keepdims=True) + NORM_EPS) * ngb_ref[:, tile(0, h)]
        gr = cols(GB0 + LANES * h, LANES)
        h_ref[:, :, tile(MIX, h)] = (gr * jax.nn.sigmoid(gr) * on).reshape(NB, L, LANES).astype(h_ref.dtype)
    lane1 = lax.broadcasted_iota(jnp.int32, (1, LANES), 1)
    for n in N:
        for h in H:
            C_s[n, h] = decay[n][h] * C_prev[n][h] + kv[n][h]
            n_s[n, h] = decay[n][h] * n_prev[n][h] + jnp.sum(kw[h][seq(n)], axis=0, keepdims=True)
            St_s[n, h] = eg_last[n][:, tile(0, h)] * St[n][h] + vk[n][h]
        m_upd = m_all[n]
        for h in H:
            m_upd = jnp.where(lane1 == h, m_new[h][n], m_upd)
        m_s[n] = m_upd

    @pl.when(c == nc - 1)
    def _finish():
        for n in N:
            for h in H:
                C_out[n, h] = C_s[n, h, 0:DK_A, :]
                S_out[n, h] = St_s[n, h].T
        n_out[...] = n_s[...]
        m_out[...] = m_s[...]


def _scan(proj, gbias, lb, nga, ngb, state, stacked, *, layer, NB, n_chunks, L, last, h_dtype=F32):
    batch = proj.shape[0]
    has_state = state is not None
    const2 = lambda b, c: (0, 0)
    in_specs = [
        pl.BlockSpec((NB, L, N_PROJ), lambda b, c: (b, c, 0)),
        pl.BlockSpec((2, LANES), const2),
        pl.BlockSpec((1, MIX), const2),
        pl.BlockSpec((1, MIX), const2),
        pl.BlockSpec((1, MIX), const2),
    ]
    args = [proj, gbias, lb, nga, ngb]
    lst4 = lambda b, c: (layer, b, 0, 0, 0)
    lst3 = lambda b, c: (layer, b, 0, 0)
    if has_state:
        in_specs += [
            pl.BlockSpec((None, NB, N_HEADS, DK_A, D_HEAD), lst4),
            pl.BlockSpec((None, NB, N_HEADS, 1, LANES), lst4),
            pl.BlockSpec((None, NB, 1, LANES), lst3),
            pl.BlockSpec((None, NB, N_HEADS, D_HEAD, D_HEAD), lst4),
        ]
        args += list(state)
    aliases = {len(args) + i: 1 + i for i in range(4)}
    in_specs += [pl.BlockSpec(memory_space=pl.ANY)] * 4
    args += list(stacked)
    out_specs = [
        pl.BlockSpec((NB, L, D_MODEL), lambda b, c: (b, c, 0)),
        pl.BlockSpec((None, NB, N_HEADS, DK_A, D_HEAD), lst4),
        pl.BlockSpec((None, NB, N_HEADS, 1, LANES), lst4),
        pl.BlockSpec((None, NB, 1, LANES), lst3),
        pl.BlockSpec((None, NB, N_HEADS, D_HEAD, D_HEAD), lst4),
    ]
    out_shape = [jax.ShapeDtypeStruct((batch, n_chunks * L, D_MODEL), h_dtype)] + [
        jax.ShapeDtypeStruct(a.shape, a.dtype) for a in stacked]
    scratch = [
        pltpu.VMEM((NB, N_HEADS, D_HEAD, D_HEAD), F32),
        pltpu.VMEM((NB, N_HEADS, 1, LANES), F32),
        pltpu.VMEM((NB, 1, LANES), F32),
        pltpu.VMEM((NB, N_HEADS, D_HEAD, D_HEAD), F32),
    ]
    kern = functools.partial(_scan_kernel, NB=NB, L=L, last=last, has_state=has_state)
    res = pl.pallas_call(
        kern, grid=(batch // NB, n_chunks), in_specs=in_specs, out_specs=out_specs, out_shape=out_shape,
        scratch_shapes=scratch, input_output_aliases=aliases,
        compiler_params=_params("parallel", "arbitrary"),
        name="scan_state" if has_state else "scan_prompt")(*args)
    return res[0], tuple(res[1:])


def _stacked_states(batch):
    return (jnp.zeros((DEPTH, batch, N_HEADS, DK_A, D_HEAD), F32),
            jnp.zeros((DEPTH, batch, N_HEADS, 1, LANES), F32),
            jnp.zeros((DEPTH, batch, 1, LANES), F32),
            jnp.zeros((DEPTH, batch, N_HEADS, D_HEAD, D_HEAD), F32))


def _post_mixer_kernel(x_ref, ha_ref, hb_ref, pa_ref, pb_ref, wout_ref, g1_ref, b1_ref, wpg_ref, wpp_ref, wr2_ref,
                       wrh_ref, br_ref, x1_ref, resid_ref, idx_ref, rank_ref, gate_ref, cnt_ref, carry_s, *, tiles_a):
    i = pl.program_id(0)

    @pl.when(i == 0)
    def _init():
        carry_s[...] = jnp.zeros(carry_s.shape, F32)

    from_a = i < tiles_a
    h = jnp.where(from_a, ha_ref[...].astype(F32), hb_ref[...])
    p = jnp.where(from_a, pa_ref[...], pb_ref[...])
    mix = jnp.dot(h.astype(BF16), wout_ref[...], preferred_element_type=F32)
    x1 = _layernorm_rows(DEEPNORM_ALPHA * x_ref[...] + mix, g1_ref[...], b1_ref[...])
    x1b = x1.astype(BF16)
    x1_ref[...] = _pack_pairs(x1)
    x1_lo = (x1 - x1b.astype(F32)).astype(BF16)
    l2 = jnp.dot(x1b, wr2_ref[...], preferred_element_type=F32)
    logits = (l2[:, :LANES] + l2[:, LANES:]
              + jnp.dot(x1_lo, wrh_ref[...], preferred_element_type=F32) + br_ref[...])

    tm = logits.shape[0]
    lane = lax.broadcasted_iota(jnp.int32, (tm, LANES), 1)
    lane_f = lane.astype(F32)
    l = jnp.where(lane < N_EXPERTS, logits, -jnp.inf)
    vals, idxs, onehots = [], [], []
    for _ in range(TOP_K):
        mx = jnp.max(l, axis=1, keepdims=True)
        ix = jnp.min(jnp.where(l == mx, lane_f, float(LANES)), axis=1, keepdims=True)
        sel = lane_f == ix
        vals.append(mx)
        idxs.append(ix)
        onehots.append(sel.astype(F32))
        l = jnp.where(sel, -jnp.inf, l)
    gate = jax.nn.sigmoid(jnp.dot(x1b, wpg_ref[...], preferred_element_type=F32))
    pp = jnp.dot(p.astype(BF16), wpp_ref[...], preferred_element_type=F32)
    resid_ref[...] = DEEPNORM_ALPHA * x1 + gate * pp

    w = [jnp.exp(val - vals[0]) for val in vals]
    tot = w[0] + w[1] + w[2] + w[3]
    oh_all = onehots[0] + onehots[1] + onehots[2] + onehots[3]
    row = lax.broadcasted_iota(jnp.int32, (tm, tm), 0)
    col = lax.broadcasted_iota(jnp.int32, (tm, tm), 1)
    earlier = (row > col).astype(BF16)
    prefix = jnp.dot(earlier, oh_all.astype(BF16), preferred_element_type=F32) + carry_s[...]
    idx_o = jnp.zeros((tm, LANES), F32)
    rank_o = jnp.zeros((tm, LANES), F32)
    gate_o = jnp.zeros((tm, LANES), F32)
    for k in range(TOP_K):
        rank_k = jnp.sum(onehots[k] * prefix, axis=1, keepdims=True)
        idx_o = jnp.where(lane == k, idxs[k], idx_o)
        rank_o = jnp.where(lane == k, rank_k, rank_o)
        gate_o = jnp.where(lane == k, w[k] / tot, gate_o)
    idx_ref[...] = idx_o.astype(jnp.int32)
    rank_ref[...] = rank_o.astype(jnp.int32)
    gate_ref[...] = gate_o
    carry_s[...] = carry_s[...] + jnp.sum(oh_all, axis=0, keepdims=True)
    cnt_ref[...] = carry_s[...].astype(jnp.int32)


def _post_mixer(layer, x, ha, hb, pa, pb, wout, g1, b1, wpg, wpp, wr2, wrh, br):
    n, d = x.shape
    row = lambda w: pl.BlockSpec((ROW_TILE, w), lambda i: (i, 0))
    full = lambda a: pl.BlockSpec(a.shape, lambda i: (0, 0))
    ha_spec, hb_spec, ta = _two_sources(ha.shape[0], hb.shape[0], d)
    pa_spec, pb_spec, _ = _two_sources(pa.shape[1], pb.shape[1], D_PLE, lead=(layer,))
    g1 = g1.reshape(1, d)
    b1 = b1.reshape(1, d)
    return pl.pallas_call(
        functools.partial(_post_mixer_kernel, tiles_a=ta), grid=(n // ROW_TILE,),
        in_specs=[row(d), ha_spec, hb_spec, pa_spec, pb_spec, full(wout), full(g1), full(b1), full(wpg), full(wpp),
                  full(wr2), full(wrh), full(br)],
        out_specs=[row(d // 2), row(d), row(LANES), row(LANES), row(LANES), pl.BlockSpec((1, LANES), lambda i: (0, 0))],
        out_shape=[jax.ShapeDtypeStruct((n, d // 2), jnp.int32), jax.ShapeDtypeStruct((n, d), F32),
                   jax.ShapeDtypeStruct((n, LANES), jnp.int32), jax.ShapeDtypeStruct((n, LANES), jnp.int32),
                   jax.ShapeDtypeStruct((n, LANES), F32), jax.ShapeDtypeStruct((1, LANES), jnp.int32)],
        scratch_shapes=[pltpu.VMEM((1, LANES), F32)],
        compiler_params=_params("arbitrary"), name="post_mixer")(
            x, ha, hb, pa, pb, wout, g1, b1, wpg, wpp, wr2, wrh, br)


def _expert_kernel(blk_ref, e_ref, lo_ref, hi_ref, nv_ref, nxt_ref, slot_ref, x_ref, wg_hbm, bg_ref, wu_hbm, bu_ref,
                   wd_hbm, bd_ref, o_ref, w32_s, wg_s, wu_s, wd_s, sem, *, layer):
    v = pl.program_id(0)
    prev = jnp.maximum(v - 1, 0)
    valid = v < nv_ref[0]
    new_expert = jnp.logical_or(v == 0, e_ref[v] != e_ref[prev])
    new_block = jnp.logical_or(v == 0, blk_ref[v] != blk_ref[prev])
    slot = slot_ref[v]

    def weight_copies(expert, to_slot):
        return [pltpu.make_async_copy(w_hbm.at[layer, expert], w32_s.at[to_slot, j], sem.at[to_slot, j])
                for j, w_hbm in enumerate((wg_hbm, wu_hbm, wd_hbm))]

    @pl.when(v == 0)
    def _prime():
        for cp in weight_copies(e_ref[0], slot):
            cp.start()

    @pl.when(jnp.logical_and(valid, new_expert))
    def _next_weights():
        for cp in weight_copies(e_ref[v], slot):
            cp.wait()

        @pl.when(nxt_ref[v] != e_ref[v])
        def _prefetch():
            for cp in weight_copies(nxt_ref[v], 1 - slot):
                cp.start()

        wg_s[...] = w32_s[slot, 0].astype(BF16)
        wu_s[...] = w32_s[slot, 1].astype(BF16)
        wd_s[...] = w32_s[slot, 2].astype(BF16)

    @pl.when(valid)
    def _compute():
        half = wg_s.shape[0] // 2
        x_hi, x_lo = (t.astype(BF16) for t in _unpack_pairs(x_ref[...]))
        gate = (jnp.dot(x_hi, wg_s[0:half, :], preferred_element_type=F32)
                + jnp.dot(x_lo, wg_s[half:, :], preferred_element_type=F32) + bg_ref[0])
        up = (jnp.dot(x_hi, wu_s[0:half, :], preferred_element_type=F32)
              + jnp.dot(x_lo, wu_s[half:, :], preferred_element_type=F32) + bu_ref[0])
        gate = jnp.minimum(gate, SWIGLU_LIMIT)
        up = jnp.clip(up, -SWIGLU_LIMIT, SWIGLU_LIMIT)
        glu = gate * jax.nn.sigmoid(gate * SWIGLU_ALPHA)
        act = ((up + 1.0) * glu).astype(BF16)
        y = _pack_pairs(jnp.dot(act, wd_s[...], preferred_element_type=F32) + bd_ref[0])
        rows = blk_ref[v] * EXPERT_ROWS + lax.broadcasted_iota(jnp.int32, (EXPERT_ROWS, 1), 0)
        mine = jnp.logical_and(rows >= lo_ref[v], rows < hi_ref[v])

        @pl.when(new_block)
        def _first():
            o_ref[...] = jnp.where(mine, y, 0)

        @pl.when(jnp.logical_not(new_block))
        def _again():
            o_ref[...] = jnp.where(mine, y, o_ref[...])


def _experts(layer, blk_v, e_v, lo_v, hi_v, n_vis, nxt_v, slot_v, xin, wg, bg, wu, bu, wd, bd):
    n_rows = xin.shape[0]
    d = wg.shape[-1]
    n_visits = blk_v.shape[0]
    xspec = pl.BlockSpec((EXPERT_ROWS, d // 2), lambda v, blk, e, *_: (blk[v], 0))
    ospec = pl.BlockSpec((EXPERT_ROWS, d // 2), lambda v, blk, e, *_: (blk[v], 0))
    wspec = pl.BlockSpec(memory_space=pl.ANY)
    bspec = pl.BlockSpec((None, 1, 1, d), lambda v, blk, e, *_: (layer, e[v], 0, 0))
    grid_spec = pltpu.PrefetchScalarGridSpec(
        num_scalar_prefetch=7, grid=(n_visits,),
        in_specs=[xspec, wspec, bspec, wspec, bspec, wspec, bspec], out_specs=ospec,
        scratch_shapes=[pltpu.VMEM((2, 3, d, d), F32)] + [pltpu.VMEM((d, d), BF16)] * 3
        + [pltpu.SemaphoreType.DMA((2, 3))])
    return pl.pallas_call(
        functools.partial(_expert_kernel, layer=layer), grid_spec=grid_spec,
        out_shape=jax.ShapeDtypeStruct((n_rows, d // 2), jnp.int32),
        compiler_params=_params("arbitrary"), name="experts")(
            blk_v, e_v, lo_v, hi_v, n_vis, nxt_v, slot_v, xin, wg, bg, wu, bu, wd, bd)


DISPATCH_ROWS = 48
COLLECT_ROWS = 24
COMBINE_PARTS = 1


def _sc_layout(n_tok, chunk_rows):
    info = plsc.get_sparse_core_info()
    n_workers = info.num_cores * info.num_subcores
    per_worker = n_tok // n_workers
    n_chunks = per_worker // chunk_rows
    assert per_worker * n_workers == n_tok and n_chunks * chunk_rows == per_worker
    return info, n_workers, per_worker, n_chunks


def _sc_index(pos, n_workers, n_chunks, chunk_rows):
    return pos.T.reshape(TOP_K, n_workers, n_chunks, chunk_rows).transpose(1, 2, 0, 3).reshape(
        n_workers, n_chunks * TOP_K, chunk_rows)


def _dispatch_rows(x1, pos):
    n_tok, w = x1.shape
    rows = DISPATCH_ROWS
    info, n_workers, per_worker, n_chunks = _sc_layout(n_tok, rows)
    mesh = plsc.VectorSubcoreMesh(core_axis_name="core", subcore_axis_name="subcore")

    @functools.partial(
        pl.kernel, out_type=jax.ShapeDtypeStruct((n_tok * TOP_K, w), x1.dtype), mesh=mesh,
        scratch_types=[pltpu.VMEM((n_chunks * TOP_K, rows), jnp.int32), pltpu.VMEM((2, rows, w), x1.dtype),
                       pltpu.SemaphoreType.DMA, pltpu.SemaphoreType.DMA, pltpu.SemaphoreType.DMA],
        name="dispatch_rows")
    def scatter_rows(x_hbm, idx_hbm, out_hbm, idx_v, rows_v, load_sem0, load_sem1, scatter_sem):
        worker = lax.axis_index("subcore") * info.num_cores + lax.axis_index("core")
        pltpu.sync_copy(idx_hbm.at[worker], idx_v)
        base = worker * per_worker
        load_sems = (load_sem0, load_sem1)

        def load(c):
            return pltpu.async_copy(x_hbm.at[pl.ds(base + c * rows, rows)], rows_v.at[c % 2], load_sems[c % 2])

        loading = load(0)
        for c in range(n_chunks):
            loading.wait()
            if c + 1 < n_chunks:
                loading = load(c + 1)
            copies = [pltpu.async_copy(rows_v.at[c % 2], out_hbm.at[idx_v.at[c * TOP_K + k]], scatter_sem)
                      for k in range(TOP_K)]
            for cp in copies:
                cp.wait()

    return scatter_rows(x1, _sc_index(pos, n_workers, n_chunks, rows))


def _collect_rows(y_rows, pos):
    n_tok = pos.shape[0]
    w = y_rows.shape[1]
    rows = COLLECT_ROWS
    info, n_workers, per_worker, n_chunks = _sc_layout(n_tok, rows)
    mesh = plsc.VectorSubcoreMesh(core_axis_name="core", subcore_axis_name="subcore")

    @functools.partial(
        pl.kernel, out_type=jax.ShapeDtypeStruct((n_tok * TOP_K, w), y_rows.dtype), mesh=mesh,
        scratch_types=[pltpu.VMEM((n_chunks * TOP_K, rows), jnp.int32), pltpu.VMEM((2, TOP_K, rows, w), y_rows.dtype),
                       pltpu.SemaphoreType.DMA, pltpu.SemaphoreType.DMA],
        name="collect_rows")
    def gather_rows(y_hbm, idx_hbm, out_hbm, idx_v, rows_v, sem0, sem1):
        worker = lax.axis_index("subcore") * info.num_cores + lax.axis_index("core")
        pltpu.sync_copy(idx_hbm.at[worker], idx_v)
        base = worker * per_worker
        sems = (sem0, sem1)

        def gather(c):
            return [pltpu.async_copy(y_hbm.at[idx_v.at[c * TOP_K + k]], rows_v.at[c % 2, k], sems[c % 2])
                    for k in range(TOP_K)]

        pending = gather(0)
        for c in range(n_chunks):
            following = gather(c + 1) if c + 1 < n_chunks else []
            for cp in pending:
                cp.wait()
            for k in range(TOP_K):
                pltpu.sync_copy(rows_v.at[c % 2, k], out_hbm.at[pl.ds(k * n_tok + base + c * rows, rows)])
            pending = following

    return gather_rows(y_rows, _sc_index(pos, n_workers, n_chunks, rows))


def _moe(layer, x1, routing, wg, bg, wu, bu, wd, bd):
    n_tok = x1.shape[0]
    n_assign = n_tok * TOP_K
    n_blocks = n_assign // EXPERT_ROWS
    idx_o, rank_o, gate_o, cnt = routing
    counts = cnt[0, :N_EXPERTS]
    ends = jnp.cumsum(counts)
    offs = ends - counts
    experts = jnp.arange(N_EXPERTS, dtype=jnp.int32)
    top_i = idx_o[:, :TOP_K]
    off_tok = jnp.sum(jnp.where(top_i[:, :, None] == experts, offs, 0), axis=-1)
    pos = (off_tok + rank_o[:, :TOP_K]).reshape(n_assign)
    pos = pos.reshape(n_tok, TOP_K)
    xin = _dispatch_rows(x1, pos)

    first_blk = offs // EXPERT_ROWS
    last_blk = (ends - 1) // EXPERT_ROWS
    nvis = jnp.where(counts > 0, last_blk - first_blk + 1, 0)
    vend = jnp.cumsum(nvis)
    vstart = vend - nvis
    n_visits = n_blocks + N_EXPERTS - 1
    v = jnp.arange(n_visits, dtype=jnp.int32)
    e_v = jnp.sum((vend[None, :] <= v[:, None]).astype(jnp.int32), axis=1)
    e_last = jnp.max(jnp.where(counts > 0, experts, 0))
    e_v = jnp.minimum(e_v, e_last)
    pick = lambda tab: jnp.sum(jnp.where(e_v[:, None] == experts, tab, 0), axis=-1)
    blk_v = jnp.minimum(pick(first_blk) + v - pick(vstart), n_blocks - 1)
    active = counts > 0
    later = jnp.where(jnp.logical_and(active[None, :], experts[None, :] > experts[:, None]), experts[None, :], N_EXPERTS)
    nxt_e = jnp.min(later, axis=1)
    nxt_e = jnp.where(nxt_e == N_EXPERTS, experts, nxt_e)
    order_e = jnp.cumsum(active.astype(jnp.int32)) - 1
    y_rows = _experts(layer, blk_v.astype(jnp.int32), e_v.astype(jnp.int32), pick(offs).astype(jnp.int32),
                      pick(ends).astype(jnp.int32), vend[-1:].astype(jnp.int32),
                      pick(nxt_e).astype(jnp.int32), (pick(order_e) % 2).astype(jnp.int32),
                      xin, wg, bg, wu, bu, wd, bd)
    return y_rows, pos, gate_o


def _pack_w_in(w_in):
    qa, ka, va, oa, ia, fa, qb, fb, ib, gb = jnp.split(w_in, SPLIT_POINTS, axis=-1)

    def pad_heads(w):
        w = w.reshape(w.shape[:-1] + (N_HEADS, DK_A))
        w = jnp.pad(w, ((0, 0),) * (w.ndim - 1) + ((0, D_HEAD - DK_A),))
        return w.reshape(w.shape[:-2] + (N_HEADS * D_HEAD,))

    def pad_gate(w):
        return jnp.pad(w, ((0, 0),) * (w.ndim - 1) + ((0, LANES - N_HEADS),))

    cols = [pad_heads(qa), pad_heads(ka * (DK_A ** -0.5)), va, oa, qb, fb, ib, gb, pad_gate(ia), pad_gate(fa)]
    return jnp.concatenate(cols, axis=-1).astype(BF16)


def kernel(x_prompt, x_sample, state_mlstm_C, state_mlstm_n, state_mlstm_m, state_hgrn_S, p_prompt, p_sample,
           ln_in_g, ln_in_b, w_in, mlstm_ig_bias, mlstm_fg_bias, mlstm_norm_g, hgrn_lb_logits, hgrn_norm_g,
           w_out, ln1_g, ln1_b, w_router, b_router, w_gate, b_gate, w_up, b_up, w_down, b_down,
           w_ple_gate, w_ple_proj, ln2_g, ln2_b):
    bp, tp, d = x_prompt.shape
    bs, ts, _ = x_sample.shape
    n_p = bp * tp
    n_s = bs * ts
    ts_pad = SUBLANES
    assert tp % CHUNK == 0 and ts <= ts_pad and ts % CHUNK != 0

    lb_soft = jax.nn.softmax(hgrn_lb_logits.astype(F32), axis=0)
    lower_bounds = jnp.cumsum(lb_soft, axis=0) - lb_soft[0]

    w_in_p = _pack_w_in(w_in)
    gbias = jnp.stack([jnp.pad(mlstm_ig_bias, ((0, 0), (0, LANES - N_HEADS))),
                       jnp.pad(mlstm_fg_bias, ((0, 0), (0, LANES - N_HEADS)))], axis=1)
    w_out_b = w_out.astype(BF16)
    w_pg_b = w_ple_gate.astype(BF16)
    w_pp_b = w_ple_proj.astype(BF16)
    w_r = jnp.pad(w_router, ((0, 0), (0, 0), (0, LANES - N_EXPERTS)))
    w_r_hi = w_r.astype(BF16)
    w_r2 = jnp.concatenate([w_r_hi, (w_r - w_r_hi.astype(F32)).astype(BF16)], axis=-1)
    b_r = jnp.pad(b_router, ((0, 0), (0, LANES - N_EXPERTS))).reshape(DEPTH, 1, LANES)
    n0_pad = jnp.pad(state_mlstm_n, ((0, 0), (0, 0), (0, 0), (0, D_HEAD - DK_A))).reshape(
        DEPTH, bs, N_HEADS, 1, D_HEAD)
    m0_pad = jnp.pad(state_mlstm_m, ((0, 0), (0, 0), (0, LANES - N_HEADS))).reshape(DEPTH, bs, 1, LANES)

    p_p = p_prompt.reshape(DEPTH, n_p, D_PLE)
    p_s = p_sample.reshape(DEPTH, n_s, D_PLE)
    x = _ln(x_prompt.reshape(n_p, d), x_sample.reshape(n_s, d), ln_in_g, ln_in_b)

    assert n_p % ROW_TILE == 0 and n_s % ROW_TILE == 0
    nb_p = LANES // CHUNK
    nb_s = LANES // ts_pad
    st_p = _stacked_states(bp)
    st_s = _stacked_states(bs)
    for l in range(DEPTH):
        lb = lower_bounds[l].reshape(1, MIX)
        nga = mlstm_norm_g[l].reshape(1, MIX)
        ngb = hgrn_norm_g[l].reshape(1, MIX)
        proj_p = _inproj(x, w_in_p, l, 0, n_p).reshape(bp, tp, N_PROJ)
        h_p, st_p = _scan(proj_p, gbias[l], lb, nga, ngb, None, st_p, layer=l,
                          NB=nb_p, n_chunks=tp // CHUNK, L=CHUNK, last=CHUNK - 1, h_dtype=BF16)
        proj_s = jnp.pad(_inproj(x, w_in_p, l, n_p, n_s).reshape(bs, ts, N_PROJ), ((0, 0), (0, ts_pad - ts), (0, 0)))
        h_s, st_s = _scan(proj_s, gbias[l], lb, nga, ngb,
                          (state_mlstm_C, n0_pad, m0_pad, state_hgrn_S), st_s, layer=l,
                          NB=nb_s, n_chunks=1, L=ts_pad, last=ts - 1)
        x1, resid, *routing = _post_mixer(l, x, h_p.reshape(n_p, d), h_s[:, :ts].reshape(n_s, d), p_p, p_s,
                                          w_out_b[l], ln1_g[l], ln1_b[l], w_pg_b[l], w_pp_b[l],
                                          w_r2[l], w_r_hi[l], b_r[l])
        y_rows, pos, gates = _moe(l, x1, routing, w_gate, b_gate.reshape(DEPTH, N_EXPERTS, 1, d),
                                  w_up, b_up.reshape(DEPTH, N_EXPERTS, 1, d),
                                  w_down, b_down.reshape(DEPTH, N_EXPERTS, 1, d))
        n_part = (n_p + n_s) // COMBINE_PARTS
        assert n_part * COMBINE_PARTS == n_p + n_s and n_part % ROW_TILE == 0
        for part in range(COMBINE_PARTS):
            y_k = _collect_rows(y_rows, pos[part * n_part:(part + 1) * n_part])
            x = _combine_ln(resid, y_k, gates, ln2_g[l], ln2_b[l], x, part * n_part)

    y_prompt = x[:n_p].reshape(bp, tp, d)
    y_sample = x[n_p:].reshape(bs, ts, d)
    unpad = lambda st: (st[0], st[1][:, :, :, 0, :DK_A], st[2][:, :, 0, :N_HEADS], st[3])
    return (y_prompt, y_sample) + unpad(st_p) + unpad(st_s)
```

```python
import functools

import jax
import jax.numpy as jnp
import numpy as np
from jax import lax
from jax.experimental import pallas as pl
from jax.experimental.pallas import tpu as pltpu
from jax.experimental.pallas import tpu_sc as plsc

F32 = jnp.float32
BF16 = jnp.bfloat16

D_MODEL = 1024
DEPTH = 4
D_PLE = 256
N_HEADS = 4
DK_A = 64
D_HEAD = 128
MIX = 512
N_EXPERTS = 32
TOP_K = 4
SWIGLU_LIMIT = 7.0
SWIGLU_ALPHA = 1.702
CHUNK = 64
LN_EPS = 1e-5
NORM_EPS = 1e-6
LB_FLOOR = 1e-20
DEEPNORM_ALPHA = (2 * DEPTH) ** 0.25
SPLIT_SIZES = (256, 256, 512, 512, 4, 4, 512, 512, 512, 512)
SPLIT_POINTS = tuple(int(s) for s in np.cumsum(SPLIT_SIZES)[:-1])

LANES = 128
SUBLANES = 8
VMEM_LIMIT = 56 * 1024 * 1024

QA0, KA0, VA0, OA0 = 0, 512, 1024, 1536
QB0, FB0, IB0, GB0 = 2048, 2560, 3072, 3584
IG0, FG0 = 4096, 4224
N_PROJ = 4352

ROW_TILE = 512
EXPERT_ROWS = 256

NT_DIMS = (((1,), (1,)), ((), ()))
TN_DIMS = (((0,), (0,)), ((), ()))


def _params(*sem):
    return pltpu.CompilerParams(dimension_semantics=sem, vmem_limit_bytes=VMEM_LIMIT)


def _log_sigmoid(x):
    return jnp.minimum(x, 0.0) - jnp.log(1.0 + jnp.exp(-jnp.abs(x)))


def _split3(x):
    x1 = x.astype(BF16)
    r1 = x - x1.astype(F32)
    x2 = r1.astype(BF16)
    x3 = (r1 - x2.astype(F32)).astype(BF16)
    return x1, x2, x3


def _dot01(sel, x):
    return sum(jnp.dot(sel, t, preferred_element_type=F32) for t in _split3(x))


def _pack_pairs(x):
    w = x.shape[1] // 2
    bits = lax.bitcast_convert_type(x.astype(BF16).astype(F32), jnp.uint32)
    return lax.bitcast_convert_type(bits[:, :w] | (bits[:, w:] >> 16), jnp.int32)


def _unpack_pairs(words):
    u = lax.bitcast_convert_type(words, jnp.uint32)
    hi = lax.bitcast_convert_type(u & jnp.uint32(0xFFFF0000), F32)
    lo = lax.bitcast_convert_type(u << 16, F32)
    return hi, lo


def _layernorm_rows(x, g, b):
    mu = jnp.mean(x, axis=-1, keepdims=True)
    xc = x - mu
    var = jnp.mean(xc * xc, axis=-1, keepdims=True)
    return xc * lax.rsqrt(var + LN_EPS) * g + b


def _two_sources(n_a, n_b, width, lead=()):
    ta = n_a // ROW_TILE
    assert n_a % ROW_TILE == 0 and n_b % ROW_TILE == 0
    block = (None,) * len(lead) + (ROW_TILE, width)
    spec_a = pl.BlockSpec(block, lambda i: lead + (jnp.minimum(i, ta - 1), 0))
    spec_b = pl.BlockSpec(block, lambda i: lead + (jnp.maximum(i - ta, 0), 0))
    return spec_a, spec_b, ta


def _ln_kernel(xa_ref, xb_ref, g_ref, b_ref, o_ref, *, tiles_a):
    x = jnp.where(pl.program_id(0) < tiles_a, xa_ref[...], xb_ref[...])
    o_ref[...] = _layernorm_rows(x, g_ref[...], b_ref[...])


def _ln(xa, xb, g, b):
    d = xa.shape[1]
    n = xa.shape[0] + xb.shape[0]
    spec_a, spec_b, ta = _two_sources(xa.shape[0], xb.shape[0], d)
    vec = pl.BlockSpec((1, d), lambda i: (0, 0))
    return pl.pallas_call(
        functools.partial(_ln_kernel, tiles_a=ta), grid=(n // ROW_TILE,), in_specs=[spec_a, spec_b, vec, vec],
        out_specs=pl.BlockSpec((ROW_TILE, d), lambda i: (i, 0)),
        out_shape=jax.ShapeDtypeStruct((n, d), F32), compiler_params=_params("parallel"),
        name="ln_in")(xa, xb, g.reshape(1, d), b.reshape(1, d))


def _combine_ln_kernel(resid_ref, y0_ref, y1_ref, y2_ref, y3_ref, gate_ref, g_ref, beta_ref, buf_ref, o_ref):
    half = resid_ref.shape[1] // 2
    acc_hi = resid_ref[:, :half]
    acc_lo = resid_ref[:, half:]
    for k, y_ref in enumerate((y0_ref, y1_ref, y2_ref, y3_ref)):
        hi, lo = _unpack_pairs(y_ref[...])
        gk = gate_ref[:, k:k + 1]
        acc_hi = acc_hi + gk * hi
        acc_lo = acc_lo + gk * lo
    o_ref[...] = _layernorm_rows(jnp.concatenate([acc_hi, acc_lo], axis=1), g_ref[...], beta_ref[...])


def _combine_ln(resid, y_k, gates, g, beta, buf, row0):
    n, d = resid.shape
    n_part = y_k.shape[0] // TOP_K
    nt = n_part // ROW_TILE
    t0 = row0 // ROW_TILE
    row = lambda w: pl.BlockSpec((ROW_TILE, w), lambda i: (t0 + i, 0))
    yk = lambda k: pl.BlockSpec((ROW_TILE, d // 2), lambda i: (k * nt + i, 0))
    vec = pl.BlockSpec((1, d), lambda i: (0, 0))
    return pl.pallas_call(
        _combine_ln_kernel, grid=(nt,),
        in_specs=[row(d)] + [yk(k) for k in range(TOP_K)] + [row(LANES), vec, vec, pl.BlockSpec(memory_space=pl.ANY)],
        out_specs=row(d), out_shape=jax.ShapeDtypeStruct((n, d), F32), input_output_aliases={8: 0},
        compiler_params=_params("parallel"),
        name="combine_ln2")(resid, y_k, y_k, y_k, y_k, gates, g.reshape(1, d), beta.reshape(1, d), buf)


def _inproj_kernel(x_ref, w_ref, o_ref):
    o_ref[...] = jnp.dot(x_ref[...].astype(BF16), w_ref[...], preferred_element_type=F32)


def _inproj(x, w, layer, row0, n_rows):
    d = x.shape[1]
    t0 = row0 // ROW_TILE
    return pl.pallas_call(
        _inproj_kernel, grid=(n_rows // ROW_TILE,),
        in_specs=[pl.BlockSpec((ROW_TILE, d), lambda i: (t0 + i, 0)),
                  pl.BlockSpec((None, d, N_PROJ), lambda i: (layer, 0, 0))],
        out_specs=pl.BlockSpec((ROW_TILE, N_PROJ), lambda i: (i, 0)),
        out_shape=jax.ShapeDtypeStruct((n_rows, N_PROJ), F32), compiler_params=_params("parallel"),
        name="in_proj")(x, w)


def _scan_kernel(*refs, NB, L, last, has_state):
    it = iter(refs)
    proj_ref, gbias_ref, lb_ref, nga_ref, ngb_ref = (next(it) for _ in range(5))
    if has_state:
        C0_ref, n0_ref, m0_ref, S0_ref = (next(it) for _ in range(4))
    for _ in range(4):
        next(it)
    h_ref, C_out, n_out, m_out, S_out = (next(it) for _ in range(5))
    C_s, n_s, m_s, St_s = (next(it) for _ in range(4))

    c = pl.program_id(1)
    nc = pl.num_programs(1)
    H = range(N_HEADS)
    N = range(NB)
    HEAD_GROUPS = (range(0, 2), range(2, 4))
    R = NB * L
    assert R == LANES

    @pl.when(c == 0)
    def _init():
        if has_state:
            for n in N:
                for h in H:
                    C_s[n, h, 0:DK_A, :] = C0_ref[n, h]
                    C_s[n, h, DK_A:D_HEAD, :] = jnp.zeros((D_HEAD - DK_A, D_HEAD), F32)
                    St_s[n, h] = S0_ref[n, h].T
            n_s[...] = n0_ref[...]
            m_s[...] = m0_ref[...]
        else:
            C_s[...] = jnp.zeros(C_s.shape, F32)
            n_s[...] = jnp.zeros(n_s.shape, F32)
            m_s[...] = jnp.zeros(m_s.shape, F32)
            St_s[...] = jnp.zeros(St_s.shape, F32)

    row = lax.broadcasted_iota(jnp.int32, (R, R), 0)
    col = lax.broadcasted_iota(jnp.int32, (R, R), 1)
    causal = jnp.logical_and(row // L == col // L, row >= col)
    tril = causal.astype(BF16)
    ones_r = jnp.ones((R, LANES), BF16)
    lane = lax.broadcasted_iota(jnp.int32, (R, LANES), 1)
    step = lax.broadcasted_iota(jnp.int32, (R, 1), 0) % L
    valid = step <= last
    padded = last < L - 1
    tile = lambda base, h: slice(base + LANES * h, base + LANES * (h + 1))
    seq = lambda n: slice(n * L, (n + 1) * L)
    cols = lambda base, width: proj_ref[:, :, base:base + width].reshape(R, width)
    per_seq = lambda vals: jnp.concatenate([jnp.broadcast_to(x, (L, x.shape[-1])) for x in vals], axis=0)
    last_rows = lambda x: [x[n * L + last:n * L + last + 1] for n in N]

    i_t = cols(IG0, LANES) + gbias_ref[0:1, :]
    f_t = _log_sigmoid(cols(FG0, LANES) + gbias_ref[1:2, :])
    b_t = _dot01(tril, f_t)
    m_all = [m_s[n] for n in N]
    m_fin = {}
    for H in HEAD_GROUPS:
        g0 = LANES * H[0]
        GW = LANES * len(H)
        lt = lambda h, H=H: slice(LANES * (h - H[0]), LANES * (h - H[0] + 1))
        zf = cols(FB0 + g0, GW)
        lb = lb_ref[:, g0:g0 + GW]
        la = jnp.log(jnp.maximum(lb, LB_FLOOR))
        bb = jnp.log1p(-lb) + _log_sigmoid(zf)
        f_log = jnp.maximum(la, bb) + jnp.log(1.0 + jnp.exp(-jnp.abs(la - bb)))
        g = _dot01(tril, f_log)
        C_prev = [{h: C_s[n, h] for h in H} for n in N]
        n_prev = [{h: n_s[n, h] for h in H} for n in N]
        St = [{h: St_s[n, h] for h in H} for n in N]
        qf = {h: cols(QA0 + LANES * h, LANES) for h in H}
        kf = {h: cols(KA0 + LANES * h, LANES) for h in H}
        q = {h: x.astype(BF16) for h, x in qf.items()}
        k = {h: x.astype(BF16) for h, x in kf.items()}
        v = {h: cols(VA0 + LANES * h, LANES).astype(BF16) for h in H}
        kb = (1.0 - lb) * jax.nn.sigmoid(-zf)
        qr = cols(QB0 + g0, GW)
        qb = qr * jax.nn.sigmoid(qr)
        vb = {h: cols(IB0 + LANES * h, LANES).astype(BF16) for h in H}
        qb16 = qb.astype(BF16)
        kb16 = kb.astype(BF16)
        r_t = i_t - b_t
        zero16 = jnp.zeros((R, LANES), BF16)
        r_all = sum(
            lax.dot_general(ones_r, jnp.concatenate([jnp.where(lane == h, term, zero16) for h in H], axis=0),
                            NT_DIMS, preferred_element_type=F32)
            for term in _split3(r_t))
        r_m = {h: r_all[:, R * (h - H[0]):R * (h - H[0] + 1)] for h in H}
        qk = {h: lax.dot_general(q[h], k[h], NT_DIMS, preferred_element_type=F32) for h in H}
        qC = {h: jnp.concatenate([jnp.dot(q[h][seq(n)], C_prev[n][h].astype(BF16), preferred_element_type=F32)
                               for n in N], axis=0) for h in H}
        levels = []
        bs = 1
        while bs < L:
            levels.append(bs)
            bs *= 2
        small = [bs for bs in levels if bs < SUBLANES]
        g_anchor = {}
        if small:
            sel = jnp.concatenate(
                [(col == (row // (2 * bs)) * (2 * bs) + bs).astype(BF16) for bs in small], axis=0)
            picked = _dot01(sel, g)
            for i, bs in enumerate(small):
                g_anchor[bs] = picked[i * R:(i + 1) * R]
        for bs in levels:
            if bs >= SUBLANES:
                g_anchor[bs] = jnp.concatenate(
                    [jnp.broadcast_to(g[p0 + bs:p0 + bs + 1], (2 * bs, GW)) for p0 in range(0, R, 2 * bs)], axis=0)
        qg = (qb * jnp.exp(g)).astype(BF16)
        o_inter = {h: jnp.concatenate([lax.dot_general(qg[seq(n), lt(h)], St[n][h].astype(BF16), NT_DIMS,
                                                    preferred_element_type=F32) for n in N], axis=0) for h in H}
        a = {h: jnp.where(row == col, lax.dot_general(qb16[:, lt(h)], kb16[:, lt(h)], NT_DIMS,
                                                    preferred_element_type=F32), 0.0) for h in H}
        qn = {h: jnp.sum(qf[h] * per_seq([n_prev[n][h] for n in N]), axis=1, keepdims=True) for h in H}
        b_col = {h: b_t[:, h:h + 1] for h in H}
        i_col = {h: i_t[:, h:h + 1] for h in H}
        m_prev = {h: per_seq([m_all[n][:, h:h + 1] for n in N]) for h in H}
        cm = {h: jnp.max(jnp.where(causal, r_m[h], -jnp.inf), axis=1, keepdims=True) for h in H}
        m_t = {h: b_col[h] + jnp.maximum(m_prev[h], cm[h]) for h in H}
        inter = {h: jnp.exp(b_col[h] + m_prev[h] - m_t[h]) for h in H}
        d = {h: jnp.exp(jnp.where(causal, (b_col[h] - m_t[h]) + r_m[h], -1e30)) for h in H}
        s = {h: qk[h] * d[h] for h in H}
        log2e = 1.4426950408889634
        e = [jnp.exp2((g - g_anchor[bs]) * jnp.where((step // bs) % 2 == 1, log2e, -log2e)) for bs in levels]
        qe = [(qb * x).astype(BF16) for x in e]
        ke = [(kb * x).astype(BF16) for x in e]
        sv = {h: jnp.dot(s[h].astype(BF16), v[h], preferred_element_type=F32) for h in H}
        sc = [{h: lax.dot_general(qe[i][:, lt(h)], ke[i][:, lt(h)], NT_DIMS, preferred_element_type=F32)
               for h in H} for i in range(len(levels))]
        den = {h: jnp.sum(s[h], axis=1, keepdims=True) + inter[h] * qn[h] for h in H}
        m_new = {h: last_rows(m_t[h]) for h in H}
        m_fin.update(m_new)
        b_last = {h: last_rows(b_col[h]) for h in H}
        w_arg = {h: per_seq(b_last[h]) - b_col[h] + i_col[h] - per_seq(m_new[h]) for h in H}
        if padded:
            w_arg = {h: jnp.where(valid, x, -1e30) for h, x in w_arg.items()}
        kw = {h: kf[h] * jnp.exp(w_arg[h]) for h in H}
        kw16 = {h: x.astype(BF16) for h, x in kw.items()}
        decay = [{h: jnp.exp(b_last[h][n] + m_all[n][:, h:h + 1] - m_new[h][n]) for h in H} for n in N]
        kv = [{h: lax.dot_general(kw16[h][seq(n)], v[h][seq(n)], TN_DIMS, preferred_element_type=F32) for h in H}
              for n in N]
        for i, bs in enumerate(levels):
            pair = jnp.logical_and(row // (2 * bs) == col // (2 * bs),
                                   jnp.logical_and((row // bs) % 2 == 1, (col // bs) % 2 == 0))
            for h in H:
                a[h] = jnp.where(pair, sc[i][h], a[h])
        o = {h: o_inter[h] + jnp.dot(a[h].astype(BF16), vb[h], preferred_element_type=F32) for h in H}
        g_last = last_rows(g)
        dec_arg = per_seq(g_last) - g
        if padded:
            dec_arg = jnp.where(valid, dec_arg, -1e30)
        kdec = (kb * jnp.exp(dec_arg)).astype(BF16)
        eg_last = [jnp.exp(x) for x in g_last]
        vk = [{h: lax.dot_general(vb[h][seq(n)], kdec[seq(n), lt(h)], TN_DIMS, preferred_element_type=F32)
               for h in H} for n in N]
        hh = {h: (sv[h] + inter[h] * qC[h]) / jnp.maximum(jnp.abs(den[h]), jnp.exp(-m_t[h])) for h in H}
        for h in H:
            hn = hh[h] * lax.rsqrt(jnp.mean(hh[h] * hh[h], axis=-1, keepdims=True) + NORM_EPS) * nga_ref[:, tile(0, h)]
            out_a = jax.nn.sigmoid(cols(OA0 + LANES * h, LANES)) * hn
            h_ref[:, :, tile(0, h)] = out_a.reshape(NB, L, LANES)
        for h in H:
            on = o[h] * lax.rsqrt(jnp.mean(o[h] * o[h], axis=-1, keepdims=True) + NORM_EPS) * ngb_ref[:, tile(0, h)]
            gr = cols(GB0 + LANES * h, LANES)
            h_ref[:, :, tile(MIX, h)] = (gr * jax.nn.sigmoid(gr) * on).reshape(NB, L, LANES)
        for n in N:
            for h in H:
                C_s[n, h] = decay[n][h] * C_prev[n][h] + kv[n][h]
                n_s[n, h] = decay[n][h] * n_prev[n][h] + jnp.sum(kw[h][seq(n)], axis=0, keepdims=True)
                St_s[n, h] = eg_last[n][:, lt(h)] * St[n][h] + vk[n][h]

    H = range(N_HEADS)
    lane1 = lax.broadcasted_iota(jnp.int32, (1, LANES), 1)
    for n in N:
        m_upd = m_all[n]
        for h in H:
            m_upd = jnp.where(lane1 == h, m_fin[h][n], m_upd)
        m_s[n] = m_upd

    @pl.when(c == nc - 1)
    def _finish():
        for n in N:
            for h in H:
                C_out[n, h] = C_s[n, h, 0:DK_A, :]
                S_out[n, h] = St_s[n, h].T
        n_out[...] = n_s[...]
        m_out[...] = m_s[...]


def _scan(proj, gbias, lb, nga, ngb, state, stacked, *, layer, NB, n_chunks, L, last):
    batch = proj.shape[0]
    has_state = state is not None
    const2 = lambda b, c: (0, 0)
    in_specs = [
        pl.BlockSpec((NB, L, N_PROJ), lambda b, c: (b, c, 0)),
        pl.BlockSpec((2, LANES), const2),
        pl.BlockSpec((1, MIX), const2),
        pl.BlockSpec((1, MIX), const2),
        pl.BlockSpec((1, MIX), const2),
    ]
    args = [proj, gbias, lb, nga, ngb]
    lst4 = lambda b, c: (layer, b, 0, 0, 0)
    lst3 = lambda b, c: (layer, b, 0, 0)
    if has_state:
        in_specs += [
            pl.BlockSpec((None, NB, N_HEADS, DK_A, D_HEAD), lst4),
            pl.BlockSpec((None, NB, N_HEADS, 1, LANES), lst4),
            pl.BlockSpec((None, NB, 1, LANES), lst3),
            pl.BlockSpec((None, NB, N_HEADS, D_HEAD, D_HEAD), lst4),
        ]
        args += list(state)
    aliases = {len(args) + i: 1 + i for i in range(4)}
    in_specs += [pl.BlockSpec(memory_space=pl.ANY)] * 4
    args += list(stacked)
    out_specs = [
        pl.BlockSpec((NB, L, D_MODEL), lambda b, c: (b, c, 0)),
        pl.BlockSpec((None, NB, N_HEADS, DK_A, D_HEAD), lst4),
        pl.BlockSpec((None, NB, N_HEADS, 1, LANES), lst4),
        pl.BlockSpec((None, NB, 1, LANES), lst3),
        pl.BlockSpec((None, NB, N_HEADS, D_HEAD, D_HEAD), lst4),
    ]
    out_shape = [jax.ShapeDtypeStruct((batch, n_chunks * L, D_MODEL), F32)] + [
        jax.ShapeDtypeStruct(a.shape, a.dtype) for a in stacked]
    scratch = [
        pltpu.VMEM((NB, N_HEADS, D_HEAD, D_HEAD), F32),
        pltpu.VMEM((NB, N_HEADS, 1, LANES), F32),
        pltpu.VMEM((NB, 1, LANES), F32),
        pltpu.VMEM((NB, N_HEADS, D_HEAD, D_HEAD), F32),
    ]
    kern = functools.partial(_scan_kernel, NB=NB, L=L, last=last, has_state=has_state)
    res = pl.pallas_call(
        kern, grid=(batch // NB, n_chunks), in_specs=in_specs, out_specs=out_specs, out_shape=out_shape,
        scratch_shapes=scratch, input_output_aliases=aliases,
        compiler_params=_params("parallel", "arbitrary"),
        name="scan_state" if has_state else "scan_prompt")(*args)
    return res[0], tuple(res[1:])


def _stacked_states(batch):
    return (jnp.zeros((DEPTH, batch, N_HEADS, DK_A, D_HEAD), F32),
            jnp.zeros((DEPTH, batch, N_HEADS, 1, LANES), F32),
            jnp.zeros((DEPTH, batch, 1, LANES), F32),
            jnp.zeros((DEPTH, batch, N_HEADS, D_HEAD, D_HEAD), F32))


def _post_mixer_kernel(x_ref, ha_ref, hb_ref, pa_ref, pb_ref, wout_ref, g1_ref, b1_ref, wpg_ref, wpp_ref, wr2_ref,
                       wrh_ref, br_ref, x1_ref, resid_ref, idx_ref, rank_ref, gate_ref, cnt_ref, carry_s, *, tiles_a):
    i = pl.program_id(0)

    @pl.when(i == 0)
    def _init():
        carry_s[...] = jnp.zeros(carry_s.shape, F32)

    from_a = i < tiles_a
    h = jnp.where(from_a, ha_ref[...], hb_ref[...])
    p = jnp.where(from_a, pa_ref[...], pb_ref[...])
    mix = jnp.dot(h.astype(BF16), wout_ref[...], preferred_element_type=F32)
    x1 = _layernorm_rows(DEEPNORM_ALPHA * x_ref[...] + mix, g1_ref[...], b1_ref[...])
    x1b = x1.astype(BF16)
    x1_ref[...] = _pack_pairs(x1)
    x1_lo = (x1 - x1b.astype(F32)).astype(BF16)
    l2 = jnp.dot(x1b, wr2_ref[...], preferred_element_type=F32)
    logits = (l2[:, :LANES] + l2[:, LANES:]
              + jnp.dot(x1_lo, wrh_ref[...], preferred_element_type=F32) + br_ref[...])

    tm = logits.shape[0]
    lane = lax.broadcasted_iota(jnp.int32, (tm, LANES), 1)
    lane_f = lane.astype(F32)
    l = jnp.where(lane < N_EXPERTS, logits, -jnp.inf)
    vals, idxs, onehots = [], [], []
    for _ in range(TOP_K):
        mx = jnp.max(l, axis=1, keepdims=True)
        ix = jnp.min(jnp.where(l == mx, lane_f, float(LANES)), axis=1, keepdims=True)
        sel = lane_f == ix
        vals.append(mx)
        idxs.append(ix)
        onehots.append(sel.astype(F32))
        l = jnp.where(sel, -jnp.inf, l)
    gate = jax.nn.sigmoid(jnp.dot(x1b, wpg_ref[...], preferred_element_type=F32))
    pp = jnp.dot(p.astype(BF16), wpp_ref[...], preferred_element_type=F32)
    resid_ref[...] = DEEPNORM_ALPHA * x1 + gate * pp

    w = [jnp.exp(val - vals[0]) for val in vals]
    tot = w[0] + w[1] + w[2] + w[3]
    oh_all = onehots[0] + onehots[1] + onehots[2] + onehots[3]
    row = lax.broadcasted_iota(jnp.int32, (tm, tm), 0)
    col = lax.broadcasted_iota(jnp.int32, (tm, tm), 1)
    earlier = (row > col).astype(BF16)
    prefix = jnp.dot(earlier, oh_all.astype(BF16), preferred_element_type=F32) + carry_s[...]
    idx_o = jnp.zeros((tm, LANES), F32)
    rank_o = jnp.zeros((tm, LANES), F32)
    gate_o = jnp.zeros((tm, LANES), F32)
    for k in range(TOP_K):
        rank_k = jnp.sum(onehots[k] * prefix, axis=1, keepdims=True)
        idx_o = jnp.where(lane == k, idxs[k], idx_o)
        rank_o = jnp.where(lane == k, rank_k, rank_o)
        gate_o = jnp.where(lane == k, w[k] / tot, gate_o)
    idx_ref[...] = idx_o.astype(jnp.int32)
    rank_ref[...] = rank_o.astype(jnp.int32)
    gate_ref[...] = gate_o
    carry_s[...] = carry_s[...] + jnp.sum(oh_all, axis=0, keepdims=True)
    cnt_ref[...] = carry_s[...].astype(jnp.int32)


def _post_mixer(layer, x, ha, hb, pa, pb, wout, g1, b1, wpg, wpp, wr2, wrh, br):
    n, d = x.shape
    row = lambda w: pl.BlockSpec((ROW_TILE, w), lambda i: (i, 0))
    full = lambda a: pl.BlockSpec(a.shape, lambda i: (0, 0))
    ha_spec, hb_spec, ta = _two_sources(ha.shape[0], hb.shape[0], d)
    pa_spec, pb_spec, _ = _two_sources(pa.shape[1], pb.shape[1], D_PLE, lead=(layer,))
    g1 = g1.reshape(1, d)
    b1 = b1.reshape(1, d)
    return pl.pallas_call(
        functools.partial(_post_mixer_kernel, tiles_a=ta), grid=(n // ROW_TILE,),
        in_specs=[row(d), ha_spec, hb_spec, pa_spec, pb_spec, full(wout), full(g1), full(b1), full(wpg), full(wpp),
                  full(wr2), full(wrh), full(br)],
        out_specs=[row(d // 2), row(d), row(LANES), row(LANES), row(LANES), pl.BlockSpec((1, LANES), lambda i: (0, 0))],
        out_shape=[jax.ShapeDtypeStruct((n, d // 2), jnp.int32), jax.ShapeDtypeStruct((n, d), F32),
                   jax.ShapeDtypeStruct((n, LANES), jnp.int32), jax.ShapeDtypeStruct((n, LANES), jnp.int32),
                   jax.ShapeDtypeStruct((n, LANES), F32), jax.ShapeDtypeStruct((1, LANES), jnp.int32)],
        scratch_shapes=[pltpu.VMEM((1, LANES), F32)],
        compiler_params=_params("arbitrary"), name="post_mixer")(
            x, ha, hb, pa, pb, wout, g1, b1, wpg, wpp, wr2, wrh, br)


def _expert_kernel(blk_ref, e_ref, lo_ref, hi_ref, nv_ref, nxt_ref, slot_ref, x_ref, wg_hbm, bg_ref, wu_hbm, bu_ref,
                   wd_hbm, bd_ref, o_ref, w32_s, wg_s, wu_s, wd_s, sem, *, layer):
    v = pl.program_id(0)
    prev = jnp.maximum(v - 1, 0)
    valid = v < nv_ref[0]
    new_expert = jnp.logical_or(v == 0, e_ref[v] != e_ref[prev])
    new_block = jnp.logical_or(v == 0, blk_ref[v] != blk_ref[prev])
    slot = slot_ref[v]

    def weight_copies(expert, to_slot):
        return [pltpu.make_async_copy(w_hbm.at[layer, expert], w32_s.at[to_slot, j], sem.at[to_slot, j])
                for j, w_hbm in enumerate((wg_hbm, wu_hbm, wd_hbm))]

    @pl.when(v == 0)
    def _prime():
        for cp in weight_copies(e_ref[0], slot):
            cp.start()

    @pl.when(jnp.logical_and(valid, new_expert))
    def _next_weights():
        for cp in weight_copies(e_ref[v], slot):
            cp.wait()

        @pl.when(nxt_ref[v] != e_ref[v])
        def _prefetch():
            for cp in weight_copies(nxt_ref[v], 1 - slot):
                cp.start()

        wg_s[...] = w32_s[slot, 0].astype(BF16)
        wu_s[...] = w32_s[slot, 1].astype(BF16)
        wd_s[...] = w32_s[slot, 2].astype(BF16)

    @pl.when(valid)
    def _compute():
        half = wg_s.shape[0] // 2
        x_hi, x_lo = (t.astype(BF16) for t in _unpack_pairs(x_ref[...]))
        gate = (jnp.dot(x_hi, wg_s[0:half, :], preferred_element_type=F32)
                + jnp.dot(x_lo, wg_s[half:, :], preferred_element_type=F32) + bg_ref[0])
        up = (jnp.dot(x_hi, wu_s[0:half, :], preferred_element_type=F32)
              + jnp.dot(x_lo, wu_s[half:, :], preferred_element_type=F32) + bu_ref[0])
        gate = jnp.minimum(gate, SWIGLU_LIMIT)
        up = jnp.clip(up, -SWIGLU_LIMIT, SWIGLU_LIMIT)
        glu = gate * jax.nn.sigmoid(gate * SWIGLU_ALPHA)
        act = ((up + 1.0) * glu).astype(BF16)
        y = _pack_pairs(jnp.dot(act, wd_s[...], preferred_element_type=F32) + bd_ref[0])
        rows = blk_ref[v] * EXPERT_ROWS + lax.broadcasted_iota(jnp.int32, (EXPERT_ROWS, 1), 0)
        mine = jnp.logical_and(rows >= lo_ref[v], rows < hi_ref[v])

        @pl.when(new_block)
        def _first():
            o_ref[...] = jnp.where(mine, y, 0)

        @pl.when(jnp.logical_not(new_block))
        def _again():
            o_ref[...] = jnp.where(mine, y, o_ref[...])


def _experts(layer, blk_v, e_v, lo_v, hi_v, n_vis, nxt_v, slot_v, xin, wg, bg, wu, bu, wd, bd):
    n_rows = xin.shape[0]
    d = wg.shape[-1]
    n_visits = blk_v.shape[0]
    xspec = pl.BlockSpec((EXPERT_ROWS, d // 2), lambda v, blk, e, *_: (blk[v], 0))
    ospec = pl.BlockSpec((EXPERT_ROWS, d // 2), lambda v, blk, e, *_: (blk[v], 0))
    wspec = pl.BlockSpec(memory_space=pl.ANY)
    bspec = pl.BlockSpec((None, 1, 1, d), lambda v, blk, e, *_: (layer, e[v], 0, 0))
    grid_spec = pltpu.PrefetchScalarGridSpec(
        num_scalar_prefetch=7, grid=(n_visits,),
        in_specs=[xspec, wspec, bspec, wspec, bspec, wspec, bspec], out_specs=ospec,
        scratch_shapes=[pltpu.VMEM((2, 3, d, d), F32)] + [pltpu.VMEM((d, d), BF16)] * 3
        + [pltpu.SemaphoreType.DMA((2, 3))])
    return pl.pallas_call(
        functools.partial(_expert_kernel, layer=layer), grid_spec=grid_spec,
        out_shape=jax.ShapeDtypeStruct((n_rows, d // 2), jnp.int32),
        compiler_params=_params("arbitrary"), name="experts")(
            blk_v, e_v, lo_v, hi_v, n_vis, nxt_v, slot_v, xin, wg, bg, wu, bu, wd, bd)


DISPATCH_ROWS = 48
COLLECT_ROWS = 24
COMBINE_PARTS = 1


def _sc_layout(n_tok, chunk_rows):
    info = plsc.get_sparse_core_info()
    n_workers = info.num_cores * info.num_subcores
    per_worker = n_tok // n_workers
    n_chunks = per_worker // chunk_rows
    assert per_worker * n_workers == n_tok and n_chunks * chunk_rows == per_worker
    return info, n_workers, per_worker, n_chunks


def _sc_index(pos, n_workers, n_chunks, chunk_rows):
    return pos.T.reshape(TOP_K, n_workers, n_chunks, chunk_rows).transpose(1, 2, 0, 3).reshape(
        n_workers, n_chunks * TOP_K, chunk_rows)


def _dispatch_rows(x1, pos):
    n_tok, w = x1.shape
    rows = DISPATCH_ROWS
    info, n_workers, per_worker, n_chunks = _sc_layout(n_tok, rows)
    mesh = plsc.VectorSubcoreMesh(core_axis_name="core", subcore_axis_name="subcore")

    @functools.partial(
        pl.kernel, out_type=jax.ShapeDtypeStruct((n_tok * TOP_K, w), x1.dtype), mesh=mesh,
        scratch_types=[pltpu.VMEM((n_chunks * TOP_K, rows), jnp.int32), pltpu.VMEM((2, rows, w), x1.dtype),
                       pltpu.SemaphoreType.DMA, pltpu.SemaphoreType.DMA, pltpu.SemaphoreType.DMA],
        name="dispatch_rows")
    def scatter_rows(x_hbm, idx_hbm, out_hbm, idx_v, rows_v, load_sem0, load_sem1, scatter_sem):
        worker = lax.axis_index("subcore") * info.num_cores + lax.axis_index("core")
        pltpu.sync_copy(idx_hbm.at[worker], idx_v)
        base = worker * per_worker
        load_sems = (load_sem0, load_sem1)

        def load(c):
            return pltpu.async_copy(x_hbm.at[pl.ds(base + c * rows, rows)], rows_v.at[c % 2], load_sems[c % 2])

        loading = load(0)
        for c in range(n_chunks):
            loading.wait()
            if c + 1 < n_chunks:
                loading = load(c + 1)
            copies = [pltpu.async_copy(rows_v.at[c % 2], out_hbm.at[idx_v.at[c * TOP_K + k]], scatter_sem)
                      for k in range(TOP_K)]
            for cp in copies:
                cp.wait()

    return scatter_rows(x1, _sc_index(pos, n_workers, n_chunks, rows))


def _collect_rows(y_rows, pos):
    n_tok = pos.shape[0]
    w = y_rows.shape[1]
    rows = COLLECT_ROWS
    info, n_workers, per_worker, n_chunks = _sc_layout(n_tok, rows)
    mesh = plsc.VectorSubcoreMesh(core_axis_name="core", subcore_axis_name="subcore")

    @functools.partial(
        pl.kernel, out_type=jax.ShapeDtypeStruct((n_tok * TOP_K, w), y_rows.dtype), mesh=mesh,
        scratch_types=[pltpu.VMEM((n_chunks * TOP_K, rows), jnp.int32), pltpu.VMEM((2, TOP_K, rows, w), y_rows.dtype),
                       pltpu.SemaphoreType.DMA, pltpu.SemaphoreType.DMA],
        name="collect_rows")
    def gather_rows(y_hbm, idx_hbm, out_hbm, idx_v, rows_v, sem0, sem1):
        worker = lax.axis_index("subcore") * info.num_cores + lax.axis_index("core")
        pltpu.sync_copy(idx_hbm.at[worker], idx_v)
        base = worker * per_worker
        sems = (sem0, sem1)

        def gather(c):
            return [pltpu.async_copy(y_hbm.at[idx_v.at[c * TOP_K + k]], rows_v.at[c % 2, k], sems[c % 2])
                    for k in range(TOP_K)]

        pending = gather(0)
        for c in range(n_chunks):
            following = gather(c + 1) if c + 1 < n_chunks else []
            for cp in pending:
                cp.wait()
            for k in range(TOP_K):
                pltpu.sync_copy(rows_v.at[c % 2, k], out_hbm.at[pl.ds(k * n_tok + base + c * rows, rows)])
            pending = following

    return gather_rows(y_rows, _sc_index(pos, n_workers, n_chunks, rows))


def _moe(layer, x1, routing, wg, bg, wu, bu, wd, bd):
    n_tok = x1.shape[0]
    n_assign = n_tok * TOP_K
    n_blocks = n_assign // EXPERT_ROWS
    idx_o, rank_o, gate_o, cnt = routing
    counts = cnt[0, :N_EXPERTS]
    ends = jnp.cumsum(counts)
    offs = ends - counts
    experts = jnp.arange(N_EXPERTS, dtype=jnp.int32)
    top_i = idx_o[:, :TOP_K]
    off_tok = jnp.sum(jnp.where(top_i[:, :, None] == experts, offs, 0), axis=-1)
    pos = (off_tok + rank_o[:, :TOP_K]).reshape(n_assign)
    pos = pos.reshape(n_tok, TOP_K)
    xin = _dispatch_rows(x1, pos)

    first_blk = offs // EXPERT_ROWS
    last_blk = (ends - 1) // EXPERT_ROWS
    nvis = jnp.where(counts > 0, last_blk - first_blk + 1, 0)
    vend = jnp.cumsum(nvis)
    vstart = vend - nvis
    n_visits = n_blocks + N_EXPERTS - 1
    v = jnp.arange(n_visits, dtype=jnp.int32)
    e_v = jnp.sum((vend[None, :] <= v[:, None]).astype(jnp.int32), axis=1)
    e_last = jnp.max(jnp.where(counts > 0, experts, 0))
    e_v = jnp.minimum(e_v, e_last)
    pick = lambda tab: jnp.sum(jnp.where(e_v[:, None] == experts, tab, 0), axis=-1)
    blk_v = jnp.minimum(pick(first_blk) + v - pick(vstart), n_blocks - 1)
    active = counts > 0
    later = jnp.where(jnp.logical_and(active[None, :], experts[None, :] > experts[:, None]), experts[None, :], N_EXPERTS)
    nxt_e = jnp.min(later, axis=1)
    nxt_e = jnp.where(nxt_e == N_EXPERTS, experts, nxt_e)
    order_e = jnp.cumsum(active.astype(jnp.int32)) - 1
    y_rows = _experts(layer, blk_v.astype(jnp.int32), e_v.astype(jnp.int32), pick(offs).astype(jnp.int32),
                      pick(ends).astype(jnp.int32), vend[-1:].astype(jnp.int32),
                      pick(nxt_e).astype(jnp.int32), (pick(order_e) % 2).astype(jnp.int32),
                      xin, wg, bg, wu, bu, wd, bd)
    return y_rows, pos, gate_o


def _pack_w_in(w_in):
    qa, ka, va, oa, ia, fa, qb, fb, ib, gb = jnp.split(w_in, SPLIT_POINTS, axis=-1)

    def pad_heads(w):
        w = w.reshape(w.shape[:-1] + (N_HEADS, DK_A))
        w = jnp.pad(w, ((0, 0),) * (w.ndim - 1) + ((0, D_HEAD - DK_A),))
        return w.reshape(w.shape[:-2] + (N_HEADS * D_HEAD,))

    def pad_gate(w):
        return jnp.pad(w, ((0, 0),) * (w.ndim - 1) + ((0, LANES - N_HEADS),))

    cols = [pad_heads(qa), pad_heads(ka * (DK_A ** -0.5)), va, oa, qb, fb, ib, gb, pad_gate(ia), pad_gate(fa)]
    return jnp.concatenate(cols, axis=-1).astype(BF16)


def kernel(x_prompt, x_sample, state_mlstm_C, state_mlstm_n, state_mlstm_m, state_hgrn_S, p_prompt, p_sample,
           ln_in_g, ln_in_b, w_in, mlstm_ig_bias, mlstm_fg_bias, mlstm_norm_g, hgrn_lb_logits, hgrn_norm_g,
           w_out, ln1_g, ln1_b, w_router, b_router, w_gate, b_gate, w_up, b_up, w_down, b_down,
           w_ple_gate, w_ple_proj, ln2_g, ln2_b):
    bp, tp, d = x_prompt.shape
    bs, ts, _ = x_sample.shape
    n_p = bp * tp
    n_s = bs * ts
    ts_pad = SUBLANES
    assert tp % CHUNK == 0 and ts <= ts_pad and ts % CHUNK != 0

    lb_soft = jax.nn.softmax(hgrn_lb_logits.astype(F32), axis=0)
    lower_bounds = jnp.cumsum(lb_soft, axis=0) - lb_soft[0]

    w_in_p = _pack_w_in(w_in)
    gbias = jnp.stack([jnp.pad(mlstm_ig_bias, ((0, 0), (0, LANES - N_HEADS))),
                       jnp.pad(mlstm_fg_bias, ((0, 0), (0, LANES - N_HEADS)))], axis=1)
    w_out_b = w_out.astype(BF16)
    w_pg_b = w_ple_gate.astype(BF16)
    w_pp_b = w_ple_proj.astype(BF16)
    w_r = jnp.pad(w_router, ((0, 0), (0, 0), (0, LANES - N_EXPERTS)))
    w_r_hi = w_r.astype(BF16)
    w_r2 = jnp.concatenate([w_r_hi, (w_r - w_r_hi.astype(F32)).astype(BF16)], axis=-1)
    b_r = jnp.pad(b_router, ((0, 0), (0, LANES - N_EXPERTS))).reshape(DEPTH, 1, LANES)
    n0_pad = jnp.pad(state_mlstm_n, ((0, 0), (0, 0), (0, 0), (0, D_HEAD - DK_A))).reshape(
        DEPTH, bs, N_HEADS, 1, D_HEAD)
    m0_pad = jnp.pad(state_mlstm_m, ((0, 0), (0, 0), (0, LANES - N_HEADS))).reshape(DEPTH, bs, 1, LANES)

    p_p = p_prompt.reshape(DEPTH, n_p, D_PLE)
    p_s = p_sample.reshape(DEPTH, n_s, D_PLE)
    x = _ln(x_prompt.reshape(n_p, d), x_sample.reshape(n_s, d), ln_in_g, ln_in_b)

    assert n_p % ROW_TILE == 0 and n_s % ROW_TILE == 0
    nb_p = LANES // CHUNK
    nb_s = LANES // ts_pad
    st_p = _stacked_states(bp)
    st_s = _stacked_states(bs)
    for l in range(DEPTH):
        lb = lower_bounds[l].reshape(1, MIX)
        nga = mlstm_norm_g[l].reshape(1, MIX)
        ngb = hgrn_norm_g[l].reshape(1, MIX)
        proj_p = _inproj(x, w_in_p, l, 0, n_p).reshape(bp, tp, N_PROJ)
        h_p, st_p = _scan(proj_p, gbias[l], lb, nga, ngb, None, st_p, layer=l,
                          NB=nb_p, n_chunks=tp // CHUNK, L=CHUNK, last=CHUNK - 1)
        proj_s = jnp.pad(_inproj(x, w_in_p, l, n_p, n_s).reshape(bs, ts, N_PROJ), ((0, 0), (0, ts_pad - ts), (0, 0)))
        h_s, st_s = _scan(proj_s, gbias[l], lb, nga, ngb,
                          (state_mlstm_C, n0_pad, m0_pad, state_hgrn_S), st_s, layer=l,
                          NB=nb_s, n_chunks=1, L=ts_pad, last=ts - 1)
        x1, resid, *routing = _post_mixer(l, x, h_p.reshape(n_p, d), h_s[:, :ts].reshape(n_s, d), p_p, p_s,
                                          w_out_b[l], ln1_g[l], ln1_b[l], w_pg_b[l], w_pp_b[l],
                                          w_r2[l], w_r_hi[l], b_r[l])
        y_rows, pos, gates = _moe(l, x1, routing, w_gate, b_gate.reshape(DEPTH, N_EXPERTS, 1, d),
                                  w_up, b_up.reshape(DEPTH, N_EXPERTS, 1, d),
                                  w_down, b_down.reshape(DEPTH, N_EXPERTS, 1, d))
        n_part = (n_p + n_s) // COMBINE_PARTS
        assert n_part * COMBINE_PARTS == n_p + n_s and n_part % ROW_TILE == 0
        for part in range(COMBINE_PARTS):
            y_k = _collect_rows(y_rows, pos[part * n_part:(part + 1) * n_part])
            x = _combine_ln(resid, y_k, gates, ln2_g[l], ln2_b[l], x, part * n_part)

    y_prompt = x[:n_p].reshape(bp, tp, d)
    y_sample = x[n_p:].reshape(bs, ts, d)
    unpad = lambda st: (st[0], st[1][:, :, :, 0, :DK_A], st[2][:, :, 0, :N_HEADS], st[3])
    return (y_prompt, y_sample) + unpad(st_p) + unpad(st_s)
```

```python
import functools

import jax
import jax.numpy as jnp
import numpy as np
from jax import lax
from jax.experimental import pallas as pl
from jax.experimental.pallas import tpu as pltpu
from jax.experimental.pallas import tpu_sc as plsc

F32 = jnp.float32
BF16 = jnp.bfloat16

D_MODEL = 1024
DEPTH = 4
D_PLE = 256
N_HEADS = 4
DK_A = 64
D_HEAD = 128
MIX = 512
N_EXPERTS = 32
TOP_K = 4
SWIGLU_LIMIT = 7.0
SWIGLU_ALPHA = 1.702
CHUNK = 64
LN_EPS = 1e-5
NORM_EPS = 1e-6
LB_FLOOR = 1e-20
DEEPNORM_ALPHA = (2 * DEPTH) ** 0.25
SPLIT_SIZES = (256, 256, 512, 512, 4, 4, 512, 512, 512, 512)
SPLIT_POINTS = tuple(int(s) for s in np.cumsum(SPLIT_SIZES)[:-1])

LANES = 128
SUBLANES = 8
VMEM_LIMIT = 56 * 1024 * 1024

QA0, KA0, VA0, OA0 = 0, 512, 1024, 1536
QB0, FB0, IB0, GB0 = 2048, 2560, 3072, 3584
IG0, FG0 = 4096, 4224
N_PROJ = 4352

ROW_TILE = 512
EXPERT_ROWS = 256

NT_DIMS = (((1,), (1,)), ((), ()))
TN_DIMS = (((0,), (0,)), ((), ()))


def _params(*sem):
    return pltpu.CompilerParams(dimension_semantics=sem, vmem_limit_bytes=VMEM_LIMIT)


def _log_sigmoid(x):
    return jnp.minimum(x, 0.0) - jnp.log(1.0 + jnp.exp(-jnp.abs(x)))


def _split3(x):
    x1 = x.astype(BF16)
    r1 = x - x1.astype(F32)
    x2 = r1.astype(BF16)
    x3 = (r1 - x2.astype(F32)).astype(BF16)
    return x1, x2, x3


def _dot01(sel, x):
    return sum(jnp.dot(sel, t, preferred_element_type=F32) for t in _split3(x))


def _pack_pairs(x):
    w = x.shape[1] // 2
    bits = lax.bitcast_convert_type(x.astype(BF16).astype(F32), jnp.uint32)
    return lax.bitcast_convert_type(bits[:, :w] | (bits[:, w:] >> 16), jnp.int32)


def _unpack_pairs(words):
    u = lax.bitcast_convert_type(words, jnp.uint32)
    hi = lax.bitcast_convert_type(u & jnp.uint32(0xFFFF0000), F32)
    lo = lax.bitcast_convert_type(u << 16, F32)
    return hi, lo


def _layernorm_rows(x, g, b):
    mu = jnp.mean(x, axis=-1, keepdims=True)
    xc = x - mu
    var = jnp.mean(xc * xc, axis=-1, keepdims=True)
    return xc * lax.rsqrt(var + LN_EPS) * g + b


def _two_sources(n_a, n_b, width, lead=()):
    ta = n_a // ROW_TILE
    assert n_a % ROW_TILE == 0 and n_b % ROW_TILE == 0
    block = (None,) * len(lead) + (ROW_TILE, width)
    spec_a = pl.BlockSpec(block, lambda i: lead + (jnp.minimum(i, ta - 1), 0))
    spec_b = pl.BlockSpec(block, lambda i: lead + (jnp.maximum(i - ta, 0), 0))
    return spec_a, spec_b, ta


def _ln_kernel(xa_ref, xb_ref, g_ref, b_ref, o_ref, *, tiles_a):
    x = jnp.where(pl.program_id(0) < tiles_a, xa_ref[...], xb_ref[...])
    o_ref[...] = _layernorm_rows(x, g_ref[...], b_ref[...])


def _ln(xa, xb, g, b):
    d = xa.shape[1]
    n = xa.shape[0] + xb.shape[0]
    spec_a, spec_b, ta = _two_sources(xa.shape[0], xb.shape[0], d)
    vec = pl.BlockSpec((1, d), lambda i: (0, 0))
    return pl.pallas_call(
        functools.partial(_ln_kernel, tiles_a=ta), grid=(n // ROW_TILE,), in_specs=[spec_a, spec_b, vec, vec],
        out_specs=pl.BlockSpec((ROW_TILE, d), lambda i: (i, 0)),
        out_shape=jax.ShapeDtypeStruct((n, d), F32), compiler_params=_params("parallel"),
        name="ln_in")(xa, xb, g.reshape(1, d), b.reshape(1, d))


def _combine_ln_kernel(resid_ref, y0_ref, y1_ref, y2_ref, y3_ref, gate_ref, g_ref, beta_ref, buf_ref, o_ref):
    half = resid_ref.shape[1] // 2
    acc_hi = resid_ref[:, :half]
    acc_lo = resid_ref[:, half:]
    for k, y_ref in enumerate((y0_ref, y1_ref, y2_ref, y3_ref)):
        hi, lo = _unpack_pairs(y_ref[...])
        gk = gate_ref[:, k:k + 1]
        acc_hi = acc_hi + gk * hi
        acc_lo = acc_lo + gk * lo
    o_ref[...] = _layernorm_rows(jnp.concatenate([acc_hi, acc_lo], axis=1), g_ref[...], beta_ref[...])


def _combine_ln(resid, y_k, gates, g, beta, buf, row0):
    n, d = resid.shape
    n_part = y_k.shape[0] // TOP_K
    nt = n_part // ROW_TILE
    t0 = row0 // ROW_TILE
    row = lambda w: pl.BlockSpec((ROW_TILE, w), lambda i: (t0 + i, 0))
    yk = lambda k: pl.BlockSpec((ROW_TILE, d // 2), lambda i: (k * nt + i, 0))
    vec = pl.BlockSpec((1, d), lambda i: (0, 0))
    return pl.pallas_call(
        _combine_ln_kernel, grid=(nt,),
        in_specs=[row(d)] + [yk(k) for k in range(TOP_K)] + [row(LANES), vec, vec, pl.BlockSpec(memory_space=pl.ANY)],
        out_specs=row(d), out_shape=jax.ShapeDtypeStruct((n, d), F32), input_output_aliases={8: 0},
        compiler_params=_params("parallel"),
        name="combine_ln2")(resid, y_k, y_k, y_k, y_k, gates, g.reshape(1, d), beta.reshape(1, d), buf)


def _inproj_kernel(x_ref, w_ref, o_ref):
    o_ref[...] = jnp.dot(x_ref[...].astype(BF16), w_ref[...], preferred_element_type=F32)


def _inproj(x, w, layer, row0, n_rows):
    d = x.shape[1]
    t0 = row0 // ROW_TILE
    return pl.pallas_call(
        _inproj_kernel, grid=(n_rows // ROW_TILE,),
        in_specs=[pl.BlockSpec((ROW_TILE, d), lambda i: (t0 + i, 0)),
                  pl.BlockSpec((None, d, N_PROJ), lambda i: (layer, 0, 0))],
        out_specs=pl.BlockSpec((ROW_TILE, N_PROJ), lambda i: (i, 0)),
        out_shape=jax.ShapeDtypeStruct((n_rows, N_PROJ), F32), compiler_params=_params("parallel"),
        name="in_proj")(x, w)


def _scan_kernel(*refs, NB, L, last, has_state):
    it = iter(refs)
    proj_ref, gbias_ref, lb_ref, nga_ref, ngb_ref = (next(it) for _ in range(5))
    if has_state:
        C0_ref, n0_ref, m0_ref, S0_ref = (next(it) for _ in range(4))
    for _ in range(4):
        next(it)
    h_ref, C_out, n_out, m_out, S_out = (next(it) for _ in range(5))
    C_s, n_s, m_s, St_s = (next(it) for _ in range(4))

    c = pl.program_id(1)
    nc = pl.num_programs(1)
    H = range(N_HEADS)
    N = range(NB)
    R = NB * L
    assert R == LANES

    @pl.when(c == 0)
    def _init():
        if has_state:
            for n in N:
                for h in H:
                    C_s[n, h, 0:DK_A, :] = C0_ref[n, h]
                    C_s[n, h, DK_A:D_HEAD, :] = jnp.zeros((D_HEAD - DK_A, D_HEAD), F32)
                    St_s[n, h] = S0_ref[n, h].T
            n_s[...] = n0_ref[...]
            m_s[...] = m0_ref[...]
        else:
            C_s[...] = jnp.zeros(C_s.shape, F32)
            n_s[...] = jnp.zeros(n_s.shape, F32)
            m_s[...] = jnp.zeros(m_s.shape, F32)
            St_s[...] = jnp.zeros(St_s.shape, F32)

    row = lax.broadcasted_iota(jnp.int32, (R, R), 0)
    col = lax.broadcasted_iota(jnp.int32, (R, R), 1)
    causal = jnp.logical_and(row // L == col // L, row >= col)
    tril = causal.astype(BF16)
    ones_r = jnp.ones((R, LANES), BF16)
    lane = lax.broadcasted_iota(jnp.int32, (R, LANES), 1)
    step = lax.broadcasted_iota(jnp.int32, (R, 1), 0) % L
    valid = step <= last
    padded = last < L - 1
    tile = lambda base, h: slice(base + LANES * h, base + LANES * (h + 1))
    seq = lambda n: slice(n * L, (n + 1) * L)
    cols = lambda base, width: proj_ref[:, :, base:base + width].reshape(R, width)
    per_seq = lambda vals: jnp.concatenate([jnp.broadcast_to(x, (L, x.shape[-1])) for x in vals], axis=0)
    last_rows = lambda x: [x[n * L + last:n * L + last + 1] for n in N]

    i_t = cols(IG0, LANES) + gbias_ref[0:1, :]
    f_t = _log_sigmoid(cols(FG0, LANES) + gbias_ref[1:2, :])
    zf = cols(FB0, MIX)
    lb = lb_ref[...]
    la = jnp.log(jnp.maximum(lb, LB_FLOOR))
    bb = jnp.log1p(-lb) + _log_sigmoid(zf)
    f_log = jnp.maximum(la, bb) + jnp.log(1.0 + jnp.exp(-jnp.abs(la - bb)))
    b_t = _dot01(tril, f_t)
    g = _dot01(tril, f_log)
    m_all = [m_s[n] for n in N]
    C_prev = [[C_s[n, h] for h in H] for n in N]
    n_prev = [[n_s[n, h] for h in H] for n in N]
    St = [[St_s[n, h] for h in H] for n in N]
    qf = [cols(QA0 + LANES * h, LANES) for h in H]
    kf = [cols(KA0 + LANES * h, LANES) for h in H]
    q = [x.astype(BF16) for x in qf]
    k = [x.astype(BF16) for x in kf]
    v = [cols(VA0 + LANES * h, LANES).astype(BF16) for h in H]
    kb = (1.0 - lb) * jax.nn.sigmoid(-zf)
    qr = cols(QB0, MIX)
    qb = qr * jax.nn.sigmoid(qr)
    vb = [cols(IB0 + LANES * h, LANES).astype(BF16) for h in H]
    qb16 = qb.astype(BF16)
    kb16 = kb.astype(BF16)
    r_t = i_t - b_t
    zero16 = jnp.zeros((R, LANES), BF16)
    r_all = sum(
        lax.dot_general(ones_r, jnp.concatenate([jnp.where(lane == h, term, zero16) for h in H], axis=0),
                        NT_DIMS, preferred_element_type=F32)
        for term in _split3(r_t))
    r_m = [r_all[:, R * h:R * (h + 1)] for h in H]
    qk = [lax.dot_general(q[h], k[h], NT_DIMS, preferred_element_type=F32) for h in H]
    qC = [jnp.concatenate([jnp.dot(q[h][seq(n)], C_prev[n][h].astype(BF16), preferred_element_type=F32)
                           for n in N], axis=0) for h in H]
    levels = []
    bs = 1
    while bs < L:
        levels.append(bs)
        bs *= 2
    small = [bs for bs in levels if bs < SUBLANES]
    g_anchor = {}
    if small:
        sel = jnp.concatenate(
            [(col == (row // (2 * bs)) * (2 * bs) + bs).astype(BF16) for bs in small], axis=0)
        picked = _dot01(sel, g)
        for i, bs in enumerate(small):
            g_anchor[bs] = picked[i * R:(i + 1) * R]
    for bs in levels:
        if bs >= SUBLANES:
            g_anchor[bs] = jnp.concatenate(
                [jnp.broadcast_to(g[p0 + bs:p0 + bs + 1], (2 * bs, MIX)) for p0 in range(0, R, 2 * bs)], axis=0)
    qg = (qb * jnp.exp(g)).astype(BF16)
    o_inter = [jnp.concatenate([lax.dot_general(qg[seq(n), tile(0, h)], St[n][h].astype(BF16), NT_DIMS,
                                                preferred_element_type=F32) for n in N], axis=0) for h in H]
    a = [jnp.where(row == col, lax.dot_general(qb16[:, tile(0, h)], kb16[:, tile(0, h)], NT_DIMS,
                                                preferred_element_type=F32), 0.0) for h in H]
    qn = [jnp.sum(qf[h] * per_seq([n_prev[n][h] for n in N]), axis=1, keepdims=True) for h in H]
    b_col = [b_t[:, h:h + 1] for h in H]
    i_col = [i_t[:, h:h + 1] for h in H]
    m_prev = [per_seq([m_all[n][:, h:h + 1] for n in N]) for h in H]
    cm = [jnp.max(jnp.where(causal, r_m[h], -jnp.inf), axis=1, keepdims=True) for h in H]
    m_t = [b_col[h] + jnp.maximum(m_prev[h], cm[h]) for h in H]
    inter = [jnp.exp(b_col[h] + m_prev[h] - m_t[h]) for h in H]
    d = [jnp.exp(jnp.where(causal, (b_col[h] - m_t[h]) + r_m[h], -1e30)) for h in H]
    s = [qk[h] * d[h] for h in H]
    log2e = 1.4426950408889634
    e = [jnp.exp2((g - g_anchor[bs]) * jnp.where((step // bs) % 2 == 1, log2e, -log2e)) for bs in levels]
    qe = [(qb * x).astype(BF16) for x in e]
    ke = [(kb * x).astype(BF16) for x in e]
    sv = [jnp.dot(s[h].astype(BF16), v[h], preferred_element_type=F32) for h in H]
    sc = [[lax.dot_general(qe[i][:, tile(0, h)], ke[i][:, tile(0, h)], NT_DIMS, preferred_element_type=F32)
           for h in H] for i in range(len(levels))]
    den = [jnp.sum(s[h], axis=1, keepdims=True) + inter[h] * qn[h] for h in H]
    m_new = [last_rows(m_t[h]) for h in H]
    b_last = [last_rows(b_col[h]) for h in H]
    w_arg = [per_seq(b_last[h]) - b_col[h] + i_col[h] - per_seq(m_new[h]) for h in H]
    if padded:
        w_arg = [jnp.where(valid, x, -1e30) for x in w_arg]
    kw = [kf[h] * jnp.exp(w_arg[h]) for h in H]
    kw16 = [x.astype(BF16) for x in kw]
    decay = [[jnp.exp(b_last[h][n] + m_all[n][:, h:h + 1] - m_new[h][n]) for h in H] for n in N]
    kv = [[lax.dot_general(kw16[h][seq(n)], v[h][seq(n)], TN_DIMS, preferred_element_type=F32) for h in H]
          for n in N]
    for i, bs in enumerate(levels):
        pair = jnp.logical_and(row // (2 * bs) == col // (2 * bs),
                               jnp.logical_and((row // bs) % 2 == 1, (col // bs) % 2 == 0))
        for h in H:
            a[h] = jnp.where(pair, sc[i][h], a[h])
    o = [o_inter[h] + jnp.dot(a[h].astype(BF16), vb[h], preferred_element_type=F32) for h in H]
    g_last = last_rows(g)
    dec_arg = per_seq(g_last) - g
    if padded:
        dec_arg = jnp.where(valid, dec_arg, -1e30)
    kdec = (kb * jnp.exp(dec_arg)).astype(BF16)
    eg_last = [jnp.exp(x) for x in g_last]
    vk = [[lax.dot_general(vb[h][seq(n)], kdec[seq(n), tile(0, h)], TN_DIMS, preferred_element_type=F32)
           for h in H] for n in N]
    hh = [(sv[h] + inter[h] * qC[h]) / jnp.maximum(jnp.abs(den[h]), jnp.exp(-m_t[h])) for h in H]
    for h in H:
        hn = hh[h] * lax.rsqrt(jnp.mean(hh[h] * hh[h], axis=-1, keepdims=True) + NORM_EPS) * nga_ref[:, tile(0, h)]
        out_a = jax.nn.sigmoid(cols(OA0 + LANES * h, LANES)) * hn
        h_ref[:, :, tile(0, h)] = out_a.reshape(NB, L, LANES)
    for h in H:
        on = o[h] * lax.rsqrt(jnp.mean(o[h] * o[h], axis=-1, keepdims=True) + NORM_EPS) * ngb_ref[:, tile(0, h)]
        gr = cols(GB0 + LANES * h, LANES)
        h_ref[:, :, tile(MIX, h)] = (gr * jax.nn.sigmoid(gr) * on).reshape(NB, L, LANES)
    lane1 = lax.broadcasted_iota(jnp.int32, (1, LANES), 1)
    for n in N:
        for h in H:
            C_s[n, h] = decay[n][h] * C_prev[n][h] + kv[n][h]
            n_s[n, h] = decay[n][h] * n_prev[n][h] + jnp.sum(kw[h][seq(n)], axis=0, keepdims=True)
            St_s[n, h] = eg_last[n][:, tile(0, h)] * St[n][h] + vk[n][h]
        m_upd = m_all[n]
        for h in H:
            m_upd = jnp.where(lane1 == h, m_new[h][n], m_upd)
        m_s[n] = m_upd

    @pl.when(c == nc - 1)
    def _finish():
        for n in N:
            for h in H:
                C_out[n, h] = C_s[n, h, 0:DK_A, :]
                S_out[n, h] = St_s[n, h].T
        n_out[...] = n_s[...]
        m_out[...] = m_s[...]


def _scan(proj, gbias, lb, nga, ngb, state, stacked, *, layer, NB, n_chunks, L, last):
    batch = proj.shape[0]
    has_state = state is not None
    const2 = lambda b, c: (0, 0)
    in_specs = [
        pl.BlockSpec((NB, L, N_PROJ), lambda b, c: (b, c, 0)),
        pl.BlockSpec((2, LANES), const2),
        pl.BlockSpec((1, MIX), const2),
        pl.BlockSpec((1, MIX), const2),
        pl.BlockSpec((1, MIX), const2),
    ]
    args = [proj, gbias, lb, nga, ngb]
    lst4 = lambda b, c: (layer, b, 0, 0, 0)
    lst3 = lambda b, c: (layer, b, 0, 0)
    if has_state:
        in_specs += [
            pl.BlockSpec((None, NB, N_HEADS, DK_A, D_HEAD), lst4),
            pl.BlockSpec((None, NB, N_HEADS, 1, LANES), lst4),
            pl.BlockSpec((None, NB, 1, LANES), lst3),
            pl.BlockSpec((None, NB, N_HEADS, D_HEAD, D_HEAD), lst4),
        ]
        args += list(state)
    aliases = {len(args) + i: 1 + i for i in range(4)}
    in_specs += [pl.BlockSpec(memory_space=pl.ANY)] * 4
    args += list(stacked)
    out_specs = [
        pl.BlockSpec((NB, L, D_MODEL), lambda b, c: (b, c, 0)),
        pl.BlockSpec((None, NB, N_HEADS, DK_A, D_HEAD), lst4),
        pl.BlockSpec((None, NB, N_HEADS, 1, LANES), lst4),
        pl.BlockSpec((None, NB, 1, LANES), lst3),
        pl.BlockSpec((None, NB, N_HEADS, D_HEAD, D_HEAD), lst4),
    ]
    out_shape = [jax.ShapeDtypeStruct((batch, n_chunks * L, D_MODEL), F32)] + [
        jax.ShapeDtypeStruct(a.shape, a.dtype) for a in stacked]
    scratch = [
        pltpu.VMEM((NB, N_HEADS, D_HEAD, D_HEAD), F32),
        pltpu.VMEM((NB, N_HEADS, 1, LANES), F32),
        pltpu.VMEM((NB, 1, LANES), F32),
        pltpu.VMEM((NB, N_HEADS, D_HEAD, D_HEAD), F32),
    ]
    kern = functools.partial(_scan_kernel, NB=NB, L=L, last=last, has_state=has_state)
    res = pl.pallas_call(
        kern, grid=(batch // NB, n_chunks), in_specs=in_specs, out_specs=out_specs, out_shape=out_shape,
        scratch_shapes=scratch, input_output_aliases=aliases,
        compiler_params=_params("parallel", "arbitrary"),
        name="scan_state" if has_state else "scan_prompt")(*args)
    return res[0], tuple(res[1:])


def _stacked_states(batch):
    return (jnp.zeros((DEPTH, batch, N_HEADS, DK_A, D_HEAD), F32),
            jnp.zeros((DEPTH, batch, N_HEADS, 1, LANES), F32),
            jnp.zeros((DEPTH, batch, 1, LANES), F32),
            jnp.zeros((DEPTH, batch, N_HEADS, D_HEAD, D_HEAD), F32))


def _post_mixer_kernel(x_ref, ha_ref, hb_ref, pa_ref, pb_ref, wout_ref, g1_ref, b1_ref, wpg_ref, wpp_ref, wr2_ref,
                       wrh_ref, br_ref, x1_ref, resid_ref, idx_ref, rank_ref, gate_ref, cnt_ref, carry_s, *, tiles_a):
    i = pl.program_id(0)

    @pl.when(i == 0)
    def _init():
        carry_s[...] = jnp.zeros(carry_s.shape, F32)

    from_a = i < tiles_a
    h = jnp.where(from_a, ha_ref[...], hb_ref[...])
    p = jnp.where(from_a, pa_ref[...], pb_ref[...])
    mix = jnp.dot(h.astype(BF16), wout_ref[...], preferred_element_type=F32)
    x1 = _layernorm_rows(DEEPNORM_ALPHA * x_ref[...] + mix, g1_ref[...], b1_ref[...])
    x1b = x1.astype(BF16)
    x1_ref[...] = _pack_pairs(x1)
    x1_lo = (x1 - x1b.astype(F32)).astype(BF16)
    l2 = jnp.dot(x1b, wr2_ref[...], preferred_element_type=F32)
    logits = (l2[:, :LANES] + l2[:, LANES:]
              + jnp.dot(x1_lo, wrh_ref[...], preferred_element_type=F32) + br_ref[...])

    tm = logits.shape[0]
    lane = lax.broadcasted_iota(jnp.int32, (tm, LANES), 1)
    lane_f = lane.astype(F32)
    l = jnp.where(lane < N_EXPERTS, logits, -jnp.inf)
    vals, idxs, onehots = [], [], []
    for _ in range(TOP_K):
        mx = jnp.max(l, axis=1, keepdims=True)
        ix = jnp.min(jnp.where(l == mx, lane_f, float(LANES)), axis=1, keepdims=True)
        sel = lane_f == ix
        vals.append(mx)
        idxs.append(ix)
        onehots.append(sel.astype(F32))
        l = jnp.where(sel, -jnp.inf, l)
    gate = jax.nn.sigmoid(jnp.dot(x1b, wpg_ref[...], preferred_element_type=F32))
    pp = jnp.dot(p.astype(BF16), wpp_ref[...], preferred_element_type=F32)
    resid_ref[...] = DEEPNORM_ALPHA * x1 + gate * pp

    w = [jnp.exp(val - vals[0]) for val in vals]
    tot = w[0] + w[1] + w[2] + w[3]
    oh_all = onehots[0] + onehots[1] + onehots[2] + onehots[3]
    row = lax.broadcasted_iota(jnp.int32, (tm, tm), 0)
    col = lax.broadcasted_iota(jnp.int32, (tm, tm), 1)
    earlier = (row > col).astype(BF16)
    prefix = jnp.dot(earlier, oh_all.astype(BF16), preferred_element_type=F32) + carry_s[...]
    idx_o = jnp.zeros((tm, LANES), F32)
    rank_o = jnp.zeros((tm, LANES), F32)
    gate_o = jnp.zeros((tm, LANES), F32)
    for k in range(TOP_K):
        rank_k = jnp.sum(onehots[k] * prefix, axis=1, keepdims=True)
        idx_o = jnp.where(lane == k, idxs[k], idx_o)
        rank_o = jnp.where(lane == k, rank_k, rank_o)
        gate_o = jnp.where(lane == k, w[k] / tot, gate_o)
    idx_ref[...] = idx_o.astype(jnp.int32)
    rank_ref[...] = rank_o.astype(jnp.int32)
    gate_ref[...] = gate_o
    carry_s[...] = carry_s[...] + jnp.sum(oh_all, axis=0, keepdims=True)
    cnt_ref[...] = carry_s[...].astype(jnp.int32)


def _post_mixer(layer, x, ha, hb, pa, pb, wout, g1, b1, wpg, wpp, wr2, wrh, br):
    n, d = x.shape
    row = lambda w: pl.BlockSpec((ROW_TILE, w), lambda i: (i, 0))
    full = lambda a: pl.BlockSpec(a.shape, lambda i: (0, 0))
    ha_spec, hb_spec, ta = _two_sources(ha.shape[0], hb.shape[0], d)
    pa_spec, pb_spec, _ = _two_sources(pa.shape[1], pb.shape[1], D_PLE, lead=(layer,))
    g1 = g1.reshape(1, d)
    b1 = b1.reshape(1, d)
    return pl.pallas_call(
        functools.partial(_post_mixer_kernel, tiles_a=ta), grid=(n // ROW_TILE,),
        in_specs=[row(d), ha_spec, hb_spec, pa_spec, pb_spec, full(wout), full(g1), full(b1), full(wpg), full(wpp),
                  full(wr2), full(wrh), full(br)],
        out_specs=[row(d // 2), row(d), row(LANES), row(LANES), row(LANES), pl.BlockSpec((1, LANES), lambda i: (0, 0))],
        out_shape=[jax.ShapeDtypeStruct((n, d // 2), jnp.int32), jax.ShapeDtypeStruct((n, d), F32),
                   jax.ShapeDtypeStruct((n, LANES), jnp.int32), jax.ShapeDtypeStruct((n, LANES), jnp.int32),
                   jax.ShapeDtypeStruct((n, LANES), F32), jax.ShapeDtypeStruct((1, LANES), jnp.int32)],
        scratch_shapes=[pltpu.VMEM((1, LANES), F32)],
        compiler_params=_params("arbitrary"), name="post_mixer")(
            x, ha, hb, pa, pb, wout, g1, b1, wpg, wpp, wr2, wrh, br)


def _expert_kernel(blk_ref, e_ref, lo_ref, hi_ref, nv_ref, nxt_ref, slot_ref, x_ref, wg_hbm, bg_ref, wu_hbm, bu_ref,
                   wd_hbm, bd_ref, o_ref, w32_s, wg_s, wu_s, wd_s, sem, *, layer):
    v = pl.program_id(0)
    prev = jnp.maximum(v - 1, 0)
    valid = v < nv_ref[0]
    new_expert = jnp.logical_or(v == 0, e_ref[v] != e_ref[prev])
    new_block = jnp.logical_or(v == 0, blk_ref[v] != blk_ref[prev])
    slot = slot_ref[v]

    def weight_copies(expert, to_slot):
        return [pltpu.make_async_copy(w_hbm.at[layer, expert], w32_s.at[to_slot, j], sem.at[to_slot, j])
                for j, w_hbm in enumerate((wg_hbm, wu_hbm, wd_hbm))]

    def start_weights(expert, to_slot):
        for j, w_hbm in enumerate((wg_hbm, wu_hbm, wd_hbm)):
            pltpu.async_copy(w_hbm.at[layer, expert], w32_s.at[to_slot, j], sem.at[to_slot, j], priority=1)

    @pl.when(v == 0)
    def _prime():
        start_weights(e_ref[0], slot)

    @pl.when(jnp.logical_and(valid, new_expert))
    def _next_weights():
        for cp in weight_copies(e_ref[v], slot):
            cp.wait()

        @pl.when(nxt_ref[v] != e_ref[v])
        def _prefetch():
            start_weights(nxt_ref[v], 1 - slot)

        wg_s[...] = w32_s[slot, 0].astype(BF16)
        wu_s[...] = w32_s[slot, 1].astype(BF16)
        wd_s[...] = w32_s[slot, 2].astype(BF16)

    @pl.when(valid)
    def _compute():
        half = wg_s.shape[0] // 2
        x_hi, x_lo = (t.astype(BF16) for t in _unpack_pairs(x_ref[...]))
        gate = (jnp.dot(x_hi, wg_s[0:half, :], preferred_element_type=F32)
                + jnp.dot(x_lo, wg_s[half:, :], preferred_element_type=F32) + bg_ref[0])
        up = (jnp.dot(x_hi, wu_s[0:half, :], preferred_element_type=F32)
              + jnp.dot(x_lo, wu_s[half:, :], preferred_element_type=F32) + bu_ref[0])
        gate = jnp.minimum(gate, SWIGLU_LIMIT)
        up = jnp.clip(up, -SWIGLU_LIMIT, SWIGLU_LIMIT)
        glu = gate * jax.nn.sigmoid(gate * SWIGLU_ALPHA)
        act = ((up + 1.0) * glu).astype(BF16)
        y = _pack_pairs(jnp.dot(act, wd_s[...], preferred_element_type=F32) + bd_ref[0])
        rows = blk_ref[v] * EXPERT_ROWS + lax.broadcasted_iota(jnp.int32, (EXPERT_ROWS, 1), 0)
        mine = jnp.logical_and(rows >= lo_ref[v], rows < hi_ref[v])

        @pl.when(new_block)
        def _first():
            o_ref[...] = jnp.where(mine, y, 0)

        @pl.when(jnp.logical_not(new_block))
        def _again():
            o_ref[...] = jnp.where(mine, y, o_ref[...])


def _experts(layer, blk_v, e_v, lo_v, hi_v, n_vis, nxt_v, slot_v, xin, wg, bg, wu, bu, wd, bd):
    n_rows = xin.shape[0]
    d = wg.shape[-1]
    n_visits = blk_v.shape[0]
    xspec = pl.BlockSpec((EXPERT_ROWS, d // 2), lambda v, blk, e, *_: (blk[v], 0))
    ospec = pl.BlockSpec((EXPERT_ROWS, d // 2), lambda v, blk, e, *_: (blk[v], 0))
    wspec = pl.BlockSpec(memory_space=pl.ANY)
    bspec = pl.BlockSpec((None, 1, 1, d), lambda v, blk, e, *_: (layer, e[v], 0, 0))
    grid_spec = pltpu.PrefetchScalarGridSpec(
        num_scalar_prefetch=7, grid=(n_visits,),
        in_specs=[xspec, wspec, bspec, wspec, bspec, wspec, bspec], out_specs=ospec,
        scratch_shapes=[pltpu.VMEM((2, 3, d, d), F32)] + [pltpu.VMEM((d, d), BF16)] * 3
        + [pltpu.SemaphoreType.DMA((2, 3))])
    return pl.pallas_call(
        functools.partial(_expert_kernel, layer=layer), grid_spec=grid_spec,
        out_shape=jax.ShapeDtypeStruct((n_rows, d // 2), jnp.int32),
        compiler_params=_params("arbitrary"), name="experts")(
            blk_v, e_v, lo_v, hi_v, n_vis, nxt_v, slot_v, xin, wg, bg, wu, bu, wd, bd)


DISPATCH_ROWS = 48
COLLECT_ROWS = 24
COMBINE_PARTS = 1


def _sc_layout(n_tok, chunk_rows):
    info = plsc.get_sparse_core_info()
    n_workers = info.num_cores * info.num_subcores
    per_worker = n_tok // n_workers
    n_chunks = per_worker // chunk_rows
    assert per_worker * n_workers == n_tok and n_chunks * chunk_rows == per_worker
    return info, n_workers, per_worker, n_chunks


def _sc_index(pos, n_workers, n_chunks, chunk_rows):
    return pos.T.reshape(TOP_K, n_workers, n_chunks, chunk_rows).transpose(1, 2, 0, 3).reshape(
        n_workers, n_chunks * TOP_K, chunk_rows)


def _dispatch_rows(x1, pos):
    n_tok, w = x1.shape
    rows = DISPATCH_ROWS
    info, n_workers, per_worker, n_chunks = _sc_layout(n_tok, rows)
    mesh = plsc.VectorSubcoreMesh(core_axis_name="core", subcore_axis_name="subcore")

    @functools.partial(
        pl.kernel, out_type=jax.ShapeDtypeStruct((n_tok * TOP_K, w), x1.dtype), mesh=mesh,
        scratch_types=[pltpu.VMEM((n_chunks * TOP_K, rows), jnp.int32), pltpu.VMEM((2, rows, w), x1.dtype),
                       pltpu.SemaphoreType.DMA, pltpu.SemaphoreType.DMA, pltpu.SemaphoreType.DMA],
        name="dispatch_rows")
    def scatter_rows(x_hbm, idx_hbm, out_hbm, idx_v, rows_v, load_sem0, load_sem1, scatter_sem):
        worker = lax.axis_index("subcore") * info.num_cores + lax.axis_index("core")
        pltpu.sync_copy(idx_hbm.at[worker], idx_v)
        base = worker * per_worker
        load_sems = (load_sem0, load_sem1)

        def load(c):
            return pltpu.async_copy(x_hbm.at[pl.ds(base + c * rows, rows)], rows_v.at[c % 2], load_sems[c % 2])

        loading = load(0)
        for c in range(n_chunks):
            loading.wait()
            if c + 1 < n_chunks:
                loading = load(c + 1)
            copies = [pltpu.async_copy(rows_v.at[c % 2], out_hbm.at[idx_v.at[c * TOP_K + k]], scatter_sem)
                      for k in range(TOP_K)]
            for cp in copies:
                cp.wait()

    return scatter_rows(x1, _sc_index(pos, n_workers, n_chunks, rows))


def _collect_rows(y_rows, pos):
    n_tok = pos.shape[0]
    w = y_rows.shape[1]
    rows = COLLECT_ROWS
    info, n_workers, per_worker, n_chunks = _sc_layout(n_tok, rows)
    mesh = plsc.VectorSubcoreMesh(core_axis_name="core", subcore_axis_name="subcore")

    @functools.partial(
        pl.kernel, out_type=jax.ShapeDtypeStruct((n_tok * TOP_K, w), y_rows.dtype), mesh=mesh,
        scratch_types=[pltpu.VMEM((n_chunks * TOP_K, rows), jnp.int32), pltpu.VMEM((2, TOP_K, rows, w), y_rows.dtype),
                       pltpu.SemaphoreType.DMA, pltpu.SemaphoreType.DMA],
        name="collect_rows")
    def gather_rows(y_hbm, idx_hbm, out_hbm, idx_v, rows_v, sem0, sem1):
        worker = lax.axis_index("subcore") * info.num_cores + lax.axis_index("core")
        pltpu.sync_copy(idx_hbm.at[worker], idx_v)
        base = worker * per_worker
        sems = (sem0, sem1)

        def gather(c):
            return [pltpu.async_copy(y_hbm.at[idx_v.at[c * TOP_K + k]], rows_v.at[c % 2, k], sems[c % 2])
                    for k in range(TOP_K)]

        pending = gather(0)
        for c in range(n_chunks):
            following = gather(c + 1) if c + 1 < n_chunks else []
            for cp in pending:
                cp.wait()
            for k in range(TOP_K):
                pltpu.sync_copy(rows_v.at[c % 2, k], out_hbm.at[pl.ds(k * n_tok + base + c * rows, rows)])
            pending = following

    return gather_rows(y_rows, _sc_index(pos, n_workers, n_chunks, rows))


def _moe(layer, x1, routing, wg, bg, wu, bu, wd, bd):
    n_tok = x1.shape[0]
    n_assign = n_tok * TOP_K
    n_blocks = n_assign // EXPERT_ROWS
    idx_o, rank_o, gate_o, cnt = routing
    counts = cnt[0, :N_EXPERTS]
    ends = jnp.cumsum(counts)
    offs = ends - counts
    experts = jnp.arange(N_EXPERTS, dtype=jnp.int32)
    top_i = idx_o[:, :TOP_K]
    off_tok = jnp.sum(jnp.where(top_i[:, :, None] == experts, offs, 0), axis=-1)
    pos = (off_tok + rank_o[:, :TOP_K]).reshape(n_assign)
    pos = pos.reshape(n_tok, TOP_K)
    xin = _dispatch_rows(x1, pos)

    first_blk = offs // EXPERT_ROWS
    last_blk = (ends - 1) // EXPERT_ROWS
    nvis = jnp.where(counts > 0, last_blk - first_blk + 1, 0)
    vend = jnp.cumsum(nvis)
    vstart = vend - nvis
    n_visits = n_blocks + N_EXPERTS - 1
    v = jnp.arange(n_visits, dtype=jnp.int32)
    e_v = jnp.sum((vend[None, :] <= v[:, None]).astype(jnp.int32), axis=1)
    e_last = jnp.max(jnp.where(counts > 0, experts, 0))
    e_v = jnp.minimum(e_v, e_last)
    pick = lambda tab: jnp.sum(jnp.where(e_v[:, None] == experts, tab, 0), axis=-1)
    blk_v = jnp.minimum(pick(first_blk) + v - pick(vstart), n_blocks - 1)
    active = counts > 0
    later = jnp.where(jnp.logical_and(active[None, :], experts[None, :] > experts[:, None]), experts[None, :], N_EXPERTS)
    nxt_e = jnp.min(later, axis=1)
    nxt_e = jnp.where(nxt_e == N_EXPERTS, experts, nxt_e)
    order_e = jnp.cumsum(active.astype(jnp.int32)) - 1
    y_rows = _experts(layer, blk_v.astype(jnp.int32), e_v.astype(jnp.int32), pick(offs).astype(jnp.int32),
                      pick(ends).astype(jnp.int32), vend[-1:].astype(jnp.int32),
                      pick(nxt_e).astype(jnp.int32), (pick(order_e) % 2).astype(jnp.int32),
                      xin, wg, bg, wu, bu, wd, bd)
    return y_rows, pos, gate_o


def _pack_w_in(w_in):
    qa, ka, va, oa, ia, fa, qb, fb, ib, gb = jnp.split(w_in, SPLIT_POINTS, axis=-1)

    def pad_heads(w):
        w = w.reshape(w.shape[:-1] + (N_HEADS, DK_A))
        w = jnp.pad(w, ((0, 0),) * (w.ndim - 1) + ((0, D_HEAD - DK_A),))
        return w.reshape(w.shape[:-2] + (N_HEADS * D_HEAD,))

    def pad_gate(w):
        return jnp.pad(w, ((0, 0),) * (w.ndim - 1) + ((0, LANES - N_HEADS),))

    cols = [pad_heads(qa), pad_heads(ka * (DK_A ** -0.5)), va, oa, qb, fb, ib, gb, pad_gate(ia), pad_gate(fa)]
    return jnp.concatenate(cols, axis=-1).astype(BF16)


def kernel(x_prompt, x_sample, state_mlstm_C, state_mlstm_n, state_mlstm_m, state_hgrn_S, p_prompt, p_sample,
           ln_in_g, ln_in_b, w_in, mlstm_ig_bias, mlstm_fg_bias, mlstm_norm_g, hgrn_lb_logits, hgrn_norm_g,
           w_out, ln1_g, ln1_b, w_router, b_router, w_gate, b_gate, w_up, b_up, w_down, b_down,
           w_ple_gate, w_ple_proj, ln2_g, ln2_b):
    bp, tp, d = x_prompt.shape
    bs, ts, _ = x_sample.shape
    n_p = bp * tp
    n_s = bs * ts
    ts_pad = SUBLANES
    assert tp % CHUNK == 0 and ts <= ts_pad and ts % CHUNK != 0

    lb_soft = jax.nn.softmax(hgrn_lb_logits.astype(F32), axis=0)
    lower_bounds = jnp.cumsum(lb_soft, axis=0) - lb_soft[0]

    w_in_p = _pack_w_in(w_in)
    gbias = jnp.stack([jnp.pad(mlstm_ig_bias, ((0, 0), (0, LANES - N_HEADS))),
                       jnp.pad(mlstm_fg_bias, ((0, 0), (0, LANES - N_HEADS)))], axis=1)
    w_out_b = w_out.astype(BF16)
    w_pg_b = w_ple_gate.astype(BF16)
    w_pp_b = w_ple_proj.astype(BF16)
    w_r = jnp.pad(w_router, ((0, 0), (0, 0), (0, LANES - N_EXPERTS)))
    w_r_hi = w_r.astype(BF16)
    w_r2 = jnp.concatenate([w_r_hi, (w_r - w_r_hi.astype(F32)).astype(BF16)], axis=-1)
    b_r = jnp.pad(b_router, ((0, 0), (0, LANES - N_EXPERTS))).reshape(DEPTH, 1, LANES)
    n0_pad = jnp.pad(state_mlstm_n, ((0, 0), (0, 0), (0, 0), (0, D_HEAD - DK_A))).reshape(
        DEPTH, bs, N_HEADS, 1, D_HEAD)
    m0_pad = jnp.pad(state_mlstm_m, ((0, 0), (0, 0), (0, LANES - N_HEADS))).reshape(DEPTH, bs, 1, LANES)

    p_p = p_prompt.reshape(DEPTH, n_p, D_PLE)
    p_s = p_sample.reshape(DEPTH, n_s, D_PLE)
    x = _ln(x_prompt.reshape(n_p, d), x_sample.reshape(n_s, d), ln_in_g, ln_in_b)

    assert n_p % ROW_TILE == 0 and n_s % ROW_TILE == 0
    nb_p = LANES // CHUNK
    nb_s = LANES // ts_pad
    st_p = _stacked_states(bp)
    st_s = _stacked_states(bs)
    for l in range(DEPTH):
        lb = lower_bounds[l].reshape(1, MIX)
        nga = mlstm_norm_g[l].reshape(1, MIX)
        ngb = hgrn_norm_g[l].reshape(1, MIX)
        proj_p = _inproj(x, w_in_p, l, 0, n_p).reshape(bp, tp, N_PROJ)
        h_p, st_p = _scan(proj_p, gbias[l], lb, nga, ngb, None, st_p, layer=l,
                          NB=nb_p, n_chunks=tp // CHUNK, L=CHUNK, last=CHUNK - 1)
        proj_s = jnp.pad(_inproj(x, w_in_p, l, n_p, n_s).reshape(bs, ts, N_PROJ), ((0, 0), (0, ts_pad - ts), (0, 0)))
        h_s, st_s = _scan(proj_s, gbias[l], lb, nga, ngb,
                          (state_mlstm_C, n0_pad, m0_pad, state_hgrn_S), st_s, layer=l,
                          NB=nb_s, n_chunks=1, L=ts_pad, last=ts - 1)
        x1, resid, *routing = _post_mixer(l, x, h_p.reshape(n_p, d), h_s[:, :ts].reshape(n_s, d), p_p, p_s,
                                          w_out_b[l], ln1_g[l], ln1_b[l], w_pg_b[l], w_pp_b[l],
                                          w_r2[l], w_r_hi[l], b_r[l])
        y_rows, pos, gates = _moe(l, x1, routing, w_gate, b_gate.reshape(DEPTH, N_EXPERTS, 1, d),
                                  w_up, b_up.reshape(DEPTH, N_EXPERTS, 1, d),
                                  w_down, b_down.reshape(DEPTH, N_EXPERTS, 1, d))
        n_part = (n_p + n_s) // COMBINE_PARTS
        assert n_part * COMBINE_PARTS == n_p + n_s and n_part % ROW_TILE == 0
        for part in range(COMBINE_PARTS):
            y_k = _collect_rows(y_rows, pos[part * n_part:(part + 1) * n_part])
            x = _combine_ln(resid, y_k, gates, ln2_g[l], ln2_b[l], x, part * n_part)

    y_prompt = x[:n_p].reshape(bp, tp, d)
    y_sample = x[n_p:].reshape(bs, ts, d)
    unpad = lambda st: (st[0], st[1][:, :, :, 0, :DK_A], st[2][:, :, 0, :N_HEADS], st[3])
    return (y_prompt, y_sample) + unpad(st_p) + unpad(st_s)
```
